```python
import math
import jax, jax.numpy as jnp
from jax import lax
import numpy as np

D_MODEL = 1024
BATCH = 16
SEQ = 2048
DEPTH = 1

EPS = 1e-6
ATT_HEADS = 8
ATT_HEAD_DIM = 64
ATT_V_DIM = 2 * ATT_HEAD_DIM
ATT_QK_COLS = ATT_HEADS * 2 * ATT_HEAD_DIM
ATT_WIDTH = ATT_HEADS * ATT_V_DIM
ROPE_THETA = 10000.0
Q_BLOCK = 128
SSM_EXPAND = 2
SSM_D_INNER = SSM_EXPAND * D_MODEL
SSM_HEAD_DIM = 64
SSM_HEADS = SSM_D_INNER // SSM_HEAD_DIM
SSM_GROUPS = 8
SSM_HEADS_PER_GROUP = SSM_HEADS // SSM_GROUPS
SSM_STATE = 128
SSM_CONV = 4
SSM_CHUNK = 128
SSM_XBC = SSM_D_INNER + 2 * SSM_GROUPS * SSM_STATE
IN_COLS = 2 * ATT_QK_COLS + ATT_WIDTH + SSM_D_INNER + SSM_XBC + SSM_HEADS + 2 * D_MODEL
N_EXPERTS = 32
TOP_K = 4
D_FF_EXPERT = D_MODEL
SWIGLU_LIMIT = 7.0
SWIGLU_ALPHA = 1.702
MOE_BLOCK = 128

kernel_name = "hybrid_diffattn_ssd_moe_block"


def rms_norm(x, g):
    xf = x.astype(jnp.float32)
    y = xf * lax.rsqrt(jnp.mean(xf * xf, axis=-1, keepdims=True) + EPS)
    return (y * g.astype(jnp.float32)).astype(x.dtype)


def rope_tables(seq_len, dim):
    inv = ROPE_THETA ** (-jnp.arange(0, dim, 2, dtype=jnp.float32) / dim)
    ang = jnp.arange(seq_len, dtype=jnp.float32)[:, None] * inv[None, :]
    return jnp.cos(ang), jnp.sin(ang)


def apply_rope(x, cos, sin):
    half = x.shape[-1] // 2
    c = cos[None, :, None, None, :].astype(x.dtype)
    s = sin[None, :, None, None, :].astype(x.dtype)
    x1, x2 = x[..., :half], x[..., half:]
    return jnp.concatenate([x1 * c - x2 * s, x2 * c + x1 * s], axis=-1)


def diff_attention(q, k, v, lam):
    B, S, H = q.shape[0], q.shape[1], q.shape[2]
    nb = S // Q_BLOCK
    scale = ATT_HEAD_DIM ** -0.5
    qb = q.reshape(B, nb, Q_BLOCK, H, 2, ATT_HEAD_DIM).transpose(1, 0, 2, 3, 4, 5)
    key_pos = jnp.arange(S)

    def block(args):
        qi, i = args
        s = jnp.einsum('bqhcd,bkhcd->bhcqk', qi, k).astype(jnp.float32) * scale
        q_pos = i * Q_BLOCK + jnp.arange(Q_BLOCK)
        mask = key_pos[None, :] <= q_pos[:, None]
        s = jnp.where(mask, s, -jnp.inf)
        p = jax.nn.softmax(s, axis=-1)
        a = p[:, :, 0] - lam * p[:, :, 1]
        return jnp.einsum('bhqk,bkhe->bqhe', a.astype(v.dtype), v)

    o = lax.map(block, (qb, jnp.arange(nb)))
    return o.transpose(1, 0, 2, 3, 4).reshape(B, S, H, ATT_V_DIM)


def causal_dwconv(x, w, b):
    C = x.shape[-1]
    y = lax.conv_general_dilated(
        x, w[:, None, :].astype(x.dtype), window_strides=(1,), padding=[(SSM_CONV - 1, 0)],
        dimension_numbers=('NWC', 'WIO', 'NWC'), feature_group_count=C)
    return y + b.astype(x.dtype)


def ssd_mixer(xbc, z, dt_raw, conv_w, conv_b, dt_bias, a_log, d_skip, norm_g):
    B, S, _ = xbc.shape
    G, R, P, N, Q = SSM_GROUPS, SSM_HEADS_PER_GROUP, SSM_HEAD_DIM, SSM_STATE, SSM_CHUNK
    nc = S // Q
    xbc = jax.nn.silu(causal_dwconv(xbc, conv_w, conv_b))
    xs, bm, cm = jnp.split(xbc, [SSM_D_INNER, SSM_D_INNER + G * N], axis=-1)
    xs = xs.reshape(B, nc, Q, G, R, P)
    bm = bm.reshape(B, nc, Q, G, N)
    cm = cm.reshape(B, nc, Q, G, N)
    dt = jax.nn.softplus(dt_raw.astype(jnp.float32) + dt_bias).reshape(B, nc, Q, G, R)
    a_neg = -jnp.exp(a_log.astype(jnp.float32)).reshape(G, R)
    acs = jnp.cumsum(dt * a_neg, axis=2)
    xdt = xs * dt[..., None]

    tri = jnp.tril(jnp.ones((Q, Q), dtype=bool))
    seg = acs[:, :, :, None] - acs[:, :, None, :]
    decay = jnp.exp(jnp.where(tri[None, None, :, :, None, None], seg, -jnp.inf))
    cb = jnp.einsum('bcign,bcjgn->bcijg', cm, bm)
    y_diag = jnp.einsum('bcijgr,bcjgrp->bcigrp', cb[..., None] * decay, xdt)

    to_end = jnp.exp(acs[:, :, -1:] - acs)
    chunk_states = jnp.einsum('bcjgn,bcjgrp->bcgrpn', bm, xdt * to_end[..., None])
    chunk_decay = jnp.exp(acs[:, :, -1])

    def step(h, inp):
        st, dec = inp
        return dec[..., None, None] * h + st, h

    h0 = jnp.zeros((B, G, R, P, N), chunk_states.dtype)
    _, prev = lax.scan(step, h0, (chunk_states.transpose(1, 0, 2, 3, 4, 5),
                                  chunk_decay.transpose(1, 0, 2, 3)))
    prev = prev.transpose(1, 0, 2, 3, 4, 5)
    y_off = jnp.einsum('bcign,bcgrpn->bcigrp', cm, prev) * jnp.exp(acs)[..., None]

    y = y_diag + y_off + d_skip.reshape(G, R)[:, :, None] * xs
    y = y.reshape(B, S, SSM_D_INNER).astype(z.dtype) * jax.nn.silu(z)
    yg = y.reshape(B, S, G, SSM_D_INNER // G).astype(jnp.float32)
    yg = yg * lax.rsqrt(jnp.mean(yg * yg, axis=-1, keepdims=True) + EPS)
    return (yg.reshape(B, S, SSM_D_INNER) * norm_g).astype(z.dtype)


def moe_ffn(h, w_router, b_router, w_gate_up, b_gate_up, w_down, b_down):
    B, S, D = h.shape
    T = B * S
    A = T * TOP_K
    hf = h.reshape(T, D)
    logits = (hf @ w_router + b_router).astype(jnp.float32)
    top_vals, top_idx = lax.top_k(logits, TOP_K)
    gates = jax.nn.softmax(top_vals, axis=-1)
    expert_of = top_idx.reshape(A).astype(jnp.int32)
    token_of = jnp.arange(A, dtype=jnp.int32) // TOP_K
    order = jnp.argsort(expert_of)
    e_sorted = expert_of[order]
    counts = jnp.zeros((N_EXPERTS,), jnp.int32).at[expert_of].add(1)
    padded = (counts + MOE_BLOCK - 1) // MOE_BLOCK * MOE_BLOCK
    start = jnp.cumsum(counts) - counts
    pend = jnp.cumsum(padded)
    pstart = pend - padded
    dest_sorted = pstart[e_sorted] + (jnp.arange(A, dtype=jnp.int32) - start[e_sorted])
    dest = jnp.zeros((A,), jnp.int32).at[order].set(dest_sorted)
    n_rows = A + N_EXPERTS * MOE_BLOCK
    n_blocks = n_rows // MOE_BLOCK
    row_token = jnp.full((n_rows,), T, jnp.int32).at[dest].set(token_of)
    h_pad = jnp.concatenate([hf, jnp.zeros((1, D), hf.dtype)], axis=0)
    rows = h_pad[row_token].reshape(n_blocks, MOE_BLOCK, D)
    block_start = jnp.arange(n_blocks, dtype=jnp.int32) * MOE_BLOCK
    block_expert = jnp.minimum(jnp.searchsorted(pend, block_start, side='right'), N_EXPERTS - 1)

    def expert_block(args):
        xb, e = args
        gu = xb @ w_gate_up[e] + b_gate_up[e]
        gate, up = jnp.split(gu, 2, axis=-1)
        gate = jnp.minimum(gate, SWIGLU_LIMIT)
        up = jnp.clip(up, -SWIGLU_LIMIT, SWIGLU_LIMIT)
        glu = gate * jax.nn.sigmoid(SWIGLU_ALPHA * gate)
        return ((up + 1.0) * glu) @ w_down[e] + b_down[e]

    y_rows = lax.map(expert_block, (rows, block_expert)).reshape(n_rows, D)
    y = y_rows[dest].reshape(T, TOP_K, D)
    out = jnp.einsum('tk,tkd->td', gates.astype(y.dtype), y)
    return out.reshape(B, S, D)


def setup_inputs(seed: int = 0) -> dict:
    key = jax.random.key(seed)
    ks = jax.random.split(key, 26)
    L = DEPTH

    def nrm(k, shape, scale):
        return scale * jax.random.normal(k, shape, jnp.float32)

    def gain(k, shape):
        return 1.0 + 0.02 * jax.random.normal(k, shape, jnp.float32)

    dt0 = jnp.exp(jax.random.uniform(ks[12], (L, SSM_HEADS), jnp.float32,
                                     minval=math.log(1e-3), maxval=math.log(1e-1)))
    return {
        "x": nrm(ks[0], (BATCH, SEQ, D_MODEL), 1.0),
        "mix_norm_g": gain(ks[1], (L, D_MODEL)),
        "w_in": nrm(ks[2], (L, D_MODEL, IN_COLS), D_MODEL ** -0.5),
        "q_norm_g": gain(ks[3], (L, ATT_HEAD_DIM)),
        "k_norm_g": gain(ks[4], (L, ATT_HEAD_DIM)),
        "lambda_q1": nrm(ks[5], (L, ATT_HEAD_DIM), 0.1),
        "lambda_k1": nrm(ks[6], (L, ATT_HEAD_DIM), 0.1),
        "lambda_q2": nrm(ks[7], (L, ATT_HEAD_DIM), 0.1),
        "lambda_k2": nrm(ks[8], (L, ATT_HEAD_DIM), 0.1),
        "attn_subln_g": gain(ks[9], (L, ATT_V_DIM)),
        "conv_w": nrm(ks[10], (L, SSM_CONV, SSM_XBC), SSM_CONV ** -0.5),
        "conv_b": nrm(ks[11], (L, SSM_XBC), 0.01),
        "dt_bias": dt0 + jnp.log(-jnp.expm1(-dt0)),
        "a_log": jnp.log(jax.random.uniform(ks[13], (L, SSM_HEADS), jnp.float32, minval=1.0, maxval=16.0)),
        "d_skip": gain(ks[14], (L, SSM_HEADS)),
        "ssm_norm_g": gain(ks[15], (L, SSM_D_INNER)),
        "w_attn_proj": nrm(ks[16], (L, ATT_WIDTH, D_MODEL), ATT_WIDTH ** -0.5),
        "w_ssm_proj": nrm(ks[17], (L, SSM_D_INNER, D_MODEL), SSM_D_INNER ** -0.5),
        "w_out": nrm(ks[18], (L, D_MODEL, D_MODEL), D_MODEL ** -0.5),
        "ffn_norm_g": gain(ks[19], (L, D_MODEL)),
        "w_router": nrm(ks[20], (L, D_MODEL, N_EXPERTS), D_MODEL ** -0.5),
        "b_router": nrm(ks[21], (L, N_EXPERTS), 0.01),
        "w_gate_up": nrm(ks[22], (L, N_EXPERTS, D_MODEL, 2 * D_FF_EXPERT), D_MODEL ** -0.5),
        "b_gate_up": nrm(ks[23], (L, N_EXPERTS, 2 * D_FF_EXPERT), 0.01),
        "w_down": nrm(ks[24], (L, N_EXPERTS, D_FF_EXPERT, D_MODEL), D_FF_EXPERT ** -0.5),
        "b_down": nrm(ks[25], (L, N_EXPERTS, D_MODEL), 0.01),
    }


def reference(x, mix_norm_g, w_in, q_norm_g, k_norm_g, lambda_q1, lambda_k1, lambda_q2,
              lambda_k2, attn_subln_g, conv_w, conv_b, dt_bias, a_log, d_skip, ssm_norm_g,
              w_attn_proj, w_ssm_proj, w_out, ffn_norm_g, w_router, b_router, w_gate_up,
              b_gate_up, w_down, b_down):
    B, S, _ = x.shape
    cos, sin = rope_tables(S, ATT_HEAD_DIM)
    splits = [ATT_QK_COLS, 2 * ATT_QK_COLS, 2 * ATT_QK_COLS + ATT_WIDTH,
              2 * ATT_QK_COLS + ATT_WIDTH + SSM_D_INNER,
              2 * ATT_QK_COLS + ATT_WIDTH + SSM_D_INNER + SSM_XBC,
              2 * ATT_QK_COLS + ATT_WIDTH + SSM_D_INNER + SSM_XBC + SSM_HEADS,
              2 * ATT_QK_COLS + ATT_WIDTH + SSM_D_INNER + SSM_XBC + SSM_HEADS + D_MODEL]
    for layer in range(DEPTH):
        h = rms_norm(x, mix_norm_g[layer])
        proj = h @ w_in[layer]
        q, k, v, z, xbc, dt_raw, g_att, g_ssm = jnp.split(proj, splits, axis=-1)

        lam_init = 0.8 - 0.6 * math.exp(-0.3 * layer)
        lam = (jnp.exp(jnp.sum(lambda_q1[layer] * lambda_k1[layer]).astype(jnp.float32))
               - jnp.exp(jnp.sum(lambda_q2[layer] * lambda_k2[layer]).astype(jnp.float32))
               + lam_init)
        q = q.reshape(B, S, ATT_HEADS, 2, ATT_HEAD_DIM)
        k = k.reshape(B, S, ATT_HEADS, 2, ATT_HEAD_DIM)
        v = v.reshape(B, S, ATT_HEADS, ATT_V_DIM)
        q = apply_rope(rms_norm(q, q_norm_g[layer]), cos, sin)
        k = apply_rope(rms_norm(k, k_norm_g[layer]), cos, sin)
        att = diff_attention(q, k, v, lam)
        att = rms_norm(att, attn_subln_g[layer]) * (1.0 - lam_init)
        att = att.reshape(B, S, ATT_WIDTH)

        ssm = ssd_mixer(xbc, z, dt_raw, conv_w[layer], conv_b[layer], dt_bias[layer],
                        a_log[layer], d_skip[layer], ssm_norm_g[layer])

        merged = (jax.nn.sigmoid(g_att) * (att @ w_attn_proj[layer])
                  + jax.nn.sigmoid(g_ssm) * (ssm @ w_ssm_proj[layer]))
        x = x + merged @ w_out[layer]

        h2 = rms_norm(x, ffn_norm_g[layer])
        x = x + moe_ffn(h2, w_router[layer], b_router[layer], w_gate_up[layer],
                        b_gate_up[layer], w_down[layer], b_down[layer])
    return x
```

```python
import functools
import math

import jax
import jax.numpy as jnp
from jax import lax
from jax.experimental import pallas as pl
from jax.experimental.pallas import tpu as pltpu

F32 = jnp.float32
BF16 = jnp.bfloat16

D_MODEL = 1024
EPS = 1e-6
ATT_HEADS = 8
ATT_HEAD_DIM = 64
ATT_V_DIM = 2 * ATT_HEAD_DIM
ATT_WIDTH = ATT_HEADS * ATT_V_DIM
ROPE_THETA = 10000.0
SSM_D_INNER = 2 * D_MODEL
SSM_HEAD_DIM = 64
SSM_HEADS = SSM_D_INNER // SSM_HEAD_DIM
SSM_GROUPS = 8
SSM_STATE = 128
SSM_CONV = 4
SSM_CHUNK = 128
SSM_XBC = SSM_D_INNER + 2 * SSM_GROUPS * SSM_STATE
N_EXPERTS = 32
TOP_K = 4
D_FF = D_MODEL
SWIGLU_LIMIT = 7.0
SWIGLU_ALPHA = 1.702
LAM_INIT = 0.8 - 0.6 * math.exp(-0.3 * 0)

LANES = 128
HALO = 8

COL_XBC = 0
COL_Z = SSM_XBC
COL_Q = COL_Z + SSM_D_INNER
COL_K = COL_Q + ATT_WIDTH
COL_V = COL_K + ATT_WIDTH
COL_GA = COL_V + ATT_WIDTH
COL_GS = COL_GA + D_MODEL
PROJ_COLS = COL_GS + D_MODEL

VMEM_LIMIT = 56 * 1024 * 1024


def _cparams(sem):
    return pltpu.CompilerParams(dimension_semantics=sem, vmem_limit_bytes=VMEM_LIMIT)


def _in_proj_kernel(x_ref, g_ref, w_ref, wdt_ref, o_ref, dt_ref, h_scr):
    @pl.when(pl.program_id(1) == 0)
    def _():
        x = x_ref[...]
        ms = jnp.mean(x * x, axis=-1, keepdims=True)
        hb = (x * lax.rsqrt(ms + EPS) * g_ref[...]).astype(BF16)
        h_scr[...] = hb
        dt_ref[...] = jnp.dot(hb, wdt_ref[...], preferred_element_type=F32)

    o_ref[...] = jnp.dot(h_scr[...], w_ref[...], preferred_element_type=F32).astype(BF16)


def _in_proj(xf, g, w_main, w_dt):
    T = xf.shape[0]
    tm = min(1024, T)
    tn = 1024
    return pl.pallas_call(
        _in_proj_kernel,
        grid=(T // tm, PROJ_COLS // tn),
        in_specs=[
            pl.BlockSpec((tm, D_MODEL), lambda i, j: (i, 0)),
            pl.BlockSpec((1, D_MODEL), lambda i, j: (0, 0)),
            pl.BlockSpec((D_MODEL, tn), lambda i, j: (0, j)),
            pl.BlockSpec((D_MODEL, LANES), lambda i, j: (0, 0)),
        ],
        out_specs=[
            pl.BlockSpec((tm, tn), lambda i, j: (i, j)),
            pl.BlockSpec((tm, LANES), lambda i, j: (i, 0)),
        ],
        out_shape=[
            jax.ShapeDtypeStruct((T, PROJ_COLS), BF16),
            jax.ShapeDtypeStruct((T, LANES), F32),
        ],
        scratch_shapes=[pltpu.VMEM((tm, D_MODEL), BF16)],
        compiler_params=_cparams(("parallel", "arbitrary")),
        name="in_proj",
    )(xf, g, w_main, w_dt)


def _attn_kernel(q_ref, k_ref, v_ref, cos_ref, sin_ref, qg_ref, kg_ref,
                 lq1_ref, lk1_ref, lq2_ref, lk2_ref, sg_ref, o_ref, k_scr, *, tq, seq):
    qi = pl.program_id(2)
    lane = lax.broadcasted_iota(jnp.int32, (1, LANES), 1)
    first = lane < ATT_HEAD_DIM
    low_half = (lane % ATT_HEAD_DIM) < (ATT_HEAD_DIM // 2)

    def norm_rope(x, g, cos, sin):
        s = x * x
        t1 = jnp.sum(jnp.where(first, s, 0.0), axis=-1, keepdims=True)
        t2 = jnp.sum(jnp.where(first, 0.0, s), axis=-1, keepdims=True)
        ms = jnp.where(first, t1, t2) * (1.0 / ATT_HEAD_DIM)
        y = x * lax.rsqrt(ms + EPS) * g
        yr = jnp.where(low_half, pltpu.roll(y, LANES - ATT_HEAD_DIM // 2, 1),
                       pltpu.roll(y, ATT_HEAD_DIM // 2, 1))
        return y * cos + yr * sin

    @pl.when(qi == 0)
    def _():
        for r in range(0, seq, tq):
            kf = k_ref[r:r + tq, :].astype(F32)
            k_scr[r:r + tq, :] = norm_rope(kf, kg_ref[...], cos_ref[r:r + tq, :],
                                           sin_ref[r:r + tq, :]).astype(BF16)

    row0 = pl.multiple_of(qi * tq, tq)
    qf = norm_rope(q_ref[...].astype(F32), qg_ref[...], cos_ref[pl.ds(row0, tq), :],
                   sin_ref[pl.ds(row0, tq), :]) * (ATT_HEAD_DIM ** -0.5)
    q1 = jnp.where(first, qf, 0.0).astype(BF16)
    q2 = jnp.where(first, 0.0, qf).astype(BF16)

    def scores(qc, kt):
        return lax.dot_general(qc, kt, (((1,), (1,)), ((), ())), preferred_element_type=F32)

    def update(s, vt, m, l, acc):
        m_new = jnp.maximum(m, jnp.max(s, axis=-1, keepdims=True))
        alpha = jnp.exp(m - m_new)
        p = jnp.exp(s - m_new)
        l_new = alpha * l + jnp.sum(p, axis=-1, keepdims=True)
        acc_new = alpha * acc + jnp.dot(p.astype(BF16), vt, preferred_element_type=F32)
        return m_new, l_new, acc_new

    def step(j, carry, masked):
        m1, l1, a1, m2, l2, a2 = carry
        c0 = pl.multiple_of(j * tq, tq)
        kt = k_scr[pl.ds(c0, tq), :]
        vt = v_ref[pl.ds(c0, tq), :]
        s1 = scores(q1, kt)
        s2 = scores(q2, kt)
        if masked:
            keep = (lax.broadcasted_iota(jnp.int32, (tq, tq), 1)
                    <= lax.broadcasted_iota(jnp.int32, (tq, tq), 0))
            s1 = jnp.where(keep, s1, -jnp.inf)
            s2 = jnp.where(keep, s2, -jnp.inf)
        m1, l1, a1 = update(s1, vt, m1, l1, a1)
        m2, l2, a2 = update(s2, vt, m2, l2, a2)
        return m1, l1, a1, m2, l2, a2

    neg = jnp.full((tq, 1), -jnp.inf, F32)
    zero1 = jnp.zeros((tq, 1), F32)
    zacc = jnp.zeros((tq, ATT_V_DIM), F32)
    carry = step(qi, (neg, zero1, zacc, neg, zero1, zacc), True)
    carry = lax.fori_loop(0, qi, lambda j, c: step(j, c, False), carry)
    m1, l1, a1, m2, l2, a2 = carry

    lam = (jnp.exp(jnp.sum(lq1_ref[...] * lk1_ref[...], axis=-1, keepdims=True))
           - jnp.exp(jnp.sum(lq2_ref[...] * lk2_ref[...], axis=-1, keepdims=True)) + LAM_INIT)
    o = a1 / l1 - lam * (a2 / l2)
    ms = jnp.mean(o * o, axis=-1, keepdims=True)
    o_ref[...] = (o * lax.rsqrt(ms + EPS) * sg_ref[...] * (1.0 - LAM_INIT)).astype(BF16)


def _attention(proj, cos2, sin2, qg2, kg2, lq1, lk1, lq2, lk2, sg, batch, seq):
    T = batch * seq
    tq = min(512, seq)
    nq = seq // tq
    vec = lambda n: pl.BlockSpec((1, n), lambda b, h, i: (0, 0))
    return pl.pallas_call(
        functools.partial(_attn_kernel, tq=tq, seq=seq),
        grid=(batch, ATT_HEADS, nq),
        in_specs=[
            pl.BlockSpec((tq, LANES), lambda b, h, i: (b * nq + i, COL_Q // LANES + h)),
            pl.BlockSpec((seq, LANES), lambda b, h, i: (b, COL_K // LANES + h)),
            pl.BlockSpec((seq, LANES), lambda b, h, i: (b, COL_V // LANES + h)),
            pl.BlockSpec((seq, LANES), lambda b, h, i: (0, 0)),
            pl.BlockSpec((seq, LANES), lambda b, h, i: (0, 0)),
            vec(LANES), vec(LANES),
            vec(ATT_HEAD_DIM), vec(ATT_HEAD_DIM), vec(ATT_HEAD_DIM), vec(ATT_HEAD_DIM),
            vec(LANES),
        ],
        out_specs=pl.BlockSpec((tq, LANES), lambda b, h, i: (b * nq + i, h)),
        out_shape=jax.ShapeDtypeStruct((T, ATT_WIDTH), BF16),
        scratch_shapes=[pltpu.VMEM((seq, LANES), BF16)],
        compiler_params=_cparams(("parallel", "parallel", "arbitrary")),
        name="attention",
    )(proj, proj, proj, cos2, sin2, qg2, kg2, lq1, lk1, lq2, lk2, sg)


def _ssd_kernel(xbc_ref, z_ref, dt_ref, cw_ref, cb_ref, dtb_ref, alog_ref, dsk_ref, ng_ref,
                o_ref, halo, state, xs_scr, bc_scr, y_scr):
    Q = SSM_CHUNK
    N = SSM_STATE
    P2 = 2 * SSM_HEAD_DIM

    @pl.when(pl.program_id(1) == 0)
    def _():
        halo[...] = jnp.zeros_like(halo)
        state[...] = jnp.zeros_like(state)

    cwid = 512
    for c0 in range(0, SSM_XBC, cwid):
        cur = xbc_ref[:, c0:c0 + cwid].astype(F32)
        ext = jnp.concatenate([halo[:, c0:c0 + cwid], cur], axis=0)
        acc = cb_ref[:, c0:c0 + cwid] + cw_ref[SSM_CONV - 1:SSM_CONV, c0:c0 + cwid] * cur
        for k in range(SSM_CONV - 1):
            off = HALO - (SSM_CONV - 1) + k
            acc = acc + cw_ref[k:k + 1, c0:c0 + cwid] * ext[off:off + Q, :]
        act = acc * jax.nn.sigmoid(acc)
        if c0 < SSM_D_INNER:
            xs_scr[:, c0:c0 + cwid] = act
        else:
            bc_scr[:, c0 - SSM_D_INNER:c0 - SSM_D_INNER + cwid] = act.astype(BF16)
        halo[:, c0:c0 + cwid] = cur[Q - HALO:Q, :]

    dt = jax.nn.softplus(dt_ref[...] + dtb_ref[...])
    da = dt * (-jnp.exp(alog_ref[...]))
    ri = lax.broadcasted_iota(jnp.int32, (Q, Q), 0)
    ci = lax.broadcasted_iota(jnp.int32, (Q, Q), 1)
    tri = ri >= ci
    acs = jnp.dot(tri.astype(F32), da, preferred_element_type=F32,
                  precision=lax.Precision.HIGHEST)
    acs_t = acs.T
    dt_t = dt.T
    w_t = dt_t * jnp.exp(acs_t[:, Q - 1:Q] - acs_t)
    lane = lax.broadcasted_iota(jnp.int32, (1, P2), 1)
    left = lane < SSM_HEAD_DIM

    for g in range(SSM_GROUPS):
        bm = bc_scr[:, g * N:(g + 1) * N]
        cm = bc_scr[:, SSM_GROUPS * N + g * N:SSM_GROUPS * N + (g + 1) * N]
        cb = lax.dot_general(cm, bm, (((1,), (1,)), ((), ())), preferred_element_type=F32)
        bm_t = bm.astype(F32).T
        cm_f = cm.astype(F32)
        for pr in range(2):
            pair = 2 * g + pr
            xs_pair = xs_scr[:, pair * P2:(pair + 1) * P2].astype(BF16)
            prev = state[pair]
            rhs = jnp.concatenate([xs_pair, prev.astype(BF16)], axis=0)
            ys, sts, decs = [], [], []
            for r in range(2):
                h = 2 * pair + r
                a_col = acs[:, h:h + 1]
                seg = a_col - acs_t[h:h + 1, :]
                decay = jnp.exp(jnp.where(tri, seg, -jnp.inf))
                m_h = cb * decay * dt_t[h:h + 1, :]
                e_h = cm_f * jnp.exp(a_col)
                lhs = jnp.concatenate([m_h, e_h], axis=1).astype(BF16)
                ys.append(jnp.dot(lhs, rhs, preferred_element_type=F32))
                sts.append(jnp.dot((bm_t * w_t[h:h + 1, :]).astype(BF16), xs_pair,
                                   preferred_element_type=F32))
                decs.append(jnp.exp(acs[Q - 1:Q, h:h + 1]))
            y_scr[:, pair * P2:(pair + 1) * P2] = jnp.where(left, ys[0], ys[1])
            dec = jnp.where(left, decs[0], decs[1])
            state[pair] = dec * prev + jnp.where(left, sts[0], sts[1])

    y = y_scr[...] + dsk_ref[...] * xs_scr[...]
    zf = z_ref[...].astype(F32)
    y = y * (zf * jax.nn.sigmoid(zf))
    gw = SSM_D_INNER // SSM_GROUPS
    for g in range(SSM_GROUPS):
        yg = y[:, g * gw:(g + 1) * gw]
        ms = jnp.mean(yg * yg, axis=-1, keepdims=True)
        o_ref[:, g * gw:(g + 1) * gw] = (yg * lax.rsqrt(ms + EPS)
                                         * ng_ref[:, g * gw:(g + 1) * gw]).astype(BF16)


def _ssd(proj, dt_raw, conv_w, conv_b, dt_bias, a_log, dsk, ng, batch, seq):
    T = batch * seq
    Q = SSM_CHUNK
    nc = seq // Q
    vec = lambda r, n: pl.BlockSpec((r, n), lambda b, c: (0, 0))
    return pl.pallas_call(
        _ssd_kernel,
        grid=(batch, nc),
        in_specs=[
            pl.BlockSpec((Q, SSM_XBC), lambda b, c: (b * nc + c, COL_XBC // SSM_XBC)),
            pl.BlockSpec((Q, SSM_D_INNER), lambda b, c: (b * nc + c, COL_Z // SSM_D_INNER)),
            pl.BlockSpec((Q, LANES), lambda b, c: (b * nc + c, 0)),
            vec(SSM_CONV, SSM_XBC), vec(1, SSM_XBC), vec(1, LANES), vec(1, LANES),
            vec(1, SSM_D_INNER), vec(1, SSM_D_INNER),
        ],
        out_specs=pl.BlockSpec((Q, SSM_D_INNER), lambda b, c: (b * nc + c, 0)),
        out_shape=jax.ShapeDtypeStruct((T, SSM_D_INNER), BF16),
        scratch_shapes=[
            pltpu.VMEM((HALO, SSM_XBC), F32),
            pltpu.VMEM((SSM_HEADS // 2, SSM_STATE, 2 * SSM_HEAD_DIM), F32),
            pltpu.VMEM((Q, SSM_D_INNER), F32),
            pltpu.VMEM((Q, 2 * SSM_GROUPS * SSM_STATE), BF16),
            pltpu.VMEM((Q, SSM_D_INNER), F32),
        ],
        compiler_params=_cparams(("parallel", "arbitrary")),
        name="ssd",
    )(proj, proj, dt_raw, conv_w, conv_b, dt_bias, a_log, dsk, ng)


def _merge_kernel(att_ref, ssm_ref, ga_ref, gs_ref, x_ref, wap_ref, wsp_ref, wo_ref, fg_ref,
                  wr_ref, br_ref, x1_ref, h2_ref, idx_ref, gate_ref, cnt_ref, base, *, tm):
    i = pl.program_id(0)

    @pl.when(i == 0)
    def _():
        base[...] = jnp.zeros_like(base)

    pa = jnp.dot(att_ref[...], wap_ref[...], preferred_element_type=F32)
    ps = jnp.dot(ssm_ref[...], wsp_ref[...], preferred_element_type=F32)
    merged = (jax.nn.sigmoid(ga_ref[...].astype(F32)) * pa
              + jax.nn.sigmoid(gs_ref[...].astype(F32)) * ps)
    x1 = x_ref[...] + jnp.dot(merged.astype(BF16), wo_ref[...], preferred_element_type=F32)
    x1_ref[...] = x1
    ms = jnp.mean(x1 * x1, axis=-1, keepdims=True)
    h2 = x1 * lax.rsqrt(ms + EPS) * fg_ref[...]
    h2_ref[...] = h2.astype(BF16)

    lane = lax.broadcasted_iota(jnp.int32, (1, LANES), 1)
    logits = jnp.dot(h2, wr_ref[...], preferred_element_type=F32,
                     precision=lax.Precision.HIGHEST) + br_ref[...]
    lg = jnp.where(lane < N_EXPERTS, logits, -jnp.inf)
    lane_f = lane.astype(F32)
    vals, idxs, sels = [], [], []
    for _ in range(TOP_K):
        m = jnp.max(lg, axis=-1, keepdims=True)
        idx = jnp.min(jnp.where(lg == m, lane_f, float(LANES)), axis=-1, keepdims=True)
        sel = lane_f == idx
        vals.append(m)
        idxs.append(idx)
        sels.append(sel)
        lg = jnp.where(sel, -jnp.inf, lg)
    es = [jnp.exp(v - vals[0]) for v in vals]
    den = es[0] + es[1] + es[2] + es[3]

    multi = jnp.zeros((tm, LANES), F32)
    for sel in sels:
        multi = jnp.where(sel, 1.0, multi)
    ri = lax.broadcasted_iota(jnp.int32, (tm, tm), 0)
    ci = lax.broadcasted_iota(jnp.int32, (tm, tm), 1)
    before = jnp.dot((ri > ci).astype(BF16), multi.astype(BF16), preferred_element_type=F32)
    before = before + base[...]
    idx_out = jnp.zeros((tm, LANES), F32)
    gate_out = jnp.zeros((tm, LANES), F32)
    for k in range(TOP_K):
        rank = jnp.sum(jnp.where(sels[k], before, 0.0), axis=-1, keepdims=True)
        idx_out = jnp.where(lane == k, idxs[k], idx_out)
        idx_out = jnp.where(lane == TOP_K + k, rank, idx_out)
        gate_out = jnp.where(lane == k, es[k] / den, gate_out)
    idx_ref[...] = idx_out.astype(jnp.int32)
    gate_ref[...] = gate_out
    base[...] = base[...] + jnp.sum(multi, axis=0, keepdims=True)
    cnt_ref[...] = jnp.broadcast_to(base[...], cnt_ref.shape).astype(jnp.int32)


def _merge(att, ssm, proj, xf, wap, wsp, wo, fg, wr, br):
    T = xf.shape[0]
    tm = min(512, T)
    full = lambda a: pl.BlockSpec(a.shape, lambda i: (0, 0))
    return pl.pallas_call(
        functools.partial(_merge_kernel, tm=tm),
        grid=(T // tm,),
        in_specs=[
            pl.BlockSpec((tm, ATT_WIDTH), lambda i: (i, 0)),
            pl.BlockSpec((tm, SSM_D_INNER), lambda i: (i, 0)),
            pl.BlockSpec((tm, D_MODEL), lambda i: (i, COL_GA // D_MODEL)),
            pl.BlockSpec((tm, D_MODEL), lambda i: (i, COL_GS // D_MODEL)),
            pl.BlockSpec((tm, D_MODEL), lambda i: (i, 0)),
            full(wap), full(wsp), full(wo), full(fg), full(wr), full(br),
        ],
        out_specs=[
            pl.BlockSpec((tm, D_MODEL), lambda i: (i, 0)),
            pl.BlockSpec((tm, D_MODEL), lambda i: (i, 0)),
            pl.BlockSpec((tm, LANES), lambda i: (i, 0)),
            pl.BlockSpec((tm, LANES), lambda i: (i, 0)),
            pl.BlockSpec((HALO, LANES), lambda i: (0, 0)),
        ],
        out_shape=[
            jax.ShapeDtypeStruct((T, D_MODEL), F32),
            jax.ShapeDtypeStruct((T, D_MODEL), BF16),
            jax.ShapeDtypeStruct((T, LANES), jnp.int32),
            jax.ShapeDtypeStruct((T, LANES), F32),
            jax.ShapeDtypeStruct((HALO, LANES), jnp.int32),
        ],
        scratch_shapes=[pltpu.VMEM((1, LANES), F32)],
        compiler_params=_cparams(("arbitrary",)),
        name="merge_router",
    )(att, ssm, proj, proj, xf, wap, wsp, wo, fg, wr, br)


def _expert_kernel(be_ref, nu_ref, x_ref, wgu_ref, bgu_ref, wd_ref, bd_ref, o_ref):
    @pl.when(pl.program_id(0) < nu_ref[0])
    def _():
        gu = jnp.dot(x_ref[...], wgu_ref[0], preferred_element_type=F32) + bgu_ref[0]
        gate = jnp.minimum(gu[:, :D_FF], SWIGLU_LIMIT)
        up = jnp.clip(gu[:, D_FF:], -SWIGLU_LIMIT, SWIGLU_LIMIT)
        glu = gate * jax.nn.sigmoid(SWIGLU_ALPHA * gate)
        act = ((up + 1.0) * glu).astype(BF16)
        o_ref[...] = (jnp.dot(act, wd_ref[0], preferred_element_type=F32) + bd_ref[0]).astype(BF16)


def _experts(block_expert, n_used, rows, wgu, bgu, wd, bd, bm):
    n_rows = rows.shape[0]
    nb = n_rows // bm
    row_map = lambda i, be, nu: (jnp.minimum(i, nu[0] - 1), 0)
    exp_map = lambda i, be, nu: (be[jnp.minimum(i, nu[0] - 1)], 0, 0)
    grid_spec = pltpu.PrefetchScalarGridSpec(
        num_scalar_prefetch=2,
        grid=(nb,),
        in_specs=[
            pl.BlockSpec((bm, D_MODEL), row_map),
            pl.BlockSpec((1, D_MODEL, 2 * D_FF), exp_map),
            pl.BlockSpec((1, 1, 2 * D_FF), exp_map),
            pl.BlockSpec((1, D_FF, D_MODEL), exp_map),
            pl.BlockSpec((1, 1, D_MODEL), exp_map),
        ],
        out_specs=pl.BlockSpec((bm, D_MODEL), row_map),
    )
    return pl.pallas_call(
        _expert_kernel,
        grid_spec=grid_spec,
        out_shape=jax.ShapeDtypeStruct((n_rows, D_MODEL), BF16),
        compiler_params=_cparams(("arbitrary",)),
        name="experts",
    )(block_expert, n_used, rows, wgu, bgu, wd, bd)


def _combine_kernel(x1_ref, y_ref, gate_ref, o_ref):
    acc = x1_ref[...]
    g = gate_ref[...]
    for k in range(TOP_K):
        acc = acc + g[:, k:k + 1] * y_ref[:, k * D_MODEL:(k + 1) * D_MODEL].astype(F32)
    o_ref[...] = acc


def _combine(x1, yk, gates):
    T = x1.shape[0]
    tm = min(512, T)
    return pl.pallas_call(
        _combine_kernel,
        grid=(T // tm,),
        in_specs=[
            pl.BlockSpec((tm, D_MODEL), lambda i: (i, 0)),
            pl.BlockSpec((tm, TOP_K * D_MODEL), lambda i: (i, 0)),
            pl.BlockSpec((tm, LANES), lambda i: (i, 0)),
        ],
        out_specs=pl.BlockSpec((tm, D_MODEL), lambda i: (i, 0)),
        out_shape=jax.ShapeDtypeStruct((T, D_MODEL), F32),
        compiler_params=_cparams(("parallel",)),
        name="combine",
    )(x1, yk, gates)


def _pad_lanes(v):
    return jnp.pad(v, ((0, 0), (0, LANES - v.shape[-1])))


def kernel(x, mix_norm_g, w_in, q_norm_g, k_norm_g, lambda_q1, lambda_k1, lambda_q2, lambda_k2,
           attn_subln_g, conv_w, conv_b, dt_bias, a_log, d_skip, ssm_norm_g, w_attn_proj,
           w_ssm_proj, w_out, ffn_norm_g, w_router, b_router, w_gate_up, b_gate_up, w_down, b_down):
    B, S, D = x.shape
    T = B * S
    xf = x.reshape(T, D)
    layer = 0

    wi = w_in[layer]
    o_q, o_k, o_v = 0, ATT_WIDTH, 2 * ATT_WIDTH
    o_z = 3 * ATT_WIDTH
    o_xbc = o_z + SSM_D_INNER
    o_dt = o_xbc + SSM_XBC
    o_ga = o_dt + SSM_HEADS
    w_main = jnp.concatenate([wi[:, o_xbc:o_dt], wi[:, o_z:o_xbc], wi[:, o_q:o_z], wi[:, o_ga:]],
                             axis=1).astype(BF16)
    w_dt = _pad_lanes(wi[:, o_dt:o_ga]).astype(BF16)

    proj, dt_raw = _in_proj(xf, mix_norm_g[layer][None, :], w_main, w_dt)

    half = ATT_HEAD_DIM // 2
    inv = ROPE_THETA ** (-jnp.arange(0, ATT_HEAD_DIM, 2, dtype=F32) / ATT_HEAD_DIM)
    ang = jnp.arange(S, dtype=F32)[:, None] * inv[None, :]
    cos2 = jnp.tile(jnp.cos(ang), (1, LANES // half))
    sin2 = jnp.tile(jnp.concatenate([-jnp.sin(ang), jnp.sin(ang)], axis=1), (1, LANES // ATT_HEAD_DIM))
    att = _attention(proj, cos2, sin2,
                     jnp.tile(q_norm_g[layer], 2)[None, :], jnp.tile(k_norm_g[layer], 2)[None, :],
                     lambda_q1[layer][None, :], lambda_k1[layer][None, :],
                     lambda_q2[layer][None, :], lambda_k2[layer][None, :],
                     attn_subln_g[layer][None, :], B, S)

    ssm = _ssd(proj, dt_raw, conv_w[layer], conv_b[layer][None, :],
               _pad_lanes(dt_bias[layer][None, :]), _pad_lanes(a_log[layer][None, :]),
               jnp.repeat(d_skip[layer], SSM_HEAD_DIM)[None, :], ssm_norm_g[layer][None, :], B, S)

    x1, h2, idx_rank, gates, counts = _merge(
        att, ssm, proj, xf, w_attn_proj[layer].astype(BF16), w_ssm_proj[layer].astype(BF16),
        w_out[layer].astype(BF16), ffn_norm_g[layer][None, :],
        _pad_lanes(w_router[layer]), _pad_lanes(b_router[layer][None, :]))

    bm = 512
    A = T * TOP_K
    n_rows = (A + N_EXPERTS * (bm - 1)) // bm * bm
    cnt = counts[0, :N_EXPERTS]
    padded = (cnt + bm - 1) // bm * bm
    pend = jnp.cumsum(padded)
    pstart = pend - padded
    n_used = (pend[-1:] // bm).astype(jnp.int32)
    block_start = jnp.arange(n_rows // bm, dtype=jnp.int32) * bm
    block_expert = jnp.minimum(jnp.searchsorted(pend, block_start, side='right'),
                               N_EXPERTS - 1).astype(jnp.int32)
    top_idx = idx_rank[:, :TOP_K]
    dest = (pstart[top_idx] + idx_rank[:, TOP_K:2 * TOP_K]).reshape(A)
    token_of = jnp.arange(A, dtype=jnp.int32) // TOP_K
    row_token = jnp.full((n_rows,), T, jnp.int32).at[dest].set(token_of)
    h_pad = jnp.concatenate([h2, jnp.zeros((1, D), h2.dtype)], axis=0)
    rows = h_pad[row_token]

    y_rows = _experts(block_expert, n_used, rows, w_gate_up[layer].astype(BF16),
                      b_gate_up[layer][:, None, :], w_down[layer].astype(BF16),
                      b_down[layer][:, None, :], bm)
    yk = y_rows[dest].reshape(T, TOP_K * D)
    out = _combine(x1, yk, gates)
    return out.reshape(B, S, D)
```

```python
import functools
import math

import jax
import jax.numpy as jnp
from jax import lax
from jax.experimental import pallas as pl
from jax.experimental.pallas import tpu as pltpu

F32 = jnp.float32
BF16 = jnp.bfloat16

D_MODEL = 1024
EPS = 1e-6
ATT_HEADS = 8
ATT_HEAD_DIM = 64
ATT_V_DIM = 2 * ATT_HEAD_DIM
ATT_WIDTH = ATT_HEADS * ATT_V_DIM
ROPE_THETA = 10000.0
SSM_D_INNER = 2 * D_MODEL
SSM_HEAD_DIM = 64
SSM_HEADS = SSM_D_INNER // SSM_HEAD_DIM
SSM_GROUPS = 8
SSM_STATE = 128
SSM_CONV = 4
SSM_CHUNK = 128
SSM_XBC = SSM_D_INNER + 2 * SSM_GROUPS * SSM_STATE
N_EXPERTS = 32
TOP_K = 4
D_FF = D_MODEL
SWIGLU_LIMIT = 7.0
SWIGLU_ALPHA = 1.702
LAM_INIT = 0.8 - 0.6 * math.exp(-0.3 * 0)

LANES = 128
HALO = 8

COL_XBC = 0
COL_Z = SSM_XBC
COL_Q = COL_Z + SSM_D_INNER
COL_K = COL_Q + ATT_WIDTH
COL_V = COL_K + ATT_WIDTH
COL_GA = COL_V + ATT_WIDTH
COL_GS = COL_GA + D_MODEL
PROJ_COLS = COL_GS + D_MODEL

VMEM_LIMIT = 56 * 1024 * 1024


def _cparams(sem):
    return pltpu.CompilerParams(dimension_semantics=sem, vmem_limit_bytes=VMEM_LIMIT)


PACKED = D_MODEL // 2


def _pack_rows(x):
    lo = lax.bitcast_convert_type(x[:, :PACKED].astype(BF16).astype(F32), jnp.uint32)
    hi = lax.bitcast_convert_type(x[:, PACKED:].astype(BF16).astype(F32), jnp.uint32)
    return hi | (lo >> 16)


def _unpack_rows(p):
    lo = lax.bitcast_convert_type(p << 16, F32)
    hi = lax.bitcast_convert_type(p & jnp.uint32(0xFFFF0000), F32)
    return lo, hi


def _in_proj_kernel(x_ref, g_ref, w_ref, wdt_ref, o_ref, dt_ref, h_scr):
    @pl.when(pl.program_id(1) == 0)
    def _():
        x = x_ref[...]
        ms = jnp.mean(x * x, axis=-1, keepdims=True)
        hb = (x * lax.rsqrt(ms + EPS) * g_ref[...]).astype(BF16)
        h_scr[...] = hb
        dt_ref[...] = jnp.dot(hb, wdt_ref[...], preferred_element_type=F32)

    o_ref[...] = jnp.dot(h_scr[...], w_ref[...], preferred_element_type=F32).astype(BF16)


def _in_proj(xf, g, w_main, w_dt):
    T = xf.shape[0]
    tm = min(1024, T)
    tn = 1024
    return pl.pallas_call(
        _in_proj_kernel,
        grid=(T // tm, PROJ_COLS // tn),
        in_specs=[
            pl.BlockSpec((tm, D_MODEL), lambda i, j: (i, 0)),
            pl.BlockSpec((1, D_MODEL), lambda i, j: (0, 0)),
            pl.BlockSpec((D_MODEL, tn), lambda i, j: (0, j)),
            pl.BlockSpec((D_MODEL, LANES), lambda i, j: (0, 0)),
        ],
        out_specs=[
            pl.BlockSpec((tm, tn), lambda i, j: (i, j)),
            pl.BlockSpec((tm, LANES), lambda i, j: (i, 0)),
        ],
        out_shape=[
            jax.ShapeDtypeStruct((T, PROJ_COLS), BF16),
            jax.ShapeDtypeStruct((T, LANES), F32),
        ],
        scratch_shapes=[pltpu.VMEM((tm, D_MODEL), BF16)],
        compiler_params=_cparams(("arbitrary", "arbitrary")),
        name="in_proj",
    )(xf, g, w_main, w_dt)


def _attn_kernel(q_ref, k_ref, v_ref, cos_ref, sin_ref, qg_ref, kg_ref,
                 lq1_ref, lk1_ref, lq2_ref, lk2_ref, sg_ref, o_ref, k_scr, *, tq, seq):
    qi = pl.program_id(2)
    lane = lax.broadcasted_iota(jnp.int32, (1, LANES), 1)
    first = lane < ATT_HEAD_DIM
    low_half = (lane % ATT_HEAD_DIM) < (ATT_HEAD_DIM // 2)

    def norm_rope(x, g, cos, sin):
        s = x * x
        t1 = jnp.sum(jnp.where(first, s, 0.0), axis=-1, keepdims=True)
        t2 = jnp.sum(jnp.where(first, 0.0, s), axis=-1, keepdims=True)
        ms = jnp.where(first, t1, t2) * (1.0 / ATT_HEAD_DIM)
        y = x * lax.rsqrt(ms + EPS) * g
        yr = jnp.where(low_half, pltpu.roll(y, LANES - ATT_HEAD_DIM // 2, 1),
                       pltpu.roll(y, ATT_HEAD_DIM // 2, 1))
        return y * cos + yr * sin

    @pl.when(qi == 0)
    def _():
        for r in range(0, seq, tq):
            kf = k_ref[r:r + tq, :].astype(F32)
            k_scr[r:r + tq, :] = norm_rope(kf, kg_ref[...], cos_ref[r:r + tq, :],
                                           sin_ref[r:r + tq, :]).astype(BF16)

    row0 = pl.multiple_of(qi * tq, tq)
    qf = norm_rope(q_ref[...].astype(F32), qg_ref[...], cos_ref[pl.ds(row0, tq), :],
                   sin_ref[pl.ds(row0, tq), :]) * (ATT_HEAD_DIM ** -0.5)
    q1 = jnp.where(first, qf, 0.0).astype(BF16)
    q2 = jnp.where(first, 0.0, qf).astype(BF16)

    def scores(qc, kt):
        return lax.dot_general(qc, kt, (((1,), (1,)), ((), ())), preferred_element_type=F32)

    def update(s, vt, m, l, acc):
        m_new = jnp.maximum(m, jnp.max(s, axis=-1, keepdims=True))
        alpha = jnp.exp(m - m_new)
        p = jnp.exp(s - m_new)
        l_new = alpha * l + jnp.sum(p, axis=-1, keepdims=True)
        acc_new = alpha * acc + jnp.dot(p.astype(BF16), vt, preferred_element_type=F32)
        return m_new, l_new, acc_new

    def step(j, carry, masked):
        m1, l1, a1, m2, l2, a2 = carry
        c0 = pl.multiple_of(j * tq, tq)
        kt = k_scr[pl.ds(c0, tq), :]
        vt = v_ref[pl.ds(c0, tq), :]
        s1 = scores(q1, kt)
        s2 = scores(q2, kt)
        if masked:
            keep = (lax.broadcasted_iota(jnp.int32, (tq, tq), 1)
                    <= lax.broadcasted_iota(jnp.int32, (tq, tq), 0))
            s1 = jnp.where(keep, s1, -jnp.inf)
            s2 = jnp.where(keep, s2, -jnp.inf)
        m1, l1, a1 = update(s1, vt, m1, l1, a1)
        m2, l2, a2 = update(s2, vt, m2, l2, a2)
        return m1, l1, a1, m2, l2, a2

    neg = jnp.full((tq, 1), -jnp.inf, F32)
    zero1 = jnp.zeros((tq, 1), F32)
    zacc = jnp.zeros((tq, ATT_V_DIM), F32)
    carry = step(qi, (neg, zero1, zacc, neg, zero1, zacc), True)
    carry = lax.fori_loop(0, qi, lambda j, c: step(j, c, False), carry)
    m1, l1, a1, m2, l2, a2 = carry

    lam = (jnp.exp(jnp.sum(lq1_ref[...] * lk1_ref[...], axis=-1, keepdims=True))
           - jnp.exp(jnp.sum(lq2_ref[...] * lk2_ref[...], axis=-1, keepdims=True)) + LAM_INIT)
    o = a1 / l1 - lam * (a2 / l2)
    ms = jnp.mean(o * o, axis=-1, keepdims=True)
    o_ref[...] = (o * lax.rsqrt(ms + EPS) * sg_ref[...] * (1.0 - LAM_INIT)).astype(BF16)


def _attention(proj, cos2, sin2, qg2, kg2, lq1, lk1, lq2, lk2, sg, batch, seq):
    T = batch * seq
    tq = min(512, seq)
    nq = seq // tq
    vec = lambda n: pl.BlockSpec((1, n), lambda b, h, i: (0, 0))
    return pl.pallas_call(
        functools.partial(_attn_kernel, tq=tq, seq=seq),
        grid=(batch, ATT_HEADS, nq),
        in_specs=[
            pl.BlockSpec((tq, LANES), lambda b, h, i: (b * nq + i, COL_Q // LANES + h)),
            pl.BlockSpec((seq, LANES), lambda b, h, i: (b, COL_K // LANES + h)),
            pl.BlockSpec((seq, LANES), lambda b, h, i: (b, COL_V // LANES + h)),
            pl.BlockSpec((seq, LANES), lambda b, h, i: (0, 0)),
            pl.BlockSpec((seq, LANES), lambda b, h, i: (0, 0)),
            vec(LANES), vec(LANES),
            vec(ATT_HEAD_DIM), vec(ATT_HEAD_DIM), vec(ATT_HEAD_DIM), vec(ATT_HEAD_DIM),
            vec(LANES),
        ],
        out_specs=pl.BlockSpec((tq, LANES), lambda b, h, i: (b * nq + i, h)),
        out_shape=jax.ShapeDtypeStruct((T, ATT_WIDTH), BF16),
        scratch_shapes=[pltpu.VMEM((seq, LANES), BF16)],
        compiler_params=_cparams(("arbitrary", "arbitrary", "arbitrary")),
        name="attention",
    )(proj, proj, proj, cos2, sin2, qg2, kg2, lq1, lk1, lq2, lk2, sg)


def _ssd_kernel(xbc_ref, z_ref, dt_ref, cw_ref, cb_ref, dtb_ref, alog_ref, dsk_ref, ng_ref,
                o_ref, halo, state, xs_scr, bc_scr, y_scr):
    Q = SSM_CHUNK
    N = SSM_STATE
    P2 = 2 * SSM_HEAD_DIM

    @pl.when(pl.program_id(1) == 0)
    def _():
        halo[...] = jnp.zeros_like(halo)
        state[...] = jnp.zeros_like(state)

    cwid = 512
    for c0 in range(0, SSM_XBC, cwid):
        cur = xbc_ref[:, c0:c0 + cwid].astype(F32)
        ext = jnp.concatenate([halo[:, c0:c0 + cwid], cur], axis=0)
        acc = cb_ref[:, c0:c0 + cwid] + cw_ref[SSM_CONV - 1:SSM_CONV, c0:c0 + cwid] * cur
        for k in range(SSM_CONV - 1):
            off = HALO - (SSM_CONV - 1) + k
            acc = acc + cw_ref[k:k + 1, c0:c0 + cwid] * ext[off:off + Q, :]
        act = acc * jax.nn.sigmoid(acc)
        if c0 < SSM_D_INNER:
            xs_scr[:, c0:c0 + cwid] = act
        else:
            bc_scr[:, c0 - SSM_D_INNER:c0 - SSM_D_INNER + cwid] = act.astype(BF16)
        halo[:, c0:c0 + cwid] = cur[Q - HALO:Q, :]

    dt = jax.nn.softplus(dt_ref[...] + dtb_ref[...])
    da = dt * (-jnp.exp(alog_ref[...]))
    ri = lax.broadcasted_iota(jnp.int32, (Q, Q), 0)
    ci = lax.broadcasted_iota(jnp.int32, (Q, Q), 1)
    tri = ri >= ci
    acs = jnp.dot(tri.astype(F32), da, preferred_element_type=F32,
                  precision=lax.Precision.HIGHEST)
    acs_t = acs.T
    dt_t = dt.T
    w_t = dt_t * jnp.exp(acs_t[:, Q - 1:Q] - acs_t)
    lane = lax.broadcasted_iota(jnp.int32, (1, P2), 1)
    left = lane < SSM_HEAD_DIM

    for g in range(SSM_GROUPS):
        bm = bc_scr[:, g * N:(g + 1) * N]
        cm = bc_scr[:, SSM_GROUPS * N + g * N:SSM_GROUPS * N + (g + 1) * N]
        cb = lax.dot_general(cm, bm, (((1,), (1,)), ((), ())), preferred_element_type=F32)
        bm_t = bm.astype(F32).T
        cm_f = cm.astype(F32)
        for pr in range(2):
            pair = 2 * g + pr
            xs_pair = xs_scr[:, pair * P2:(pair + 1) * P2].astype(BF16)
            prev = state[pair]
            rhs = jnp.concatenate([xs_pair, prev.astype(BF16)], axis=0)
            ys, sts, decs = [], [], []
            for r in range(2):
                h = 2 * pair + r
                a_col = acs[:, h:h + 1]
                seg = a_col - acs_t[h:h + 1, :]
                decay = jnp.exp(jnp.where(tri, seg, -jnp.inf))
                m_h = cb * decay * dt_t[h:h + 1, :]
                e_h = cm_f * jnp.exp(a_col)
                lhs = jnp.concatenate([m_h, e_h], axis=1).astype(BF16)
                ys.append(jnp.dot(lhs, rhs, preferred_element_type=F32))
                sts.append(jnp.dot((bm_t * w_t[h:h + 1, :]).astype(BF16), xs_pair,
                                   preferred_element_type=F32))
                decs.append(jnp.exp(acs[Q - 1:Q, h:h + 1]))
            y_scr[:, pair * P2:(pair + 1) * P2] = jnp.where(left, ys[0], ys[1])
            dec = jnp.where(left, decs[0], decs[1])
            state[pair] = dec * prev + jnp.where(left, sts[0], sts[1])

    y = y_scr[...] + dsk_ref[...] * xs_scr[...]
    zf = z_ref[...].astype(F32)
    y = y * (zf * jax.nn.sigmoid(zf))
    gw = SSM_D_INNER // SSM_GROUPS
    for g in range(SSM_GROUPS):
        yg = y[:, g * gw:(g + 1) * gw]
        ms = jnp.mean(yg * yg, axis=-1, keepdims=True)
        o_ref[:, g * gw:(g + 1) * gw] = (yg * lax.rsqrt(ms + EPS)
                                         * ng_ref[:, g * gw:(g + 1) * gw]).astype(BF16)


def _ssd(proj, dt_raw, conv_w, conv_b, dt_bias, a_log, dsk, ng, batch, seq):
    T = batch * seq
    Q = SSM_CHUNK
    nc = seq // Q
    vec = lambda r, n: pl.BlockSpec((r, n), lambda b, c: (0, 0))
    return pl.pallas_call(
        _ssd_kernel,
        grid=(batch, nc),
        in_specs=[
            pl.BlockSpec((Q, SSM_XBC), lambda b, c: (b * nc + c, COL_XBC // SSM_XBC)),
            pl.BlockSpec((Q, SSM_D_INNER), lambda b, c: (b * nc + c, COL_Z // SSM_D_INNER)),
            pl.BlockSpec((Q, LANES), lambda b, c: (b * nc + c, 0)),
            vec(SSM_CONV, SSM_XBC), vec(1, SSM_XBC), vec(1, LANES), vec(1, LANES),
            vec(1, SSM_D_INNER), vec(1, SSM_D_INNER),
        ],
        out_specs=pl.BlockSpec((Q, SSM_D_INNER), lambda b, c: (b * nc + c, 0)),
        out_shape=jax.ShapeDtypeStruct((T, SSM_D_INNER), BF16),
        scratch_shapes=[
            pltpu.VMEM((HALO, SSM_XBC), F32),
            pltpu.VMEM((SSM_HEADS // 2, SSM_STATE, 2 * SSM_HEAD_DIM), F32),
            pltpu.VMEM((Q, SSM_D_INNER), F32),
            pltpu.VMEM((Q, 2 * SSM_GROUPS * SSM_STATE), BF16),
            pltpu.VMEM((Q, SSM_D_INNER), F32),
        ],
        compiler_params=_cparams(("arbitrary", "arbitrary")),
        name="ssd",
    )(proj, proj, dt_raw, conv_w, conv_b, dt_bias, a_log, dsk, ng)


def _merge_kernel(att_ref, ssm_ref, ga_ref, gs_ref, x_ref, wap_ref, wsp_ref, wo_ref, fg_ref,
                  wr_ref, br_ref, x1_ref, h2_ref, idx_ref, gate_ref, cnt_ref, base, *, tm):
    i = pl.program_id(0)

    @pl.when(i == 0)
    def _():
        base[...] = jnp.zeros_like(base)

    pa = jnp.dot(att_ref[...], wap_ref[...], preferred_element_type=F32)
    ps = jnp.dot(ssm_ref[...], wsp_ref[...], preferred_element_type=F32)
    merged = (jax.nn.sigmoid(ga_ref[...].astype(F32)) * pa
              + jax.nn.sigmoid(gs_ref[...].astype(F32)) * ps)
    x1 = x_ref[...] + jnp.dot(merged.astype(BF16), wo_ref[...], preferred_element_type=F32)
    x1_ref[...] = x1
    ms = jnp.mean(x1 * x1, axis=-1, keepdims=True)
    h2 = x1 * lax.rsqrt(ms + EPS) * fg_ref[...]
    h2_ref[...] = _pack_rows(h2)

    lane = lax.broadcasted_iota(jnp.int32, (1, LANES), 1)
    logits = jnp.dot(h2, wr_ref[...], preferred_element_type=F32,
                     precision=lax.Precision.HIGHEST) + br_ref[...]
    lg = jnp.where(lane < N_EXPERTS, logits, -jnp.inf)
    lane_f = lane.astype(F32)
    vals, idxs, sels = [], [], []
    for _ in range(TOP_K):
        m = jnp.max(lg, axis=-1, keepdims=True)
        idx = jnp.min(jnp.where(lg == m, lane_f, float(LANES)), axis=-1, keepdims=True)
        sel = lane_f == idx
        vals.append(m)
        idxs.append(idx)
        sels.append(sel)
        lg = jnp.where(sel, -jnp.inf, lg)
    es = [jnp.exp(v - vals[0]) for v in vals]
    den = es[0] + es[1] + es[2] + es[3]

    multi = jnp.zeros((tm, LANES), F32)
    for sel in sels:
        multi = jnp.where(sel, 1.0, multi)
    ri = lax.broadcasted_iota(jnp.int32, (tm, tm), 0)
    ci = lax.broadcasted_iota(jnp.int32, (tm, tm), 1)
    before = jnp.dot((ri > ci).astype(BF16), multi.astype(BF16), preferred_element_type=F32)
    before = before + base[...]
    idx_out = jnp.zeros((tm, LANES), F32)
    gate_out = jnp.zeros((tm, LANES), F32)
    for k in range(TOP_K):
        rank = jnp.sum(jnp.where(sels[k], before, 0.0), axis=-1, keepdims=True)
        idx_out = jnp.where(lane == k, idxs[k], idx_out)
        idx_out = jnp.where(lane == TOP_K + k, rank, idx_out)
        gate_out = jnp.where(lane == k, es[k] / den, gate_out)
    idx_ref[...] = idx_out.astype(jnp.int32)
    gate_ref[...] = gate_out
    base[...] = base[...] + jnp.sum(multi, axis=0, keepdims=True)
    cnt_ref[...] = jnp.broadcast_to(base[...], cnt_ref.shape).astype(jnp.int32)


def _merge(att, ssm, proj, xf, wap, wsp, wo, fg, wr, br):
    T = xf.shape[0]
    tm = min(512, T)
    full = lambda a: pl.BlockSpec(a.shape, lambda i: (0, 0))
    return pl.pallas_call(
        functools.partial(_merge_kernel, tm=tm),
        grid=(T // tm,),
        in_specs=[
            pl.BlockSpec((tm, ATT_WIDTH), lambda i: (i, 0)),
            pl.BlockSpec((tm, SSM_D_INNER), lambda i: (i, 0)),
            pl.BlockSpec((tm, D_MODEL), lambda i: (i, COL_GA // D_MODEL)),
            pl.BlockSpec((tm, D_MODEL), lambda i: (i, COL_GS // D_MODEL)),
            pl.BlockSpec((tm, D_MODEL), lambda i: (i, 0)),
            full(wap), full(wsp), full(wo), full(fg), full(wr), full(br),
        ],
        out_specs=[
            pl.BlockSpec((tm, D_MODEL), lambda i: (i, 0)),
            pl.BlockSpec((tm, PACKED), lambda i: (i, 0)),
            pl.BlockSpec((tm, LANES), lambda i: (i, 0)),
            pl.BlockSpec((tm, LANES), lambda i: (i, 0)),
            pl.BlockSpec((HALO, LANES), lambda i: (0, 0)),
        ],
        out_shape=[
            jax.ShapeDtypeStruct((T, D_MODEL), F32),
            jax.ShapeDtypeStruct((T, PACKED), jnp.uint32),
            jax.ShapeDtypeStruct((T, LANES), jnp.int32),
            jax.ShapeDtypeStruct((T, LANES), F32),
            jax.ShapeDtypeStruct((HALO, LANES), jnp.int32),
        ],
        scratch_shapes=[pltpu.VMEM((1, LANES), F32)],
        compiler_params=_cparams(("arbitrary",)),
        name="merge_router",
    )(att, ssm, proj, proj, xf, wap, wsp, wo, fg, wr, br)


DMA_UNROLL = 8


def _dispatch_kernel(lb_ref, cnt_ref, dest_ref, h_ref, rows_ref, zeros, zsem, sem, *, tt, bm):
    @pl.when(pl.program_id(0) == 0)
    def _():
        zeros[...] = jnp.zeros_like(zeros)
        for wait in (False, True):
            for e in range(N_EXPERTS):
                @pl.when(cnt_ref[e] > 0)
                def _():
                    start = pl.multiple_of(lb_ref[e], bm)
                    cp = pltpu.make_async_copy(zeros, rows_ref.at[pl.ds(start, bm)], zsem)
                    cp.wait() if wait else cp.start()

    def row_copy(t, k):
        return pltpu.make_async_copy(h_ref.at[pl.ds(t, 1)],
                                     rows_ref.at[pl.ds(dest_ref[t * TOP_K + k], 1)], sem)

    for wait in (False, True):
        def body(g, carry):
            for u in range(DMA_UNROLL):
                for k in range(TOP_K):
                    cp = row_copy(g * DMA_UNROLL + u, k)
                    cp.wait() if wait else cp.start()
            return carry
        lax.fori_loop(0, tt // DMA_UNROLL, body, 0)


def _dispatch(last_block, cnt, dest, h2p, n_rows, bm):
    T = h2p.shape[0]
    tt = min(512, T)
    grid_spec = pltpu.PrefetchScalarGridSpec(
        num_scalar_prefetch=2,
        grid=(T // tt,),
        in_specs=[
            pl.BlockSpec((tt * TOP_K,), lambda i, lb, c: (i,), memory_space=pltpu.SMEM),
            pl.BlockSpec((tt, PACKED), lambda i, lb, c: (i, 0)),
        ],
        out_specs=pl.BlockSpec(memory_space=pl.ANY),
        scratch_shapes=[pltpu.VMEM((bm, PACKED), jnp.uint32),
                        pltpu.SemaphoreType.DMA, pltpu.SemaphoreType.DMA],
    )
    return pl.pallas_call(
        functools.partial(_dispatch_kernel, tt=tt, bm=bm),
        grid_spec=grid_spec,
        out_shape=jax.ShapeDtypeStruct((n_rows, PACKED), jnp.uint32),
        compiler_params=_cparams(("arbitrary",)),
        name="dispatch",
    )(last_block, cnt, dest, h2p)


def _expert_kernel(be_ref, nu_ref, x_ref, wgu_ref, bgu_ref, wd_ref, bd_ref, o_ref, wgu_bf, wd_bf):
    i = pl.program_id(0)

    @pl.when(i < nu_ref[0])
    def _():
        @pl.when((i == 0) | (be_ref[i] != be_ref[jnp.maximum(i - 1, 0)]))
        def _():
            wgu_bf[...] = wgu_ref[0].astype(BF16)
            wd_bf[...] = wd_ref[0].astype(BF16)

        lo, hi = _unpack_rows(x_ref[...])
        x = jnp.concatenate([lo, hi], axis=1).astype(BF16)
        gu = jnp.dot(x, wgu_bf[...], preferred_element_type=F32) + bgu_ref[0]
        gate = jnp.minimum(gu[:, :D_FF], SWIGLU_LIMIT)
        up = jnp.clip(gu[:, D_FF:], -SWIGLU_LIMIT, SWIGLU_LIMIT)
        glu = gate * jax.nn.sigmoid(SWIGLU_ALPHA * gate)
        act = ((up + 1.0) * glu).astype(BF16)
        o_ref[...] = _pack_rows(jnp.dot(act, wd_bf[...], preferred_element_type=F32) + bd_ref[0])


def _experts(block_expert, n_used, rows, wgu, bgu, wd, bd, bm):
    n_rows = rows.shape[0]
    nb = n_rows // bm
    row_map = lambda i, be, nu: (jnp.minimum(i, nu[0] - 1), 0)
    exp_map = lambda i, be, nu: (be[jnp.minimum(i, nu[0] - 1)], 0, 0)
    grid_spec = pltpu.PrefetchScalarGridSpec(
        num_scalar_prefetch=2,
        grid=(nb,),
        in_specs=[
            pl.BlockSpec((bm, PACKED), row_map),
            pl.BlockSpec((1, D_MODEL, 2 * D_FF), exp_map),
            pl.BlockSpec((1, 1, 2 * D_FF), exp_map),
            pl.BlockSpec((1, D_FF, D_MODEL), exp_map),
            pl.BlockSpec((1, 1, D_MODEL), exp_map),
        ],
        out_specs=pl.BlockSpec((bm, PACKED), row_map),
        scratch_shapes=[pltpu.VMEM((D_MODEL, 2 * D_FF), BF16), pltpu.VMEM((D_FF, D_MODEL), BF16)],
    )
    return pl.pallas_call(
        _expert_kernel,
        grid_spec=grid_spec,
        out_shape=jax.ShapeDtypeStruct((n_rows, PACKED), jnp.uint32),
        compiler_params=_cparams(("arbitrary",)),
        name="experts",
    )(block_expert, n_used, rows, wgu, bgu, wd, bd)


def _combine_kernel(dest_ref, x1_ref, gate_ref, y_ref, o_ref, ybuf, sem, *, tc):
    def row_copy(t, k):
        return pltpu.make_async_copy(y_ref.at[pl.ds(dest_ref[t * TOP_K + k], 1)],
                                     ybuf.at[k, pl.ds(t, 1)], sem)

    for wait in (False, True):
        def body(g, carry):
            for u in range(DMA_UNROLL):
                for k in range(TOP_K):
                    cp = row_copy(g * DMA_UNROLL + u, k)
                    cp.wait() if wait else cp.start()
            return carry
        lax.fori_loop(0, tc // DMA_UNROLL, body, 0)

    x1 = x1_ref[...]
    acc_lo, acc_hi = x1[:, :PACKED], x1[:, PACKED:]
    g = gate_ref[...]
    for k in range(TOP_K):
        lo, hi = _unpack_rows(ybuf[k])
        acc_lo = acc_lo + g[:, k:k + 1] * lo
        acc_hi = acc_hi + g[:, k:k + 1] * hi
    o_ref[:, :PACKED] = acc_lo
    o_ref[:, PACKED:] = acc_hi


def _combine(dest, x1, gates, y_rows):
    T = x1.shape[0]
    tc = min(256, T)
    return pl.pallas_call(
        functools.partial(_combine_kernel, tc=tc),
        grid=(T // tc,),
        in_specs=[
            pl.BlockSpec((tc * TOP_K,), lambda i: (i,), memory_space=pltpu.SMEM),
            pl.BlockSpec((tc, D_MODEL), lambda i: (i, 0)),
            pl.BlockSpec((tc, LANES), lambda i: (i, 0)),
            pl.BlockSpec(memory_space=pl.ANY),
        ],
        out_specs=pl.BlockSpec((tc, D_MODEL), lambda i: (i, 0)),
        out_shape=jax.ShapeDtypeStruct((T, D_MODEL), F32),
        scratch_shapes=[pltpu.VMEM((TOP_K, tc, PACKED), jnp.uint32), pltpu.SemaphoreType.DMA],
        compiler_params=_cparams(("arbitrary",)),
        name="combine",
    )(dest, x1, gates, y_rows)


def _pad_lanes(v):
    return jnp.pad(v, ((0, 0), (0, LANES - v.shape[-1])))


def kernel(x, mix_norm_g, w_in, q_norm_g, k_norm_g, lambda_q1, lambda_k1, lambda_q2, lambda_k2,
           attn_subln_g, conv_w, conv_b, dt_bias, a_log, d_skip, ssm_norm_g, w_attn_proj,
           w_ssm_proj, w_out, ffn_norm_g, w_router, b_router, w_gate_up, b_gate_up, w_down, b_down):
    B, S, D = x.shape
    T = B * S
    xf = x.reshape(T, D)
    layer = 0

    wi = w_in[layer]
    o_q, o_k, o_v = 0, ATT_WIDTH, 2 * ATT_WIDTH
    o_z = 3 * ATT_WIDTH
    o_xbc = o_z + SSM_D_INNER
    o_dt = o_xbc + SSM_XBC
    o_ga = o_dt + SSM_HEADS
    w_main = jnp.concatenate([wi[:, o_xbc:o_dt], wi[:, o_z:o_xbc], wi[:, o_q:o_z], wi[:, o_ga:]],
                             axis=1).astype(BF16)
    w_dt = _pad_lanes(wi[:, o_dt:o_ga]).astype(BF16)

    proj, dt_raw = _in_proj(xf, mix_norm_g[layer][None, :], w_main, w_dt)

    half = ATT_HEAD_DIM // 2
    inv = ROPE_THETA ** (-jnp.arange(0, ATT_HEAD_DIM, 2, dtype=F32) / ATT_HEAD_DIM)
    ang = jnp.arange(S, dtype=F32)[:, None] * inv[None, :]
    cos2 = jnp.tile(jnp.cos(ang), (1, LANES // half))
    sin2 = jnp.tile(jnp.concatenate([-jnp.sin(ang), jnp.sin(ang)], axis=1), (1, LANES // ATT_HEAD_DIM))
    att = _attention(proj, cos2, sin2,
                     jnp.tile(q_norm_g[layer], 2)[None, :], jnp.tile(k_norm_g[layer], 2)[None, :],
                     lambda_q1[layer][None, :], lambda_k1[layer][None, :],
                     lambda_q2[layer][None, :], lambda_k2[layer][None, :],
                     attn_subln_g[layer][None, :], B, S)

    ssm = _ssd(proj, dt_raw, conv_w[layer], conv_b[layer][None, :],
               _pad_lanes(dt_bias[layer][None, :]), _pad_lanes(a_log[layer][None, :]),
               jnp.repeat(d_skip[layer], SSM_HEAD_DIM)[None, :], ssm_norm_g[layer][None, :], B, S)

    x1, h2p, idx_rank, gates, counts = _merge(
        att, ssm, proj, xf, w_attn_proj[layer].astype(BF16), w_ssm_proj[layer].astype(BF16),
        w_out[layer].astype(BF16), ffn_norm_g[layer][None, :],
        _pad_lanes(w_router[layer]), _pad_lanes(b_router[layer][None, :]))

    bm = 512
    A = T * TOP_K
    n_rows = (A + N_EXPERTS * (bm - 1)) // bm * bm
    cnt = counts[0, :N_EXPERTS]
    padded = (cnt + bm - 1) // bm * bm
    pend = jnp.cumsum(padded)
    pstart = pend - padded
    n_used = (pend[-1:] // bm).astype(jnp.int32)
    block_start = jnp.arange(n_rows // bm, dtype=jnp.int32) * bm
    block_expert = jnp.minimum(jnp.sum(block_start[:, None] >= pend[None, :], axis=1),
                               N_EXPERTS - 1).astype(jnp.int32)
    top_idx = idx_rank[:, :TOP_K]
    onehot = top_idx[:, :, None] == jnp.arange(N_EXPERTS, dtype=jnp.int32)[None, None, :]
    dest = (jnp.sum(jnp.where(onehot, pstart[None, None, :], 0), axis=-1)
            + idx_rank[:, TOP_K:2 * TOP_K]).reshape(A).astype(jnp.int32)

    rows = _dispatch((pend - bm).astype(jnp.int32), cnt.astype(jnp.int32), dest, h2p, n_rows, bm)
    y_rows = _experts(block_expert, n_used, rows, w_gate_up[layer], b_gate_up[layer][:, None, :],
                      w_down[layer], b_down[layer][:, None, :], bm)
    out = _combine(dest, x1, gates, y_rows)
    return out.reshape(B, S, D)
```

```python
import functools
import math

import jax
import jax.numpy as jnp
from jax import lax
from jax.experimental import pallas as pl
from jax.experimental.pallas import tpu as pltpu

F32 = jnp.float32
BF16 = jnp.bfloat16

D_MODEL = 1024
EPS = 1e-6
ATT_HEADS = 8
ATT_HEAD_DIM = 64
ATT_V_DIM = 2 * ATT_HEAD_DIM
ATT_WIDTH = ATT_HEADS * ATT_V_DIM
ROPE_THETA = 10000.0
SSM_D_INNER = 2 * D_MODEL
SSM_HEAD_DIM = 64
SSM_HEADS = SSM_D_INNER // SSM_HEAD_DIM
SSM_GROUPS = 8
SSM_STATE = 128
SSM_CONV = 4
SSM_CHUNK = 128
SSM_XBC = SSM_D_INNER + 2 * SSM_GROUPS * SSM_STATE
N_EXPERTS = 32
TOP_K = 4
D_FF = D_MODEL
SWIGLU_LIMIT = 7.0
SWIGLU_ALPHA = 1.702
LAM_INIT = 0.8 - 0.6 * math.exp(-0.3 * 0)

LANES = 128
HALO = 8

COL_XBC = 0
COL_Z = SSM_XBC
COL_Q = COL_Z + SSM_D_INNER
COL_K = COL_Q + ATT_WIDTH
COL_V = COL_K + ATT_WIDTH
COL_GA = COL_V + ATT_WIDTH
COL_GS = COL_GA + D_MODEL
PROJ_COLS = COL_GS + D_MODEL

VMEM_LIMIT = 56 * 1024 * 1024


def _cparams(sem):
    return pltpu.CompilerParams(dimension_semantics=sem, vmem_limit_bytes=VMEM_LIMIT)


PACKED = D_MODEL // 2


def _pack_rows(x):
    lo = lax.bitcast_convert_type(x[:, :PACKED].astype(BF16).astype(F32), jnp.uint32)
    hi = lax.bitcast_convert_type(x[:, PACKED:].astype(BF16).astype(F32), jnp.uint32)
    return hi | (lo >> 16)


def _unpack_rows(p):
    lo = lax.bitcast_convert_type(p << 16, F32)
    hi = lax.bitcast_convert_type(p & jnp.uint32(0xFFFF0000), F32)
    return lo, hi


def _in_proj_kernel(x_ref, g_ref, w_ref, wdt_ref, o_ref, dt_ref, h_scr):
    @pl.when(pl.program_id(1) == 0)
    def _():
        x = x_ref[...]
        ms = jnp.mean(x * x, axis=-1, keepdims=True)
        hb = (x * lax.rsqrt(ms + EPS) * g_ref[...]).astype(BF16)
        h_scr[...] = hb
        dt_ref[...] = jnp.dot(hb, wdt_ref[...], preferred_element_type=F32)

    o_ref[...] = jnp.dot(h_scr[...], w_ref[...], preferred_element_type=F32).astype(BF16)


def _in_proj(xf, g, w_main, w_dt):
    T = xf.shape[0]
    tm = min(1024, T)
    tn = 1024
    return pl.pallas_call(
        _in_proj_kernel,
        grid=(T // tm, PROJ_COLS // tn),
        in_specs=[
            pl.BlockSpec((tm, D_MODEL), lambda i, j: (i, 0)),
            pl.BlockSpec((1, D_MODEL), lambda i, j: (0, 0)),
            pl.BlockSpec((D_MODEL, tn), lambda i, j: (0, j)),
            pl.BlockSpec((D_MODEL, LANES), lambda i, j: (0, 0)),
        ],
        out_specs=[
            pl.BlockSpec((tm, tn), lambda i, j: (i, j)),
            pl.BlockSpec((tm, LANES), lambda i, j: (i, 0)),
        ],
        out_shape=[
            jax.ShapeDtypeStruct((T, PROJ_COLS), BF16),
            jax.ShapeDtypeStruct((T, LANES), F32),
        ],
        scratch_shapes=[pltpu.VMEM((tm, D_MODEL), BF16)],
        compiler_params=_cparams(("arbitrary", "arbitrary")),
        name="in_proj",
    )(xf, g, w_main, w_dt)


LOG2E = 1.4426950408889634
SHIFT_LIMIT = 57.0
BOUND_MARGIN = 1.02


def _attn_kernel(q_ref, k_ref, v_ref, qc_ref, qs_ref, kc_ref, ks_ref, qg_ref, kg_ref,
                 lq1_ref, lk1_ref, lq2_ref, lk2_ref, sg_ref, o_ref, k_scr, v_scr, *, tq, seq):
    qi = pl.program_id(2)
    lane = lax.broadcasted_iota(jnp.int32, (1, LANES), 1)
    first = lane < ATT_HEAD_DIM
    one_col = jnp.where(lane == 0, 1.0, 0.0)

    ri = lax.broadcasted_iota(jnp.int32, (LANES, LANES), 0)
    ci = lax.broadcasted_iota(jnp.int32, (LANES, LANES), 1)
    same_comp = (ri // ATT_HEAD_DIM == ci // ATT_HEAD_DIM).astype(BF16)
    swap_half = ((ri // ATT_HEAD_DIM == ci // ATT_HEAD_DIM)
                 & ((ri - ci == ATT_HEAD_DIM // 2) | (ci - ri == ATT_HEAD_DIM // 2))).astype(BF16)

    def norm_rope(xb, gcos, gsin):
        xf = xb.astype(F32)
        sq = xf * xf
        hi = sq.astype(BF16)
        lo = (sq - hi.astype(F32)).astype(BF16)
        ms = (jnp.dot(hi, same_comp, preferred_element_type=F32)
              + jnp.dot(lo, same_comp, preferred_element_type=F32)) * (1.0 / ATT_HEAD_DIM)
        xr = jnp.dot(xb, swap_half, preferred_element_type=F32)
        return lax.rsqrt(ms + EPS) * (xf * gcos + xr * gsin)

    @pl.when(qi == 0)
    def _():
        ones = jnp.broadcast_to(one_col, (tq, LANES)).astype(BF16)
        for r in range(0, seq, tq):
            k_scr[r:r + tq, :LANES] = norm_rope(k_ref[r:r + tq, :], kc_ref[r:r + tq, :],
                                                ks_ref[r:r + tq, :]).astype(BF16)
            k_scr[r:r + tq, LANES:] = ones
            v_scr[r:r + tq, :LANES] = v_ref[r:r + tq, :]
            v_scr[r:r + tq, LANES:] = ones

    row0 = pl.multiple_of(qi * tq, tq)
    qb = norm_rope(q_ref[...], qc_ref[pl.ds(row0, tq), :], qs_ref[pl.ds(row0, tq), :]).astype(BF16)
    q1 = jnp.where(first, qb, jnp.zeros_like(qb))
    q2 = jnp.where(first, jnp.zeros_like(qb), qb)
    ub = (ATT_HEAD_DIM * ATT_HEAD_DIM ** -0.5 * LOG2E * BOUND_MARGIN
          * jnp.max(jnp.abs(qg_ref[...])) * jnp.max(jnp.abs(kg_ref[...])))
    safe = ub <= SHIFT_LIMIT

    keep = (lax.broadcasted_iota(jnp.int32, (tq, tq), 1) <= lax.broadcasted_iota(jnp.int32, (tq, tq), 0))
    nt = (((1,), (1,)), ((), ()))

    def finish(o1, o2):
        lam = (jnp.exp(jnp.sum(lq1_ref[...] * lk1_ref[...], axis=-1, keepdims=True))
               - jnp.exp(jnp.sum(lq2_ref[...] * lk2_ref[...], axis=-1, keepdims=True)) + LAM_INIT)
        o = o1 - lam * o2
        ms = jnp.mean(o * o, axis=-1, keepdims=True)
        o_ref[...] = (o * lax.rsqrt(ms + EPS) * sg_ref[...] * (1.0 - LAM_INIT)).astype(BF16)

    @pl.when(safe)
    def _():
        shift = jnp.broadcast_to(jnp.where(lane == 0, -ub, 0.0), (tq, LANES)).astype(BF16)
        q1x = jnp.concatenate([q1, shift], axis=1)
        q2x = jnp.concatenate([q2, shift], axis=1)

        def step(j, carry, masked):
            a1, a2 = carry
            c0 = pl.multiple_of(j * tq, tq)
            kt = k_scr[pl.ds(c0, tq), :]
            vt = v_scr[pl.ds(c0, tq), :]
            t1 = lax.dot_general(q1x, kt, nt, preferred_element_type=F32)
            t2 = lax.dot_general(q2x, kt, nt, preferred_element_type=F32)
            if masked:
                t1 = jnp.where(keep, t1, -jnp.inf)
                t2 = jnp.where(keep, t2, -jnp.inf)
            a1 = a1 + jnp.dot(jnp.exp2(t1).astype(BF16), vt, preferred_element_type=F32)
            a2 = a2 + jnp.dot(jnp.exp2(t2).astype(BF16), vt, preferred_element_type=F32)
            return a1, a2

        zacc = jnp.zeros((tq, 2 * LANES), F32)
        carry = step(qi, (zacc, zacc), True)
        a1, a2 = lax.fori_loop(0, qi, lambda j, c: step(j, c, False), carry)
        finish(a1[:, :LANES] / a1[:, LANES:LANES + 1], a2[:, :LANES] / a2[:, LANES:LANES + 1])

    @pl.when(jnp.logical_not(safe))
    def _():
        def update(t, vt, m, l, acc):
            m_new = jnp.maximum(m, jnp.max(t, axis=-1, keepdims=True))
            alpha = jnp.exp2(m - m_new)
            p = jnp.exp2(t - m_new)
            l_new = alpha * l + jnp.sum(p, axis=-1, keepdims=True)
            acc_new = alpha * acc + jnp.dot(p.astype(BF16), vt, preferred_element_type=F32)
            return m_new, l_new, acc_new

        def step(j, carry, masked):
            m1, l1, a1, m2, l2, a2 = carry
            c0 = pl.multiple_of(j * tq, tq)
            kt = k_scr[pl.ds(c0, tq), :LANES]
            vt = v_scr[pl.ds(c0, tq), :LANES]
            t1 = lax.dot_general(q1, kt, nt, preferred_element_type=F32)
            t2 = lax.dot_general(q2, kt, nt, preferred_element_type=F32)
            if masked:
                t1 = jnp.where(keep, t1, -jnp.inf)
                t2 = jnp.where(keep, t2, -jnp.inf)
            m1, l1, a1 = update(t1, vt, m1, l1, a1)
            m2, l2, a2 = update(t2, vt, m2, l2, a2)
            return m1, l1, a1, m2, l2, a2

        neg = jnp.full((tq, 1), -jnp.inf, F32)
        zero1 = jnp.zeros((tq, 1), F32)
        zacc = jnp.zeros((tq, ATT_V_DIM), F32)
        carry = step(qi, (neg, zero1, zacc, neg, zero1, zacc), True)
        m1, l1, a1, m2, l2, a2 = lax.fori_loop(0, qi, lambda j, c: step(j, c, False), carry)
        finish(a1 / l1, a2 / l2)


def _attention(proj, qcos, qsin, kcos, ksin, qg2, kg2, lq1, lk1, lq2, lk2, sg, batch, seq):
    T = batch * seq
    tq = min(512, seq)
    nq = seq // tq
    vec = lambda n: pl.BlockSpec((1, n), lambda b, h, i: (0, 0))
    table = pl.BlockSpec((seq, LANES), lambda b, h, i: (0, 0))
    return pl.pallas_call(
        functools.partial(_attn_kernel, tq=tq, seq=seq),
        grid=(batch, ATT_HEADS, nq),
        in_specs=[
            pl.BlockSpec((tq, LANES), lambda b, h, i: (b * nq + i, COL_Q // LANES + h)),
            pl.BlockSpec((seq, LANES), lambda b, h, i: (b, COL_K // LANES + h)),
            pl.BlockSpec((seq, LANES), lambda b, h, i: (b, COL_V // LANES + h)),
            table, table, table, table,
            vec(LANES), vec(LANES),
            vec(ATT_HEAD_DIM), vec(ATT_HEAD_DIM), vec(ATT_HEAD_DIM), vec(ATT_HEAD_DIM),
            vec(LANES),
        ],
        out_specs=pl.BlockSpec((tq, LANES), lambda b, h, i: (b * nq + i, h)),
        out_shape=jax.ShapeDtypeStruct((T, ATT_WIDTH), BF16),
        scratch_shapes=[pltpu.VMEM((seq, 2 * LANES), BF16), pltpu.VMEM((seq, 2 * LANES), BF16)],
        compiler_params=_cparams(("arbitrary", "arbitrary", "arbitrary")),
        name="attention",
    )(proj, proj, proj, qcos, qsin, kcos, ksin, qg2, kg2, lq1, lk1, lq2, lk2, sg)


def _ssd_kernel(xbc_ref, z_ref, dt_ref, cw_ref, cb_ref, dtb_ref, alog_ref, dsk_ref, ng_ref,
                o_ref, halo, state, xs_scr, bc_scr, y_scr):
    Q = SSM_CHUNK
    N = SSM_STATE
    P2 = 2 * SSM_HEAD_DIM

    @pl.when(pl.program_id(1) == 0)
    def _():
        halo[...] = jnp.zeros_like(halo)
        state[...] = jnp.zeros_like(state)

    cwid = 512
    for c0 in range(0, SSM_XBC, cwid):
        cur = xbc_ref[:, c0:c0 + cwid].astype(F32)
        ext = jnp.concatenate([halo[:, c0:c0 + cwid], cur], axis=0)
        acc = cb_ref[:, c0:c0 + cwid] + cw_ref[SSM_CONV - 1:SSM_CONV, c0:c0 + cwid] * cur
        for k in range(SSM_CONV - 1):
            off = HALO - (SSM_CONV - 1) + k
            acc = acc + cw_ref[k:k + 1, c0:c0 + cwid] * ext[off:off + Q, :]
        act = acc * jax.nn.sigmoid(acc)
        if c0 < SSM_D_INNER:
            xs_scr[:, c0:c0 + cwid] = act
        else:
            bc_scr[:, c0 - SSM_D_INNER:c0 - SSM_D_INNER + cwid] = act.astype(BF16)
        halo[:, c0:c0 + cwid] = cur[Q - HALO:Q, :]

    dt = jax.nn.softplus(dt_ref[...] + dtb_ref[...])
    da = dt * (-jnp.exp(alog_ref[...]))
    ri = lax.broadcasted_iota(jnp.int32, (Q, Q), 0)
    ci = lax.broadcasted_iota(jnp.int32, (Q, Q), 1)
    tri = ri >= ci
    acs = jnp.dot(tri.astype(F32), da, preferred_element_type=F32,
                  precision=lax.Precision.HIGHEST)
    acs_t = acs.T
    dt_t = dt.T
    w_t = dt_t * jnp.exp(acs_t[:, Q - 1:Q] - acs_t)
    lane = lax.broadcasted_iota(jnp.int32, (1, P2), 1)
    left = lane < SSM_HEAD_DIM

    for g in range(SSM_GROUPS):
        bm = bc_scr[:, g * N:(g + 1) * N]
        cm = bc_scr[:, SSM_GROUPS * N + g * N:SSM_GROUPS * N + (g + 1) * N]
        cb = lax.dot_general(cm, bm, (((1,), (1,)), ((), ())), preferred_element_type=F32)
        bm_t = bm.astype(F32).T
        cm_f = cm.astype(F32)
        for pr in range(2):
            pair = 2 * g + pr
            xs_pair = xs_scr[:, pair * P2:(pair + 1) * P2].astype(BF16)
            prev = state[pair]
            rhs = jnp.concatenate([xs_pair, prev.astype(BF16)], axis=0)
            ys, sts, decs = [], [], []
            for r in range(2):
                h = 2 * pair + r
                a_col = acs[:, h:h + 1]
                seg = a_col - acs_t[h:h + 1, :]
                decay = jnp.exp(jnp.where(tri, seg, -jnp.inf))
                m_h = cb * decay * dt_t[h:h + 1, :]
                e_h = cm_f * jnp.exp(a_col)
                lhs = jnp.concatenate([m_h, e_h], axis=1).astype(BF16)
                ys.append(jnp.dot(lhs, rhs, preferred_element_type=F32))
                sts.append(jnp.dot((bm_t * w_t[h:h + 1, :]).astype(BF16), xs_pair,
                                   preferred_element_type=F32))
                decs.append(jnp.exp(acs[Q - 1:Q, h:h + 1]))
            y_scr[:, pair * P2:(pair + 1) * P2] = jnp.where(left, ys[0], ys[1])
            dec = jnp.where(left, decs[0], decs[1])
            state[pair] = dec * prev + jnp.where(left, sts[0], sts[1])

    y = y_scr[...] + dsk_ref[...] * xs_scr[...]
    zf = z_ref[...].astype(F32)
    y = y * (zf * jax.nn.sigmoid(zf))
    gw = SSM_D_INNER // SSM_GROUPS
    for g in range(SSM_GROUPS):
        yg = y[:, g * gw:(g + 1) * gw]
        ms = jnp.mean(yg * yg, axis=-1, keepdims=True)
        o_ref[:, g * gw:(g + 1) * gw] = (yg * lax.rsqrt(ms + EPS)
                                         * ng_ref[:, g * gw:(g + 1) * gw]).astype(BF16)


def _ssd(proj, dt_raw, conv_w, conv_b, dt_bias, a_log, dsk, ng, batch, seq):
    T = batch * seq
    Q = SSM_CHUNK
    nc = seq // Q
    vec = lambda r, n: pl.BlockSpec((r, n), lambda b, c: (0, 0))
    return pl.pallas_call(
        _ssd_kernel,
        grid=(batch, nc),
        in_specs=[
            pl.BlockSpec((Q, SSM_XBC), lambda b, c: (b * nc + c, COL_XBC // SSM_XBC)),
            pl.BlockSpec((Q, SSM_D_INNER), lambda b, c: (b * nc + c, COL_Z // SSM_D_INNER)),
            pl.BlockSpec((Q, LANES), lambda b, c: (b * nc + c, 0)),
            vec(SSM_CONV, SSM_XBC), vec(1, SSM_XBC), vec(1, LANES), vec(1, LANES),
            vec(1, SSM_D_INNER), vec(1, SSM_D_INNER),
        ],
        out_specs=pl.BlockSpec((Q, SSM_D_INNER), lambda b, c: (b * nc + c, 0)),
        out_shape=jax.ShapeDtypeStruct((T, SSM_D_INNER), BF16),
        scratch_shapes=[
            pltpu.VMEM((HALO, SSM_XBC), F32),
            pltpu.VMEM((SSM_HEADS // 2, SSM_STATE, 2 * SSM_HEAD_DIM), F32),
            pltpu.VMEM((Q, SSM_D_INNER), F32),
            pltpu.VMEM((Q, 2 * SSM_GROUPS * SSM_STATE), BF16),
            pltpu.VMEM((Q, SSM_D_INNER), F32),
        ],
        compiler_params=_cparams(("arbitrary", "arbitrary")),
        name="ssd",
    )(proj, proj, dt_raw, conv_w, conv_b, dt_bias, a_log, dsk, ng)


def _merge_kernel(att_ref, ssm_ref, ga_ref, gs_ref, x_ref, wap_ref, wsp_ref, wo_ref, fg_ref,
                  wr_ref, br_ref, x1_ref, h2_ref, idx_ref, gate_ref, cnt_ref, base, *, tm):
    i = pl.program_id(0)

    @pl.when(i == 0)
    def _():
        base[...] = jnp.zeros_like(base)

    pa = jnp.dot(att_ref[...], wap_ref[...], preferred_element_type=F32)
    ps = jnp.dot(ssm_ref[...], wsp_ref[...], preferred_element_type=F32)
    merged = (jax.nn.sigmoid(ga_ref[...].astype(F32)) * pa
              + jax.nn.sigmoid(gs_ref[...].astype(F32)) * ps)
    x1 = x_ref[...] + jnp.dot(merged.astype(BF16), wo_ref[...], preferred_element_type=F32)
    x1_ref[...] = x1
    ms = jnp.mean(x1 * x1, axis=-1, keepdims=True)
    h2 = x1 * lax.rsqrt(ms + EPS) * fg_ref[...]
    h2_ref[...] = _pack_rows(h2)

    lane = lax.broadcasted_iota(jnp.int32, (1, LANES), 1)
    logits = jnp.dot(h2.astype(BF16), wr_ref[...], preferred_element_type=F32) + br_ref[...]
    lg = jnp.where(lane < N_EXPERTS, logits, -jnp.inf)
    lane_f = lane.astype(F32)
    vals, idxs, sels = [], [], []
    for _ in range(TOP_K):
        m = jnp.max(lg, axis=-1, keepdims=True)
        idx = jnp.min(jnp.where(lg == m, lane_f, float(LANES)), axis=-1, keepdims=True)
        sel = lane_f == idx
        vals.append(m)
        idxs.append(idx)
        sels.append(sel)
        lg = jnp.where(sel, -jnp.inf, lg)
    es = [jnp.exp(v - vals[0]) for v in vals]
    den = es[0] + es[1] + es[2] + es[3]

    multi = jnp.zeros((tm, LANES), F32)
    for sel in sels:
        multi = jnp.where(sel, 1.0, multi)
    ri = lax.broadcasted_iota(jnp.int32, (tm, tm), 0)
    ci = lax.broadcasted_iota(jnp.int32, (tm, tm), 1)
    before = jnp.dot((ri > ci).astype(BF16), multi.astype(BF16), preferred_element_type=F32)
    before = before + base[...]
    idx_out = jnp.zeros((tm, LANES), F32)
    gate_out = jnp.zeros((tm, LANES), F32)
    for k in range(TOP_K):
        rank = jnp.sum(jnp.where(sels[k], before, 0.0), axis=-1, keepdims=True)
        idx_out = jnp.where(lane == k, idxs[k], idx_out)
        idx_out = jnp.where(lane == TOP_K + k, rank, idx_out)
        gate_out = jnp.where(lane == k, es[k] / den, gate_out)
    idx_ref[...] = idx_out.astype(jnp.int32)
    gate_ref[...] = gate_out
    base[...] = base[...] + jnp.sum(multi, axis=0, keepdims=True)
    cnt_ref[...] = jnp.broadcast_to(base[...], cnt_ref.shape).astype(jnp.int32)


def _merge(att, ssm, proj, xf, wap, wsp, wo, fg, wr, br):
    T = xf.shape[0]
    tm = min(512, T)
    full = lambda a: pl.BlockSpec(a.shape, lambda i: (0, 0))
    return pl.pallas_call(
        functools.partial(_merge_kernel, tm=tm),
        grid=(T // tm,),
        in_specs=[
            pl.BlockSpec((tm, ATT_WIDTH), lambda i: (i, 0)),
            pl.BlockSpec((tm, SSM_D_INNER), lambda i: (i, 0)),
            pl.BlockSpec((tm, D_MODEL), lambda i: (i, COL_GA // D_MODEL)),
            pl.BlockSpec((tm, D_MODEL), lambda i: (i, COL_GS // D_MODEL)),
            pl.BlockSpec((tm, D_MODEL), lambda i: (i, 0)),
            full(wap), full(wsp), full(wo), full(fg), full(wr), full(br),
        ],
        out_specs=[
            pl.BlockSpec((tm, D_MODEL), lambda i: (i, 0)),
            pl.BlockSpec((tm, PACKED), lambda i: (i, 0)),
            pl.BlockSpec((tm, LANES), lambda i: (i, 0)),
            pl.BlockSpec((tm, LANES), lambda i: (i, 0)),
            pl.BlockSpec((HALO, LANES), lambda i: (0, 0)),
        ],
        out_shape=[
            jax.ShapeDtypeStruct((T, D_MODEL), F32),
            jax.ShapeDtypeStruct((T, PACKED), jnp.uint32),
            jax.ShapeDtypeStruct((T, LANES), jnp.int32),
            jax.ShapeDtypeStruct((T, LANES), F32),
            jax.ShapeDtypeStruct((HALO, LANES), jnp.int32),
        ],
        scratch_shapes=[pltpu.VMEM((1, LANES), F32)],
        compiler_params=_cparams(("arbitrary",)),
        name="merge_router",
    )(att, ssm, proj, proj, xf, wap, wsp, wo, fg, wr, br)


DMA_UNROLL = 8


def _dispatch_kernel(lb_ref, cnt_ref, dest_ref, h_ref, rows_ref, zeros, zsem, sem, *, tt, bm):
    @pl.when(pl.program_id(0) == 0)
    def _():
        zeros[...] = jnp.zeros_like(zeros)
        for wait in (False, True):
            for e in range(N_EXPERTS):
                @pl.when(cnt_ref[e] > 0)
                def _():
                    start = pl.multiple_of(lb_ref[e], bm)
                    cp = pltpu.make_async_copy(zeros, rows_ref.at[pl.ds(start, bm)], zsem)
                    cp.wait() if wait else cp.start()

    def row_copy(t, k):
        return pltpu.make_async_copy(h_ref.at[pl.ds(t, 1)],
                                     rows_ref.at[pl.ds(dest_ref[t * TOP_K + k], 1)], sem)

    for wait in (False, True):
        def body(g, carry):
            for u in range(DMA_UNROLL):
                for k in range(TOP_K):
                    cp = row_copy(g * DMA_UNROLL + u, k)
                    cp.wait() if wait else cp.start()
            return carry
        lax.fori_loop(0, tt // DMA_UNROLL, body, 0)


def _dispatch(last_block, cnt, dest, h2p, n_rows, bm):
    T = h2p.shape[0]
    tt = min(512, T)
    grid_spec = pltpu.PrefetchScalarGridSpec(
        num_scalar_prefetch=2,
        grid=(T // tt,),
        in_specs=[
            pl.BlockSpec((tt * TOP_K,), lambda i, lb, c: (i,), memory_space=pltpu.SMEM),
            pl.BlockSpec((tt, PACKED), lambda i, lb, c: (i, 0)),
        ],
        out_specs=pl.BlockSpec(memory_space=pl.ANY),
        scratch_shapes=[pltpu.VMEM((bm, PACKED), jnp.uint32),
                        pltpu.SemaphoreType.DMA, pltpu.SemaphoreType.DMA],
    )
    return pl.pallas_call(
        functools.partial(_dispatch_kernel, tt=tt, bm=bm),
        grid_spec=grid_spec,
        out_shape=jax.ShapeDtypeStruct((n_rows, PACKED), jnp.uint32),
        compiler_params=_cparams(("arbitrary",)),
        name="dispatch",
    )(last_block, cnt, dest, h2p)


def _expert_kernel(be_ref, nu_ref, x_ref, wgu_ref, bgu_ref, wd_ref, bd_ref, o_ref, wgu_bf, wd_bf):
    i = pl.program_id(0)

    @pl.when(i < nu_ref[0])
    def _():
        @pl.when((i == 0) | (be_ref[i] != be_ref[jnp.maximum(i - 1, 0)]))
        def _():
            wgu_bf[...] = wgu_ref[0].astype(BF16)
            wd_bf[...] = wd_ref[0].astype(BF16)

        lo, hi = _unpack_rows(x_ref[...])
        x = jnp.concatenate([lo, hi], axis=1).astype(BF16)
        gu = jnp.dot(x, wgu_bf[...], preferred_element_type=F32) + bgu_ref[0]
        gate = jnp.minimum(gu[:, :D_FF], SWIGLU_LIMIT)
        up = jnp.clip(gu[:, D_FF:], -SWIGLU_LIMIT, SWIGLU_LIMIT)
        glu = gate * jax.nn.sigmoid(SWIGLU_ALPHA * gate)
        act = ((up + 1.0) * glu).astype(BF16)
        o_ref[...] = _pack_rows(jnp.dot(act, wd_bf[...], preferred_element_type=F32) + bd_ref[0])


def _experts(block_expert, n_used, rows, wgu, bgu, wd, bd, bm):
    n_rows = rows.shape[0]
    nb = n_rows // bm
    row_map = lambda i, be, nu: (jnp.minimum(i, nu[0] - 1), 0)
    exp_map = lambda i, be, nu: (be[jnp.minimum(i, nu[0] - 1)], 0, 0)
    grid_spec = pltpu.PrefetchScalarGridSpec(
        num_scalar_prefetch=2,
        grid=(nb,),
        in_specs=[
            pl.BlockSpec((bm, PACKED), row_map),
            pl.BlockSpec((1, D_MODEL, 2 * D_FF), exp_map),
            pl.BlockSpec((1, 1, 2 * D_FF), exp_map),
            pl.BlockSpec((1, D_FF, D_MODEL), exp_map),
            pl.BlockSpec((1, 1, D_MODEL), exp_map),
        ],
        out_specs=pl.BlockSpec((bm, PACKED), row_map),
        scratch_shapes=[pltpu.VMEM((D_MODEL, 2 * D_FF), BF16), pltpu.VMEM((D_FF, D_MODEL), BF16)],
    )
    return pl.pallas_call(
        _expert_kernel,
        grid_spec=grid_spec,
        out_shape=jax.ShapeDtypeStruct((n_rows, PACKED), jnp.uint32),
        compiler_params=_cparams(("arbitrary",)),
        name="experts",
    )(block_expert, n_used, rows, wgu, bgu, wd, bd)


def _combine_kernel(dest_ref, x1_ref, gate_ref, y_ref, o_ref, ybuf, sem, *, tc):
    def row_copy(t, k):
        return pltpu.make_async_copy(y_ref.at[pl.ds(dest_ref[t * TOP_K + k], 1)],
                                     ybuf.at[k, pl.ds(t, 1)], sem)

    for wait in (False, True):
        def body(g, carry):
            for u in range(DMA_UNROLL):
                for k in range(TOP_K):
                    cp = row_copy(g * DMA_UNROLL + u, k)
                    cp.wait() if wait else cp.start()
            return carry
        lax.fori_loop(0, tc // DMA_UNROLL, body, 0)

    x1 = x1_ref[...]
    acc_lo, acc_hi = x1[:, :PACKED], x1[:, PACKED:]
    g = gate_ref[...]
    for k in range(TOP_K):
        lo, hi = _unpack_rows(ybuf[k])
        acc_lo = acc_lo + g[:, k:k + 1] * lo
        acc_hi = acc_hi + g[:, k:k + 1] * hi
    o_ref[:, :PACKED] = acc_lo
    o_ref[:, PACKED:] = acc_hi


def _combine(dest, x1, gates, y_rows):
    T = x1.shape[0]
    tc = min(256, T)
    return pl.pallas_call(
        functools.partial(_combine_kernel, tc=tc),
        grid=(T // tc,),
        in_specs=[
            pl.BlockSpec((tc * TOP_K,), lambda i: (i,), memory_space=pltpu.SMEM),
            pl.BlockSpec((tc, D_MODEL), lambda i: (i, 0)),
            pl.BlockSpec((tc, LANES), lambda i: (i, 0)),
            pl.BlockSpec(memory_space=pl.ANY),
        ],
        out_specs=pl.BlockSpec((tc, D_MODEL), lambda i: (i, 0)),
        out_shape=jax.ShapeDtypeStruct((T, D_MODEL), F32),
        scratch_shapes=[pltpu.VMEM((TOP_K, tc, PACKED), jnp.uint32), pltpu.SemaphoreType.DMA],
        compiler_params=_cparams(("arbitrary",)),
        name="combine",
    )(dest, x1, gates, y_rows)


def _pad_lanes(v):
    return jnp.pad(v, ((0, 0), (0, LANES - v.shape[-1])))


def kernel(x, mix_norm_g, w_in, q_norm_g, k_norm_g, lambda_q1, lambda_k1, lambda_q2, lambda_k2,
           attn_subln_g, conv_w, conv_b, dt_bias, a_log, d_skip, ssm_norm_g, w_attn_proj,
           w_ssm_proj, w_out, ffn_norm_g, w_router, b_router, w_gate_up, b_gate_up, w_down, b_down):
    B, S, D = x.shape
    T = B * S
    xf = x.reshape(T, D)
    layer = 0

    wi = w_in[layer]
    o_q, o_k, o_v = 0, ATT_WIDTH, 2 * ATT_WIDTH
    o_z = 3 * ATT_WIDTH
    o_xbc = o_z + SSM_D_INNER
    o_dt = o_xbc + SSM_XBC
    o_ga = o_dt + SSM_HEADS
    w_main = jnp.concatenate([wi[:, o_xbc:o_dt], wi[:, o_z:o_xbc], wi[:, o_q:o_z], wi[:, o_ga:]],
                             axis=1).astype(BF16)
    w_dt = _pad_lanes(wi[:, o_dt:o_ga]).astype(BF16)

    proj, dt_raw = _in_proj(xf, mix_norm_g[layer][None, :], w_main, w_dt)

    half = ATT_HEAD_DIM // 2
    inv = ROPE_THETA ** (-jnp.arange(0, ATT_HEAD_DIM, 2, dtype=F32) / ATT_HEAD_DIM)
    ang = jnp.arange(S, dtype=F32)[:, None] * inv[None, :]
    cos2 = jnp.tile(jnp.cos(ang), (1, LANES // half))
    sin2 = jnp.tile(jnp.concatenate([-jnp.sin(ang), jnp.sin(ang)], axis=1), (1, LANES // ATT_HEAD_DIM))
    qg2 = jnp.tile(q_norm_g[layer], 2)[None, :]
    kg2 = jnp.tile(k_norm_g[layer], 2)[None, :]
    partner = lambda g: jnp.tile(jnp.roll(g, half), 2)[None, :]
    q_scale = ATT_HEAD_DIM ** -0.5 * LOG2E
    att = _attention(proj, cos2 * qg2 * q_scale, sin2 * partner(q_norm_g[layer]) * q_scale,
                     cos2 * kg2, sin2 * partner(k_norm_g[layer]), qg2, kg2,
                     lambda_q1[layer][None, :], lambda_k1[layer][None, :],
                     lambda_q2[layer][None, :], lambda_k2[layer][None, :],
                     attn_subln_g[layer][None, :], B, S)

    ssm = _ssd(proj, dt_raw, conv_w[layer], conv_b[layer][None, :],
               _pad_lanes(dt_bias[layer][None, :]), _pad_lanes(a_log[layer][None, :]),
               jnp.repeat(d_skip[layer], SSM_HEAD_DIM)[None, :], ssm_norm_g[layer][None, :], B, S)

    x1, h2p, idx_rank, gates, counts = _merge(
        att, ssm, proj, xf, w_attn_proj[layer].astype(BF16), w_ssm_proj[layer].astype(BF16),
        w_out[layer].astype(BF16), ffn_norm_g[layer][None, :],
        _pad_lanes(w_router[layer]).astype(BF16), _pad_lanes(b_router[layer][None, :]))

    bm = 512
    A = T * TOP_K
    n_rows = (A + N_EXPERTS * (bm - 1)) // bm * bm
    cnt = counts[0, :N_EXPERTS]
    padded = (cnt + bm - 1) // bm * bm
    pend = jnp.cumsum(padded)
    pstart = pend - padded
    n_used = (pend[-1:] // bm).astype(jnp.int32)
    block_start = jnp.arange(n_rows // bm, dtype=jnp.int32) * bm
    block_expert = jnp.minimum(jnp.sum(block_start[:, None] >= pend[None, :], axis=1),
                               N_EXPERTS - 1).astype(jnp.int32)
    top_idx = idx_rank[:, :TOP_K]
    onehot = top_idx[:, :, None] == jnp.arange(N_EXPERTS, dtype=jnp.int32)[None, None, :]
    dest = (jnp.sum(jnp.where(onehot, pstart[None, None, :], 0), axis=-1)
            + idx_rank[:, TOP_K:2 * TOP_K]).reshape(A).astype(jnp.int32)

    rows = _dispatch((pend - bm).astype(jnp.int32), cnt.astype(jnp.int32), dest, h2p, n_rows, bm)
    y_rows = _experts(block_expert, n_used, rows, w_gate_up[layer], b_gate_up[layer][:, None, :],
                      w_down[layer], b_down[layer][:, None, :], bm)
    out = _combine(dest, x1, gates, y_rows)
    return out.reshape(B, S, D)
```

```python
import functools
import math

import jax
import jax.numpy as jnp
from jax import lax
from jax.experimental import pallas as pl
from jax.experimental.pallas import tpu as pltpu

F32 = jnp.float32
BF16 = jnp.bfloat16

D_MODEL = 1024
EPS = 1e-6
ATT_HEADS = 8
ATT_HEAD_DIM = 64
ATT_V_DIM = 2 * ATT_HEAD_DIM
ATT_WIDTH = ATT_HEADS * ATT_V_DIM
ROPE_THETA = 10000.0
SSM_D_INNER = 2 * D_MODEL
SSM_HEAD_DIM = 64
SSM_HEADS = SSM_D_INNER // SSM_HEAD_DIM
SSM_GROUPS = 8
SSM_STATE = 128
SSM_CONV = 4
SSM_CHUNK = 128
SSM_XBC = SSM_D_INNER + 2 * SSM_GROUPS * SSM_STATE
N_EXPERTS = 32
TOP_K = 4
D_FF = D_MODEL
SWIGLU_LIMIT = 7.0
SWIGLU_ALPHA = 1.702
LAM_INIT = 0.8 - 0.6 * math.exp(-0.3 * 0)

LANES = 128
HALO = 16

COL_XBC = 0
COL_Z = SSM_XBC
COL_Q = COL_Z + SSM_D_INNER
COL_K = COL_Q + ATT_WIDTH
COL_V = COL_K + ATT_WIDTH
COL_GA = COL_V + ATT_WIDTH
COL_GS = COL_GA + D_MODEL
PROJ_COLS = COL_GS + D_MODEL

VMEM_LIMIT = 56 * 1024 * 1024


def _cparams(sem):
    return pltpu.CompilerParams(dimension_semantics=sem, vmem_limit_bytes=VMEM_LIMIT)


PACKED = D_MODEL // 2


def _pack_rows(x):
    lo = lax.bitcast_convert_type(x[:, :PACKED].astype(BF16).astype(F32), jnp.uint32)
    hi = lax.bitcast_convert_type(x[:, PACKED:].astype(BF16).astype(F32), jnp.uint32)
    return hi | (lo >> 16)


def _unpack_rows(p):
    lo = lax.bitcast_convert_type(p << 16, F32)
    hi = lax.bitcast_convert_type(p & jnp.uint32(0xFFFF0000), F32)
    return lo, hi


ROW_SUB = PACKED // LANES


def _store_rows(ref, packed):
    for s in range(ROW_SUB):
        ref[pl.ds(s, packed.shape[0], stride=ROW_SUB), :] = packed[:, s * LANES:(s + 1) * LANES]


def _load_rows(ref):
    n = ref.shape[0] // ROW_SUB
    return jnp.concatenate([ref[pl.ds(s, n, stride=ROW_SUB), :] for s in range(ROW_SUB)], axis=1)


def _row(ref, r):
    return ref.at[pl.ds(pl.multiple_of(r * ROW_SUB, ROW_SUB), ROW_SUB)]


def _in_proj_kernel(x_ref, g_ref, w_ref, wdt_ref, o_ref, dt_ref, h_scr):
    @pl.when(pl.program_id(1) == 0)
    def _():
        x = x_ref[...]
        ms = jnp.mean(x * x, axis=-1, keepdims=True)
        hb = (x * lax.rsqrt(ms + EPS) * g_ref[...]).astype(BF16)
        h_scr[...] = hb
        dt_ref[...] = jnp.dot(hb, wdt_ref[...], preferred_element_type=F32)

    o_ref[...] = jnp.dot(h_scr[...], w_ref[...], preferred_element_type=F32).astype(BF16)


def _in_proj(xf, g, w_main, w_dt):
    T = xf.shape[0]
    tm = min(1024, T)
    tn = 1024
    return pl.pallas_call(
        _in_proj_kernel,
        grid=(T // tm, PROJ_COLS // tn),
        in_specs=[
            pl.BlockSpec((tm, D_MODEL), lambda i, j: (i, 0)),
            pl.BlockSpec((1, D_MODEL), lambda i, j: (0, 0)),
            pl.BlockSpec((D_MODEL, tn), lambda i, j: (0, j)),
            pl.BlockSpec((D_MODEL, LANES), lambda i, j: (0, 0)),
        ],
        out_specs=[
            pl.BlockSpec((tm, tn), lambda i, j: (i, j)),
            pl.BlockSpec((tm, LANES), lambda i, j: (i, 0)),
        ],
        out_shape=[
            jax.ShapeDtypeStruct((T, PROJ_COLS), BF16),
            jax.ShapeDtypeStruct((T, LANES), F32),
        ],
        scratch_shapes=[pltpu.VMEM((tm, D_MODEL), BF16)],
        compiler_params=_cparams(("arbitrary", "arbitrary")),
        name="in_proj",
    )(xf, g, w_main, w_dt)


LOG2E = 1.4426950408889634
SHIFT_LIMIT = 57.0
BOUND_MARGIN = 1.02


def _attn_kernel(q_ref, k_ref, v_ref, qc_ref, qs_ref, kc_ref, ks_ref, qg_ref, kg_ref,
                 lq1_ref, lk1_ref, lq2_ref, lk2_ref, sg_ref, o_ref, k_scr, v_scr, *, tq, seq):
    qi = pl.program_id(2)
    lane = lax.broadcasted_iota(jnp.int32, (1, LANES), 1)
    first = lane < ATT_HEAD_DIM
    one_col = jnp.where(lane == 0, 1.0, 0.0)

    ri = lax.broadcasted_iota(jnp.int32, (LANES, LANES), 0)
    ci = lax.broadcasted_iota(jnp.int32, (LANES, LANES), 1)
    same_comp = (ri // ATT_HEAD_DIM == ci // ATT_HEAD_DIM).astype(BF16)
    swap_half = ((ri // ATT_HEAD_DIM == ci // ATT_HEAD_DIM)
                 & ((ri - ci == ATT_HEAD_DIM // 2) | (ci - ri == ATT_HEAD_DIM // 2))).astype(BF16)

    def norm_rope(xb, gcos, gsin):
        xf = xb.astype(F32)
        sq = xf * xf
        hi = sq.astype(BF16)
        lo = (sq - hi.astype(F32)).astype(BF16)
        ms = (jnp.dot(hi, same_comp, preferred_element_type=F32)
              + jnp.dot(lo, same_comp, preferred_element_type=F32)) * (1.0 / ATT_HEAD_DIM)
        xr = jnp.dot(xb, swap_half, preferred_element_type=F32)
        return lax.rsqrt(ms + EPS) * (xf * gcos + xr * gsin)

    @pl.when(qi == 0)
    def _():
        ones = jnp.broadcast_to(one_col, (tq, LANES)).astype(BF16)
        for r in range(0, seq, tq):
            k_scr[r:r + tq, :LANES] = norm_rope(k_ref[r:r + tq, :], kc_ref[r:r + tq, :],
                                                ks_ref[r:r + tq, :]).astype(BF16)
            k_scr[r:r + tq, LANES:] = ones
            v_scr[r:r + tq, :LANES] = v_ref[r:r + tq, :]
            v_scr[r:r + tq, LANES:] = ones

    row0 = pl.multiple_of(qi * tq, tq)
    qb = norm_rope(q_ref[...], qc_ref[pl.ds(row0, tq), :], qs_ref[pl.ds(row0, tq), :]).astype(BF16)
    q1 = jnp.where(first, qb, jnp.zeros_like(qb))
    q2 = jnp.where(first, jnp.zeros_like(qb), qb)
    ub = (ATT_HEAD_DIM * ATT_HEAD_DIM ** -0.5 * LOG2E * BOUND_MARGIN
          * jnp.max(jnp.abs(qg_ref[...])) * jnp.max(jnp.abs(kg_ref[...])))
    safe = ub <= SHIFT_LIMIT

    keep = (lax.broadcasted_iota(jnp.int32, (tq, tq), 1) <= lax.broadcasted_iota(jnp.int32, (tq, tq), 0))
    nt = (((1,), (1,)), ((), ()))

    def finish(o1, o2):
        lam = (jnp.exp(jnp.sum(lq1_ref[...] * lk1_ref[...], axis=-1, keepdims=True))
               - jnp.exp(jnp.sum(lq2_ref[...] * lk2_ref[...], axis=-1, keepdims=True)) + LAM_INIT)
        o = o1 - lam * o2
        ms = jnp.mean(o * o, axis=-1, keepdims=True)
        o_ref[...] = (o * lax.rsqrt(ms + EPS) * sg_ref[...] * (1.0 - LAM_INIT)).astype(BF16)

    @pl.when(safe)
    def _():
        shift = jnp.broadcast_to(jnp.where(lane == 0, -ub, 0.0), (tq, LANES)).astype(BF16)
        q1x = jnp.concatenate([q1, shift], axis=1)
        q2x = jnp.concatenate([q2, shift], axis=1)

        def step(j, carry, masked):
            a1, a2 = carry
            c0 = pl.multiple_of(j * tq, tq)
            kt = k_scr[pl.ds(c0, tq), :]
            vt = v_scr[pl.ds(c0, tq), :]
            t1 = lax.dot_general(q1x, kt, nt, preferred_element_type=F32)
            t2 = lax.dot_general(q2x, kt, nt, preferred_element_type=F32)
            if masked:
                t1 = jnp.where(keep, t1, -jnp.inf)
                t2 = jnp.where(keep, t2, -jnp.inf)
            a1 = a1 + jnp.dot(jnp.exp2(t1).astype(BF16), vt, preferred_element_type=F32)
            a2 = a2 + jnp.dot(jnp.exp2(t2).astype(BF16), vt, preferred_element_type=F32)
            return a1, a2

        zacc = jnp.zeros((tq, 2 * LANES), F32)
        carry = step(qi, (zacc, zacc), True)
        a1, a2 = lax.fori_loop(0, qi, lambda j, c: step(j, c, False), carry)
        finish(a1[:, :LANES] / a1[:, LANES:LANES + 1], a2[:, :LANES] / a2[:, LANES:LANES + 1])

    @pl.when(jnp.logical_not(safe))
    def _():
        def update(t, vt, m, l, acc):
            m_new = jnp.maximum(m, jnp.max(t, axis=-1, keepdims=True))
            alpha = jnp.exp2(m - m_new)
            p = jnp.exp2(t - m_new)
            l_new = alpha * l + jnp.sum(p, axis=-1, keepdims=True)
            acc_new = alpha * acc + jnp.dot(p.astype(BF16), vt, preferred_element_type=F32)
            return m_new, l_new, acc_new

        def step(j, carry, masked):
            m1, l1, a1, m2, l2, a2 = carry
            c0 = pl.multiple_of(j * tq, tq)
            kt = k_scr[pl.ds(c0, tq), :LANES]
            vt = v_scr[pl.ds(c0, tq), :LANES]
            t1 = lax.dot_general(q1, kt, nt, preferred_element_type=F32)
            t2 = lax.dot_general(q2, kt, nt, preferred_element_type=F32)
            if masked:
                t1 = jnp.where(keep, t1, -jnp.inf)
                t2 = jnp.where(keep, t2, -jnp.inf)
            m1, l1, a1 = update(t1, vt, m1, l1, a1)
            m2, l2, a2 = update(t2, vt, m2, l2, a2)
            return m1, l1, a1, m2, l2, a2

        neg = jnp.full((tq, 1), -jnp.inf, F32)
        zero1 = jnp.zeros((tq, 1), F32)
        zacc = jnp.zeros((tq, ATT_V_DIM), F32)
        carry = step(qi, (neg, zero1, zacc, neg, zero1, zacc), True)
        m1, l1, a1, m2, l2, a2 = lax.fori_loop(0, qi, lambda j, c: step(j, c, False), carry)
        finish(a1 / l1, a2 / l2)


def _attention(proj, qcos, qsin, kcos, ksin, qg2, kg2, lq1, lk1, lq2, lk2, sg, batch, seq):
    T = batch * seq
    tq = min(512, seq)
    nq = seq // tq
    vec = lambda n: pl.BlockSpec((1, n), lambda b, h, i: (0, 0))
    table = pl.BlockSpec((seq, LANES), lambda b, h, i: (0, 0))
    return pl.pallas_call(
        functools.partial(_attn_kernel, tq=tq, seq=seq),
        grid=(batch, ATT_HEADS, nq),
        in_specs=[
            pl.BlockSpec((tq, LANES), lambda b, h, i: (b * nq + i, COL_Q // LANES + h)),
            pl.BlockSpec((seq, LANES), lambda b, h, i: (b, COL_K // LANES + h)),
            pl.BlockSpec((seq, LANES), lambda b, h, i: (b, COL_V // LANES + h)),
            table, table, table, table,
            vec(LANES), vec(LANES),
            vec(ATT_HEAD_DIM), vec(ATT_HEAD_DIM), vec(ATT_HEAD_DIM), vec(ATT_HEAD_DIM),
            vec(LANES),
        ],
        out_specs=pl.BlockSpec((tq, LANES), lambda b, h, i: (b * nq + i, h)),
        out_shape=jax.ShapeDtypeStruct((T, ATT_WIDTH), BF16),
        scratch_shapes=[pltpu.VMEM((seq, 2 * LANES), BF16), pltpu.VMEM((seq, 2 * LANES), BF16)],
        compiler_params=_cparams(("arbitrary", "arbitrary", "arbitrary")),
        name="attention",
    )(proj, proj, proj, qcos, qsin, kcos, ksin, qg2, kg2, lq1, lk1, lq2, lk2, sg)


def _ssd_kernel(xbc_ref, z_ref, dt_ref, cw_ref, cb_ref, dtb_ref, alog_ref, dsk_ref, ng_ref,
                o_ref, halo, state, xs_scr, bc_scr, y_scr):
    Q = SSM_CHUNK
    N = SSM_STATE
    P2 = 2 * SSM_HEAD_DIM

    @pl.when(pl.program_id(1) == 0)
    def _():
        halo[...] = jnp.zeros_like(halo)
        state[...] = jnp.zeros_like(state)

    taps = SSM_CONV - 1
    sr = lax.broadcasted_iota(jnp.int32, (taps * Q, HALO + Q), 0)
    sc = lax.broadcasted_iota(jnp.int32, (taps * Q, HALO + Q), 1)
    shift = (sc == (sr % Q) + HALO - taps + sr // Q).astype(BF16)
    cwid = 512
    for c0 in range(0, SSM_XBC, cwid):
        cur = xbc_ref[:, c0:c0 + cwid]
        ext = jnp.concatenate([halo[:, c0:c0 + cwid], cur], axis=0)
        shifted = jnp.dot(shift, ext, preferred_element_type=F32)
        acc = cb_ref[:, c0:c0 + cwid] + cw_ref[taps:SSM_CONV, c0:c0 + cwid] * cur.astype(F32)
        for k in range(taps):
            acc = acc + cw_ref[k:k + 1, c0:c0 + cwid] * shifted[k * Q:(k + 1) * Q, :]
        act = acc * jax.nn.sigmoid(acc)
        if c0 < SSM_D_INNER:
            xs_scr[:, c0:c0 + cwid] = act
        else:
            bc_scr[:, c0 - SSM_D_INNER:c0 - SSM_D_INNER + cwid] = act.astype(BF16)
        halo[:, c0:c0 + cwid] = cur[Q - HALO:Q, :]

    dt = jax.nn.softplus(dt_ref[...] + dtb_ref[...])
    da = dt * (-jnp.exp(alog_ref[...]))
    ri = lax.broadcasted_iota(jnp.int32, (Q, Q), 0)
    ci = lax.broadcasted_iota(jnp.int32, (Q, Q), 1)
    tri = ri >= ci
    acs = jnp.dot(tri.astype(F32), da, preferred_element_type=F32,
                  precision=lax.Precision.HIGHEST)
    acs_t = acs.T
    dt_t = dt.T
    w_t = dt_t * jnp.exp(acs_t[:, Q - 1:Q] - acs_t)
    lane = lax.broadcasted_iota(jnp.int32, (1, P2), 1)
    left = lane < SSM_HEAD_DIM

    for g in range(SSM_GROUPS):
        bm = bc_scr[:, g * N:(g + 1) * N]
        cm = bc_scr[:, SSM_GROUPS * N + g * N:SSM_GROUPS * N + (g + 1) * N]
        cb = lax.dot_general(cm, bm, (((1,), (1,)), ((), ())), preferred_element_type=F32)
        bm_t = bm.astype(F32).T
        cm_f = cm.astype(F32)
        for pr in range(2):
            pair = 2 * g + pr
            xs_pair = xs_scr[:, pair * P2:(pair + 1) * P2].astype(BF16)
            prev = state[pair]
            rhs = jnp.concatenate([xs_pair, prev.astype(BF16)], axis=0)
            ys, sts, decs = [], [], []
            for r in range(2):
                h = 2 * pair + r
                a_col = acs[:, h:h + 1]
                seg = a_col - acs_t[h:h + 1, :]
                decay = jnp.exp(jnp.where(tri, seg, -jnp.inf))
                m_h = cb * decay * dt_t[h:h + 1, :]
                e_h = cm_f * jnp.exp(a_col)
                lhs = jnp.concatenate([m_h, e_h], axis=1).astype(BF16)
                ys.append(jnp.dot(lhs, rhs, preferred_element_type=F32))
                sts.append(jnp.dot((bm_t * w_t[h:h + 1, :]).astype(BF16), xs_pair,
                                   preferred_element_type=F32))
                decs.append(jnp.exp(acs[Q - 1:Q, h:h + 1]))
            y_scr[:, pair * P2:(pair + 1) * P2] = jnp.where(left, ys[0], ys[1])
            dec = jnp.where(left, decs[0], decs[1])
            state[pair] = dec * prev + jnp.where(left, sts[0], sts[1])

    y = y_scr[...] + dsk_ref[...] * xs_scr[...]
    zf = z_ref[...].astype(F32)
    y = y * (zf * jax.nn.sigmoid(zf))
    gw = SSM_D_INNER // SSM_GROUPS
    for g in range(SSM_GROUPS):
        yg = y[:, g * gw:(g + 1) * gw]
        ms = jnp.mean(yg * yg, axis=-1, keepdims=True)
        o_ref[:, g * gw:(g + 1) * gw] = (yg * lax.rsqrt(ms + EPS)
                                         * ng_ref[:, g * gw:(g + 1) * gw]).astype(BF16)


def _ssd(proj, dt_raw, conv_w, conv_b, dt_bias, a_log, dsk, ng, batch, seq):
    T = batch * seq
    Q = SSM_CHUNK
    nc = seq // Q
    vec = lambda r, n: pl.BlockSpec((r, n), lambda b, c: (0, 0))
    return pl.pallas_call(
        _ssd_kernel,
        grid=(batch, nc),
        in_specs=[
            pl.BlockSpec((Q, SSM_XBC), lambda b, c: (b * nc + c, COL_XBC // SSM_XBC)),
            pl.BlockSpec((Q, SSM_D_INNER), lambda b, c: (b * nc + c, COL_Z // SSM_D_INNER)),
            pl.BlockSpec((Q, LANES), lambda b, c: (b * nc + c, 0)),
            vec(SSM_CONV, SSM_XBC), vec(1, SSM_XBC), vec(1, LANES), vec(1, LANES),
            vec(1, SSM_D_INNER), vec(1, SSM_D_INNER),
        ],
        out_specs=pl.BlockSpec((Q, SSM_D_INNER), lambda b, c: (b * nc + c, 0)),
        out_shape=jax.ShapeDtypeStruct((T, SSM_D_INNER), BF16),
        scratch_shapes=[
            pltpu.VMEM((HALO, SSM_XBC), BF16),
            pltpu.VMEM((SSM_HEADS // 2, SSM_STATE, 2 * SSM_HEAD_DIM), F32),
            pltpu.VMEM((Q, SSM_D_INNER), F32),
            pltpu.VMEM((Q, 2 * SSM_GROUPS * SSM_STATE), BF16),
            pltpu.VMEM((Q, SSM_D_INNER), F32),
        ],
        compiler_params=_cparams(("arbitrary", "arbitrary")),
        name="ssd",
    )(proj, proj, dt_raw, conv_w, conv_b, dt_bias, a_log, dsk, ng)


def _merge_kernel(att_ref, ssm_ref, ga_ref, gs_ref, x_ref, wap_ref, wsp_ref, wo_ref, fg_ref,
                  wr_ref, br_ref, x1_ref, h2_ref, idx_ref, gate_ref, cnt_ref, base, *, tm):
    i = pl.program_id(0)

    @pl.when(i == 0)
    def _():
        base[...] = jnp.zeros_like(base)

    pa = jnp.dot(att_ref[...], wap_ref[...], preferred_element_type=F32)
    ps = jnp.dot(ssm_ref[...], wsp_ref[...], preferred_element_type=F32)
    merged = (jax.nn.sigmoid(ga_ref[...].astype(F32)) * pa
              + jax.nn.sigmoid(gs_ref[...].astype(F32)) * ps)
    x1 = x_ref[...] + jnp.dot(merged.astype(BF16), wo_ref[...], preferred_element_type=F32)
    x1_ref[...] = x1
    ms = jnp.mean(x1 * x1, axis=-1, keepdims=True)
    h2 = x1 * lax.rsqrt(ms + EPS) * fg_ref[...]
    _store_rows(h2_ref, _pack_rows(h2))

    lane = lax.broadcasted_iota(jnp.int32, (1, LANES), 1)
    logits = jnp.dot(h2.astype(BF16), wr_ref[...], preferred_element_type=F32) + br_ref[...]
    lg = jnp.where(lane < N_EXPERTS, logits, -jnp.inf)
    lane_f = lane.astype(F32)
    vals, idxs, sels = [], [], []
    for _ in range(TOP_K):
        m = jnp.max(lg, axis=-1, keepdims=True)
        idx = jnp.min(jnp.where(lg == m, lane_f, float(LANES)), axis=-1, keepdims=True)
        sel = lane_f == idx
        vals.append(m)
        idxs.append(idx)
        sels.append(sel)
        lg = jnp.where(sel, -jnp.inf, lg)
    es = [jnp.exp(v - vals[0]) for v in vals]
    den = es[0] + es[1] + es[2] + es[3]

    multi = jnp.zeros((tm, LANES), F32)
    for sel in sels:
        multi = jnp.where(sel, 1.0, multi)
    ri = lax.broadcasted_iota(jnp.int32, (tm, tm), 0)
    ci = lax.broadcasted_iota(jnp.int32, (tm, tm), 1)
    before = jnp.dot((ri > ci).astype(BF16), multi.astype(BF16), preferred_element_type=F32)
    before = before + base[...]
    idx_out = jnp.zeros((tm, LANES), F32)
    gate_out = jnp.zeros((tm, LANES), F32)
    for k in range(TOP_K):
        rank = jnp.sum(jnp.where(sels[k], before, 0.0), axis=-1, keepdims=True)
        idx_out = jnp.where(lane == k, idxs[k], idx_out)
        idx_out = jnp.where(lane == TOP_K + k, rank, idx_out)
        gate_out = jnp.where(lane == k, es[k] / den, gate_out)
    idx_ref[...] = idx_out.astype(jnp.int32)
    gate_ref[...] = gate_out
    base[...] = base[...] + jnp.sum(multi, axis=0, keepdims=True)
    cnt_ref[...] = jnp.broadcast_to(base[...], cnt_ref.shape).astype(jnp.int32)


def _merge(att, ssm, proj, xf, wap, wsp, wo, fg, wr, br):
    T = xf.shape[0]
    tm = min(512, T)
    full = lambda a: pl.BlockSpec(a.shape, lambda i: (0, 0))
    return pl.pallas_call(
        functools.partial(_merge_kernel, tm=tm),
        grid=(T // tm,),
        in_specs=[
            pl.BlockSpec((tm, ATT_WIDTH), lambda i: (i, 0)),
            pl.BlockSpec((tm, SSM_D_INNER), lambda i: (i, 0)),
            pl.BlockSpec((tm, D_MODEL), lambda i: (i, COL_GA // D_MODEL)),
            pl.BlockSpec((tm, D_MODEL), lambda i: (i, COL_GS // D_MODEL)),
            pl.BlockSpec((tm, D_MODEL), lambda i: (i, 0)),
            full(wap), full(wsp), full(wo), full(fg), full(wr), full(br),
        ],
        out_specs=[
            pl.BlockSpec((tm, D_MODEL), lambda i: (i, 0)),
            pl.BlockSpec((tm * ROW_SUB, LANES), lambda i: (i, 0)),
            pl.BlockSpec((tm, LANES), lambda i: (i, 0)),
            pl.BlockSpec((tm, LANES), lambda i: (i, 0)),
            pl.BlockSpec((HALO, LANES), lambda i: (0, 0)),
        ],
        out_shape=[
            jax.ShapeDtypeStruct((T, D_MODEL), F32),
            jax.ShapeDtypeStruct((T * ROW_SUB, LANES), jnp.uint32),
            jax.ShapeDtypeStruct((T, LANES), jnp.int32),
            jax.ShapeDtypeStruct((T, LANES), F32),
            jax.ShapeDtypeStruct((HALO, LANES), jnp.int32),
        ],
        scratch_shapes=[pltpu.VMEM((1, LANES), F32)],
        compiler_params=_cparams(("arbitrary",)),
        name="merge_router",
    )(att, ssm, proj, proj, xf, wap, wsp, wo, fg, wr, br)


DMA_UNROLL = 8


def _dispatch_kernel(lb_ref, cnt_ref, dest_ref, h_ref, rows_ref, zeros, zsem, sem, *, tt, bm):
    @pl.when(pl.program_id(0) == 0)
    def _():
        zeros[...] = jnp.zeros_like(zeros)
        for wait in (False, True):
            for e in range(N_EXPERTS):
                @pl.when(cnt_ref[e] > 0)
                def _():
                    start = pl.multiple_of(lb_ref[e], bm)
                    cp = pltpu.make_async_copy(zeros, rows_ref.at[pl.ds(start * ROW_SUB, bm * ROW_SUB)], zsem)
                    cp.wait() if wait else cp.start()

    def row_copy(t, k):
        return pltpu.make_async_copy(_row(h_ref, t), _row(rows_ref, dest_ref[t * TOP_K + k]), sem)

    for wait in (False, True):
        def body(g, carry):
            for u in range(DMA_UNROLL):
                for k in range(TOP_K):
                    cp = row_copy(g * DMA_UNROLL + u, k)
                    cp.wait() if wait else cp.start(priority=k % 2)
            return carry
        lax.fori_loop(0, tt // DMA_UNROLL, body, 0)


def _dispatch(last_block, cnt, dest, h2p, n_rows, bm):
    T = h2p.shape[0] // ROW_SUB
    tt = min(512, T)
    grid_spec = pltpu.PrefetchScalarGridSpec(
        num_scalar_prefetch=2,
        grid=(T // tt,),
        in_specs=[
            pl.BlockSpec((tt * TOP_K,), lambda i, lb, c: (i,), memory_space=pltpu.SMEM),
            pl.BlockSpec((tt * ROW_SUB, LANES), lambda i, lb, c: (i, 0)),
        ],
        out_specs=pl.BlockSpec(memory_space=pl.ANY),
        scratch_shapes=[pltpu.VMEM((bm * ROW_SUB, LANES), jnp.uint32),
                        pltpu.SemaphoreType.DMA, pltpu.SemaphoreType.DMA],
    )
    return pl.pallas_call(
        functools.partial(_dispatch_kernel, tt=tt, bm=bm),
        grid_spec=grid_spec,
        out_shape=jax.ShapeDtypeStruct((n_rows * ROW_SUB, LANES), jnp.uint32),
        compiler_params=_cparams(("arbitrary",)),
        name="dispatch",
    )(last_block, cnt, dest, h2p)


def _expert_kernel(be_ref, nu_ref, x_ref, wgu_ref, bgu_ref, wd_ref, bd_ref, o_ref, wgu_bf, wd_bf):
    i = pl.program_id(0)

    @pl.when(i < nu_ref[0])
    def _():
        @pl.when((i == 0) | (be_ref[i] != be_ref[jnp.maximum(i - 1, 0)]))
        def _():
            wgu_bf[...] = wgu_ref[0].astype(BF16)
            wd_bf[...] = wd_ref[0].astype(BF16)

        lo, hi = _unpack_rows(_load_rows(x_ref))
        x = jnp.concatenate([lo, hi], axis=1).astype(BF16)
        gu = jnp.dot(x, wgu_bf[...], preferred_element_type=F32) + bgu_ref[0]
        gate = jnp.minimum(gu[:, :D_FF], SWIGLU_LIMIT)
        up = jnp.clip(gu[:, D_FF:], -SWIGLU_LIMIT, SWIGLU_LIMIT)
        glu = gate * jax.nn.sigmoid(SWIGLU_ALPHA * gate)
        act = ((up + 1.0) * glu).astype(BF16)
        _store_rows(o_ref, _pack_rows(jnp.dot(act, wd_bf[...], preferred_element_type=F32) + bd_ref[0]))


def _experts(block_expert, n_used, rows, wgu, bgu, wd, bd, bm):
    n_rows = rows.shape[0] // ROW_SUB
    nb = n_rows // bm
    row_map = lambda i, be, nu: (jnp.minimum(i, nu[0] - 1), 0)
    exp_map = lambda i, be, nu: (be[jnp.minimum(i, nu[0] - 1)], 0, 0)
    grid_spec = pltpu.PrefetchScalarGridSpec(
        num_scalar_prefetch=2,
        grid=(nb,),
        in_specs=[
            pl.BlockSpec((bm * ROW_SUB, LANES), row_map),
            pl.BlockSpec((1, D_MODEL, 2 * D_FF), exp_map),
            pl.BlockSpec((1, 1, 2 * D_FF), exp_map),
            pl.BlockSpec((1, D_FF, D_MODEL), exp_map),
            pl.BlockSpec((1, 1, D_MODEL), exp_map),
        ],
        out_specs=pl.BlockSpec((bm * ROW_SUB, LANES), row_map),
        scratch_shapes=[pltpu.VMEM((D_MODEL, 2 * D_FF), BF16), pltpu.VMEM((D_FF, D_MODEL), BF16)],
    )
    return pl.pallas_call(
        _expert_kernel,
        grid_spec=grid_spec,
        out_shape=jax.ShapeDtypeStruct((n_rows * ROW_SUB, LANES), jnp.uint32),
        compiler_params=_cparams(("arbitrary",)),
        name="experts",
    )(block_expert, n_used, rows, wgu, bgu, wd, bd)


def _combine_kernel(dest_ref, x1_ref, gate_ref, y_ref, o_ref, ybuf, sem, *, tc):
    def row_copy(t, k):
        return pltpu.make_async_copy(_row(y_ref, dest_ref[t * TOP_K + k]), _row(ybuf.at[k], t), sem)

    for wait in (False, True):
        def body(g, carry):
            for u in range(DMA_UNROLL):
                for k in range(TOP_K):
                    cp = row_copy(g * DMA_UNROLL + u, k)
                    cp.wait() if wait else cp.start(priority=k % 2)
            return carry
        lax.fori_loop(0, tc // DMA_UNROLL, body, 0)

    x1 = x1_ref[...]
    acc_lo, acc_hi = x1[:, :PACKED], x1[:, PACKED:]
    g = gate_ref[...]
    for k in range(TOP_K):
        lo, hi = _unpack_rows(_load_rows(ybuf.at[k]))
        acc_lo = acc_lo + g[:, k:k + 1] * lo
        acc_hi = acc_hi + g[:, k:k + 1] * hi
    o_ref[:, :PACKED] = acc_lo
    o_ref[:, PACKED:] = acc_hi


def _combine(dest, x1, gates, y_rows):
    T = x1.shape[0]
    tc = min(256, T)
    return pl.pallas_call(
        functools.partial(_combine_kernel, tc=tc),
        grid=(T // tc,),
        in_specs=[
            pl.BlockSpec((tc * TOP_K,), lambda i: (i,), memory_space=pltpu.SMEM),
            pl.BlockSpec((tc, D_MODEL), lambda i: (i, 0)),
            pl.BlockSpec((tc, LANES), lambda i: (i, 0)),
            pl.BlockSpec(memory_space=pl.ANY),
        ],
        out_specs=pl.BlockSpec((tc, D_MODEL), lambda i: (i, 0)),
        out_shape=jax.ShapeDtypeStruct((T, D_MODEL), F32),
        scratch_shapes=[pltpu.VMEM((TOP_K, tc * ROW_SUB, LANES), jnp.uint32), pltpu.SemaphoreType.DMA],
        compiler_params=_cparams(("arbitrary",)),
        name="combine",
    )(dest, x1, gates, y_rows)


def _pad_lanes(v):
    return jnp.pad(v, ((0, 0), (0, LANES - v.shape[-1])))


def kernel(x, mix_norm_g, w_in, q_norm_g, k_norm_g, lambda_q1, lambda_k1, lambda_q2, lambda_k2,
           attn_subln_g, conv_w, conv_b, dt_bias, a_log, d_skip, ssm_norm_g, w_attn_proj,
           w_ssm_proj, w_out, ffn_norm_g, w_router, b_router, w_gate_up, b_gate_up, w_down, b_down):
    B, S, D = x.shape
    T = B * S
    xf = x.reshape(T, D)
    layer = 0

    wi = w_in[layer]
    o_q, o_k, o_v = 0, ATT_WIDTH, 2 * ATT_WIDTH
    o_z = 3 * ATT_WIDTH
    o_xbc = o_z + SSM_D_INNER
    o_dt = o_xbc + SSM_XBC
    o_ga = o_dt + SSM_HEADS
    w_main = jnp.concatenate([wi[:, o_xbc:o_dt], wi[:, o_z:o_xbc], wi[:, o_q:o_z], wi[:, o_ga:]],
                             axis=1).astype(BF16)
    w_dt = _pad_lanes(wi[:, o_dt:o_ga]).astype(BF16)

    proj, dt_raw = _in_proj(xf, mix_norm_g[layer][None, :], w_main, w_dt)

    half = ATT_HEAD_DIM // 2
    inv = ROPE_THETA ** (-jnp.arange(0, ATT_HEAD_DIM, 2, dtype=F32) / ATT_HEAD_DIM)
    ang = jnp.arange(S, dtype=F32)[:, None] * inv[None, :]
    cos2 = jnp.tile(jnp.cos(ang), (1, LANES // half))
    sin2 = jnp.tile(jnp.concatenate([-jnp.sin(ang), jnp.sin(ang)], axis=1), (1, LANES // ATT_HEAD_DIM))
    qg2 = jnp.tile(q_norm_g[layer], 2)[None, :]
    kg2 = jnp.tile(k_norm_g[layer], 2)[None, :]
    partner = lambda g: jnp.tile(jnp.roll(g, half), 2)[None, :]
    q_scale = ATT_HEAD_DIM ** -0.5 * LOG2E
    att = _attention(proj, cos2 * qg2 * q_scale, sin2 * partner(q_norm_g[layer]) * q_scale,
                     cos2 * kg2, sin2 * partner(k_norm_g[layer]), qg2, kg2,
                     lambda_q1[layer][None, :], lambda_k1[layer][None, :],
                     lambda_q2[layer][None, :], lambda_k2[layer][None, :],
                     attn_subln_g[layer][None, :], B, S)

    ssm = _ssd(proj, dt_raw, conv_w[layer], conv_b[layer][None, :],
               _pad_lanes(dt_bias[layer][None, :]), _pad_lanes(a_log[layer][None, :]),
               jnp.repeat(d_skip[layer], SSM_HEAD_DIM)[None, :], ssm_norm_g[layer][None, :], B, S)

    x1, h2p, idx_rank, gates, counts = _merge(
        att, ssm, proj, xf, w_attn_proj[layer].astype(BF16), w_ssm_proj[layer].astype(BF16),
        w_out[layer].astype(BF16), ffn_norm_g[layer][None, :],
        _pad_lanes(w_router[layer]).astype(BF16), _pad_lanes(b_router[layer][None, :]))

    bm = 512
    A = T * TOP_K
    n_rows = (A + N_EXPERTS * (bm - 1)) // bm * bm
    cnt = counts[0, :N_EXPERTS]
    padded = (cnt + bm - 1) // bm * bm
    pend = jnp.cumsum(padded)
    pstart = pend - padded
    n_used = (pend[-1:] // bm).astype(jnp.int32)
    block_start = jnp.arange(n_rows // bm, dtype=jnp.int32) * bm
    block_expert = jnp.minimum(jnp.sum(block_start[:, None] >= pend[None, :], axis=1),
                               N_EXPERTS - 1).astype(jnp.int32)
    top_idx = idx_rank[:, :TOP_K]
    onehot = top_idx[:, :, None] == jnp.arange(N_EXPERTS, dtype=jnp.int32)[None, None, :]
    dest = (jnp.sum(jnp.where(onehot, pstart[None, None, :], 0), axis=-1)
            + idx_rank[:, TOP_K:2 * TOP_K]).reshape(A).astype(jnp.int32)

    rows = _dispatch((pend - bm).astype(jnp.int32), cnt.astype(jnp.int32), dest, h2p, n_rows, bm)
    y_rows = _experts(block_expert, n_used, rows, w_gate_up[layer], b_gate_up[layer][:, None, :],
                      w_down[layer], b_down[layer][:, None, :], bm)
    out = _combine(dest, x1, gates, y_rows)
    return out.reshape(B, S, D)
```

```python
import functools
import math

import jax
import jax.numpy as jnp
from jax import lax
from jax.experimental import pallas as pl
from jax.experimental.pallas import tpu as pltpu

F32 = jnp.float32
BF16 = jnp.bfloat16

D_MODEL = 1024
EPS = 1e-6
ATT_HEADS = 8
ATT_HEAD_DIM = 64
ATT_V_DIM = 2 * ATT_HEAD_DIM
ATT_WIDTH = ATT_HEADS * ATT_V_DIM
ROPE_THETA = 10000.0
SSM_D_INNER = 2 * D_MODEL
SSM_HEAD_DIM = 64
SSM_HEADS = SSM_D_INNER // SSM_HEAD_DIM
SSM_GROUPS = 8
SSM_STATE = 128
SSM_CONV = 4
SSM_CHUNK = 128
SSM_XBC = SSM_D_INNER + 2 * SSM_GROUPS * SSM_STATE
N_EXPERTS = 32
TOP_K = 4
D_FF = D_MODEL
SWIGLU_LIMIT = 7.0
SWIGLU_ALPHA = 1.702
LAM_INIT = 0.8 - 0.6 * math.exp(-0.3 * 0)

LANES = 128
HALO = 16

COL_XBC = 0
COL_Z = SSM_XBC
COL_Q = COL_Z + SSM_D_INNER
COL_K = COL_Q + ATT_WIDTH
COL_V = COL_K + ATT_WIDTH
COL_GA = COL_V + ATT_WIDTH
COL_GS = COL_GA + D_MODEL
PROJ_COLS = COL_GS + D_MODEL

VMEM_LIMIT = 56 * 1024 * 1024


def _cparams(sem):
    return pltpu.CompilerParams(dimension_semantics=sem, vmem_limit_bytes=VMEM_LIMIT)


PACKED = D_MODEL // 2


def _pack_rows(x):
    lo = lax.bitcast_convert_type(x[:, :PACKED].astype(BF16).astype(F32), jnp.uint32)
    hi = lax.bitcast_convert_type(x[:, PACKED:].astype(BF16).astype(F32), jnp.uint32)
    return hi | (lo >> 16)


def _unpack_rows(p):
    lo = lax.bitcast_convert_type(p << 16, F32)
    hi = lax.bitcast_convert_type(p & jnp.uint32(0xFFFF0000), F32)
    return lo, hi


ROW_SUB = PACKED // LANES


def _store_rows(ref, packed):
    for s in range(ROW_SUB):
        ref[pl.ds(s, packed.shape[0], stride=ROW_SUB), :] = packed[:, s * LANES:(s + 1) * LANES]


def _load_rows(ref):
    n = ref.shape[0] // ROW_SUB
    return jnp.concatenate([ref[pl.ds(s, n, stride=ROW_SUB), :] for s in range(ROW_SUB)], axis=1)


def _row(ref, r):
    return ref.at[pl.ds(pl.multiple_of(r * ROW_SUB, ROW_SUB), ROW_SUB)]


def _in_proj_kernel(x_ref, g_ref, w_ref, wdt_ref, o_ref, dt_ref, h_scr):
    @pl.when(pl.program_id(1) == 0)
    def _():
        x = x_ref[...]
        ms = jnp.mean(x * x, axis=-1, keepdims=True)
        hb = (x * lax.rsqrt(ms + EPS) * g_ref[...]).astype(BF16)
        h_scr[...] = hb
        dt_ref[...] = jnp.dot(hb, wdt_ref[...], preferred_element_type=F32)

    o_ref[...] = jnp.dot(h_scr[...], w_ref[...], preferred_element_type=F32).astype(BF16)


def _in_proj(xf, g, w_main, w_dt):
    T = xf.shape[0]
    tm = min(2048, T)
    tn = 1024
    return pl.pallas_call(
        _in_proj_kernel,
        grid=(T // tm, PROJ_COLS // tn),
        in_specs=[
            pl.BlockSpec((tm, D_MODEL), lambda i, j: (i, 0)),
            pl.BlockSpec((1, D_MODEL), lambda i, j: (0, 0)),
            pl.BlockSpec((D_MODEL, tn), lambda i, j: (0, j)),
            pl.BlockSpec((D_MODEL, LANES), lambda i, j: (0, 0)),
        ],
        out_specs=[
            pl.BlockSpec((tm, tn), lambda i, j: (i, j)),
            pl.BlockSpec((tm, LANES), lambda i, j: (i, 0)),
        ],
        out_shape=[
            jax.ShapeDtypeStruct((T, PROJ_COLS), BF16),
            jax.ShapeDtypeStruct((T, LANES), F32),
        ],
        scratch_shapes=[pltpu.VMEM((tm, D_MODEL), BF16)],
        compiler_params=_cparams(("arbitrary", "arbitrary")),
        name="in_proj",
    )(xf, g, w_main, w_dt)


LOG2E = 1.4426950408889634
SHIFT_LIMIT = 57.0
BOUND_MARGIN = 1.02


def _attn_kernel(q_ref, k_ref, v_ref, qc_ref, qs_ref, kc_ref, ks_ref, qg_ref, kg_ref,
                 lq1_ref, lk1_ref, lq2_ref, lk2_ref, sg_ref, o_ref, k_scr, v_scr, *, tq, seq):
    qi = pl.program_id(2)
    lane = lax.broadcasted_iota(jnp.int32, (1, LANES), 1)
    first = lane < ATT_HEAD_DIM
    one_col = jnp.where(lane == 0, 1.0, 0.0)

    ri = lax.broadcasted_iota(jnp.int32, (LANES, LANES), 0)
    ci = lax.broadcasted_iota(jnp.int32, (LANES, LANES), 1)
    same_comp = (ri // ATT_HEAD_DIM == ci // ATT_HEAD_DIM).astype(BF16)
    swap_half = ((ri // ATT_HEAD_DIM == ci // ATT_HEAD_DIM)
                 & ((ri - ci == ATT_HEAD_DIM // 2) | (ci - ri == ATT_HEAD_DIM // 2))).astype(BF16)

    def norm_rope(xb, gcos, gsin):
        xf = xb.astype(F32)
        sq = xf * xf
        hi = sq.astype(BF16)
        lo = (sq - hi.astype(F32)).astype(BF16)
        ms = (jnp.dot(hi, same_comp, preferred_element_type=F32)
              + jnp.dot(lo, same_comp, preferred_element_type=F32)) * (1.0 / ATT_HEAD_DIM)
        xr = jnp.dot(xb, swap_half, preferred_element_type=F32)
        return lax.rsqrt(ms + EPS) * (xf * gcos + xr * gsin)

    @pl.when(qi == 0)
    def _():
        ones = jnp.broadcast_to(one_col, (tq, LANES)).astype(BF16)
        for r in range(0, seq, tq):
            k_scr[r:r + tq, :LANES] = norm_rope(k_ref[r:r + tq, :], kc_ref[r:r + tq, :],
                                                ks_ref[r:r + tq, :]).astype(BF16)
            k_scr[r:r + tq, LANES:] = ones
            v_scr[r:r + tq, :LANES] = v_ref[r:r + tq, :]
            v_scr[r:r + tq, LANES:] = ones

    row0 = pl.multiple_of(qi * tq, tq)
    qb = norm_rope(q_ref[...], qc_ref[pl.ds(row0, tq), :], qs_ref[pl.ds(row0, tq), :]).astype(BF16)
    q1 = jnp.where(first, qb, jnp.zeros_like(qb))
    q2 = jnp.where(first, jnp.zeros_like(qb), qb)
    ub = (ATT_HEAD_DIM * ATT_HEAD_DIM ** -0.5 * LOG2E * BOUND_MARGIN
          * jnp.max(jnp.abs(qg_ref[...])) * jnp.max(jnp.abs(kg_ref[...])))
    safe = ub <= SHIFT_LIMIT

    keep = (lax.broadcasted_iota(jnp.int32, (tq, tq), 1) <= lax.broadcasted_iota(jnp.int32, (tq, tq), 0))
    nt = (((1,), (1,)), ((), ()))

    def finish(o1, o2):
        lam = (jnp.exp(jnp.sum(lq1_ref[...] * lk1_ref[...], axis=-1, keepdims=True))
               - jnp.exp(jnp.sum(lq2_ref[...] * lk2_ref[...], axis=-1, keepdims=True)) + LAM_INIT)
        o = o1 - lam * o2
        ms = jnp.mean(o * o, axis=-1, keepdims=True)
        o_ref[...] = (o * lax.rsqrt(ms + EPS) * sg_ref[...] * (1.0 - LAM_INIT)).astype(BF16)

    @pl.when(safe)
    def _():
        shift = jnp.broadcast_to(jnp.where(lane == 0, -ub, 0.0), (tq, LANES)).astype(BF16)
        q1x = jnp.concatenate([q1, shift], axis=1)
        q2x = jnp.concatenate([q2, shift], axis=1)

        def step(j, carry, masked):
            a1, a2 = carry
            c0 = pl.multiple_of(j * tq, tq)
            kt = k_scr[pl.ds(c0, tq), :]
            vt = v_scr[pl.ds(c0, tq), :]
            t1 = lax.dot_general(q1x, kt, nt, preferred_element_type=F32)
            t2 = lax.dot_general(q2x, kt, nt, preferred_element_type=F32)
            if masked:
                t1 = jnp.where(keep, t1, -jnp.inf)
                t2 = jnp.where(keep, t2, -jnp.inf)
            a1 = a1 + jnp.dot(jnp.exp2(t1).astype(BF16), vt, preferred_element_type=F32)
            a2 = a2 + jnp.dot(jnp.exp2(t2).astype(BF16), vt, preferred_element_type=F32)
            return a1, a2

        zacc = jnp.zeros((tq, 2 * LANES), F32)
        carry = step(qi, (zacc, zacc), True)
        a1, a2 = lax.fori_loop(0, qi, lambda j, c: step(j, c, False), carry)
        finish(a1[:, :LANES] / a1[:, LANES:LANES + 1], a2[:, :LANES] / a2[:, LANES:LANES + 1])

    @pl.when(jnp.logical_not(safe))
    def _():
        def update(t, vt, m, l, acc):
            m_new = jnp.maximum(m, jnp.max(t, axis=-1, keepdims=True))
            alpha = jnp.exp2(m - m_new)
            p = jnp.exp2(t - m_new)
            l_new = alpha * l + jnp.sum(p, axis=-1, keepdims=True)
            acc_new = alpha * acc + jnp.dot(p.astype(BF16), vt, preferred_element_type=F32)
            return m_new, l_new, acc_new

        def step(j, carry, masked):
            m1, l1, a1, m2, l2, a2 = carry
            c0 = pl.multiple_of(j * tq, tq)
            kt = k_scr[pl.ds(c0, tq), :LANES]
            vt = v_scr[pl.ds(c0, tq), :LANES]
            t1 = lax.dot_general(q1, kt, nt, preferred_element_type=F32)
            t2 = lax.dot_general(q2, kt, nt, preferred_element_type=F32)
            if masked:
                t1 = jnp.where(keep, t1, -jnp.inf)
                t2 = jnp.where(keep, t2, -jnp.inf)
            m1, l1, a1 = update(t1, vt, m1, l1, a1)
            m2, l2, a2 = update(t2, vt, m2, l2, a2)
            return m1, l1, a1, m2, l2, a2

        neg = jnp.full((tq, 1), -jnp.inf, F32)
        zero1 = jnp.zeros((tq, 1), F32)
        zacc = jnp.zeros((tq, ATT_V_DIM), F32)
        carry = step(qi, (neg, zero1, zacc, neg, zero1, zacc), True)
        m1, l1, a1, m2, l2, a2 = lax.fori_loop(0, qi, lambda j, c: step(j, c, False), carry)
        finish(a1 / l1, a2 / l2)


def _attention(proj, qcos, qsin, kcos, ksin, qg2, kg2, lq1, lk1, lq2, lk2, sg, batch, seq):
    T = batch * seq
    tq = min(512, seq)
    nq = seq // tq
    vec = lambda n: pl.BlockSpec((1, n), lambda b, h, i: (0, 0))
    table = pl.BlockSpec((seq, LANES), lambda b, h, i: (0, 0))
    return pl.pallas_call(
        functools.partial(_attn_kernel, tq=tq, seq=seq),
        grid=(batch, ATT_HEADS, nq),
        in_specs=[
            pl.BlockSpec((tq, LANES), lambda b, h, i: (b * nq + i, COL_Q // LANES + h)),
            pl.BlockSpec((seq, LANES), lambda b, h, i: (b, COL_K // LANES + h)),
            pl.BlockSpec((seq, LANES), lambda b, h, i: (b, COL_V // LANES + h)),
            table, table, table, table,
            vec(LANES), vec(LANES),
            vec(ATT_HEAD_DIM), vec(ATT_HEAD_DIM), vec(ATT_HEAD_DIM), vec(ATT_HEAD_DIM),
            vec(LANES),
        ],
        out_specs=pl.BlockSpec((tq, LANES), lambda b, h, i: (b * nq + i, h)),
        out_shape=jax.ShapeDtypeStruct((T, ATT_WIDTH), BF16),
        scratch_shapes=[pltpu.VMEM((seq, 2 * LANES), BF16), pltpu.VMEM((seq, 2 * LANES), BF16)],
        compiler_params=_cparams(("arbitrary", "arbitrary", "arbitrary")),
        name="attention",
    )(proj, proj, proj, qcos, qsin, kcos, ksin, qg2, kg2, lq1, lk1, lq2, lk2, sg)


def _ssd_kernel(xbc_ref, z_ref, dt_ref, cw_ref, cb_ref, dtb_ref, alog_ref, dsk_ref, ng_ref,
                o_ref, halo, state, xs_scr, bc_scr, y_scr):
    Q = SSM_CHUNK
    N = SSM_STATE
    P2 = 2 * SSM_HEAD_DIM

    @pl.when(pl.program_id(1) == 0)
    def _():
        halo[...] = jnp.zeros_like(halo)
        state[...] = jnp.zeros_like(state)

    taps = SSM_CONV - 1
    sr = lax.broadcasted_iota(jnp.int32, (taps * Q, HALO + Q), 0)
    sc = lax.broadcasted_iota(jnp.int32, (taps * Q, HALO + Q), 1)
    shift = (sc == (sr % Q) + HALO - taps + sr // Q).astype(BF16)
    cwid = 512
    for c0 in range(0, SSM_XBC, cwid):
        cur = xbc_ref[:, c0:c0 + cwid]
        ext = jnp.concatenate([halo[:, c0:c0 + cwid], cur], axis=0)
        shifted = jnp.dot(shift, ext, preferred_element_type=F32)
        acc = cb_ref[:, c0:c0 + cwid] + cw_ref[taps:SSM_CONV, c0:c0 + cwid] * cur.astype(F32)
        for k in range(taps):
            acc = acc + cw_ref[k:k + 1, c0:c0 + cwid] * shifted[k * Q:(k + 1) * Q, :]
        act = acc * jax.nn.sigmoid(acc)
        if c0 < SSM_D_INNER:
            xs_scr[:, c0:c0 + cwid] = act
        else:
            bc_scr[:, c0 - SSM_D_INNER:c0 - SSM_D_INNER + cwid] = act.astype(BF16)
        halo[:, c0:c0 + cwid] = cur[Q - HALO:Q, :]

    dt = jax.nn.softplus(dt_ref[...] + dtb_ref[...])
    da = dt * (-jnp.exp(alog_ref[...]))
    ri = lax.broadcasted_iota(jnp.int32, (Q, Q), 0)
    ci = lax.broadcasted_iota(jnp.int32, (Q, Q), 1)
    tri = ri >= ci
    acs = jnp.dot(tri.astype(F32), da, preferred_element_type=F32,
                  precision=lax.Precision.HIGHEST)
    acs_t = acs.T
    dt_t = dt.T
    w_t = dt_t * jnp.exp(acs_t[:, Q - 1:Q] - acs_t)
    lane = lax.broadcasted_iota(jnp.int32, (1, P2), 1)
    left = lane < SSM_HEAD_DIM

    for g in range(SSM_GROUPS):
        bm = bc_scr[:, g * N:(g + 1) * N]
        cm = bc_scr[:, SSM_GROUPS * N + g * N:SSM_GROUPS * N + (g + 1) * N]
        cb = lax.dot_general(cm, bm, (((1,), (1,)), ((), ())), preferred_element_type=F32)
        bm_t = bm.astype(F32).T
        cm_f = cm.astype(F32)
        for pr in range(2):
            pair = 2 * g + pr
            xs_pair = xs_scr[:, pair * P2:(pair + 1) * P2].astype(BF16)
            prev = state[pair]
            rhs = jnp.concatenate([xs_pair, prev.astype(BF16)], axis=0)
            ys, sts, decs = [], [], []
            for r in range(2):
                h = 2 * pair + r
                a_col = acs[:, h:h + 1]
                seg = a_col - acs_t[h:h + 1, :]
                decay = jnp.exp(jnp.where(tri, seg, -jnp.inf))
                m_h = cb * decay * dt_t[h:h + 1, :]
                e_h = cm_f * jnp.exp(a_col)
                lhs = jnp.concatenate([m_h, e_h], axis=1).astype(BF16)
                ys.append(jnp.dot(lhs, rhs, preferred_element_type=F32))
                sts.append(jnp.dot((bm_t * w_t[h:h + 1, :]).astype(BF16), xs_pair,
                                   preferred_element_type=F32))
                decs.append(jnp.exp(acs[Q - 1:Q, h:h + 1]))
            y_scr[:, pair * P2:(pair + 1) * P2] = jnp.where(left, ys[0], ys[1])
            dec = jnp.where(left, decs[0], decs[1])
            state[pair] = dec * prev + jnp.where(left, sts[0], sts[1])

    y = y_scr[...] + dsk_ref[...] * xs_scr[...]
    zf = z_ref[...].astype(F32)
    y = y * (zf * jax.nn.sigmoid(zf))
    gw = SSM_D_INNER // SSM_GROUPS
    for g in range(SSM_GROUPS):
        yg = y[:, g * gw:(g + 1) * gw]
        ms = jnp.mean(yg * yg, axis=-1, keepdims=True)
        o_ref[:, g * gw:(g + 1) * gw] = (yg * lax.rsqrt(ms + EPS)
                                         * ng_ref[:, g * gw:(g + 1) * gw]).astype(BF16)


def _ssd(proj, dt_raw, conv_w, conv_b, dt_bias, a_log, dsk, ng, batch, seq):
    T = batch * seq
    Q = SSM_CHUNK
    nc = seq // Q
    vec = lambda r, n: pl.BlockSpec((r, n), lambda b, c: (0, 0))
    return pl.pallas_call(
        _ssd_kernel,
        grid=(batch, nc),
        in_specs=[
            pl.BlockSpec((Q, SSM_XBC), lambda b, c: (b * nc + c, COL_XBC // SSM_XBC)),
            pl.BlockSpec((Q, SSM_D_INNER), lambda b, c: (b * nc + c, COL_Z // SSM_D_INNER)),
            pl.BlockSpec((Q, LANES), lambda b, c: (b * nc + c, 0)),
            vec(SSM_CONV, SSM_XBC), vec(1, SSM_XBC), vec(1, LANES), vec(1, LANES),
            vec(1, SSM_D_INNER), vec(1, SSM_D_INNER),
        ],
        out_specs=pl.BlockSpec((Q, SSM_D_INNER), lambda b, c: (b * nc + c, 0)),
        out_shape=jax.ShapeDtypeStruct((T, SSM_D_INNER), BF16),
        scratch_shapes=[
            pltpu.VMEM((HALO, SSM_XBC), BF16),
            pltpu.VMEM((SSM_HEADS // 2, SSM_STATE, 2 * SSM_HEAD_DIM), F32),
            pltpu.VMEM((Q, SSM_D_INNER), F32),
            pltpu.VMEM((Q, 2 * SSM_GROUPS * SSM_STATE), BF16),
            pltpu.VMEM((Q, SSM_D_INNER), F32),
        ],
        compiler_params=_cparams(("arbitrary", "arbitrary")),
        name="ssd",
    )(proj, proj, dt_raw, conv_w, conv_b, dt_bias, a_log, dsk, ng)


def _merge_kernel(att_ref, ssm_ref, ga_ref, gs_ref, x_ref, wap_ref, wsp_ref, wo_ref, fg_ref,
                  wr_ref, br_ref, x1_ref, h2_ref, idx_ref, gate_ref, cnt_ref, base, *, tm):
    i = pl.program_id(0)

    @pl.when(i == 0)
    def _():
        base[...] = jnp.zeros_like(base)

    pa = jnp.dot(att_ref[...], wap_ref[...], preferred_element_type=F32)
    ps = jnp.dot(ssm_ref[...], wsp_ref[...], preferred_element_type=F32)
    merged = (jax.nn.sigmoid(ga_ref[...].astype(F32)) * pa
              + jax.nn.sigmoid(gs_ref[...].astype(F32)) * ps)
    x1 = x_ref[...] + jnp.dot(merged.astype(BF16), wo_ref[...], preferred_element_type=F32)
    x1_ref[...] = x1
    ms = jnp.mean(x1 * x1, axis=-1, keepdims=True)
    h2 = x1 * lax.rsqrt(ms + EPS) * fg_ref[...]
    _store_rows(h2_ref, _pack_rows(h2))

    lane = lax.broadcasted_iota(jnp.int32, (1, LANES), 1)
    logits = jnp.dot(h2.astype(BF16), wr_ref[...], preferred_element_type=F32) + br_ref[...]
    lg = jnp.where(lane < N_EXPERTS, logits, -jnp.inf)
    lane_f = lane.astype(F32)
    vals, idxs, sels = [], [], []
    for _ in range(TOP_K):
        m = jnp.max(lg, axis=-1, keepdims=True)
        idx = jnp.min(jnp.where(lg == m, lane_f, float(LANES)), axis=-1, keepdims=True)
        sel = lane_f == idx
        vals.append(m)
        idxs.append(idx)
        sels.append(sel)
        lg = jnp.where(sel, -jnp.inf, lg)
    es = [jnp.exp(v - vals[0]) for v in vals]
    den = es[0] + es[1] + es[2] + es[3]

    multi = jnp.zeros((tm, LANES), F32)
    for sel in sels:
        multi = jnp.where(sel, 1.0, multi)
    ri = lax.broadcasted_iota(jnp.int32, (tm, tm), 0)
    ci = lax.broadcasted_iota(jnp.int32, (tm, tm), 1)
    before = jnp.dot((ri > ci).astype(BF16), multi.astype(BF16), preferred_element_type=F32)
    before = before + base[...]
    idx_out = jnp.zeros((tm, LANES), F32)
    gate_out = jnp.zeros((tm, LANES), F32)
    for k in range(TOP_K):
        rank = jnp.sum(jnp.where(sels[k], before, 0.0), axis=-1, keepdims=True)
        idx_out = jnp.where(lane == k, idxs[k], idx_out)
        idx_out = jnp.where(lane == TOP_K + k, rank, idx_out)
        gate_out = jnp.where(lane == k, es[k] / den, gate_out)
    idx_ref[...] = idx_out.astype(jnp.int32)
    gate_ref[...] = gate_out
    base[...] = base[...] + jnp.sum(multi, axis=0, keepdims=True)
    cnt_ref[...] = jnp.broadcast_to(base[...], cnt_ref.shape).astype(jnp.int32)


def _merge(att, ssm, proj, xf, wap, wsp, wo, fg, wr, br):
    T = xf.shape[0]
    tm = min(512, T)
    full = lambda a: pl.BlockSpec(a.shape, lambda i: (0, 0))
    return pl.pallas_call(
        functools.partial(_merge_kernel, tm=tm),
        grid=(T // tm,),
        in_specs=[
            pl.BlockSpec((tm, ATT_WIDTH), lambda i: (i, 0)),
            pl.BlockSpec((tm, SSM_D_INNER), lambda i: (i, 0)),
            pl.BlockSpec((tm, D_MODEL), lambda i: (i, COL_GA // D_MODEL)),
            pl.BlockSpec((tm, D_MODEL), lambda i: (i, COL_GS // D_MODEL)),
            pl.BlockSpec((tm, D_MODEL), lambda i: (i, 0)),
            full(wap), full(wsp), full(wo), full(fg), full(wr), full(br),
        ],
        out_specs=[
            pl.BlockSpec((tm, D_MODEL), lambda i: (i, 0)),
            pl.BlockSpec((tm * ROW_SUB, LANES), lambda i: (i, 0)),
            pl.BlockSpec((tm, LANES), lambda i: (i, 0)),
            pl.BlockSpec((tm, LANES), lambda i: (i, 0)),
            pl.BlockSpec((HALO, LANES), lambda i: (0, 0)),
        ],
        out_shape=[
            jax.ShapeDtypeStruct((T, D_MODEL), F32),
            jax.ShapeDtypeStruct((T * ROW_SUB, LANES), jnp.uint32),
            jax.ShapeDtypeStruct((T, LANES), jnp.int32),
            jax.ShapeDtypeStruct((T, LANES), F32),
            jax.ShapeDtypeStruct((HALO, LANES), jnp.int32),
        ],
        scratch_shapes=[pltpu.VMEM((1, LANES), F32)],
        compiler_params=_cparams(("arbitrary",)),
        name="merge_router",
    )(att, ssm, proj, proj, xf, wap, wsp, wo, fg, wr, br)


DMA_UNROLL = 8


def _dispatch_kernel(lb_ref, cnt_ref, dest_ref, h_ref, rows_ref, zeros, zsem, sem, *, tt, bm):
    @pl.when(pl.program_id(0) == 0)
    def _():
        zeros[...] = jnp.zeros_like(zeros)
        for wait in (False, True):
            for e in range(N_EXPERTS):
                @pl.when(cnt_ref[e] > 0)
                def _():
                    start = pl.multiple_of(lb_ref[e], bm)
                    cp = pltpu.make_async_copy(zeros, rows_ref.at[pl.ds(start * ROW_SUB, bm * ROW_SUB)], zsem)
                    cp.wait() if wait else cp.start()

    def row_copy(t, k):
        return pltpu.make_async_copy(_row(h_ref, t), _row(rows_ref, dest_ref[t * TOP_K + k]), sem)

    for wait in (False, True):
        def body(g, carry):
            for u in range(DMA_UNROLL):
                for k in range(TOP_K):
                    cp = row_copy(g * DMA_UNROLL + u, k)
                    cp.wait() if wait else cp.start(priority=k % 2)
            return carry
        lax.fori_loop(0, tt // DMA_UNROLL, body, 0)


def _dispatch(last_block, cnt, dest, h2p, n_rows, bm):
    T = h2p.shape[0] // ROW_SUB
    tt = min(512, T)
    grid_spec = pltpu.PrefetchScalarGridSpec(
        num_scalar_prefetch=2,
        grid=(T // tt,),
        in_specs=[
            pl.BlockSpec((tt * TOP_K,), lambda i, lb, c: (i,), memory_space=pltpu.SMEM),
            pl.BlockSpec((tt * ROW_SUB, LANES), lambda i, lb, c: (i, 0)),
        ],
        out_specs=pl.BlockSpec(memory_space=pl.ANY),
        scratch_shapes=[pltpu.VMEM((bm * ROW_SUB, LANES), jnp.uint32),
                        pltpu.SemaphoreType.DMA, pltpu.SemaphoreType.DMA],
    )
    return pl.pallas_call(
        functools.partial(_dispatch_kernel, tt=tt, bm=bm),
        grid_spec=grid_spec,
        out_shape=jax.ShapeDtypeStruct((n_rows * ROW_SUB, LANES), jnp.uint32),
        compiler_params=_cparams(("arbitrary",)),
        name="dispatch",
    )(last_block, cnt, dest, h2p)


def _expert_kernel(be_ref, nu_ref, x_ref, wgu_ref, bgu_ref, wd_ref, bd_ref, o_ref, wgu_bf, wd_bf):
    i = pl.program_id(0)

    @pl.when(i < nu_ref[0])
    def _():
        @pl.when((i == 0) | (be_ref[i] != be_ref[jnp.maximum(i - 1, 0)]))
        def _():
            wgu_bf[...] = wgu_ref[0].astype(BF16)
            wd_bf[...] = wd_ref[0].astype(BF16)

        lo, hi = _unpack_rows(_load_rows(x_ref))
        x = jnp.concatenate([lo, hi], axis=1).astype(BF16)
        gu = jnp.dot(x, wgu_bf[...], preferred_element_type=F32) + bgu_ref[0]
        gate = jnp.minimum(gu[:, :D_FF], SWIGLU_LIMIT)
        up = jnp.clip(gu[:, D_FF:], -SWIGLU_LIMIT, SWIGLU_LIMIT)
        glu = gate * jax.nn.sigmoid(SWIGLU_ALPHA * gate)
        act = ((up + 1.0) * glu).astype(BF16)
        _store_rows(o_ref, _pack_rows(jnp.dot(act, wd_bf[...], preferred_element_type=F32) + bd_ref[0]))


def _experts(block_expert, n_used, rows, wgu, bgu, wd, bd, bm):
    n_rows = rows.shape[0] // ROW_SUB
    nb = n_rows // bm
    row_map = lambda i, be, nu: (jnp.minimum(i, nu[0] - 1), 0)
    exp_map = lambda i, be, nu: (be[jnp.minimum(i, nu[0] - 1)], 0, 0)
    grid_spec = pltpu.PrefetchScalarGridSpec(
        num_scalar_prefetch=2,
        grid=(nb,),
        in_specs=[
            pl.BlockSpec((bm * ROW_SUB, LANES), row_map),
            pl.BlockSpec((1, D_MODEL, 2 * D_FF), exp_map),
            pl.BlockSpec((1, 1, 2 * D_FF), exp_map),
            pl.BlockSpec((1, D_FF, D_MODEL), exp_map),
            pl.BlockSpec((1, 1, D_MODEL), exp_map),
        ],
        out_specs=pl.BlockSpec((bm * ROW_SUB, LANES), row_map),
        scratch_shapes=[pltpu.VMEM((D_MODEL, 2 * D_FF), BF16), pltpu.VMEM((D_FF, D_MODEL), BF16)],
    )
    return pl.pallas_call(
        _expert_kernel,
        grid_spec=grid_spec,
        out_shape=jax.ShapeDtypeStruct((n_rows * ROW_SUB, LANES), jnp.uint32),
        compiler_params=_cparams(("arbitrary",)),
        name="experts",
    )(block_expert, n_used, rows, wgu, bgu, wd, bd)


def _combine_kernel(dest_ref, next_ref, x1_ref, gate_ref, y_ref, o_ref, ybuf, sems, *, tc):
    i = pl.program_id(0)
    n = pl.num_programs(0)

    def gather(idx_ref, slot, wait):
        def row_copy(t, k):
            return pltpu.make_async_copy(_row(y_ref, idx_ref[t * TOP_K + k]), _row(ybuf.at[slot, k], t),
                                         sems.at[slot])

        def body(g, carry):
            for u in range(DMA_UNROLL):
                for k in range(TOP_K):
                    cp = row_copy(g * DMA_UNROLL + u, k)
                    cp.wait() if wait else cp.start(priority=k % 2)
            return carry
        lax.fori_loop(0, tc // DMA_UNROLL, body, 0)

    slot = i % 2

    @pl.when(i == 0)
    def _():
        gather(dest_ref, 0, False)

    @pl.when(i + 1 < n)
    def _():
        gather(next_ref, 1 - slot, False)

    gather(dest_ref, slot, True)

    x1 = x1_ref[...]
    acc_lo, acc_hi = x1[:, :PACKED], x1[:, PACKED:]
    g = gate_ref[...]
    for k in range(TOP_K):
        lo, hi = _unpack_rows(_load_rows(ybuf.at[slot, k]))
        acc_lo = acc_lo + g[:, k:k + 1] * lo
        acc_hi = acc_hi + g[:, k:k + 1] * hi
    o_ref[:, :PACKED] = acc_lo
    o_ref[:, PACKED:] = acc_hi


def _combine(dest, x1, gates, y_rows):
    T = x1.shape[0]
    tc = min(256, T)
    n = T // tc
    return pl.pallas_call(
        functools.partial(_combine_kernel, tc=tc),
        grid=(n,),
        in_specs=[
            pl.BlockSpec((tc * TOP_K,), lambda i: (i,), memory_space=pltpu.SMEM),
            pl.BlockSpec((tc * TOP_K,), lambda i: (jnp.minimum(i + 1, n - 1),), memory_space=pltpu.SMEM),
            pl.BlockSpec((tc, D_MODEL), lambda i: (i, 0)),
            pl.BlockSpec((tc, LANES), lambda i: (i, 0)),
            pl.BlockSpec(memory_space=pl.ANY),
        ],
        out_specs=pl.BlockSpec((tc, D_MODEL), lambda i: (i, 0)),
        out_shape=jax.ShapeDtypeStruct((T, D_MODEL), F32),
        scratch_shapes=[pltpu.VMEM((2, TOP_K, tc * ROW_SUB, LANES), jnp.uint32),
                        pltpu.SemaphoreType.DMA((2,))],
        compiler_params=_cparams(("arbitrary",)),
        name="combine",
    )(dest, dest, x1, gates, y_rows)


def _pad_lanes(v):
    return jnp.pad(v, ((0, 0), (0, LANES - v.shape[-1])))


def kernel(x, mix_norm_g, w_in, q_norm_g, k_norm_g, lambda_q1, lambda_k1, lambda_q2, lambda_k2,
           attn_subln_g, conv_w, conv_b, dt_bias, a_log, d_skip, ssm_norm_g, w_attn_proj,
           w_ssm_proj, w_out, ffn_norm_g, w_router, b_router, w_gate_up, b_gate_up, w_down, b_down):
    B, S, D = x.shape
    T = B * S
    xf = x.reshape(T, D)
    layer = 0

    wi = w_in[layer]
    o_q, o_k, o_v = 0, ATT_WIDTH, 2 * ATT_WIDTH
    o_z = 3 * ATT_WIDTH
    o_xbc = o_z + SSM_D_INNER
    o_dt = o_xbc + SSM_XBC
    o_ga = o_dt + SSM_HEADS
    w_main = jnp.concatenate([wi[:, o_xbc:o_dt], wi[:, o_z:o_xbc], wi[:, o_q:o_z], wi[:, o_ga:]],
                             axis=1).astype(BF16)
    w_dt = _pad_lanes(wi[:, o_dt:o_ga]).astype(BF16)

    proj, dt_raw = _in_proj(xf, mix_norm_g[layer][None, :], w_main, w_dt)

    half = ATT_HEAD_DIM // 2
    inv = ROPE_THETA ** (-jnp.arange(0, ATT_HEAD_DIM, 2, dtype=F32) / ATT_HEAD_DIM)
    ang = jnp.arange(S, dtype=F32)[:, None] * inv[None, :]
    cos2 = jnp.tile(jnp.cos(ang), (1, LANES // half))
    sin2 = jnp.tile(jnp.concatenate([-jnp.sin(ang), jnp.sin(ang)], axis=1), (1, LANES // ATT_HEAD_DIM))
    qg2 = jnp.tile(q_norm_g[layer], 2)[None, :]
    kg2 = jnp.tile(k_norm_g[layer], 2)[None, :]
    partner = lambda g: jnp.tile(jnp.roll(g, half), 2)[None, :]
    q_scale = ATT_HEAD_DIM ** -0.5 * LOG2E
    att = _attention(proj, cos2 * qg2 * q_scale, sin2 * partner(q_norm_g[layer]) * q_scale,
                     cos2 * kg2, sin2 * partner(k_norm_g[layer]), qg2, kg2,
                     lambda_q1[layer][None, :], lambda_k1[layer][None, :],
                     lambda_q2[layer][None, :], lambda_k2[layer][None, :],
                     attn_subln_g[layer][None, :], B, S)

    ssm = _ssd(proj, dt_raw, conv_w[layer], conv_b[layer][None, :],
               _pad_lanes(dt_bias[layer][None, :]), _pad_lanes(a_log[layer][None, :]),
               jnp.repeat(d_skip[layer], SSM_HEAD_DIM)[None, :], ssm_norm_g[layer][None, :], B, S)

    x1, h2p, idx_rank, gates, counts = _merge(
        att, ssm, proj, xf, w_attn_proj[layer].astype(BF16), w_ssm_proj[layer].astype(BF16),
        w_out[layer].astype(BF16), ffn_norm_g[layer][None, :],
        _pad_lanes(w_router[layer]).astype(BF16), _pad_lanes(b_router[layer][None, :]))

    bm = 512
    A = T * TOP_K
    n_rows = (A + N_EXPERTS * (bm - 1)) // bm * bm
    cnt = counts[0, :N_EXPERTS]
    padded = (cnt + bm - 1) // bm * bm
    pend = jnp.cumsum(padded)
    pstart = pend - padded
    n_used = (pend[-1:] // bm).astype(jnp.int32)
    block_start = jnp.arange(n_rows // bm, dtype=jnp.int32) * bm
    block_expert = jnp.minimum(jnp.sum(block_start[:, None] >= pend[None, :], axis=1),
                               N_EXPERTS - 1).astype(jnp.int32)
    top_idx = idx_rank[:, :TOP_K]
    onehot = top_idx[:, :, None] == jnp.arange(N_EXPERTS, dtype=jnp.int32)[None, None, :]
    dest = (jnp.sum(jnp.where(onehot, pstart[None, None, :], 0), axis=-1)
            + idx_rank[:, TOP_K:2 * TOP_K]).reshape(A).astype(jnp.int32)

    rows = _dispatch((pend - bm).astype(jnp.int32), cnt.astype(jnp.int32), dest, h2p, n_rows, bm)
    y_rows = _experts(block_expert, n_used, rows, w_gate_up[layer], b_gate_up[layer][:, None, :],
                      w_down[layer], b_down[layer][:, None, :], bm)
    out = _combine(dest, x1, gates, y_rows)
    return out.reshape(B, S, D)
```

```python
import functools
import math

import jax
import jax.numpy as jnp
from jax import lax
from jax.experimental import pallas as pl
from jax.experimental.pallas import tpu as pltpu

F32 = jnp.float32
BF16 = jnp.bfloat16

D_MODEL = 1024
EPS = 1e-6
ATT_HEADS = 8
ATT_HEAD_DIM = 64
ATT_V_DIM = 2 * ATT_HEAD_DIM
ATT_WIDTH = ATT_HEADS * ATT_V_DIM
ROPE_THETA = 10000.0
SSM_D_INNER = 2 * D_MODEL
SSM_HEAD_DIM = 64
SSM_HEADS = SSM_D_INNER // SSM_HEAD_DIM
SSM_GROUPS = 8
SSM_STATE = 128
SSM_CONV = 4
SSM_CHUNK = 128
SSM_XBC = SSM_D_INNER + 2 * SSM_GROUPS * SSM_STATE
N_EXPERTS = 32
TOP_K = 4
D_FF = D_MODEL
SWIGLU_LIMIT = 7.0
SWIGLU_ALPHA = 1.702
LAM_INIT = 0.8 - 0.6 * math.exp(-0.3 * 0)

LANES = 128
HALO = 16

COL_XBC = 0
COL_Z = SSM_XBC
COL_Q = COL_Z + SSM_D_INNER
COL_K = COL_Q + ATT_WIDTH
COL_V = COL_K + ATT_WIDTH
COL_GA = COL_V + ATT_WIDTH
COL_GS = COL_GA + D_MODEL
PROJ_COLS = COL_GS + D_MODEL

VMEM_LIMIT = 56 * 1024 * 1024


def _cparams(sem):
    return pltpu.CompilerParams(dimension_semantics=sem, vmem_limit_bytes=VMEM_LIMIT)


PACKED = D_MODEL // 2


def _pack_rows(x):
    lo = lax.bitcast_convert_type(x[:, :PACKED].astype(BF16).astype(F32), jnp.uint32)
    hi = lax.bitcast_convert_type(x[:, PACKED:].astype(BF16).astype(F32), jnp.uint32)
    return hi | (lo >> 16)


def _unpack_rows(p):
    lo = lax.bitcast_convert_type(p << 16, F32)
    hi = lax.bitcast_convert_type(p & jnp.uint32(0xFFFF0000), F32)
    return lo, hi


ROW_SUB = PACKED // LANES


def _store_rows(ref, packed):
    for s in range(ROW_SUB):
        ref[pl.ds(s, packed.shape[0], stride=ROW_SUB), :] = packed[:, s * LANES:(s + 1) * LANES]


def _load_rows(ref):
    n = ref.shape[0] // ROW_SUB
    return jnp.concatenate([ref[pl.ds(s, n, stride=ROW_SUB), :] for s in range(ROW_SUB)], axis=1)


def _row(ref, r):
    return ref.at[pl.ds(pl.multiple_of(r * ROW_SUB, ROW_SUB), ROW_SUB)]


def _in_proj_kernel(x_ref, g_ref, w_ref, wdt_ref, o_ref, dt_ref, h_scr):
    @pl.when(pl.program_id(1) == 0)
    def _():
        x = x_ref[...]
        ms = jnp.mean(x * x, axis=-1, keepdims=True)
        hb = (x * lax.rsqrt(ms + EPS) * g_ref[...]).astype(BF16)
        h_scr[...] = hb
        dt_ref[...] = jnp.dot(hb, wdt_ref[...], preferred_element_type=F32)

    o_ref[...] = jnp.dot(h_scr[...], w_ref[...], preferred_element_type=F32).astype(BF16)


def _in_proj(xf, g, w_main, w_dt):
    T = xf.shape[0]
    tm = min(2048, T)
    tn = 1024
    return pl.pallas_call(
        _in_proj_kernel,
        grid=(T // tm, PROJ_COLS // tn),
        in_specs=[
            pl.BlockSpec((tm, D_MODEL), lambda i, j: (i, 0)),
            pl.BlockSpec((1, D_MODEL), lambda i, j: (0, 0)),
            pl.BlockSpec((D_MODEL, tn), lambda i, j: (0, j)),
            pl.BlockSpec((D_MODEL, LANES), lambda i, j: (0, 0)),
        ],
        out_specs=[
            pl.BlockSpec((tm, tn), lambda i, j: (i, j)),
            pl.BlockSpec((tm, LANES), lambda i, j: (i, 0)),
        ],
        out_shape=[
            jax.ShapeDtypeStruct((T, PROJ_COLS), BF16),
            jax.ShapeDtypeStruct((T, LANES), F32),
        ],
        scratch_shapes=[pltpu.VMEM((tm, D_MODEL), BF16)],
        compiler_params=_cparams(("arbitrary", "arbitrary")),
        name="in_proj",
    )(xf, g, w_main, w_dt)


LOG2E = 1.4426950408889634
SHIFT_LIMIT = 57.0
BOUND_MARGIN = 1.02


def _attn_kernel(q_ref, k_ref, v_ref, qc_ref, qs_ref, kc_ref, ks_ref, qg_ref, kg_ref,
                 lq1_ref, lk1_ref, lq2_ref, lk2_ref, sg_ref, o_ref, k_scr, v_scr, q1_scr, q2_scr, *, tq, seq):
    p = pl.program_id(2)
    nq = seq // tq
    paired = nq > 1
    lane = lax.broadcasted_iota(jnp.int32, (1, LANES), 1)
    first = lane < ATT_HEAD_DIM
    one_col = jnp.where(lane == 0, 1.0, 0.0)

    ri = lax.broadcasted_iota(jnp.int32, (LANES, LANES), 0)
    ci = lax.broadcasted_iota(jnp.int32, (LANES, LANES), 1)
    same_comp = (ri // ATT_HEAD_DIM == ci // ATT_HEAD_DIM).astype(BF16)
    swap_half = ((ri // ATT_HEAD_DIM == ci // ATT_HEAD_DIM)
                 & ((ri - ci == ATT_HEAD_DIM // 2) | (ci - ri == ATT_HEAD_DIM // 2))).astype(BF16)

    def norm_rope(xb, gcos, gsin):
        xf = xb.astype(F32)
        sq = xf * xf
        hi = sq.astype(BF16)
        lo = (sq - hi.astype(F32)).astype(BF16)
        ms = (jnp.dot(hi, same_comp, preferred_element_type=F32)
              + jnp.dot(lo, same_comp, preferred_element_type=F32)) * (1.0 / ATT_HEAD_DIM)
        xr = jnp.dot(xb, swap_half, preferred_element_type=F32)
        return lax.rsqrt(ms + EPS) * (xf * gcos + xr * gsin)

    ub = (ATT_HEAD_DIM * ATT_HEAD_DIM ** -0.5 * LOG2E * BOUND_MARGIN
          * jnp.max(jnp.abs(qg_ref[...])) * jnp.max(jnp.abs(kg_ref[...])))
    safe = ub <= SHIFT_LIMIT

    @pl.when(p == 0)
    def _():
        ones = jnp.broadcast_to(one_col, (tq, LANES)).astype(BF16)
        shift = jnp.broadcast_to(jnp.where(lane == 0, -ub, 0.0), (tq, LANES)).astype(BF16)
        for r in range(0, seq, tq):
            rows = slice(r, r + tq)
            k_scr[rows, :LANES] = norm_rope(k_ref[rows, :], kc_ref[rows, :], ks_ref[rows, :]).astype(BF16)
            k_scr[rows, LANES:] = ones
            v_scr[rows, :LANES] = v_ref[rows, :]
            v_scr[rows, LANES:] = ones
            qb = norm_rope(q_ref[rows, :], qc_ref[rows, :], qs_ref[rows, :]).astype(BF16)
            q1_scr[rows, :LANES] = jnp.where(first, qb, jnp.zeros_like(qb))
            q1_scr[rows, LANES:] = shift
            q2_scr[rows, :LANES] = jnp.where(first, jnp.zeros_like(qb), qb)
            q2_scr[rows, LANES:] = shift

    keep = (lax.broadcasted_iota(jnp.int32, (tq, tq), 1) <= lax.broadcasted_iota(jnp.int32, (tq, tq), 0))
    nt = (((1,), (1,)), ((), ()))
    rows_a = pl.multiple_of(p * tq, tq)
    rows_b = pl.multiple_of((nq - 1 - p) * tq, tq)

    def finish(o1, o2, rows0):
        lam = (jnp.exp(jnp.sum(lq1_ref[...] * lk1_ref[...], axis=-1, keepdims=True))
               - jnp.exp(jnp.sum(lq2_ref[...] * lk2_ref[...], axis=-1, keepdims=True)) + LAM_INIT)
        o = o1 - lam * o2
        ms = jnp.mean(o * o, axis=-1, keepdims=True)
        o_ref[pl.ds(rows0, tq), :] = (o * lax.rsqrt(ms + EPS) * sg_ref[...] * (1.0 - LAM_INIT)).astype(BF16)

    @pl.when(safe)
    def _():
        def tile(qrow0, krow0, masked):
            kt = k_scr[pl.ds(krow0, tq), :]
            vt = v_scr[pl.ds(krow0, tq), :]
            t1 = lax.dot_general(q1_scr[pl.ds(qrow0, tq), :], kt, nt, preferred_element_type=F32)
            t2 = lax.dot_general(q2_scr[pl.ds(qrow0, tq), :], kt, nt, preferred_element_type=F32)
            if masked:
                t1 = jnp.where(keep, t1, -jnp.inf)
                t2 = jnp.where(keep, t2, -jnp.inf)
            return (jnp.dot(jnp.exp2(t1).astype(BF16), vt, preferred_element_type=F32),
                    jnp.dot(jnp.exp2(t2).astype(BF16), vt, preferred_element_type=F32))

        def normalised(acc):
            return acc[:, :LANES] / acc[:, LANES:LANES + 1]

        a1, a2 = tile(rows_a, rows_a, True)
        if paired:
            b1, b2 = tile(rows_b, rows_b, True)
            for s in range(nq - 1):
                to_a = s < p
                qrow0 = jnp.where(to_a, rows_a, rows_b)
                krow0 = pl.multiple_of(jnp.where(to_a, s, s - p) * tq, tq)
                d1, d2 = tile(pl.multiple_of(qrow0, tq), krow0, False)
                a1 = a1 + jnp.where(to_a, d1, 0.0)
                a2 = a2 + jnp.where(to_a, d2, 0.0)
                b1 = b1 + jnp.where(to_a, 0.0, d1)
                b2 = b2 + jnp.where(to_a, 0.0, d2)
            finish(normalised(b1), normalised(b2), rows_b)
        finish(normalised(a1), normalised(a2), rows_a)

    @pl.when(jnp.logical_not(safe))
    def _():
        def update(t, vt, m, l, acc):
            m_new = jnp.maximum(m, jnp.max(t, axis=-1, keepdims=True))
            alpha = jnp.exp2(m - m_new)
            pr = jnp.exp2(t - m_new)
            l_new = alpha * l + jnp.sum(pr, axis=-1, keepdims=True)
            acc_new = alpha * acc + jnp.dot(pr.astype(BF16), vt, preferred_element_type=F32)
            return m_new, l_new, acc_new

        def online(rows0, n_off):
            q1 = q1_scr[pl.ds(rows0, tq), :LANES]
            q2 = q2_scr[pl.ds(rows0, tq), :LANES]

            def step(krow0, carry, masked):
                m1, l1, c1, m2, l2, c2 = carry
                kt = k_scr[pl.ds(krow0, tq), :LANES]
                vt = v_scr[pl.ds(krow0, tq), :LANES]
                t1 = lax.dot_general(q1, kt, nt, preferred_element_type=F32)
                t2 = lax.dot_general(q2, kt, nt, preferred_element_type=F32)
                if masked:
                    t1 = jnp.where(keep, t1, -jnp.inf)
                    t2 = jnp.where(keep, t2, -jnp.inf)
                m1, l1, c1 = update(t1, vt, m1, l1, c1)
                m2, l2, c2 = update(t2, vt, m2, l2, c2)
                return m1, l1, c1, m2, l2, c2

            neg = jnp.full((tq, 1), -jnp.inf, F32)
            zero1 = jnp.zeros((tq, 1), F32)
            zacc = jnp.zeros((tq, ATT_V_DIM), F32)
            carry = step(rows0, (neg, zero1, zacc, neg, zero1, zacc), True)
            m1, l1, c1, m2, l2, c2 = lax.fori_loop(
                0, n_off, lambda j, c: step(pl.multiple_of(j * tq, tq), c, False), carry)
            finish(c1 / l1, c2 / l2, rows0)

        online(rows_a, p)
        if paired:
            online(rows_b, nq - 1 - p)


def _attention(proj, qcos, qsin, kcos, ksin, qg2, kg2, lq1, lk1, lq2, lk2, sg, batch, seq):
    T = batch * seq
    tq = min(512, seq)
    nq = seq // tq
    assert nq == 1 or nq % 2 == 0, "query tiles are processed in balanced pairs"
    vec = lambda n: pl.BlockSpec((1, n), lambda b, h, i: (0, 0))
    table = pl.BlockSpec((seq, LANES), lambda b, h, i: (0, 0))
    head = lambda col: pl.BlockSpec((seq, LANES), lambda b, h, i: (b, col // LANES + h))
    return pl.pallas_call(
        functools.partial(_attn_kernel, tq=tq, seq=seq),
        grid=(batch, ATT_HEADS, max(nq // 2, 1)),
        in_specs=[
            head(COL_Q), head(COL_K), head(COL_V),
            table, table, table, table,
            vec(LANES), vec(LANES),
            vec(ATT_HEAD_DIM), vec(ATT_HEAD_DIM), vec(ATT_HEAD_DIM), vec(ATT_HEAD_DIM),
            vec(LANES),
        ],
        out_specs=pl.BlockSpec((seq, LANES), lambda b, h, i: (b, h)),
        out_shape=jax.ShapeDtypeStruct((T, ATT_WIDTH), BF16),
        scratch_shapes=[pltpu.VMEM((seq, 2 * LANES), BF16) for _ in range(4)],
        compiler_params=_cparams(("arbitrary", "arbitrary", "arbitrary")),
        name="attention",
    )(proj, proj, proj, qcos, qsin, kcos, ksin, qg2, kg2, lq1, lk1, lq2, lk2, sg)


def _ssd_kernel(xbc_ref, z_ref, dt_ref, cw_ref, cb_ref, dtb_ref, alog_ref, dsk_ref, ng_ref,
                o_ref, halo, state, xs_scr, bc_scr, y_scr):
    Q = SSM_CHUNK
    N = SSM_STATE
    P2 = 2 * SSM_HEAD_DIM

    @pl.when(pl.program_id(1) == 0)
    def _():
        halo[...] = jnp.zeros_like(halo)
        state[...] = jnp.zeros_like(state)

    taps = SSM_CONV - 1
    sr = lax.broadcasted_iota(jnp.int32, (taps * Q, HALO + Q), 0)
    sc = lax.broadcasted_iota(jnp.int32, (taps * Q, HALO + Q), 1)
    shift = (sc == (sr % Q) + HALO - taps + sr // Q).astype(BF16)
    cwid = 512
    for c0 in range(0, SSM_XBC, cwid):
        cur = xbc_ref[:, c0:c0 + cwid]
        ext = jnp.concatenate([halo[:, c0:c0 + cwid], cur], axis=0)
        shifted = jnp.dot(shift, ext, preferred_element_type=F32)
        acc = cb_ref[:, c0:c0 + cwid] + cw_ref[taps:SSM_CONV, c0:c0 + cwid] * cur.astype(F32)
        for k in range(taps):
            acc = acc + cw_ref[k:k + 1, c0:c0 + cwid] * shifted[k * Q:(k + 1) * Q, :]
        act = acc * jax.nn.sigmoid(acc)
        if c0 < SSM_D_INNER:
            xs_scr[:, c0:c0 + cwid] = act
        else:
            bc_scr[:, c0 - SSM_D_INNER:c0 - SSM_D_INNER + cwid] = act.astype(BF16)
        halo[:, c0:c0 + cwid] = cur[Q - HALO:Q, :]

    dt = jax.nn.softplus(dt_ref[...] + dtb_ref[...])
    da = dt * (-jnp.exp(alog_ref[...]))
    ri = lax.broadcasted_iota(jnp.int32, (Q, Q), 0)
    ci = lax.broadcasted_iota(jnp.int32, (Q, Q), 1)
    tri = ri >= ci
    acs = jnp.dot(tri.astype(F32), da, preferred_element_type=F32,
                  precision=lax.Precision.HIGHEST)
    acs_t = acs.T
    dt_t = dt.T
    w_t = dt_t * jnp.exp(acs_t[:, Q - 1:Q] - acs_t)
    lane = lax.broadcasted_iota(jnp.int32, (1, P2), 1)
    left = lane < SSM_HEAD_DIM

    for g in range(SSM_GROUPS):
        bm = bc_scr[:, g * N:(g + 1) * N]
        cm = bc_scr[:, SSM_GROUPS * N + g * N:SSM_GROUPS * N + (g + 1) * N]
        cb = lax.dot_general(cm, bm, (((1,), (1,)), ((), ())), preferred_element_type=F32)
        bm_t = bm.astype(F32).T
        cm_f = cm.astype(F32)
        for pr in range(2):
            pair = 2 * g + pr
            xs_pair = xs_scr[:, pair * P2:(pair + 1) * P2].astype(BF16)
            prev = state[pair]
            rhs = jnp.concatenate([xs_pair, prev.astype(BF16)], axis=0)
            ys, sts, decs = [], [], []
            for r in range(2):
                h = 2 * pair + r
                a_col = acs[:, h:h + 1]
                seg = a_col - acs_t[h:h + 1, :]
                decay = jnp.exp(jnp.where(tri, seg, -jnp.inf))
                m_h = cb * decay * dt_t[h:h + 1, :]
                e_h = cm_f * jnp.exp(a_col)
                lhs = jnp.concatenate([m_h, e_h], axis=1).astype(BF16)
                ys.append(jnp.dot(lhs, rhs, preferred_element_type=F32))
                sts.append(jnp.dot((bm_t * w_t[h:h + 1, :]).astype(BF16), xs_pair,
                                   preferred_element_type=F32))
                decs.append(jnp.exp(acs[Q - 1:Q, h:h + 1]))
            y_scr[:, pair * P2:(pair + 1) * P2] = jnp.where(left, ys[0], ys[1])
            dec = jnp.where(left, decs[0], decs[1])
            state[pair] = dec * prev + jnp.where(left, sts[0], sts[1])

    y = y_scr[...] + dsk_ref[...] * xs_scr[...]
    zf = z_ref[...].astype(F32)
    y = y * (zf * jax.nn.sigmoid(zf))
    gw = SSM_D_INNER // SSM_GROUPS
    for g in range(SSM_GROUPS):
        yg = y[:, g * gw:(g + 1) * gw]
        ms = jnp.mean(yg * yg, axis=-1, keepdims=True)
        o_ref[:, g * gw:(g + 1) * gw] = (yg * lax.rsqrt(ms + EPS)
                                         * ng_ref[:, g * gw:(g + 1) * gw]).astype(BF16)


def _ssd(proj, dt_raw, conv_w, conv_b, dt_bias, a_log, dsk, ng, batch, seq):
    T = batch * seq
    Q = SSM_CHUNK
    nc = seq // Q
    vec = lambda r, n: pl.BlockSpec((r, n), lambda b, c: (0, 0))
    return pl.pallas_call(
        _ssd_kernel,
        grid=(batch, nc),
        in_specs=[
            pl.BlockSpec((Q, SSM_XBC), lambda b, c: (b * nc + c, COL_XBC // SSM_XBC)),
            pl.BlockSpec((Q, SSM_D_INNER), lambda b, c: (b * nc + c, COL_Z // SSM_D_INNER)),
            pl.BlockSpec((Q, LANES), lambda b, c: (b * nc + c, 0)),
            vec(SSM_CONV, SSM_XBC), vec(1, SSM_XBC), vec(1, LANES), vec(1, LANES),
            vec(1, SSM_D_INNER), vec(1, SSM_D_INNER),
        ],
        out_specs=pl.BlockSpec((Q, SSM_D_INNER), lambda b, c: (b * nc + c, 0)),
        out_shape=jax.ShapeDtypeStruct((T, SSM_D_INNER), BF16),
        scratch_shapes=[
            pltpu.VMEM((HALO, SSM_XBC), BF16),
            pltpu.VMEM((SSM_HEADS // 2, SSM_STATE, 2 * SSM_HEAD_DIM), F32),
            pltpu.VMEM((Q, SSM_D_INNER), F32),
            pltpu.VMEM((Q, 2 * SSM_GROUPS * SSM_STATE), BF16),
            pltpu.VMEM((Q, SSM_D_INNER), F32),
        ],
        compiler_params=_cparams(("arbitrary", "arbitrary")),
        name="ssd",
    )(proj, proj, dt_raw, conv_w, conv_b, dt_bias, a_log, dsk, ng)


def _merge_kernel(att_ref, ssm_ref, ga_ref, gs_ref, x_ref, wap_ref, wsp_ref, wo_ref, fg_ref,
                  wr_ref, br_ref, x1_ref, h2_ref, idx_ref, gate_ref, cnt_ref, base, *, tm):
    i = pl.program_id(0)

    @pl.when(i == 0)
    def _():
        base[...] = jnp.zeros_like(base)

    pa = jnp.dot(att_ref[...], wap_ref[...], preferred_element_type=F32)
    ps = jnp.dot(ssm_ref[...], wsp_ref[...], preferred_element_type=F32)
    merged = (jax.nn.sigmoid(ga_ref[...].astype(F32)) * pa
              + jax.nn.sigmoid(gs_ref[...].astype(F32)) * ps)
    x1 = x_ref[...] + jnp.dot(merged.astype(BF16), wo_ref[...], preferred_element_type=F32)
    x1_ref[...] = x1
    ms = jnp.mean(x1 * x1, axis=-1, keepdims=True)
    h2 = x1 * lax.rsqrt(ms + EPS) * fg_ref[...]
    _store_rows(h2_ref, _pack_rows(h2))

    lane = lax.broadcasted_iota(jnp.int32, (1, LANES), 1)
    logits = jnp.dot(h2.astype(BF16), wr_ref[...], preferred_element_type=F32) + br_ref[...]
    lg = jnp.where(lane < N_EXPERTS, logits, -jnp.inf)
    lane_f = lane.astype(F32)
    vals, idxs, sels = [], [], []
    for _ in range(TOP_K):
        m = jnp.max(lg, axis=-1, keepdims=True)
        idx = jnp.min(jnp.where(lg == m, lane_f, float(LANES)), axis=-1, keepdims=True)
        sel = lane_f == idx
        vals.append(m)
        idxs.append(idx)
        sels.append(sel)
        lg = jnp.where(sel, -jnp.inf, lg)
    es = [jnp.exp(v - vals[0]) for v in vals]
    den = es[0] + es[1] + es[2] + es[3]

    multi = jnp.zeros((tm, LANES), F32)
    for sel in sels:
        multi = jnp.where(sel, 1.0, multi)
    ri = lax.broadcasted_iota(jnp.int32, (tm, tm), 0)
    ci = lax.broadcasted_iota(jnp.int32, (tm, tm), 1)
    before = jnp.dot((ri > ci).astype(BF16), multi.astype(BF16), preferred_element_type=F32)
    before = before + base[...]
    idx_out = jnp.zeros((tm, LANES), F32)
    gate_out = jnp.zeros((tm, LANES), F32)
    for k in range(TOP_K):
        rank = jnp.sum(jnp.where(sels[k], before, 0.0), axis=-1, keepdims=True)
        idx_out = jnp.where(lane == k, idxs[k], idx_out)
        idx_out = jnp.where(lane == TOP_K + k, rank, idx_out)
        gate_out = jnp.where(lane == k, es[k] / den, gate_out)
    idx_ref[...] = idx_out.astype(jnp.int32)
    gate_ref[...] = gate_out
    base[...] = base[...] + jnp.sum(multi, axis=0, keepdims=True)
    cnt_ref[...] = jnp.broadcast_to(base[...], cnt_ref.shape).astype(jnp.int32)


def _merge(att, ssm, proj, xf, wap, wsp, wo, fg, wr, br):
    T = xf.shape[0]
    tm = min(512, T)
    full = lambda a: pl.BlockSpec(a.shape, lambda i: (0, 0))
    return pl.pallas_call(
        functools.partial(_merge_kernel, tm=tm),
        grid=(T // tm,),
        in_specs=[
            pl.BlockSpec((tm, ATT_WIDTH), lambda i: (i, 0)),
            pl.BlockSpec((tm, SSM_D_INNER), lambda i: (i, 0)),
            pl.BlockSpec((tm, D_MODEL), lambda i: (i, COL_GA // D_MODEL)),
            pl.BlockSpec((tm, D_MODEL), lambda i: (i, COL_GS // D_MODEL)),
            pl.BlockSpec((tm, D_MODEL), lambda i: (i, 0)),
            full(wap), full(wsp), full(wo), full(fg), full(wr), full(br),
        ],
        out_specs=[
            pl.BlockSpec((tm, D_MODEL), lambda i: (i, 0)),
            pl.BlockSpec((tm * ROW_SUB, LANES), lambda i: (i, 0)),
            pl.BlockSpec((tm, LANES), lambda i: (i, 0)),
            pl.BlockSpec((tm, LANES), lambda i: (i, 0)),
            pl.BlockSpec((HALO, LANES), lambda i: (0, 0)),
        ],
        out_shape=[
            jax.ShapeDtypeStruct((T, D_MODEL), F32),
            jax.ShapeDtypeStruct((T * ROW_SUB, LANES), jnp.uint32),
            jax.ShapeDtypeStruct((T, LANES), jnp.int32),
            jax.ShapeDtypeStruct((T, LANES), F32),
            jax.ShapeDtypeStruct((HALO, LANES), jnp.int32),
        ],
        scratch_shapes=[pltpu.VMEM((1, LANES), F32)],
        compiler_params=_cparams(("arbitrary",)),
        name="merge_router",
    )(att, ssm, proj, proj, xf, wap, wsp, wo, fg, wr, br)


DMA_UNROLL = 8


def _dispatch_kernel(lb_ref, cnt_ref, dest_ref, h_ref, rows_ref, zeros, zsem, sem, *, tt, bm):
    @pl.when(pl.program_id(0) == 0)
    def _():
        zeros[...] = jnp.zeros_like(zeros)
        for wait in (False, True):
            for e in range(N_EXPERTS):
                @pl.when(cnt_ref[e] > 0)
                def _():
                    start = pl.multiple_of(lb_ref[e], bm)
                    cp = pltpu.make_async_copy(zeros, rows_ref.at[pl.ds(start * ROW_SUB, bm * ROW_SUB)], zsem)
                    cp.wait() if wait else cp.start()

    def row_copy(t, k):
        return pltpu.make_async_copy(_row(h_ref, t), _row(rows_ref, dest_ref[t * TOP_K + k]), sem)

    for wait in (False, True):
        def body(g, carry):
            for u in range(DMA_UNROLL):
                for k in range(TOP_K):
                    cp = row_copy(g * DMA_UNROLL + u, k)
                    cp.wait() if wait else cp.start(priority=k % 2)
            return carry
        lax.fori_loop(0, tt // DMA_UNROLL, body, 0)


def _dispatch(last_block, cnt, dest, h2p, n_rows, bm):
    T = h2p.shape[0] // ROW_SUB
    tt = min(512, T)
    grid_spec = pltpu.PrefetchScalarGridSpec(
        num_scalar_prefetch=2,
        grid=(T // tt,),
        in_specs=[
            pl.BlockSpec((tt * TOP_K,), lambda i, lb, c: (i,), memory_space=pltpu.SMEM),
            pl.BlockSpec((tt * ROW_SUB, LANES), lambda i, lb, c: (i, 0)),
        ],
        out_specs=pl.BlockSpec(memory_space=pl.ANY),
        scratch_shapes=[pltpu.VMEM((bm * ROW_SUB, LANES), jnp.uint32),
                        pltpu.SemaphoreType.DMA, pltpu.SemaphoreType.DMA],
    )
    return pl.pallas_call(
        functools.partial(_dispatch_kernel, tt=tt, bm=bm),
        grid_spec=grid_spec,
        out_shape=jax.ShapeDtypeStruct((n_rows * ROW_SUB, LANES), jnp.uint32),
        compiler_params=_cparams(("arbitrary",)),
        name="dispatch",
    )(last_block, cnt, dest, h2p)


def _expert_kernel(be_ref, nu_ref, x_ref, wgu_ref, bgu_ref, wd_ref, bd_ref, o_ref, wgu_bf, wd_bf):
    i = pl.program_id(0)

    @pl.when(i < nu_ref[0])
    def _():
        @pl.when((i == 0) | (be_ref[i] != be_ref[jnp.maximum(i - 1, 0)]))
        def _():
            wgu_bf[...] = wgu_ref[0].astype(BF16)
            wd_bf[...] = wd_ref[0].astype(BF16)

        lo, hi = _unpack_rows(_load_rows(x_ref))
        x = jnp.concatenate([lo, hi], axis=1).astype(BF16)
        gu = jnp.dot(x, wgu_bf[...], preferred_element_type=F32) + bgu_ref[0]
        gate = jnp.minimum(gu[:, :D_FF], SWIGLU_LIMIT)
        up = jnp.clip(gu[:, D_FF:], -SWIGLU_LIMIT, SWIGLU_LIMIT)
        glu = gate * jax.nn.sigmoid(SWIGLU_ALPHA * gate)
        act = ((up + 1.0) * glu).astype(BF16)
        _store_rows(o_ref, _pack_rows(jnp.dot(act, wd_bf[...], preferred_element_type=F32) + bd_ref[0]))


def _experts(block_expert, n_used, rows, wgu, bgu, wd, bd, bm):
    n_rows = rows.shape[0] // ROW_SUB
    nb = n_rows // bm
    row_map = lambda i, be, nu: (jnp.minimum(i, nu[0] - 1), 0)
    exp_map = lambda i, be, nu: (be[jnp.minimum(i, nu[0] - 1)], 0, 0)
    grid_spec = pltpu.PrefetchScalarGridSpec(
        num_scalar_prefetch=2,
        grid=(nb,),
        in_specs=[
            pl.BlockSpec((bm * ROW_SUB, LANES), row_map),
            pl.BlockSpec((1, D_MODEL, 2 * D_FF), exp_map),
            pl.BlockSpec((1, 1, 2 * D_FF), exp_map),
            pl.BlockSpec((1, D_FF, D_MODEL), exp_map),
            pl.BlockSpec((1, 1, D_MODEL), exp_map),
        ],
        out_specs=pl.BlockSpec((bm * ROW_SUB, LANES), row_map),
        scratch_shapes=[pltpu.VMEM((D_MODEL, 2 * D_FF), BF16), pltpu.VMEM((D_FF, D_MODEL), BF16)],
    )
    return pl.pallas_call(
        _expert_kernel,
        grid_spec=grid_spec,
        out_shape=jax.ShapeDtypeStruct((n_rows * ROW_SUB, LANES), jnp.uint32),
        compiler_params=_cparams(("arbitrary",)),
        name="experts",
    )(block_expert, n_used, rows, wgu, bgu, wd, bd)


def _combine_kernel(dest_ref, next_ref, x1_ref, gate_ref, y_ref, o_ref, ybuf, sems, *, tc):
    i = pl.program_id(0)
    n = pl.num_programs(0)

    def gather(idx_ref, slot, wait):
        def row_copy(t, k):
            return pltpu.make_async_copy(_row(y_ref, idx_ref[t * TOP_K + k]), _row(ybuf.at[slot, k], t),
                                         sems.at[slot])

        def body(g, carry):
            for u in range(DMA_UNROLL):
                for k in range(TOP_K):
                    cp = row_copy(g * DMA_UNROLL + u, k)
                    cp.wait() if wait else cp.start(priority=k % 2)
            return carry
        lax.fori_loop(0, tc // DMA_UNROLL, body, 0)

    slot = i % 2

    @pl.when(i == 0)
    def _():
        gather(dest_ref, 0, False)

    @pl.when(i + 1 < n)
    def _():
        gather(next_ref, 1 - slot, False)

    gather(dest_ref, slot, True)

    x1 = x1_ref[...]
    acc_lo, acc_hi = x1[:, :PACKED], x1[:, PACKED:]
    g = gate_ref[...]
    for k in range(TOP_K):
        lo, hi = _unpack_rows(_load_rows(ybuf.at[slot, k]))
        acc_lo = acc_lo + g[:, k:k + 1] * lo
        acc_hi = acc_hi + g[:, k:k + 1] * hi
    o_ref[:, :PACKED] = acc_lo
    o_ref[:, PACKED:] = acc_hi


def _combine(dest, x1, gates, y_rows):
    T = x1.shape[0]
    tc = min(256, T)
    n = T // tc
    return pl.pallas_call(
        functools.partial(_combine_kernel, tc=tc),
        grid=(n,),
        in_specs=[
            pl.BlockSpec((tc * TOP_K,), lambda i: (i,), memory_space=pltpu.SMEM),
            pl.BlockSpec((tc * TOP_K,), lambda i: (jnp.minimum(i + 1, n - 1),), memory_space=pltpu.SMEM),
            pl.BlockSpec((tc, D_MODEL), lambda i: (i, 0)),
            pl.BlockSpec((tc, LANES), lambda i: (i, 0)),
            pl.BlockSpec(memory_space=pl.ANY),
        ],
        out_specs=pl.BlockSpec((tc, D_MODEL), lambda i: (i, 0)),
        out_shape=jax.ShapeDtypeStruct((T, D_MODEL), F32),
        scratch_shapes=[pltpu.VMEM((2, TOP_K, tc * ROW_SUB, LANES), jnp.uint32),
                        pltpu.SemaphoreType.DMA((2,))],
        compiler_params=_cparams(("arbitrary",)),
        name="combine",
    )(dest, dest, x1, gates, y_rows)


def _pad_lanes(v):
    return jnp.pad(v, ((0, 0), (0, LANES - v.shape[-1])))


def kernel(x, mix_norm_g, w_in, q_norm_g, k_norm_g, lambda_q1, lambda_k1, lambda_q2, lambda_k2,
           attn_subln_g, conv_w, conv_b, dt_bias, a_log, d_skip, ssm_norm_g, w_attn_proj,
           w_ssm_proj, w_out, ffn_norm_g, w_router, b_router, w_gate_up, b_gate_up, w_down, b_down):
    B, S, D = x.shape
    T = B * S
    xf = x.reshape(T, D)
    layer = 0

    wi = w_in[layer]
    o_q, o_k, o_v = 0, ATT_WIDTH, 2 * ATT_WIDTH
    o_z = 3 * ATT_WIDTH
    o_xbc = o_z + SSM_D_INNER
    o_dt = o_xbc + SSM_XBC
    o_ga = o_dt + SSM_HEADS
    w_main = jnp.concatenate([wi[:, o_xbc:o_dt], wi[:, o_z:o_xbc], wi[:, o_q:o_z], wi[:, o_ga:]],
                             axis=1).astype(BF16)
    w_dt = _pad_lanes(wi[:, o_dt:o_ga]).astype(BF16)

    proj, dt_raw = _in_proj(xf, mix_norm_g[layer][None, :], w_main, w_dt)

    half = ATT_HEAD_DIM // 2
    inv = ROPE_THETA ** (-jnp.arange(0, ATT_HEAD_DIM, 2, dtype=F32) / ATT_HEAD_DIM)
    ang = jnp.arange(S, dtype=F32)[:, None] * inv[None, :]
    cos2 = jnp.tile(jnp.cos(ang), (1, LANES // half))
    sin2 = jnp.tile(jnp.concatenate([-jnp.sin(ang), jnp.sin(ang)], axis=1), (1, LANES // ATT_HEAD_DIM))
    qg2 = jnp.tile(q_norm_g[layer], 2)[None, :]
    kg2 = jnp.tile(k_norm_g[layer], 2)[None, :]
    partner = lambda g: jnp.tile(jnp.roll(g, half), 2)[None, :]
    q_scale = ATT_HEAD_DIM ** -0.5 * LOG2E
    att = _attention(proj, cos2 * qg2 * q_scale, sin2 * partner(q_norm_g[layer]) * q_scale,
                     cos2 * kg2, sin2 * partner(k_norm_g[layer]), qg2, kg2,
                     lambda_q1[layer][None, :], lambda_k1[layer][None, :],
                     lambda_q2[layer][None, :], lambda_k2[layer][None, :],
                     attn_subln_g[layer][None, :], B, S)

    ssm = _ssd(proj, dt_raw, conv_w[layer], conv_b[layer][None, :],
               _pad_lanes(dt_bias[layer][None, :]), _pad_lanes(a_log[layer][None, :]),
               jnp.repeat(d_skip[layer], SSM_HEAD_DIM)[None, :], ssm_norm_g[layer][None, :], B, S)

    x1, h2p, idx_rank, gates, counts = _merge(
        att, ssm, proj, xf, w_attn_proj[layer].astype(BF16), w_ssm_proj[layer].astype(BF16),
        w_out[layer].astype(BF16), ffn_norm_g[layer][None, :],
        _pad_lanes(w_router[layer]).astype(BF16), _pad_lanes(b_router[layer][None, :]))

    bm = 512
    A = T * TOP_K
    n_rows = (A + N_EXPERTS * (bm - 1)) // bm * bm
    cnt = counts[0, :N_EXPERTS]
    padded = (cnt + bm - 1) // bm * bm
    pend = jnp.cumsum(padded)
    pstart = pend - padded
    n_used = (pend[-1:] // bm).astype(jnp.int32)
    block_start = jnp.arange(n_rows // bm, dtype=jnp.int32) * bm
    block_expert = jnp.minimum(jnp.sum(block_start[:, None] >= pend[None, :], axis=1),
                               N_EXPERTS - 1).astype(jnp.int32)
    top_idx = idx_rank[:, :TOP_K]
    onehot = top_idx[:, :, None] == jnp.arange(N_EXPERTS, dtype=jnp.int32)[None, None, :]
    dest = (jnp.sum(jnp.where(onehot, pstart[None, None, :], 0), axis=-1)
            + idx_rank[:, TOP_K:2 * TOP_K]).reshape(A).astype(jnp.int32)

    rows = _dispatch((pend - bm).astype(jnp.int32), cnt.astype(jnp.int32), dest, h2p, n_rows, bm)
    y_rows = _experts(block_expert, n_used, rows, w_gate_up[layer], b_gate_up[layer][:, None, :],
                      w_down[layer], b_down[layer][:, None, :], bm)
    out = _combine(dest, x1, gates, y_rows)
    return out.reshape(B, S, D)
```

```python
import functools
import math

import jax
import jax.numpy as jnp
from jax import lax
from jax.experimental import pallas as pl
from jax.experimental.pallas import tpu as pltpu

F32 = jnp.float32
BF16 = jnp.bfloat16

D_MODEL = 1024
EPS = 1e-6
ATT_HEADS = 8
ATT_HEAD_DIM = 64
ATT_V_DIM = 2 * ATT_HEAD_DIM
ATT_WIDTH = ATT_HEADS * ATT_V_DIM
ROPE_THETA = 10000.0
SSM_D_INNER = 2 * D_MODEL
SSM_HEAD_DIM = 64
SSM_HEADS = SSM_D_INNER // SSM_HEAD_DIM
SSM_GROUPS = 8
SSM_STATE = 128
SSM_CONV = 4
SSM_CHUNK = 128
SSM_XBC = SSM_D_INNER + 2 * SSM_GROUPS * SSM_STATE
N_EXPERTS = 32
TOP_K = 4
D_FF = D_MODEL
SWIGLU_LIMIT = 7.0
SWIGLU_ALPHA = 1.702
LAM_INIT = 0.8 - 0.6 * math.exp(-0.3 * 0)

LANES = 128
HALO = 16

COL_XBC = 0
COL_Z = SSM_XBC
COL_Q = COL_Z + SSM_D_INNER
COL_K = COL_Q + ATT_WIDTH
COL_V = COL_K + ATT_WIDTH
COL_GA = COL_V + ATT_WIDTH
COL_GS = COL_GA + D_MODEL
PROJ_COLS = COL_GS + D_MODEL

VMEM_LIMIT = 56 * 1024 * 1024


def _cparams(sem):
    return pltpu.CompilerParams(dimension_semantics=sem, vmem_limit_bytes=VMEM_LIMIT)


PACKED = D_MODEL // 2


def _pack_rows(x):
    lo = lax.bitcast_convert_type(x[:, :PACKED].astype(BF16).astype(F32), jnp.uint32)
    hi = lax.bitcast_convert_type(x[:, PACKED:].astype(BF16).astype(F32), jnp.uint32)
    return hi | (lo >> 16)


def _unpack_rows(p):
    lo = lax.bitcast_convert_type(p << 16, F32)
    hi = lax.bitcast_convert_type(p & jnp.uint32(0xFFFF0000), F32)
    return lo, hi


ROW_SUB = PACKED // LANES


def _store_rows(ref, packed):
    for s in range(ROW_SUB):
        ref[pl.ds(s, packed.shape[0], stride=ROW_SUB), :] = packed[:, s * LANES:(s + 1) * LANES]


def _load_rows(ref):
    n = ref.shape[0] // ROW_SUB
    return jnp.concatenate([ref[pl.ds(s, n, stride=ROW_SUB), :] for s in range(ROW_SUB)], axis=1)


def _row(ref, r):
    return ref.at[pl.ds(pl.multiple_of(r * ROW_SUB, ROW_SUB), ROW_SUB)]


def _in_proj_kernel(x_ref, g_ref, w_ref, wdt_ref, o_ref, dt_ref, h_scr):
    @pl.when(pl.program_id(1) == 0)
    def _():
        x = x_ref[...]
        ms = jnp.mean(x * x, axis=-1, keepdims=True)
        hb = (x * lax.rsqrt(ms + EPS) * g_ref[...]).astype(BF16)
        h_scr[...] = hb
        dt_ref[...] = jnp.dot(hb, wdt_ref[...], preferred_element_type=F32)

    o_ref[...] = jnp.dot(h_scr[...], w_ref[...], preferred_element_type=F32).astype(BF16)


def _in_proj(xf, g, w_main, w_dt):
    T = xf.shape[0]
    tm = min(2048, T)
    tn = 1024
    return pl.pallas_call(
        _in_proj_kernel,
        grid=(T // tm, PROJ_COLS // tn),
        in_specs=[
            pl.BlockSpec((tm, D_MODEL), lambda i, j: (i, 0)),
            pl.BlockSpec((1, D_MODEL), lambda i, j: (0, 0)),
            pl.BlockSpec((D_MODEL, tn), lambda i, j: (0, j)),
            pl.BlockSpec((D_MODEL, LANES), lambda i, j: (0, 0)),
        ],
        out_specs=[
            pl.BlockSpec((tm, tn), lambda i, j: (i, j)),
            pl.BlockSpec((tm, LANES), lambda i, j: (i, 0)),
        ],
        out_shape=[
            jax.ShapeDtypeStruct((T, PROJ_COLS), BF16),
            jax.ShapeDtypeStruct((T, LANES), F32),
        ],
        scratch_shapes=[pltpu.VMEM((tm, D_MODEL), BF16)],
        compiler_params=_cparams(("arbitrary", "arbitrary")),
        name="in_proj",
    )(xf, g, w_main, w_dt)


LOG2E = 1.4426950408889634
SHIFT_LIMIT = 57.0
BOUND_MARGIN = 1.02


def _attn_kernel(q_ref, k_ref, v_ref, qc_ref, qs_ref, kc_ref, ks_ref, qg_ref, kg_ref,
                 lq1_ref, lk1_ref, lq2_ref, lk2_ref, sg_ref, o_ref, k_scr, v_scr, q1_scr, q2_scr, *, tq, seq):
    p = pl.program_id(2)
    nq = seq // tq
    paired = nq > 1
    lane = lax.broadcasted_iota(jnp.int32, (1, LANES), 1)
    first = lane < ATT_HEAD_DIM
    one_col = jnp.where(lane == 0, 1.0, 0.0)

    ri = lax.broadcasted_iota(jnp.int32, (LANES, LANES), 0)
    ci = lax.broadcasted_iota(jnp.int32, (LANES, LANES), 1)
    same_comp = (ri // ATT_HEAD_DIM == ci // ATT_HEAD_DIM).astype(BF16)
    swap_half = ((ri // ATT_HEAD_DIM == ci // ATT_HEAD_DIM)
                 & ((ri - ci == ATT_HEAD_DIM // 2) | (ci - ri == ATT_HEAD_DIM // 2))).astype(BF16)

    def norm_rope(xb, gcos, gsin):
        xf = xb.astype(F32)
        sq = xf * xf
        hi = sq.astype(BF16)
        lo = (sq - hi.astype(F32)).astype(BF16)
        ms = (jnp.dot(hi, same_comp, preferred_element_type=F32)
              + jnp.dot(lo, same_comp, preferred_element_type=F32)) * (1.0 / ATT_HEAD_DIM)
        xr = jnp.dot(xb, swap_half, preferred_element_type=F32)
        return lax.rsqrt(ms + EPS) * (xf * gcos + xr * gsin)

    ub = (ATT_HEAD_DIM * ATT_HEAD_DIM ** -0.5 * LOG2E * BOUND_MARGIN
          * jnp.max(jnp.abs(qg_ref[...])) * jnp.max(jnp.abs(kg_ref[...])))
    safe = ub <= SHIFT_LIMIT

    @pl.when(p == 0)
    def _():
        ones = jnp.broadcast_to(one_col, (tq, LANES)).astype(BF16)
        shift = jnp.broadcast_to(jnp.where(lane == 0, -ub, 0.0), (tq, LANES)).astype(BF16)
        for r in range(0, seq, tq):
            rows = slice(r, r + tq)
            k_scr[rows, :LANES] = norm_rope(k_ref[rows, :], kc_ref[rows, :], ks_ref[rows, :]).astype(BF16)
            k_scr[rows, LANES:] = ones
            v_scr[rows, :LANES] = v_ref[rows, :]
            v_scr[rows, LANES:] = ones
            qb = norm_rope(q_ref[rows, :], qc_ref[rows, :], qs_ref[rows, :]).astype(BF16)
            q1_scr[rows, :LANES] = jnp.where(first, qb, jnp.zeros_like(qb))
            q1_scr[rows, LANES:] = shift
            q2_scr[rows, :LANES] = jnp.where(first, jnp.zeros_like(qb), qb)
            q2_scr[rows, LANES:] = shift

    keep = (lax.broadcasted_iota(jnp.int32, (tq, tq), 1) <= lax.broadcasted_iota(jnp.int32, (tq, tq), 0))
    nt = (((1,), (1,)), ((), ()))
    rows_a = pl.multiple_of(p * tq, tq)
    rows_b = pl.multiple_of((nq - 1 - p) * tq, tq)

    def finish(o1, o2, rows0):
        lam = (jnp.exp(jnp.sum(lq1_ref[...] * lk1_ref[...], axis=-1, keepdims=True))
               - jnp.exp(jnp.sum(lq2_ref[...] * lk2_ref[...], axis=-1, keepdims=True)) + LAM_INIT)
        o = o1 - lam * o2
        ms = jnp.mean(o * o, axis=-1, keepdims=True)
        o_ref[pl.ds(rows0, tq), :] = (o * lax.rsqrt(ms + EPS) * sg_ref[...] * (1.0 - LAM_INIT)).astype(BF16)

    @pl.when(safe)
    def _():
        def block(q_scr, qrow0, krow0, rows, cols, mask):
            t = lax.dot_general(q_scr[pl.ds(qrow0, rows), :], k_scr[pl.ds(krow0, cols), :], nt,
                                preferred_element_type=F32)
            if mask is not None:
                t = jnp.where(mask, t, -jnp.inf)
            return jnp.dot(jnp.exp2(t).astype(BF16), v_scr[pl.ds(krow0, cols), :], preferred_element_type=F32)

        def tile(qrow0, krow0, masked):
            if not masked:
                return (block(q1_scr, qrow0, krow0, tq, tq, None), block(q2_scr, qrow0, krow0, tq, tq, None))
            h = tq // 2
            q_low = pl.multiple_of(qrow0 + h, h)
            k_hi = pl.multiple_of(krow0 + h, h)
            outs = []
            for q_scr in (q1_scr, q2_scr):
                left_cols = block(q_scr, qrow0, krow0, tq, h, keep[:, :h])
                corner = block(q_scr, q_low, k_hi, h, h, keep[:h, :h])
                outs.append(jnp.concatenate([left_cols[:h], left_cols[h:] + corner], axis=0))
            return tuple(outs)

        def normalised(acc):
            return acc[:, :LANES] / acc[:, LANES:LANES + 1]

        a1, a2 = tile(rows_a, rows_a, True)
        if paired:
            b1, b2 = tile(rows_b, rows_b, True)
            for s in range(nq - 1):
                to_a = s < p
                qrow0 = jnp.where(to_a, rows_a, rows_b)
                krow0 = pl.multiple_of(jnp.where(to_a, s, s - p) * tq, tq)
                d1, d2 = tile(pl.multiple_of(qrow0, tq), krow0, False)
                a1 = a1 + jnp.where(to_a, d1, 0.0)
                a2 = a2 + jnp.where(to_a, d2, 0.0)
                b1 = b1 + jnp.where(to_a, 0.0, d1)
                b2 = b2 + jnp.where(to_a, 0.0, d2)
            finish(normalised(b1), normalised(b2), rows_b)
        finish(normalised(a1), normalised(a2), rows_a)

    @pl.when(jnp.logical_not(safe))
    def _():
        def update(t, vt, m, l, acc):
            m_new = jnp.maximum(m, jnp.max(t, axis=-1, keepdims=True))
            alpha = jnp.exp2(m - m_new)
            pr = jnp.exp2(t - m_new)
            l_new = alpha * l + jnp.sum(pr, axis=-1, keepdims=True)
            acc_new = alpha * acc + jnp.dot(pr.astype(BF16), vt, preferred_element_type=F32)
            return m_new, l_new, acc_new

        def online(rows0, n_off):
            q1 = q1_scr[pl.ds(rows0, tq), :LANES]
            q2 = q2_scr[pl.ds(rows0, tq), :LANES]

            def step(krow0, carry, masked):
                m1, l1, c1, m2, l2, c2 = carry
                kt = k_scr[pl.ds(krow0, tq), :LANES]
                vt = v_scr[pl.ds(krow0, tq), :LANES]
                t1 = lax.dot_general(q1, kt, nt, preferred_element_type=F32)
                t2 = lax.dot_general(q2, kt, nt, preferred_element_type=F32)
                if masked:
                    t1 = jnp.where(keep, t1, -jnp.inf)
                    t2 = jnp.where(keep, t2, -jnp.inf)
                m1, l1, c1 = update(t1, vt, m1, l1, c1)
                m2, l2, c2 = update(t2, vt, m2, l2, c2)
                return m1, l1, c1, m2, l2, c2

            neg = jnp.full((tq, 1), -jnp.inf, F32)
            zero1 = jnp.zeros((tq, 1), F32)
            zacc = jnp.zeros((tq, ATT_V_DIM), F32)
            carry = step(rows0, (neg, zero1, zacc, neg, zero1, zacc), True)
            m1, l1, c1, m2, l2, c2 = lax.fori_loop(
                0, n_off, lambda j, c: step(pl.multiple_of(j * tq, tq), c, False), carry)
            finish(c1 / l1, c2 / l2, rows0)

        online(rows_a, p)
        if paired:
            online(rows_b, nq - 1 - p)


def _attention(proj, qcos, qsin, kcos, ksin, qg2, kg2, lq1, lk1, lq2, lk2, sg, batch, seq):
    T = batch * seq
    tq = min(512, seq)
    nq = seq // tq
    assert nq == 1 or nq % 2 == 0, "query tiles are processed in balanced pairs"
    vec = lambda n: pl.BlockSpec((1, n), lambda b, h, i: (0, 0))
    table = pl.BlockSpec((seq, LANES), lambda b, h, i: (0, 0))
    head = lambda col: pl.BlockSpec((seq, LANES), lambda b, h, i: (b, col // LANES + h))
    return pl.pallas_call(
        functools.partial(_attn_kernel, tq=tq, seq=seq),
        grid=(batch, ATT_HEADS, max(nq // 2, 1)),
        in_specs=[
            head(COL_Q), head(COL_K), head(COL_V),
            table, table, table, table,
            vec(LANES), vec(LANES),
            vec(ATT_HEAD_DIM), vec(ATT_HEAD_DIM), vec(ATT_HEAD_DIM), vec(ATT_HEAD_DIM),
            vec(LANES),
        ],
        out_specs=pl.BlockSpec((seq, LANES), lambda b, h, i: (b, h)),
        out_shape=jax.ShapeDtypeStruct((T, ATT_WIDTH), BF16),
        scratch_shapes=[pltpu.VMEM((seq, 2 * LANES), BF16) for _ in range(4)],
        compiler_params=_cparams(("arbitrary", "arbitrary", "arbitrary")),
        name="attention",
    )(proj, proj, proj, qcos, qsin, kcos, ksin, qg2, kg2, lq1, lk1, lq2, lk2, sg)


SSD_SUB = 2


def _ssd_kernel(xbc_ref, z_ref, dt_ref, cw_ref, cb_ref, dtb_ref, alog_ref, dsk_ref, ng_ref,
                o_ref, halo, state, xs_scr, bc_scr, y_scr):
    Q = SSM_CHUNK
    N = SSM_STATE
    P2 = 2 * SSM_HEAD_DIM

    @pl.when(pl.program_id(1) == 0)
    def _():
        halo[...] = jnp.zeros_like(halo)
        state[...] = jnp.zeros_like(state)

    taps = SSM_CONV - 1
    sr = lax.broadcasted_iota(jnp.int32, (taps * Q, HALO + Q), 0)
    sc = lax.broadcasted_iota(jnp.int32, (taps * Q, HALO + Q), 1)
    shift = (sc == (sr % Q) + HALO - taps + sr // Q).astype(BF16)
    ri = lax.broadcasted_iota(jnp.int32, (Q, Q), 0)
    ci = lax.broadcasted_iota(jnp.int32, (Q, Q), 1)
    tri = ri >= ci
    lane = lax.broadcasted_iota(jnp.int32, (1, P2), 1)
    left = lane < SSM_HEAD_DIM
    gw = SSM_D_INNER // SSM_GROUPS

    for sub in range(SSD_SUB):
        r0 = sub * Q
        xs_sub, bc_sub, y_sub = xs_scr.at[sub], bc_scr.at[sub], y_scr.at[sub]

        cwid = 512
        for c0 in range(0, SSM_XBC, cwid):
            cur = xbc_ref[r0:r0 + Q, c0:c0 + cwid]
            hist = halo[:, c0:c0 + cwid] if sub == 0 else xbc_ref[r0 - HALO:r0, c0:c0 + cwid]
            ext = jnp.concatenate([hist, cur], axis=0)
            shifted = jnp.dot(shift, ext, preferred_element_type=F32)
            acc = cb_ref[:, c0:c0 + cwid] + cw_ref[taps:SSM_CONV, c0:c0 + cwid] * cur.astype(F32)
            for k in range(taps):
                acc = acc + cw_ref[k:k + 1, c0:c0 + cwid] * shifted[k * Q:(k + 1) * Q, :]
            act = acc * jax.nn.sigmoid(acc)
            if c0 < SSM_D_INNER:
                xs_sub[:, c0:c0 + cwid] = act
            else:
                bc_sub[:, c0 - SSM_D_INNER:c0 - SSM_D_INNER + cwid] = act.astype(BF16)

        dt = jax.nn.softplus(dt_ref[r0:r0 + Q, :] + dtb_ref[...])
        da = dt * (-jnp.exp(alog_ref[...]) * LOG2E)
        acs = jnp.dot(tri.astype(F32), da, preferred_element_type=F32,
                      precision=lax.Precision.HIGHEST)
        acs_t = acs.T
        dt_t = dt.T
        w_t = dt_t * jnp.exp2(acs_t[:, Q - 1:Q] - acs_t)

        for g in range(SSM_GROUPS):
            bm = bc_sub[:, g * N:(g + 1) * N]
            cm = bc_sub[:, SSM_GROUPS * N + g * N:SSM_GROUPS * N + (g + 1) * N]
            cb = lax.dot_general(cm, bm, (((1,), (1,)), ((), ())), preferred_element_type=F32)
            bm_t = bm.astype(F32).T
            cm_f = cm.astype(F32)
            for pr in range(2):
                pair = 2 * g + pr
                xs_pair = xs_sub[:, pair * P2:(pair + 1) * P2].astype(BF16)
                prev = state[pair]
                rhs = jnp.concatenate([xs_pair, prev.astype(BF16)], axis=0)
                ys, sts, decs = [], [], []
                for r in range(2):
                    h = 2 * pair + r
                    a_col = acs[:, h:h + 1]
                    seg = a_col - acs_t[h:h + 1, :]
                    decay = jnp.exp2(jnp.where(tri, seg, -jnp.inf))
                    m_h = cb * decay * dt_t[h:h + 1, :]
                    e_h = cm_f * jnp.exp2(a_col)
                    lhs = jnp.concatenate([m_h, e_h], axis=1).astype(BF16)
                    ys.append(jnp.dot(lhs, rhs, preferred_element_type=F32))
                    sts.append(jnp.dot((bm_t * w_t[h:h + 1, :]).astype(BF16), xs_pair,
                                       preferred_element_type=F32))
                    decs.append(jnp.exp2(acs[Q - 1:Q, h:h + 1]))
                y_sub[:, pair * P2:(pair + 1) * P2] = jnp.where(left, ys[0], ys[1])
                dec = jnp.where(left, decs[0], decs[1])
                state[pair] = dec * prev + jnp.where(left, sts[0], sts[1])

        y = y_sub[...] + dsk_ref[...] * xs_sub[...]
        zf = z_ref[r0:r0 + Q, :].astype(F32)
        y = y * (zf * jax.nn.sigmoid(zf))
        for g in range(SSM_GROUPS):
            yg = y[:, g * gw:(g + 1) * gw]
            ms = jnp.mean(yg * yg, axis=-1, keepdims=True)
            o_ref[r0:r0 + Q, g * gw:(g + 1) * gw] = (yg * lax.rsqrt(ms + EPS)
                                                     * ng_ref[:, g * gw:(g + 1) * gw]).astype(BF16)

    halo[...] = xbc_ref[SSD_SUB * Q - HALO:SSD_SUB * Q, :]


def _ssd(proj, dt_raw, conv_w, conv_b, dt_bias, a_log, dsk, ng, batch, seq):
    T = batch * seq
    Q = SSM_CHUNK
    rows = SSD_SUB * Q
    nc = seq // rows
    vec = lambda r, n: pl.BlockSpec((r, n), lambda b, c: (0, 0))
    return pl.pallas_call(
        _ssd_kernel,
        grid=(batch, nc),
        in_specs=[
            pl.BlockSpec((rows, SSM_XBC), lambda b, c: (b * nc + c, COL_XBC // SSM_XBC)),
            pl.BlockSpec((rows, SSM_D_INNER), lambda b, c: (b * nc + c, COL_Z // SSM_D_INNER)),
            pl.BlockSpec((rows, LANES), lambda b, c: (b * nc + c, 0)),
            vec(SSM_CONV, SSM_XBC), vec(1, SSM_XBC), vec(1, LANES), vec(1, LANES),
            vec(1, SSM_D_INNER), vec(1, SSM_D_INNER),
        ],
        out_specs=pl.BlockSpec((rows, SSM_D_INNER), lambda b, c: (b * nc + c, 0)),
        out_shape=jax.ShapeDtypeStruct((T, SSM_D_INNER), BF16),
        scratch_shapes=[
            pltpu.VMEM((HALO, SSM_XBC), BF16),
            pltpu.VMEM((SSM_HEADS // 2, SSM_STATE, 2 * SSM_HEAD_DIM), F32),
            pltpu.VMEM((SSD_SUB, Q, SSM_D_INNER), F32),
            pltpu.VMEM((SSD_SUB, Q, 2 * SSM_GROUPS * SSM_STATE), BF16),
            pltpu.VMEM((SSD_SUB, Q, SSM_D_INNER), F32),
        ],
        compiler_params=_cparams(("arbitrary", "arbitrary")),
        name="ssd",
    )(proj, proj, dt_raw, conv_w, conv_b, dt_bias, a_log, dsk, ng)


def _merge_kernel(att_ref, ssm_ref, ga_ref, gs_ref, x_ref, wap_ref, wsp_ref, wo_ref, fg_ref,
                  wr_ref, br_ref, x1_ref, h2_ref, idx_ref, gate_ref, cnt_ref, base, *, tm):
    i = pl.program_id(0)

    @pl.when(i == 0)
    def _():
        base[...] = jnp.zeros_like(base)

    pa = jnp.dot(att_ref[...], wap_ref[...], preferred_element_type=F32)
    ps = jnp.dot(ssm_ref[...], wsp_ref[...], preferred_element_type=F32)
    merged = (jax.nn.sigmoid(ga_ref[...].astype(F32)) * pa
              + jax.nn.sigmoid(gs_ref[...].astype(F32)) * ps)
    x1 = x_ref[...] + jnp.dot(merged.astype(BF16), wo_ref[...], preferred_element_type=F32)
    x1_ref[...] = x1
    ms = jnp.mean(x1 * x1, axis=-1, keepdims=True)
    h2 = x1 * lax.rsqrt(ms + EPS) * fg_ref[...]
    _store_rows(h2_ref, _pack_rows(h2))

    lane = lax.broadcasted_iota(jnp.int32, (1, LANES), 1)
    logits = jnp.dot(h2.astype(BF16), wr_ref[...], preferred_element_type=F32) + br_ref[...]
    lg = jnp.where(lane < N_EXPERTS, logits, -jnp.inf)
    lane_f = lane.astype(F32)
    vals, idxs, sels = [], [], []
    for _ in range(TOP_K):
        m = jnp.max(lg, axis=-1, keepdims=True)
        idx = jnp.min(jnp.where(lg == m, lane_f, float(LANES)), axis=-1, keepdims=True)
        sel = lane_f == idx
        vals.append(m)
        idxs.append(idx)
        sels.append(sel)
        lg = jnp.where(sel, -jnp.inf, lg)
    es = [jnp.exp(v - vals[0]) for v in vals]
    den = es[0] + es[1] + es[2] + es[3]

    multi = jnp.zeros((tm, LANES), F32)
    for sel in sels:
        multi = jnp.where(sel, 1.0, multi)
    ri = lax.broadcasted_iota(jnp.int32, (tm, tm), 0)
    ci = lax.broadcasted_iota(jnp.int32, (tm, tm), 1)
    before = jnp.dot((ri > ci).astype(BF16), multi.astype(BF16), preferred_element_type=F32)
    before = before + base[...]
    idx_out = jnp.zeros((tm, LANES), F32)
    gate_out = jnp.zeros((tm, LANES), F32)
    for k in range(TOP_K):
        rank = jnp.sum(jnp.where(sels[k], before, 0.0), axis=-1, keepdims=True)
        idx_out = jnp.where(lane == k, idxs[k], idx_out)
        idx_out = jnp.where(lane == TOP_K + k, rank, idx_out)
        gate_out = jnp.where(lane == k, es[k] / den, gate_out)
    idx_ref[...] = idx_out.astype(jnp.int32)
    gate_ref[...] = gate_out
    base[...] = base[...] + jnp.sum(multi, axis=0, keepdims=True)
    cnt_ref[...] = jnp.broadcast_to(base[...], cnt_ref.shape).astype(jnp.int32)


def _merge(att, ssm, proj, xf, wap, wsp, wo, fg, wr, br):
    T = xf.shape[0]
    tm = min(512, T)
    full = lambda a: pl.BlockSpec(a.shape, lambda i: (0, 0))
    return pl.pallas_call(
        functools.partial(_merge_kernel, tm=tm),
        grid=(T // tm,),
        in_specs=[
            pl.BlockSpec((tm, ATT_WIDTH), lambda i: (i, 0)),
            pl.BlockSpec((tm, SSM_D_INNER), lambda i: (i, 0)),
            pl.BlockSpec((tm, D_MODEL), lambda i: (i, COL_GA // D_MODEL)),
            pl.BlockSpec((tm, D_MODEL), lambda i: (i, COL_GS // D_MODEL)),
            pl.BlockSpec((tm, D_MODEL), lambda i: (i, 0)),
            full(wap), full(wsp), full(wo), full(fg), full(wr), full(br),
        ],
        out_specs=[
            pl.BlockSpec((tm, D_MODEL), lambda i: (i, 0)),
            pl.BlockSpec((tm * ROW_SUB, LANES), lambda i: (i, 0)),
            pl.BlockSpec((tm, LANES), lambda i: (i, 0)),
            pl.BlockSpec((tm, LANES), lambda i: (i, 0)),
            pl.BlockSpec((HALO, LANES), lambda i: (0, 0)),
        ],
        out_shape=[
            jax.ShapeDtypeStruct((T, D_MODEL), F32),
            jax.ShapeDtypeStruct((T * ROW_SUB, LANES), jnp.uint32),
            jax.ShapeDtypeStruct((T, LANES), jnp.int32),
            jax.ShapeDtypeStruct((T, LANES), F32),
            jax.ShapeDtypeStruct((HALO, LANES), jnp.int32),
        ],
        scratch_shapes=[pltpu.VMEM((1, LANES), F32)],
        compiler_params=_cparams(("arbitrary",)),
        name="merge_router",
    )(att, ssm, proj, proj, xf, wap, wsp, wo, fg, wr, br)


DMA_UNROLL = 8


def _dispatch_kernel(lb_ref, cnt_ref, dest_ref, h_ref, rows_ref, zeros, zsem, sem, *, tt, bm):
    @pl.when(pl.program_id(0) == 0)
    def _():
        zeros[...] = jnp.zeros_like(zeros)
        for wait in (False, True):
            for e in range(N_EXPERTS):
                @pl.when(cnt_ref[e] > 0)
                def _():
                    start = pl.multiple_of(lb_ref[e], bm)
                    cp = pltpu.make_async_copy(zeros, rows_ref.at[pl.ds(start * ROW_SUB, bm * ROW_SUB)], zsem)
                    cp.wait() if wait else cp.start()

    def row_copy(t, k):
        return pltpu.make_async_copy(_row(h_ref, t), _row(rows_ref, dest_ref[t * TOP_K + k]), sem)

    for wait in (False, True):
        def body(g, carry):
            for u in range(DMA_UNROLL):
                for k in range(TOP_K):
                    cp = row_copy(g * DMA_UNROLL + u, k)
                    cp.wait() if wait else cp.start(priority=k % 2)
            return carry
        lax.fori_loop(0, tt // DMA_UNROLL, body, 0)


def _dispatch(last_block, cnt, dest, h2p, n_rows, bm):
    T = h2p.shape[0] // ROW_SUB
    tt = min(512, T)
    grid_spec = pltpu.PrefetchScalarGridSpec(
        num_scalar_prefetch=2,
        grid=(T // tt,),
        in_specs=[
            pl.BlockSpec((tt * TOP_K,), lambda i, lb, c: (i,), memory_space=pltpu.SMEM),
            pl.BlockSpec((tt * ROW_SUB, LANES), lambda i, lb, c: (i, 0)),
        ],
        out_specs=pl.BlockSpec(memory_space=pl.ANY),
        scratch_shapes=[pltpu.VMEM((bm * ROW_SUB, LANES), jnp.uint32),
                        pltpu.SemaphoreType.DMA, pltpu.SemaphoreType.DMA],
    )
    return pl.pallas_call(
        functools.partial(_dispatch_kernel, tt=tt, bm=bm),
        grid_spec=grid_spec,
        out_shape=jax.ShapeDtypeStruct((n_rows * ROW_SUB, LANES), jnp.uint32),
        compiler_params=_cparams(("arbitrary",)),
        name="dispatch",
    )(last_block, cnt, dest, h2p)


def _expert_kernel(be_ref, nu_ref, x_ref, wgu_ref, bgu_ref, wd_ref, bd_ref, o_ref, wgu_bf, wd_bf):
    i = pl.program_id(0)

    @pl.when(i < nu_ref[0])
    def _():
        @pl.when((i == 0) | (be_ref[i] != be_ref[jnp.maximum(i - 1, 0)]))
        def _():
            wgu_bf[...] = wgu_ref[0].astype(BF16)
            wd_bf[...] = wd_ref[0].astype(BF16)

        lo, hi = _unpack_rows(_load_rows(x_ref))
        x = jnp.concatenate([lo, hi], axis=1).astype(BF16)
        gu = jnp.dot(x, wgu_bf[...], preferred_element_type=F32) + bgu_ref[0]
        gate = jnp.minimum(gu[:, :D_FF], SWIGLU_LIMIT)
        up = jnp.clip(gu[:, D_FF:], -SWIGLU_LIMIT, SWIGLU_LIMIT)
        glu = gate * jax.nn.sigmoid(SWIGLU_ALPHA * gate)
        act = ((up + 1.0) * glu).astype(BF16)
        _store_rows(o_ref, _pack_rows(jnp.dot(act, wd_bf[...], preferred_element_type=F32) + bd_ref[0]))


def _experts(block_expert, n_used, rows, wgu, bgu, wd, bd, bm):
    n_rows = rows.shape[0] // ROW_SUB
    nb = n_rows // bm
    row_map = lambda i, be, nu: (jnp.minimum(i, nu[0] - 1), 0)
    exp_map = lambda i, be, nu: (be[jnp.minimum(i, nu[0] - 1)], 0, 0)
    grid_spec = pltpu.PrefetchScalarGridSpec(
        num_scalar_prefetch=2,
        grid=(nb,),
        in_specs=[
            pl.BlockSpec((bm * ROW_SUB, LANES), row_map),
            pl.BlockSpec((1, D_MODEL, 2 * D_FF), exp_map),
            pl.BlockSpec((1, 1, 2 * D_FF), exp_map),
            pl.BlockSpec((1, D_FF, D_MODEL), exp_map),
            pl.BlockSpec((1, 1, D_MODEL), exp_map),
        ],
        out_specs=pl.BlockSpec((bm * ROW_SUB, LANES), row_map),
        scratch_shapes=[pltpu.VMEM((D_MODEL, 2 * D_FF), BF16), pltpu.VMEM((D_FF, D_MODEL), BF16)],
    )
    return pl.pallas_call(
        _expert_kernel,
        grid_spec=grid_spec,
        out_shape=jax.ShapeDtypeStruct((n_rows * ROW_SUB, LANES), jnp.uint32),
        compiler_params=_cparams(("arbitrary",)),
        name="experts",
    )(block_expert, n_used, rows, wgu, bgu, wd, bd)


def _combine_kernel(dest_ref, next_ref, x1_ref, gate_ref, y_ref, o_ref, ybuf, sems, *, tc):
    i = pl.program_id(0)
    n = pl.num_programs(0)

    def gather(idx_ref, slot, wait):
        def row_copy(t, k):
            return pltpu.make_async_copy(_row(y_ref, idx_ref[t * TOP_K + k]), _row(ybuf.at[slot, k], t),
                                         sems.at[slot])

        def body(g, carry):
            for u in range(DMA_UNROLL):
                for k in range(TOP_K):
                    cp = row_copy(g * DMA_UNROLL + u, k)
                    cp.wait() if wait else cp.start(priority=k % 2)
            return carry
        lax.fori_loop(0, tc // DMA_UNROLL, body, 0)

    slot = i % 2

    @pl.when(i == 0)
    def _():
        gather(dest_ref, 0, False)

    @pl.when(i + 1 < n)
    def _():
        gather(next_ref, 1 - slot, False)

    gather(dest_ref, slot, True)

    x1 = x1_ref[...]
    acc_lo, acc_hi = x1[:, :PACKED], x1[:, PACKED:]
    g = gate_ref[...]
    for k in range(TOP_K):
        lo, hi = _unpack_rows(_load_rows(ybuf.at[slot, k]))
        acc_lo = acc_lo + g[:, k:k + 1] * lo
        acc_hi = acc_hi + g[:, k:k + 1] * hi
    o_ref[:, :PACKED] = acc_lo
    o_ref[:, PACKED:] = acc_hi


def _combine(dest, x1, gates, y_rows):
    T = x1.shape[0]
    tc = min(256, T)
    n = T // tc
    return pl.pallas_call(
        functools.partial(_combine_kernel, tc=tc),
        grid=(n,),
        in_specs=[
            pl.BlockSpec((tc * TOP_K,), lambda i: (i,), memory_space=pltpu.SMEM),
            pl.BlockSpec((tc * TOP_K,), lambda i: (jnp.minimum(i + 1, n - 1),), memory_space=pltpu.SMEM),
            pl.BlockSpec((tc, D_MODEL), lambda i: (i, 0)),
            pl.BlockSpec((tc, LANES), lambda i: (i, 0)),
            pl.BlockSpec(memory_space=pl.ANY),
        ],
        out_specs=pl.BlockSpec((tc, D_MODEL), lambda i: (i, 0)),
        out_shape=jax.ShapeDtypeStruct((T, D_MODEL), F32),
        scratch_shapes=[pltpu.VMEM((2, TOP_K, tc * ROW_SUB, LANES), jnp.uint32),
                        pltpu.SemaphoreType.DMA((2,))],
        compiler_params=_cparams(("arbitrary",)),
        name="combine",
    )(dest, dest, x1, gates, y_rows)


def _pad_lanes(v):
    return jnp.pad(v, ((0, 0), (0, LANES - v.shape[-1])))


def kernel(x, mix_norm_g, w_in, q_norm_g, k_norm_g, lambda_q1, lambda_k1, lambda_q2, lambda_k2,
           attn_subln_g, conv_w, conv_b, dt_bias, a_log, d_skip, ssm_norm_g, w_attn_proj,
           w_ssm_proj, w_out, ffn_norm_g, w_router, b_router, w_gate_up, b_gate_up, w_down, b_down):
    B, S, D = x.shape
    T = B * S
    xf = x.reshape(T, D)
    layer = 0

    wi = w_in[layer]
    o_q, o_k, o_v = 0, ATT_WIDTH, 2 * ATT_WIDTH
    o_z = 3 * ATT_WIDTH
    o_xbc = o_z + SSM_D_INNER
    o_dt = o_xbc + SSM_XBC
    o_ga = o_dt + SSM_HEADS
    w_main = jnp.concatenate([wi[:, o_xbc:o_dt], wi[:, o_z:o_xbc], wi[:, o_q:o_z], wi[:, o_ga:]],
                             axis=1).astype(BF16)
    w_dt = _pad_lanes(wi[:, o_dt:o_ga]).astype(BF16)

    proj, dt_raw = _in_proj(xf, mix_norm_g[layer][None, :], w_main, w_dt)

    half = ATT_HEAD_DIM // 2
    inv = ROPE_THETA ** (-jnp.arange(0, ATT_HEAD_DIM, 2, dtype=F32) / ATT_HEAD_DIM)
    ang = jnp.arange(S, dtype=F32)[:, None] * inv[None, :]
    cos2 = jnp.tile(jnp.cos(ang), (1, LANES // half))
    sin2 = jnp.tile(jnp.concatenate([-jnp.sin(ang), jnp.sin(ang)], axis=1), (1, LANES // ATT_HEAD_DIM))
    qg2 = jnp.tile(q_norm_g[layer], 2)[None, :]
    kg2 = jnp.tile(k_norm_g[layer], 2)[None, :]
    partner = lambda g: jnp.tile(jnp.roll(g, half), 2)[None, :]
    q_scale = ATT_HEAD_DIM ** -0.5 * LOG2E
    att = _attention(proj, cos2 * qg2 * q_scale, sin2 * partner(q_norm_g[layer]) * q_scale,
                     cos2 * kg2, sin2 * partner(k_norm_g[layer]), qg2, kg2,
                     lambda_q1[layer][None, :], lambda_k1[layer][None, :],
                     lambda_q2[layer][None, :], lambda_k2[layer][None, :],
                     attn_subln_g[layer][None, :], B, S)

    ssm = _ssd(proj, dt_raw, conv_w[layer], conv_b[layer][None, :],
               _pad_lanes(dt_bias[layer][None, :]), _pad_lanes(a_log[layer][None, :]),
               jnp.repeat(d_skip[layer], SSM_HEAD_DIM)[None, :], ssm_norm_g[layer][None, :], B, S)

    x1, h2p, idx_rank, gates, counts = _merge(
        att, ssm, proj, xf, w_attn_proj[layer].astype(BF16), w_ssm_proj[layer].astype(BF16),
        w_out[layer].astype(BF16), ffn_norm_g[layer][None, :],
        _pad_lanes(w_router[layer]).astype(BF16), _pad_lanes(b_router[layer][None, :]))

    bm = 512
    A = T * TOP_K
    n_rows = (A + N_EXPERTS * (bm - 1)) // bm * bm
    cnt = counts[0, :N_EXPERTS]
    padded = (cnt + bm - 1) // bm * bm
    pend = jnp.cumsum(padded)
    pstart = pend - padded
    n_used = (pend[-1:] // bm).astype(jnp.int32)
    block_start = jnp.arange(n_rows // bm, dtype=jnp.int32) * bm
    block_expert = jnp.minimum(jnp.sum(block_start[:, None] >= pend[None, :], axis=1),
                               N_EXPERTS - 1).astype(jnp.int32)
    top_idx = idx_rank[:, :TOP_K]
    onehot = top_idx[:, :, None] == jnp.arange(N_EXPERTS, dtype=jnp.int32)[None, None, :]
    dest = (jnp.sum(jnp.where(onehot, pstart[None, None, :], 0), axis=-1)
            + idx_rank[:, TOP_K:2 * TOP_K]).reshape(A).astype(jnp.int32)

    rows = _dispatch((pend - bm).astype(jnp.int32), cnt.astype(jnp.int32), dest, h2p, n_rows, bm)
    y_rows = _experts(block_expert, n_used, rows, w_gate_up[layer], b_gate_up[layer][:, None, :],
                      w_down[layer], b_down[layer][:, None, :], bm)
    out = _combine(dest, x1, gates, y_rows)
    return out.reshape(B, S, D)
```

```python
import functools
import math

import jax
import jax.numpy as jnp
from jax import lax
from jax.experimental import pallas as pl
from jax.experimental.pallas import tpu as pltpu

F32 = jnp.float32
BF16 = jnp.bfloat16

D_MODEL = 1024
EPS = 1e-6
ATT_HEADS = 8
ATT_HEAD_DIM = 64
ATT_V_DIM = 2 * ATT_HEAD_DIM
ATT_WIDTH = ATT_HEADS * ATT_V_DIM
ROPE_THETA = 10000.0
SSM_D_INNER = 2 * D_MODEL
SSM_HEAD_DIM = 64
SSM_HEADS = SSM_D_INNER // SSM_HEAD_DIM
SSM_GROUPS = 8
SSM_STATE = 128
SSM_CONV = 4
SSM_CHUNK = 128
SSM_XBC = SSM_D_INNER + 2 * SSM_GROUPS * SSM_STATE
N_EXPERTS = 32
TOP_K = 4
D_FF = D_MODEL
SWIGLU_LIMIT = 7.0
SWIGLU_ALPHA = 1.702
LAM_INIT = 0.8 - 0.6 * math.exp(-0.3 * 0)

LANES = 128
HALO = 16

COL_XBC = 0
COL_Z = SSM_XBC
COL_Q = COL_Z + SSM_D_INNER
COL_K = COL_Q + ATT_WIDTH
COL_V = COL_K + ATT_WIDTH
COL_GA = COL_V + ATT_WIDTH
COL_GS = COL_GA + D_MODEL
PROJ_COLS = COL_GS + D_MODEL

VMEM_LIMIT = 56 * 1024 * 1024


def _cparams(sem):
    return pltpu.CompilerParams(dimension_semantics=sem, vmem_limit_bytes=VMEM_LIMIT)


PACKED = D_MODEL // 2


def _pack_rows(x):
    lo = lax.bitcast_convert_type(x[:, :PACKED].astype(BF16).astype(F32), jnp.uint32)
    hi = lax.bitcast_convert_type(x[:, PACKED:].astype(BF16).astype(F32), jnp.uint32)
    return hi | (lo >> 16)


def _unpack_rows(p):
    lo = lax.bitcast_convert_type(p << 16, F32)
    hi = lax.bitcast_convert_type(p & jnp.uint32(0xFFFF0000), F32)
    return lo, hi


ROW_SUB = PACKED // LANES


def _store_rows(ref, packed):
    for s in range(ROW_SUB):
        ref[pl.ds(s, packed.shape[0], stride=ROW_SUB), :] = packed[:, s * LANES:(s + 1) * LANES]


def _load_rows(ref):
    n = ref.shape[0] // ROW_SUB
    return jnp.concatenate([ref[pl.ds(s, n, stride=ROW_SUB), :] for s in range(ROW_SUB)], axis=1)


def _row(ref, r):
    return ref.at[pl.ds(pl.multiple_of(r * ROW_SUB, ROW_SUB), ROW_SUB)]


def _in_proj_kernel(x_ref, g_ref, w_ref, wdt_ref, o_ref, dt_ref, h_scr):
    @pl.when(pl.program_id(1) == 0)
    def _():
        x = x_ref[...]
        ms = jnp.mean(x * x, axis=-1, keepdims=True)
        hb = (x * lax.rsqrt(ms + EPS) * g_ref[...]).astype(BF16)
        h_scr[...] = hb
        dt_ref[...] = jnp.dot(hb, wdt_ref[...], preferred_element_type=F32)

    o_ref[...] = jnp.dot(h_scr[...], w_ref[...], preferred_element_type=F32).astype(BF16)


def _in_proj(xf, g, w_main, w_dt):
    T = xf.shape[0]
    tm = min(2048, T)
    tn = 1024
    return pl.pallas_call(
        _in_proj_kernel,
        grid=(T // tm, PROJ_COLS // tn),
        in_specs=[
            pl.BlockSpec((tm, D_MODEL), lambda i, j: (i, 0)),
            pl.BlockSpec((1, D_MODEL), lambda i, j: (0, 0)),
            pl.BlockSpec((D_MODEL, tn), lambda i, j: (0, j)),
            pl.BlockSpec((D_MODEL, LANES), lambda i, j: (0, 0)),
        ],
        out_specs=[
            pl.BlockSpec((tm, tn), lambda i, j: (i, j)),
            pl.BlockSpec((tm, LANES), lambda i, j: (i, 0)),
        ],
        out_shape=[
            jax.ShapeDtypeStruct((T, PROJ_COLS), BF16),
            jax.ShapeDtypeStruct((T, LANES), F32),
        ],
        scratch_shapes=[pltpu.VMEM((tm, D_MODEL), BF16)],
        compiler_params=_cparams(("arbitrary", "arbitrary")),
        name="in_proj",
    )(xf, g, w_main, w_dt)


LOG2E = 1.4426950408889634
SHIFT_LIMIT = 57.0
BOUND_MARGIN = 1.02


def _attn_kernel(q_ref, k_ref, v_ref, qc_ref, qs_ref, kc_ref, ks_ref, qg_ref, kg_ref,
                 lq1_ref, lk1_ref, lq2_ref, lk2_ref, sg_ref, o_ref, k_scr, v_scr, q1_scr, q2_scr, *, tq, seq):
    p = pl.program_id(2)
    nq = seq // tq
    paired = nq > 1
    lane = lax.broadcasted_iota(jnp.int32, (1, LANES), 1)
    first = lane < ATT_HEAD_DIM
    one_col = jnp.where(lane == 0, 1.0, 0.0)

    ri = lax.broadcasted_iota(jnp.int32, (LANES, LANES), 0)
    ci = lax.broadcasted_iota(jnp.int32, (LANES, LANES), 1)
    same_comp = (ri // ATT_HEAD_DIM == ci // ATT_HEAD_DIM).astype(BF16)
    swap_half = ((ri // ATT_HEAD_DIM == ci // ATT_HEAD_DIM)
                 & ((ri - ci == ATT_HEAD_DIM // 2) | (ci - ri == ATT_HEAD_DIM // 2))).astype(BF16)

    def norm_rope(xb, gcos, gsin):
        xf = xb.astype(F32)
        sq = xf * xf
        hi = sq.astype(BF16)
        lo = (sq - hi.astype(F32)).astype(BF16)
        ms = (jnp.dot(hi, same_comp, preferred_element_type=F32)
              + jnp.dot(lo, same_comp, preferred_element_type=F32)) * (1.0 / ATT_HEAD_DIM)
        xr = jnp.dot(xb, swap_half, preferred_element_type=F32)
        return lax.rsqrt(ms + EPS) * (xf * gcos + xr * gsin)

    ub = (ATT_HEAD_DIM * ATT_HEAD_DIM ** -0.5 * LOG2E * BOUND_MARGIN
          * jnp.max(jnp.abs(qg_ref[...])) * jnp.max(jnp.abs(kg_ref[...])))
    safe = ub <= SHIFT_LIMIT

    @pl.when(p == 0)
    def _():
        ones = jnp.broadcast_to(one_col, (tq, LANES)).astype(BF16)
        shift = jnp.broadcast_to(jnp.where(lane == 0, -ub, 0.0), (tq, LANES)).astype(BF16)
        for r in range(0, seq, tq):
            rows = slice(r, r + tq)
            k_scr[rows, :LANES] = norm_rope(k_ref[rows, :], kc_ref[rows, :], ks_ref[rows, :]).astype(BF16)
            k_scr[rows, LANES:] = ones
            v_scr[rows, :LANES] = v_ref[rows, :]
            v_scr[rows, LANES:] = ones
            qb = norm_rope(q_ref[rows, :], qc_ref[rows, :], qs_ref[rows, :]).astype(BF16)
            q1_scr[rows, :LANES] = jnp.where(first, qb, jnp.zeros_like(qb))
            q1_scr[rows, LANES:] = shift
            q2_scr[rows, :LANES] = jnp.where(first, jnp.zeros_like(qb), qb)
            q2_scr[rows, LANES:] = shift

    keep = (lax.broadcasted_iota(jnp.int32, (tq, tq), 1) <= lax.broadcasted_iota(jnp.int32, (tq, tq), 0))
    nt = (((1,), (1,)), ((), ()))
    rows_a = pl.multiple_of(p * tq, tq)
    rows_b = pl.multiple_of((nq - 1 - p) * tq, tq)

    def finish(o1, o2, rows0):
        lam = (jnp.exp(jnp.sum(lq1_ref[...] * lk1_ref[...], axis=-1, keepdims=True))
               - jnp.exp(jnp.sum(lq2_ref[...] * lk2_ref[...], axis=-1, keepdims=True)) + LAM_INIT)
        o = o1 - lam * o2
        ms = jnp.mean(o * o, axis=-1, keepdims=True)
        o_ref[pl.ds(rows0, tq), :] = (o * lax.rsqrt(ms + EPS) * sg_ref[...] * (1.0 - LAM_INIT)).astype(BF16)

    @pl.when(safe)
    def _():
        def block(q_scr, qrow0, krow0, rows, cols, mask):
            t = lax.dot_general(q_scr[pl.ds(qrow0, rows), :], k_scr[pl.ds(krow0, cols), :], nt,
                                preferred_element_type=F32)
            if mask is not None:
                t = jnp.where(mask, t, -jnp.inf)
            return jnp.dot(jnp.exp2(t).astype(BF16), v_scr[pl.ds(krow0, cols), :], preferred_element_type=F32)

        def tile(qrow0, krow0, masked):
            if not masked:
                return (block(q1_scr, qrow0, krow0, tq, tq, None), block(q2_scr, qrow0, krow0, tq, tq, None))
            h = tq // 2
            q_low = pl.multiple_of(qrow0 + h, h)
            k_hi = pl.multiple_of(krow0 + h, h)
            outs = []
            for q_scr in (q1_scr, q2_scr):
                left_cols = block(q_scr, qrow0, krow0, tq, h, keep[:, :h])
                corner = block(q_scr, q_low, k_hi, h, h, keep[:h, :h])
                outs.append(jnp.concatenate([left_cols[:h], left_cols[h:] + corner], axis=0))
            return tuple(outs)

        def normalised(acc):
            return acc[:, :LANES] / acc[:, LANES:LANES + 1]

        a1, a2 = tile(rows_a, rows_a, True)
        if paired:
            b1, b2 = tile(rows_b, rows_b, True)
            for s in range(nq - 1):
                to_a = s < p
                qrow0 = jnp.where(to_a, rows_a, rows_b)
                krow0 = pl.multiple_of(jnp.where(to_a, s, s - p) * tq, tq)
                d1, d2 = tile(pl.multiple_of(qrow0, tq), krow0, False)
                a1 = a1 + jnp.where(to_a, d1, 0.0)
                a2 = a2 + jnp.where(to_a, d2, 0.0)
                b1 = b1 + jnp.where(to_a, 0.0, d1)
                b2 = b2 + jnp.where(to_a, 0.0, d2)
            finish(normalised(b1), normalised(b2), rows_b)
        finish(normalised(a1), normalised(a2), rows_a)

    @pl.when(jnp.logical_not(safe))
    def _():
        def update(t, vt, m, l, acc):
            m_new = jnp.maximum(m, jnp.max(t, axis=-1, keepdims=True))
            alpha = jnp.exp2(m - m_new)
            pr = jnp.exp2(t - m_new)
            l_new = alpha * l + jnp.sum(pr, axis=-1, keepdims=True)
            acc_new = alpha * acc + jnp.dot(pr.astype(BF16), vt, preferred_element_type=F32)
            return m_new, l_new, acc_new

        def online(rows0, n_off):
            q1 = q1_scr[pl.ds(rows0, tq), :LANES]
            q2 = q2_scr[pl.ds(rows0, tq), :LANES]

            def step(krow0, carry, masked):
                m1, l1, c1, m2, l2, c2 = carry
                kt = k_scr[pl.ds(krow0, tq), :LANES]
                vt = v_scr[pl.ds(krow0, tq), :LANES]
                t1 = lax.dot_general(q1, kt, nt, preferred_element_type=F32)
                t2 = lax.dot_general(q2, kt, nt, preferred_element_type=F32)
                if masked:
                    t1 = jnp.where(keep, t1, -jnp.inf)
                    t2 = jnp.where(keep, t2, -jnp.inf)
                m1, l1, c1 = update(t1, vt, m1, l1, c1)
                m2, l2, c2 = update(t2, vt, m2, l2, c2)
                return m1, l1, c1, m2, l2, c2

            neg = jnp.full((tq, 1), -jnp.inf, F32)
            zero1 = jnp.zeros((tq, 1), F32)
            zacc = jnp.zeros((tq, ATT_V_DIM), F32)
            carry = step(rows0, (neg, zero1, zacc, neg, zero1, zacc), True)
            m1, l1, c1, m2, l2, c2 = lax.fori_loop(
                0, n_off, lambda j, c: step(pl.multiple_of(j * tq, tq), c, False), carry)
            finish(c1 / l1, c2 / l2, rows0)

        online(rows_a, p)
        if paired:
            online(rows_b, nq - 1 - p)


def _attention(proj, qcos, qsin, kcos, ksin, qg2, kg2, lq1, lk1, lq2, lk2, sg, batch, seq):
    T = batch * seq
    tq = min(512, seq)
    nq = seq // tq
    assert nq == 1 or nq % 2 == 0, "query tiles are processed in balanced pairs"
    vec = lambda n: pl.BlockSpec((1, n), lambda b, h, i: (0, 0))
    table = pl.BlockSpec((seq, LANES), lambda b, h, i: (0, 0))
    head = lambda col: pl.BlockSpec((seq, LANES), lambda b, h, i: (b, col // LANES + h))
    return pl.pallas_call(
        functools.partial(_attn_kernel, tq=tq, seq=seq),
        grid=(batch, ATT_HEADS, max(nq // 2, 1)),
        in_specs=[
            head(COL_Q), head(COL_K), head(COL_V),
            table, table, table, table,
            vec(LANES), vec(LANES),
            vec(ATT_HEAD_DIM), vec(ATT_HEAD_DIM), vec(ATT_HEAD_DIM), vec(ATT_HEAD_DIM),
            vec(LANES),
        ],
        out_specs=pl.BlockSpec((seq, LANES), lambda b, h, i: (b, h)),
        out_shape=jax.ShapeDtypeStruct((T, ATT_WIDTH), BF16),
        scratch_shapes=[pltpu.VMEM((seq, 2 * LANES), BF16) for _ in range(4)],
        compiler_params=_cparams(("arbitrary", "arbitrary", "arbitrary")),
        name="attention",
    )(proj, proj, proj, qcos, qsin, kcos, ksin, qg2, kg2, lq1, lk1, lq2, lk2, sg)


SSD_SUB = 2


def _ssd_kernel(xbc_ref, z_ref, dt_ref, cw_ref, cb_ref, dtb_ref, alog_ref, dsk_ref, ng_ref,
                o_ref, halo, state, xs_scr, bc_scr, y_scr):
    Q = SSM_CHUNK
    N = SSM_STATE
    P2 = 2 * SSM_HEAD_DIM

    @pl.when(pl.program_id(1) == 0)
    def _():
        halo[...] = jnp.zeros_like(halo)
        state[...] = jnp.zeros_like(state)

    taps = SSM_CONV - 1
    sr = lax.broadcasted_iota(jnp.int32, (taps * Q, HALO + Q), 0)
    sc = lax.broadcasted_iota(jnp.int32, (taps * Q, HALO + Q), 1)
    shift = (sc == (sr % Q) + HALO - taps + sr // Q).astype(BF16)
    ri = lax.broadcasted_iota(jnp.int32, (Q, Q), 0)
    ci = lax.broadcasted_iota(jnp.int32, (Q, Q), 1)
    tri = ri >= ci
    lane = lax.broadcasted_iota(jnp.int32, (1, P2), 1)
    left = lane < SSM_HEAD_DIM
    gw = SSM_D_INNER // SSM_GROUPS

    for sub in range(SSD_SUB):
        r0 = sub * Q
        xs_sub, bc_sub, y_sub = xs_scr.at[sub], bc_scr.at[sub], y_scr.at[sub]

        cwid = 512
        for c0 in range(0, SSM_XBC, cwid):
            cur = xbc_ref[r0:r0 + Q, c0:c0 + cwid]
            hist = halo[:, c0:c0 + cwid] if sub == 0 else xbc_ref[r0 - HALO:r0, c0:c0 + cwid]
            ext = jnp.concatenate([hist, cur], axis=0)
            shifted = jnp.dot(shift, ext, preferred_element_type=F32)
            acc = cb_ref[:, c0:c0 + cwid] + cw_ref[taps:SSM_CONV, c0:c0 + cwid] * cur.astype(F32)
            for k in range(taps):
                acc = acc + cw_ref[k:k + 1, c0:c0 + cwid] * shifted[k * Q:(k + 1) * Q, :]
            act = acc * jax.nn.sigmoid(acc)
            if c0 < SSM_D_INNER:
                xs_sub[:, c0:c0 + cwid] = act
            else:
                bc_sub[:, c0 - SSM_D_INNER:c0 - SSM_D_INNER + cwid] = act.astype(BF16)

        dt = jax.nn.softplus(dt_ref[r0:r0 + Q, :] + dtb_ref[...])
        da = dt * (-jnp.exp(alog_ref[...]) * LOG2E)
        acs = jnp.dot(tri.astype(F32), da, preferred_element_type=F32,
                      precision=lax.Precision.HIGHEST)
        acs_t = acs.T
        dt_t = dt.T
        w_t = dt_t * jnp.exp2(acs_t[:, Q - 1:Q] - acs_t)

        for g in range(SSM_GROUPS):
            bm = bc_sub[:, g * N:(g + 1) * N]
            cm = bc_sub[:, SSM_GROUPS * N + g * N:SSM_GROUPS * N + (g + 1) * N]
            cb = lax.dot_general(cm, bm, (((1,), (1,)), ((), ())), preferred_element_type=F32)
            bm_t = bm.astype(F32).T
            cm_f = cm.astype(F32)
            for pr in range(2):
                pair = 2 * g + pr
                xs_pair = xs_sub[:, pair * P2:(pair + 1) * P2].astype(BF16)
                prev = state[pair]
                rhs = jnp.concatenate([xs_pair, prev.astype(BF16)], axis=0)
                ys, sts, decs = [], [], []
                for r in range(2):
                    h = 2 * pair + r
                    a_col = acs[:, h:h + 1]
                    seg = a_col - acs_t[h:h + 1, :]
                    decay = jnp.exp2(jnp.where(tri, seg, -jnp.inf))
                    m_h = cb * decay * dt_t[h:h + 1, :]
                    e_h = cm_f * jnp.exp2(a_col)
                    lhs = jnp.concatenate([m_h, e_h], axis=1).astype(BF16)
                    ys.append(jnp.dot(lhs, rhs, preferred_element_type=F32))
                    sts.append(jnp.dot((bm_t * w_t[h:h + 1, :]).astype(BF16), xs_pair,
                                       preferred_element_type=F32))
                    decs.append(jnp.exp2(acs[Q - 1:Q, h:h + 1]))
                y_sub[:, pair * P2:(pair + 1) * P2] = jnp.where(left, ys[0], ys[1])
                dec = jnp.where(left, decs[0], decs[1])
                state[pair] = dec * prev + jnp.where(left, sts[0], sts[1])

        y = y_sub[...] + dsk_ref[...] * xs_sub[...]
        zf = z_ref[r0:r0 + Q, :].astype(F32)
        y = y * (zf * jax.nn.sigmoid(zf))
        for g in range(SSM_GROUPS):
            yg = y[:, g * gw:(g + 1) * gw]
            ms = jnp.mean(yg * yg, axis=-1, keepdims=True)
            o_ref[r0:r0 + Q, g * gw:(g + 1) * gw] = (yg * lax.rsqrt(ms + EPS)
                                                     * ng_ref[:, g * gw:(g + 1) * gw]).astype(BF16)

    halo[...] = xbc_ref[SSD_SUB * Q - HALO:SSD_SUB * Q, :]


def _ssd(proj, dt_raw, conv_w, conv_b, dt_bias, a_log, dsk, ng, batch, seq):
    T = batch * seq
    Q = SSM_CHUNK
    rows = SSD_SUB * Q
    nc = seq // rows
    vec = lambda r, n: pl.BlockSpec((r, n), lambda b, c: (0, 0))
    return pl.pallas_call(
        _ssd_kernel,
        grid=(batch, nc),
        in_specs=[
            pl.BlockSpec((rows, SSM_XBC), lambda b, c: (b * nc + c, COL_XBC // SSM_XBC)),
            pl.BlockSpec((rows, SSM_D_INNER), lambda b, c: (b * nc + c, COL_Z // SSM_D_INNER)),
            pl.BlockSpec((rows, LANES), lambda b, c: (b * nc + c, 0)),
            vec(SSM_CONV, SSM_XBC), vec(1, SSM_XBC), vec(1, LANES), vec(1, LANES),
            vec(1, SSM_D_INNER), vec(1, SSM_D_INNER),
        ],
        out_specs=pl.BlockSpec((rows, SSM_D_INNER), lambda b, c: (b * nc + c, 0)),
        out_shape=jax.ShapeDtypeStruct((T, SSM_D_INNER), BF16),
        scratch_shapes=[
            pltpu.VMEM((HALO, SSM_XBC), BF16),
            pltpu.VMEM((SSM_HEADS // 2, SSM_STATE, 2 * SSM_HEAD_DIM), F32),
            pltpu.VMEM((SSD_SUB, Q, SSM_D_INNER), F32),
            pltpu.VMEM((SSD_SUB, Q, 2 * SSM_GROUPS * SSM_STATE), BF16),
            pltpu.VMEM((SSD_SUB, Q, SSM_D_INNER), F32),
        ],
        compiler_params=_cparams(("arbitrary", "arbitrary")),
        name="ssd",
    )(proj, proj, dt_raw, conv_w, conv_b, dt_bias, a_log, dsk, ng)


def _merge_kernel(att_ref, ssm_ref, ga_ref, gs_ref, x_ref, wap_ref, wsp_ref, wo_ref, fg_ref,
                  wr_ref, br_ref, x1_ref, h2_ref, idx_ref, gate_ref, cnt_ref, base, *, tm):
    i = pl.program_id(0)

    @pl.when(i == 0)
    def _():
        base[...] = jnp.zeros_like(base)

    pa = jnp.dot(att_ref[...], wap_ref[...], preferred_element_type=F32)
    ps = jnp.dot(ssm_ref[...], wsp_ref[...], preferred_element_type=F32)
    merged = (jax.nn.sigmoid(ga_ref[...].astype(F32)) * pa
              + jax.nn.sigmoid(gs_ref[...].astype(F32)) * ps)
    x1 = x_ref[...] + jnp.dot(merged.astype(BF16), wo_ref[...], preferred_element_type=F32)
    x1_ref[...] = x1
    ms = jnp.mean(x1 * x1, axis=-1, keepdims=True)
    h2 = x1 * lax.rsqrt(ms + EPS) * fg_ref[...]
    _store_rows(h2_ref, _pack_rows(h2))

    logits = jnp.dot(h2.astype(BF16), wr_ref[...], preferred_element_type=F32) + br_ref[...]
    lg = logits.T[:N_EXPERTS, :]
    row_f = lax.broadcasted_iota(jnp.int32, (N_EXPERTS, tm), 0).astype(F32)
    vals, idxs, sels = [], [], []
    for _ in range(TOP_K):
        m = jnp.max(lg, axis=0, keepdims=True)
        idx = jnp.min(jnp.where(lg == m, row_f, float(N_EXPERTS)), axis=0, keepdims=True)
        sel = row_f == idx
        vals.append(m)
        idxs.append(idx)
        sels.append(sel)
        lg = jnp.where(sel, -jnp.inf, lg)
    es = [jnp.exp(v - vals[0]) for v in vals]
    den = es[0] + es[1] + es[2] + es[3]

    multi = jnp.zeros((N_EXPERTS, tm), F32)
    for sel in sels:
        multi = jnp.where(sel, 1.0, multi)
    ri = lax.broadcasted_iota(jnp.int32, (tm, tm), 0)
    ci = lax.broadcasted_iota(jnp.int32, (tm, tm), 1)
    before = jnp.dot(multi.astype(BF16), (ri < ci).astype(BF16), preferred_element_type=F32)
    before = before + base[...]
    ranks = [jnp.sum(jnp.where(sel, before, 0.0), axis=0, keepdims=True) for sel in sels]
    idx_ref[...] = jnp.concatenate(idxs + ranks, axis=0).astype(jnp.int32)
    gate_ref[...] = jnp.concatenate([e / den for e in es] + [jnp.zeros_like(den)] * TOP_K, axis=0)
    base[...] = base[...] + jnp.sum(multi, axis=1, keepdims=True)
    cnt_ref[...] = jnp.broadcast_to(base[...], cnt_ref.shape).astype(jnp.int32)


def _merge(att, ssm, proj, xf, wap, wsp, wo, fg, wr, br):
    T = xf.shape[0]
    tm = min(512, T)
    full = lambda a: pl.BlockSpec(a.shape, lambda i: (0, 0))
    return pl.pallas_call(
        functools.partial(_merge_kernel, tm=tm),
        grid=(T // tm,),
        in_specs=[
            pl.BlockSpec((tm, ATT_WIDTH), lambda i: (i, 0)),
            pl.BlockSpec((tm, SSM_D_INNER), lambda i: (i, 0)),
            pl.BlockSpec((tm, D_MODEL), lambda i: (i, COL_GA // D_MODEL)),
            pl.BlockSpec((tm, D_MODEL), lambda i: (i, COL_GS // D_MODEL)),
            pl.BlockSpec((tm, D_MODEL), lambda i: (i, 0)),
            full(wap), full(wsp), full(wo), full(fg), full(wr), full(br),
        ],
        out_specs=[
            pl.BlockSpec((tm, D_MODEL), lambda i: (i, 0)),
            pl.BlockSpec((tm * ROW_SUB, LANES), lambda i: (i, 0)),
            pl.BlockSpec((2 * TOP_K, tm), lambda i: (0, i)),
            pl.BlockSpec((2 * TOP_K, tm), lambda i: (0, i)),
            pl.BlockSpec((N_EXPERTS, LANES), lambda i: (0, 0)),
        ],
        out_shape=[
            jax.ShapeDtypeStruct((T, D_MODEL), F32),
            jax.ShapeDtypeStruct((T * ROW_SUB, LANES), jnp.uint32),
            jax.ShapeDtypeStruct((2 * TOP_K, T), jnp.int32),
            jax.ShapeDtypeStruct((2 * TOP_K, T), F32),
            jax.ShapeDtypeStruct((N_EXPERTS, LANES), jnp.int32),
        ],
        scratch_shapes=[pltpu.VMEM((N_EXPERTS, 1), F32)],
        compiler_params=_cparams(("arbitrary",)),
        name="merge_router",
    )(att, ssm, proj, proj, xf, wap, wsp, wo, fg, wr, br)


DMA_UNROLL = 8


def _dispatch_kernel(lb_ref, cnt_ref, dest_ref, h_ref, rows_ref, zeros, zsem, sem, *, tt, bm):
    @pl.when(pl.program_id(0) == 0)
    def _():
        zeros[...] = jnp.zeros_like(zeros)
        for wait in (False, True):
            for e in range(N_EXPERTS):
                @pl.when(cnt_ref[e] > 0)
                def _():
                    start = pl.multiple_of(lb_ref[e], bm)
                    cp = pltpu.make_async_copy(zeros, rows_ref.at[pl.ds(start * ROW_SUB, bm * ROW_SUB)], zsem)
                    cp.wait() if wait else cp.start()

    def row_copy(t, k):
        return pltpu.make_async_copy(_row(h_ref, t), _row(rows_ref, dest_ref[t * TOP_K + k]), sem)

    for wait in (False, True):
        def body(g, carry):
            for u in range(DMA_UNROLL):
                for k in range(TOP_K):
                    cp = row_copy(g * DMA_UNROLL + u, k)
                    cp.wait() if wait else cp.start(priority=k % 2)
            return carry
        lax.fori_loop(0, tt // DMA_UNROLL, body, 0)


def _dispatch(last_block, cnt, dest, h2p, n_rows, bm):
    T = h2p.shape[0] // ROW_SUB
    tt = min(1024, T)
    grid_spec = pltpu.PrefetchScalarGridSpec(
        num_scalar_prefetch=2,
        grid=(T // tt,),
        in_specs=[
            pl.BlockSpec((tt * TOP_K,), lambda i, lb, c: (i,), memory_space=pltpu.SMEM),
            pl.BlockSpec((tt * ROW_SUB, LANES), lambda i, lb, c: (i, 0)),
        ],
        out_specs=pl.BlockSpec(memory_space=pl.ANY),
        scratch_shapes=[pltpu.VMEM((bm * ROW_SUB, LANES), jnp.uint32),
                        pltpu.SemaphoreType.DMA, pltpu.SemaphoreType.DMA],
    )
    return pl.pallas_call(
        functools.partial(_dispatch_kernel, tt=tt, bm=bm),
        grid_spec=grid_spec,
        out_shape=jax.ShapeDtypeStruct((n_rows * ROW_SUB, LANES), jnp.uint32),
        compiler_params=_cparams(("arbitrary",)),
        name="dispatch",
    )(last_block, cnt, dest, h2p)


def _expert_kernel(be_ref, nu_ref, x_ref, wgu_ref, bgu_ref, wd_ref, bd_ref, o_ref, wgu_bf, wd_bf):
    i = pl.program_id(0)

    @pl.when(i < nu_ref[0])
    def _():
        @pl.when((i == 0) | (be_ref[i] != be_ref[jnp.maximum(i - 1, 0)]))
        def _():
            wgu_bf[...] = wgu_ref[0].astype(BF16)
            wd_bf[...] = wd_ref[0].astype(BF16)

        lo, hi = _unpack_rows(_load_rows(x_ref))
        x = jnp.concatenate([lo, hi], axis=1).astype(BF16)
        gu = jnp.dot(x, wgu_bf[...], preferred_element_type=F32) + bgu_ref[0]
        gate = jnp.minimum(gu[:, :D_FF], SWIGLU_LIMIT)
        up = jnp.clip(gu[:, D_FF:], -SWIGLU_LIMIT, SWIGLU_LIMIT)
        glu = gate * jax.nn.sigmoid(SWIGLU_ALPHA * gate)
        act = ((up + 1.0) * glu).astype(BF16)
        _store_rows(o_ref, _pack_rows(jnp.dot(act, wd_bf[...], preferred_element_type=F32) + bd_ref[0]))


def _experts(block_expert, n_used, rows, wgu, bgu, wd, bd, bm):
    n_rows = rows.shape[0] // ROW_SUB
    nb = n_rows // bm
    row_map = lambda i, be, nu: (jnp.minimum(i, nu[0] - 1), 0)
    exp_map = lambda i, be, nu: (be[jnp.minimum(i, nu[0] - 1)], 0, 0)
    grid_spec = pltpu.PrefetchScalarGridSpec(
        num_scalar_prefetch=2,
        grid=(nb,),
        in_specs=[
            pl.BlockSpec((bm * ROW_SUB, LANES), row_map),
            pl.BlockSpec((1, D_MODEL, 2 * D_FF), exp_map),
            pl.BlockSpec((1, 1, 2 * D_FF), exp_map),
            pl.BlockSpec((1, D_FF, D_MODEL), exp_map),
            pl.BlockSpec((1, 1, D_MODEL), exp_map),
        ],
        out_specs=pl.BlockSpec((bm * ROW_SUB, LANES), row_map),
        scratch_shapes=[pltpu.VMEM((D_MODEL, 2 * D_FF), BF16), pltpu.VMEM((D_FF, D_MODEL), BF16)],
    )
    return pl.pallas_call(
        _expert_kernel,
        grid_spec=grid_spec,
        out_shape=jax.ShapeDtypeStruct((n_rows * ROW_SUB, LANES), jnp.uint32),
        compiler_params=_cparams(("arbitrary",)),
        name="experts",
    )(block_expert, n_used, rows, wgu, bgu, wd, bd)


def _combine_kernel(dest_ref, next_ref, x1_ref, gate_ref, y_ref, o_ref, ybuf, sems, *, tc):
    i = pl.program_id(0)
    n = pl.num_programs(0)

    def gather(idx_ref, slot, wait):
        def row_copy(t, k):
            return pltpu.make_async_copy(_row(y_ref, idx_ref[t * TOP_K + k]), _row(ybuf.at[slot, k], t),
                                         sems.at[slot])

        def body(g, carry):
            for u in range(DMA_UNROLL):
                for k in range(TOP_K):
                    cp = row_copy(g * DMA_UNROLL + u, k)
                    cp.wait() if wait else cp.start(priority=k % 2)
            return carry
        lax.fori_loop(0, tc // DMA_UNROLL, body, 0)

    slot = i % 2

    @pl.when(i == 0)
    def _():
        gather(dest_ref, 0, False)

    @pl.when(i + 1 < n)
    def _():
        gather(next_ref, 1 - slot, False)

    gather(dest_ref, slot, True)

    x1 = x1_ref[...]
    acc_lo, acc_hi = x1[:, :PACKED], x1[:, PACKED:]
    g = gate_ref[...]
    for k in range(TOP_K):
        lo, hi = _unpack_rows(_load_rows(ybuf.at[slot, k]))
        acc_lo = acc_lo + g[:, k:k + 1] * lo
        acc_hi = acc_hi + g[:, k:k + 1] * hi
    o_ref[:, :PACKED] = acc_lo
    o_ref[:, PACKED:] = acc_hi


def _combine(dest, x1, gates, y_rows):
    T = x1.shape[0]
    tc = min(512, T)
    n = T // tc
    return pl.pallas_call(
        functools.partial(_combine_kernel, tc=tc),
        grid=(n,),
        in_specs=[
            pl.BlockSpec((tc * TOP_K,), lambda i: (i,), memory_space=pltpu.SMEM),
            pl.BlockSpec((tc * TOP_K,), lambda i: (jnp.minimum(i + 1, n - 1),), memory_space=pltpu.SMEM),
            pl.BlockSpec((tc, D_MODEL), lambda i: (i, 0)),
            pl.BlockSpec((tc, LANES), lambda i: (i, 0)),
            pl.BlockSpec(memory_space=pl.ANY),
        ],
        out_specs=pl.BlockSpec((tc, D_MODEL), lambda i: (i, 0)),
        out_shape=jax.ShapeDtypeStruct((T, D_MODEL), F32),
        scratch_shapes=[pltpu.VMEM((2, TOP_K, tc * ROW_SUB, LANES), jnp.uint32),
                        pltpu.SemaphoreType.DMA((2,))],
        compiler_params=_cparams(("arbitrary",)),
        name="combine",
    )(dest, dest, x1, gates, y_rows)


def _pad_lanes(v):
    return jnp.pad(v, ((0, 0), (0, LANES - v.shape[-1])))


def kernel(x, mix_norm_g, w_in, q_norm_g, k_norm_g, lambda_q1, lambda_k1, lambda_q2, lambda_k2,
           attn_subln_g, conv_w, conv_b, dt_bias, a_log, d_skip, ssm_norm_g, w_attn_proj,
           w_ssm_proj, w_out, ffn_norm_g, w_router, b_router, w_gate_up, b_gate_up, w_down, b_down):
    B, S, D = x.shape
    T = B * S
    xf = x.reshape(T, D)
    layer = 0

    wi = w_in[layer]
    o_q, o_k, o_v = 0, ATT_WIDTH, 2 * ATT_WIDTH
    o_z = 3 * ATT_WIDTH
    o_xbc = o_z + SSM_D_INNER
    o_dt = o_xbc + SSM_XBC
    o_ga = o_dt + SSM_HEADS
    w_main = jnp.concatenate([wi[:, o_xbc:o_dt], wi[:, o_z:o_xbc], wi[:, o_q:o_z], wi[:, o_ga:]],
                             axis=1).astype(BF16)
    w_dt = _pad_lanes(wi[:, o_dt:o_ga]).astype(BF16)

    proj, dt_raw = _in_proj(xf, mix_norm_g[layer][None, :], w_main, w_dt)

    half = ATT_HEAD_DIM // 2
    inv = ROPE_THETA ** (-jnp.arange(0, ATT_HEAD_DIM, 2, dtype=F32) / ATT_HEAD_DIM)
    ang = jnp.arange(S, dtype=F32)[:, None] * inv[None, :]
    cos2 = jnp.tile(jnp.cos(ang), (1, LANES // half))
    sin2 = jnp.tile(jnp.concatenate([-jnp.sin(ang), jnp.sin(ang)], axis=1), (1, LANES // ATT_HEAD_DIM))
    qg2 = jnp.tile(q_norm_g[layer], 2)[None, :]
    kg2 = jnp.tile(k_norm_g[layer], 2)[None, :]
    partner = lambda g: jnp.tile(jnp.roll(g, half), 2)[None, :]
    q_scale = ATT_HEAD_DIM ** -0.5 * LOG2E
    att = _attention(proj, cos2 * qg2 * q_scale, sin2 * partner(q_norm_g[layer]) * q_scale,
                     cos2 * kg2, sin2 * partner(k_norm_g[layer]), qg2, kg2,
                     lambda_q1[layer][None, :], lambda_k1[layer][None, :],
                     lambda_q2[layer][None, :], lambda_k2[layer][None, :],
                     attn_subln_g[layer][None, :], B, S)

    ssm = _ssd(proj, dt_raw, conv_w[layer], conv_b[layer][None, :],
               _pad_lanes(dt_bias[layer][None, :]), _pad_lanes(a_log[layer][None, :]),
               jnp.repeat(d_skip[layer], SSM_HEAD_DIM)[None, :], ssm_norm_g[layer][None, :], B, S)

    x1, h2p, idx_rank_t, gates_t, counts = _merge(
        att, ssm, proj, xf, w_attn_proj[layer].astype(BF16), w_ssm_proj[layer].astype(BF16),
        w_out[layer].astype(BF16), ffn_norm_g[layer][None, :],
        _pad_lanes(w_router[layer]).astype(BF16), _pad_lanes(b_router[layer][None, :]))

    bm = 512
    A = T * TOP_K
    n_rows = (A + N_EXPERTS * (bm - 1)) // bm * bm
    cnt = counts[:, 0]
    idx_rank = idx_rank_t.T
    gates = _pad_lanes(gates_t[:TOP_K].T)
    padded = (cnt + bm - 1) // bm * bm
    pend = jnp.cumsum(padded)
    pstart = pend - padded
    n_used = (pend[-1:] // bm).astype(jnp.int32)
    block_start = jnp.arange(n_rows // bm, dtype=jnp.int32) * bm
    block_expert = jnp.minimum(jnp.sum(block_start[:, None] >= pend[None, :], axis=1),
                               N_EXPERTS - 1).astype(jnp.int32)
    top_idx = idx_rank[:, :TOP_K]
    onehot = top_idx[:, :, None] == jnp.arange(N_EXPERTS, dtype=jnp.int32)[None, None, :]
    dest = (jnp.sum(jnp.where(onehot, pstart[None, None, :], 0), axis=-1)
            + idx_rank[:, TOP_K:2 * TOP_K]).reshape(A).astype(jnp.int32)

    rows = _dispatch((pend - bm).astype(jnp.int32), cnt.astype(jnp.int32), dest, h2p, n_rows, bm)
    y_rows = _experts(block_expert, n_used, rows, w_gate_up[layer], b_gate_up[layer][:, None, :],
                      w_down[layer], b_down[layer][:, None, :], bm)
    out = _combine(dest, x1, gates, y_rows)
    return out.reshape(B, S, D)
```

```python
import functools
import math

import jax
import jax.numpy as jnp
from jax import lax
from jax.experimental import pallas as pl
from jax.experimental.pallas import tpu as pltpu

F32 = jnp.float32
BF16 = jnp.bfloat16

D_MODEL = 1024
EPS = 1e-6
ATT_HEADS = 8
ATT_HEAD_DIM = 64
ATT_V_DIM = 2 * ATT_HEAD_DIM
ATT_WIDTH = ATT_HEADS * ATT_V_DIM
ROPE_THETA = 10000.0
SSM_D_INNER = 2 * D_MODEL
SSM_HEAD_DIM = 64
SSM_HEADS = SSM_D_INNER // SSM_HEAD_DIM
SSM_GROUPS = 8
SSM_STATE = 128
SSM_CONV = 4
SSM_CHUNK = 128
SSM_XBC = SSM_D_INNER + 2 * SSM_GROUPS * SSM_STATE
N_EXPERTS = 32
TOP_K = 4
D_FF = D_MODEL
SWIGLU_LIMIT = 7.0
SWIGLU_ALPHA = 1.702
LAM_INIT = 0.8 - 0.6 * math.exp(-0.3 * 0)

LANES = 128
HALO = 16

COL_XBC = 0
COL_Z = SSM_XBC
COL_Q = COL_Z + SSM_D_INNER
COL_K = COL_Q + ATT_WIDTH
COL_V = COL_K + ATT_WIDTH
COL_GA = COL_V + ATT_WIDTH
COL_GS = COL_GA + D_MODEL
PROJ_COLS = COL_GS + D_MODEL

VMEM_LIMIT = 56 * 1024 * 1024


def _cparams(sem):
    return pltpu.CompilerParams(dimension_semantics=sem, vmem_limit_bytes=VMEM_LIMIT)


PACKED = D_MODEL // 2


def _pack_rows(x):
    lo = lax.bitcast_convert_type(x[:, :PACKED].astype(BF16).astype(F32), jnp.uint32)
    hi = lax.bitcast_convert_type(x[:, PACKED:].astype(BF16).astype(F32), jnp.uint32)
    return hi | (lo >> 16)


def _unpack_rows(p):
    lo = lax.bitcast_convert_type(p << 16, F32)
    hi = lax.bitcast_convert_type(p & jnp.uint32(0xFFFF0000), F32)
    return lo, hi


ROW_SUB = PACKED // LANES


def _store_rows(ref, packed):
    for s in range(ROW_SUB):
        ref[pl.ds(s, packed.shape[0], stride=ROW_SUB), :] = packed[:, s * LANES:(s + 1) * LANES]


def _load_rows(ref):
    n = ref.shape[0] // ROW_SUB
    return jnp.concatenate([ref[pl.ds(s, n, stride=ROW_SUB), :] for s in range(ROW_SUB)], axis=1)


def _row(ref, r):
    return ref.at[pl.ds(pl.multiple_of(r * ROW_SUB, ROW_SUB), ROW_SUB)]


PROJ_TN = 1024


def _in_proj_kernel(x_ref, g_ref, wa_ref, wb_ref, wdt_ref, o_ref, dt_ref, h_scr, *, n_head):
    j = pl.program_id(1)

    @pl.when(j == 0)
    def _():
        x = x_ref[...]
        ms = jnp.mean(x * x, axis=-1, keepdims=True)
        hb = (x * lax.rsqrt(ms + EPS) * g_ref[...]).astype(BF16)
        h_scr[...] = hb
        dt_ref[...] = jnp.dot(hb, wdt_ref[...], preferred_element_type=F32)

    @pl.when(j < n_head)
    def _():
        o_ref[...] = jnp.dot(h_scr[...], wa_ref[...], preferred_element_type=F32).astype(BF16)

    @pl.when(j >= n_head)
    def _():
        o_ref[...] = jnp.dot(h_scr[...], wb_ref[...], preferred_element_type=F32).astype(BF16)


def _in_proj(xf, g, w_head, w_gates, w_dt):
    T = xf.shape[0]
    tm = min(2048, T)
    tn = PROJ_TN
    n_head = w_head.shape[1] // tn
    t_xbc, t_z, t_qkv = SSM_XBC // tn, SSM_D_INNER // tn, 3 * ATT_WIDTH // tn

    def head_map(i, j):
        src = jnp.where(j < t_xbc, j + t_qkv + t_z, jnp.where(j < t_xbc + t_z, j - t_xbc + t_qkv, j - t_xbc - t_z))
        return 0, jnp.clip(src, 0, n_head - 1)

    return pl.pallas_call(
        functools.partial(_in_proj_kernel, n_head=n_head),
        grid=(T // tm, PROJ_COLS // tn),
        in_specs=[
            pl.BlockSpec((tm, D_MODEL), lambda i, j: (i, 0)),
            pl.BlockSpec((1, D_MODEL), lambda i, j: (0, 0)),
            pl.BlockSpec((D_MODEL, tn), head_map),
            pl.BlockSpec((D_MODEL, tn), lambda i, j: (0, jnp.maximum(j - n_head, 0))),
            pl.BlockSpec((D_MODEL, LANES), lambda i, j: (0, 0)),
        ],
        out_specs=[
            pl.BlockSpec((tm, tn), lambda i, j: (i, j)),
            pl.BlockSpec((tm, LANES), lambda i, j: (i, 0)),
        ],
        out_shape=[
            jax.ShapeDtypeStruct((T, PROJ_COLS), BF16),
            jax.ShapeDtypeStruct((T, LANES), F32),
        ],
        scratch_shapes=[pltpu.VMEM((tm, D_MODEL), BF16)],
        compiler_params=_cparams(("arbitrary", "arbitrary")),
        name="in_proj",
    )(xf, g, w_head, w_gates, w_dt)


LOG2E = 1.4426950408889634
SHIFT_LIMIT = 57.0
BOUND_MARGIN = 1.02


def _attn_kernel(q_ref, k_ref, v_ref, qc_ref, qs_ref, kc_ref, ks_ref, qg_ref, kg_ref,
                 lq1_ref, lk1_ref, lq2_ref, lk2_ref, sg_ref, o_ref, k_scr, v_scr, q1_scr, q2_scr, *, tq, seq):
    p = pl.program_id(2)
    nq = seq // tq
    paired = nq > 1
    lane = lax.broadcasted_iota(jnp.int32, (1, LANES), 1)
    first = lane < ATT_HEAD_DIM
    one_col = jnp.where(lane == 0, 1.0, 0.0)

    ri = lax.broadcasted_iota(jnp.int32, (LANES, LANES), 0)
    ci = lax.broadcasted_iota(jnp.int32, (LANES, LANES), 1)
    same_comp = (ri // ATT_HEAD_DIM == ci // ATT_HEAD_DIM).astype(BF16)
    swap_half = ((ri // ATT_HEAD_DIM == ci // ATT_HEAD_DIM)
                 & ((ri - ci == ATT_HEAD_DIM // 2) | (ci - ri == ATT_HEAD_DIM // 2))).astype(BF16)

    def norm_rope(xb, gcos, gsin):
        xf = xb.astype(F32)
        sq = xf * xf
        hi = sq.astype(BF16)
        lo = (sq - hi.astype(F32)).astype(BF16)
        ms = (jnp.dot(hi, same_comp, preferred_element_type=F32)
              + jnp.dot(lo, same_comp, preferred_element_type=F32)) * (1.0 / ATT_HEAD_DIM)
        xr = jnp.dot(xb, swap_half, preferred_element_type=F32)
        return lax.rsqrt(ms + EPS) * (xf * gcos + xr * gsin)

    ub = (ATT_HEAD_DIM * ATT_HEAD_DIM ** -0.5 * LOG2E * BOUND_MARGIN
          * jnp.max(jnp.abs(qg_ref[...])) * jnp.max(jnp.abs(kg_ref[...])))
    safe = ub <= SHIFT_LIMIT

    @pl.when(p == 0)
    def _():
        ones = jnp.broadcast_to(one_col, (tq, LANES)).astype(BF16)
        shift = jnp.broadcast_to(jnp.where(lane == 0, -ub, 0.0), (tq, LANES)).astype(BF16)
        for r in range(0, seq, tq):
            rows = slice(r, r + tq)
            k_scr[rows, :LANES] = norm_rope(k_ref[rows, :], kc_ref[rows, :], ks_ref[rows, :]).astype(BF16)
            k_scr[rows, LANES:] = ones
            v_scr[rows, :LANES] = v_ref[rows, :]
            v_scr[rows, LANES:] = ones
            qb = norm_rope(q_ref[rows, :], qc_ref[rows, :], qs_ref[rows, :]).astype(BF16)
            q1_scr[rows, :LANES] = jnp.where(first, qb, jnp.zeros_like(qb))
            q1_scr[rows, LANES:] = shift
            q2_scr[rows, :LANES] = jnp.where(first, jnp.zeros_like(qb), qb)
            q2_scr[rows, LANES:] = shift

    keep = (lax.broadcasted_iota(jnp.int32, (tq, tq), 1) <= lax.broadcasted_iota(jnp.int32, (tq, tq), 0))
    nt = (((1,), (1,)), ((), ()))
    rows_a = pl.multiple_of(p * tq, tq)
    rows_b = pl.multiple_of((nq - 1 - p) * tq, tq)

    def finish(o1, o2, rows0):
        lam = (jnp.exp(jnp.sum(lq1_ref[...] * lk1_ref[...], axis=-1, keepdims=True))
               - jnp.exp(jnp.sum(lq2_ref[...] * lk2_ref[...], axis=-1, keepdims=True)) + LAM_INIT)
        o = o1 - lam * o2
        ms = jnp.mean(o * o, axis=-1, keepdims=True)
        o_ref[pl.ds(rows0, tq), :] = (o * lax.rsqrt(ms + EPS) * sg_ref[...] * (1.0 - LAM_INIT)).astype(BF16)

    @pl.when(safe)
    def _():
        def block(q_scr, qrow0, krow0, rows, cols, mask):
            t = lax.dot_general(q_scr[pl.ds(qrow0, rows), :], k_scr[pl.ds(krow0, cols), :], nt,
                                preferred_element_type=F32)
            if mask is not None:
                t = jnp.where(mask, t, -jnp.inf)
            return jnp.dot(jnp.exp2(t).astype(BF16), v_scr[pl.ds(krow0, cols), :], preferred_element_type=F32)

        def tile(qrow0, krow0, masked):
            if not masked:
                return (block(q1_scr, qrow0, krow0, tq, tq, None), block(q2_scr, qrow0, krow0, tq, tq, None))
            h = tq // 2
            q_low = pl.multiple_of(qrow0 + h, h)
            k_hi = pl.multiple_of(krow0 + h, h)
            outs = []
            for q_scr in (q1_scr, q2_scr):
                left_cols = block(q_scr, qrow0, krow0, tq, h, keep[:, :h])
                corner = block(q_scr, q_low, k_hi, h, h, keep[:h, :h])
                outs.append(jnp.concatenate([left_cols[:h], left_cols[h:] + corner], axis=0))
            return tuple(outs)

        def normalised(acc):
            return acc[:, :LANES] / acc[:, LANES:LANES + 1]

        a1, a2 = tile(rows_a, rows_a, True)
        if paired:
            b1, b2 = tile(rows_b, rows_b, True)
            for s in range(nq - 1):
                to_a = s < p
                qrow0 = jnp.where(to_a, rows_a, rows_b)
                krow0 = pl.multiple_of(jnp.where(to_a, s, s - p) * tq, tq)
                d1, d2 = tile(pl.multiple_of(qrow0, tq), krow0, False)
                a1 = a1 + jnp.where(to_a, d1, 0.0)
                a2 = a2 + jnp.where(to_a, d2, 0.0)
                b1 = b1 + jnp.where(to_a, 0.0, d1)
                b2 = b2 + jnp.where(to_a, 0.0, d2)
            finish(normalised(b1), normalised(b2), rows_b)
        finish(normalised(a1), normalised(a2), rows_a)

    @pl.when(jnp.logical_not(safe))
    def _():
        def update(t, vt, m, l, acc):
            m_new = jnp.maximum(m, jnp.max(t, axis=-1, keepdims=True))
            alpha = jnp.exp2(m - m_new)
            pr = jnp.exp2(t - m_new)
            l_new = alpha * l + jnp.sum(pr, axis=-1, keepdims=True)
            acc_new = alpha * acc + jnp.dot(pr.astype(BF16), vt, preferred_element_type=F32)
            return m_new, l_new, acc_new

        def online(rows0, n_off):
            q1 = q1_scr[pl.ds(rows0, tq), :LANES]
            q2 = q2_scr[pl.ds(rows0, tq), :LANES]

            def step(krow0, carry, masked):
                m1, l1, c1, m2, l2, c2 = carry
                kt = k_scr[pl.ds(krow0, tq), :LANES]
                vt = v_scr[pl.ds(krow0, tq), :LANES]
                t1 = lax.dot_general(q1, kt, nt, preferred_element_type=F32)
                t2 = lax.dot_general(q2, kt, nt, preferred_element_type=F32)
                if masked:
                    t1 = jnp.where(keep, t1, -jnp.inf)
                    t2 = jnp.where(keep, t2, -jnp.inf)
                m1, l1, c1 = update(t1, vt, m1, l1, c1)
                m2, l2, c2 = update(t2, vt, m2, l2, c2)
                return m1, l1, c1, m2, l2, c2

            neg = jnp.full((tq, 1), -jnp.inf, F32)
            zero1 = jnp.zeros((tq, 1), F32)
            zacc = jnp.zeros((tq, ATT_V_DIM), F32)
            carry = step(rows0, (neg, zero1, zacc, neg, zero1, zacc), True)
            m1, l1, c1, m2, l2, c2 = lax.fori_loop(
                0, n_off, lambda j, c: step(pl.multiple_of(j * tq, tq), c, False), carry)
            finish(c1 / l1, c2 / l2, rows0)

        online(rows_a, p)
        if paired:
            online(rows_b, nq - 1 - p)


def _attention(proj, qcos, qsin, kcos, ksin, qg2, kg2, lq1, lk1, lq2, lk2, sg, batch, seq):
    T = batch * seq
    tq = min(512, seq)
    nq = seq // tq
    assert nq == 1 or nq % 2 == 0, "query tiles are processed in balanced pairs"
    vec = lambda n: pl.BlockSpec((1, n), lambda b, h, i: (0, 0))
    table = pl.BlockSpec((seq, LANES), lambda b, h, i: (0, 0))
    head = lambda col: pl.BlockSpec((seq, LANES), lambda b, h, i: (b, col // LANES + h))
    return pl.pallas_call(
        functools.partial(_attn_kernel, tq=tq, seq=seq),
        grid=(batch, ATT_HEADS, max(nq // 2, 1)),
        in_specs=[
            head(COL_Q), head(COL_K), head(COL_V),
            table, table, table, table,
            vec(LANES), vec(LANES),
            vec(ATT_HEAD_DIM), vec(ATT_HEAD_DIM), vec(ATT_HEAD_DIM), vec(ATT_HEAD_DIM),
            vec(LANES),
        ],
        out_specs=pl.BlockSpec((seq, LANES), lambda b, h, i: (b, h)),
        out_shape=jax.ShapeDtypeStruct((T, ATT_WIDTH), BF16),
        scratch_shapes=[pltpu.VMEM((seq, 2 * LANES), BF16) for _ in range(4)],
        compiler_params=_cparams(("arbitrary", "arbitrary", "arbitrary")),
        name="attention",
    )(proj, proj, proj, qcos, qsin, kcos, ksin, qg2, kg2, lq1, lk1, lq2, lk2, sg)


SSD_SUB = 2


def _ssd_kernel(xbc_ref, z_ref, dt_ref, cw_ref, cb_ref, dtb_ref, alog_ref, dsk_ref, ng_ref,
                o_ref, halo, state, xs_scr, bc_scr, y_scr):
    Q = SSM_CHUNK
    N = SSM_STATE
    P2 = 2 * SSM_HEAD_DIM

    @pl.when(pl.program_id(1) == 0)
    def _():
        halo[...] = jnp.zeros_like(halo)
        state[...] = jnp.zeros_like(state)

    taps = SSM_CONV - 1
    sr = lax.broadcasted_iota(jnp.int32, (taps * Q, HALO + Q), 0)
    sc = lax.broadcasted_iota(jnp.int32, (taps * Q, HALO + Q), 1)
    shift = (sc == (sr % Q) + HALO - taps + sr // Q).astype(BF16)
    ri = lax.broadcasted_iota(jnp.int32, (Q, Q), 0)
    ci = lax.broadcasted_iota(jnp.int32, (Q, Q), 1)
    tri = ri >= ci
    lane = lax.broadcasted_iota(jnp.int32, (1, P2), 1)
    left = lane < SSM_HEAD_DIM
    gw = SSM_D_INNER // SSM_GROUPS

    for sub in range(SSD_SUB):
        r0 = sub * Q
        xs_sub, bc_sub, y_sub = xs_scr.at[sub], bc_scr.at[sub], y_scr.at[sub]

        cwid = 512
        for c0 in range(0, SSM_XBC, cwid):
            cur = xbc_ref[r0:r0 + Q, c0:c0 + cwid]
            hist = halo[:, c0:c0 + cwid] if sub == 0 else xbc_ref[r0 - HALO:r0, c0:c0 + cwid]
            ext = jnp.concatenate([hist, cur], axis=0)
            shifted = jnp.dot(shift, ext, preferred_element_type=F32)
            acc = cb_ref[:, c0:c0 + cwid] + cw_ref[taps:SSM_CONV, c0:c0 + cwid] * cur.astype(F32)
            for k in range(taps):
                acc = acc + cw_ref[k:k + 1, c0:c0 + cwid] * shifted[k * Q:(k + 1) * Q, :]
            act = acc * jax.nn.sigmoid(acc)
            if c0 < SSM_D_INNER:
                xs_sub[:, c0:c0 + cwid] = act
            else:
                bc_sub[:, c0 - SSM_D_INNER:c0 - SSM_D_INNER + cwid] = act.astype(BF16)

        dt = jax.nn.softplus(dt_ref[r0:r0 + Q, :] + dtb_ref[...])
        da = dt * (-jnp.exp(alog_ref[...]) * LOG2E)
        acs = jnp.dot(tri.astype(F32), da, preferred_element_type=F32,
                      precision=lax.Precision.HIGHEST)
        acs_t = acs.T
        dt_t = dt.T
        w_t = dt_t * jnp.exp2(acs_t[:, Q - 1:Q] - acs_t)

        for g in range(SSM_GROUPS):
            bm = bc_sub[:, g * N:(g + 1) * N]
            cm = bc_sub[:, SSM_GROUPS * N + g * N:SSM_GROUPS * N + (g + 1) * N]
            cb = lax.dot_general(cm, bm, (((1,), (1,)), ((), ())), preferred_element_type=F32)
            bm_t = bm.astype(F32).T
            cm_f = cm.astype(F32)
            for pr in range(2):
                pair = 2 * g + pr
                xs_pair = xs_sub[:, pair * P2:(pair + 1) * P2].astype(BF16)
                prev = state[pair]
                rhs = jnp.concatenate([xs_pair, prev.astype(BF16)], axis=0)
                ys, sts, decs = [], [], []
                for r in range(2):
                    h = 2 * pair + r
                    a_col = acs[:, h:h + 1]
                    seg = a_col - acs_t[h:h + 1, :]
                    decay = jnp.exp2(jnp.where(tri, seg, -jnp.inf))
                    m_h = cb * decay * dt_t[h:h + 1, :]
                    e_h = cm_f * jnp.exp2(a_col)
                    lhs = jnp.concatenate([m_h, e_h], axis=1).astype(BF16)
                    ys.append(jnp.dot(lhs, rhs, preferred_element_type=F32))
                    sts.append(jnp.dot((bm_t * w_t[h:h + 1, :]).astype(BF16), xs_pair,
                                       preferred_element_type=F32))
                    decs.append(jnp.exp2(acs[Q - 1:Q, h:h + 1]))
                y_sub[:, pair * P2:(pair + 1) * P2] = jnp.where(left, ys[0], ys[1])
                dec = jnp.where(left, decs[0], decs[1])
                state[pair] = dec * prev + jnp.where(left, sts[0], sts[1])

        y = y_sub[...] + dsk_ref[...] * xs_sub[...]
        zf = z_ref[r0:r0 + Q, :].astype(F32)
        y = y * (zf * jax.nn.sigmoid(zf))
        for g in range(SSM_GROUPS):
            yg = y[:, g * gw:(g + 1) * gw]
            ms = jnp.mean(yg * yg, axis=-1, keepdims=True)
            o_ref[r0:r0 + Q, g * gw:(g + 1) * gw] = (yg * lax.rsqrt(ms + EPS)
                                                     * ng_ref[:, g * gw:(g + 1) * gw]).astype(BF16)

    halo[...] = xbc_ref[SSD_SUB * Q - HALO:SSD_SUB * Q, :]


def _ssd(proj, dt_raw, conv_w, conv_b, dt_bias, a_log, dsk, ng, batch, seq):
    T = batch * seq
    Q = SSM_CHUNK
    rows = SSD_SUB * Q
    nc = seq // rows
    vec = lambda r, n: pl.BlockSpec((r, n), lambda b, c: (0, 0))
    return pl.pallas_call(
        _ssd_kernel,
        grid=(batch, nc),
        in_specs=[
            pl.BlockSpec((rows, SSM_XBC), lambda b, c: (b * nc + c, COL_XBC // SSM_XBC)),
            pl.BlockSpec((rows, SSM_D_INNER), lambda b, c: (b * nc + c, COL_Z // SSM_D_INNER)),
            pl.BlockSpec((rows, LANES), lambda b, c: (b * nc + c, 0)),
            vec(SSM_CONV, SSM_XBC), vec(1, SSM_XBC), vec(1, LANES), vec(1, LANES),
            vec(1, SSM_D_INNER), vec(1, SSM_D_INNER),
        ],
        out_specs=pl.BlockSpec((rows, SSM_D_INNER), lambda b, c: (b * nc + c, 0)),
        out_shape=jax.ShapeDtypeStruct((T, SSM_D_INNER), BF16),
        scratch_shapes=[
            pltpu.VMEM((HALO, SSM_XBC), BF16),
            pltpu.VMEM((SSM_HEADS // 2, SSM_STATE, 2 * SSM_HEAD_DIM), F32),
            pltpu.VMEM((SSD_SUB, Q, SSM_D_INNER), F32),
            pltpu.VMEM((SSD_SUB, Q, 2 * SSM_GROUPS * SSM_STATE), BF16),
            pltpu.VMEM((SSD_SUB, Q, SSM_D_INNER), F32),
        ],
        compiler_params=_cparams(("arbitrary", "arbitrary")),
        name="ssd",
    )(proj, proj, dt_raw, conv_w, conv_b, dt_bias, a_log, dsk, ng)


def _merge_kernel(att_ref, ssm_ref, ga_ref, gs_ref, x_ref, wap_ref, wsp_ref, wo_ref, fg_ref,
                  wr_ref, br_ref, x1_ref, h2_ref, idx_ref, gate_ref, cnt_ref, base, *, tm):
    i = pl.program_id(0)

    @pl.when(i == 0)
    def _():
        base[...] = jnp.zeros_like(base)

    pa = jnp.dot(att_ref[...], wap_ref[...], preferred_element_type=F32)
    ps = jnp.dot(ssm_ref[...], wsp_ref[...], preferred_element_type=F32)
    merged = (jax.nn.sigmoid(ga_ref[...].astype(F32)) * pa
              + jax.nn.sigmoid(gs_ref[...].astype(F32)) * ps)
    x1 = x_ref[...] + jnp.dot(merged.astype(BF16), wo_ref[...], preferred_element_type=F32)
    x1_ref[...] = x1
    ms = jnp.mean(x1 * x1, axis=-1, keepdims=True)
    h2 = x1 * lax.rsqrt(ms + EPS) * fg_ref[...]
    _store_rows(h2_ref, _pack_rows(h2))

    logits = jnp.dot(h2.astype(BF16), wr_ref[...], preferred_element_type=F32) + br_ref[...]
    lg = logits.T[:N_EXPERTS, :]
    row_f = lax.broadcasted_iota(jnp.int32, (N_EXPERTS, tm), 0).astype(F32)
    vals, idxs, sels = [], [], []
    for _ in range(TOP_K):
        m = jnp.max(lg, axis=0, keepdims=True)
        idx = jnp.min(jnp.where(lg == m, row_f, float(N_EXPERTS)), axis=0, keepdims=True)
        sel = row_f == idx
        vals.append(m)
        idxs.append(idx)
        sels.append(sel)
        lg = jnp.where(sel, -jnp.inf, lg)
    es = [jnp.exp(v - vals[0]) for v in vals]
    den = es[0] + es[1] + es[2] + es[3]

    multi = jnp.zeros((N_EXPERTS, tm), F32)
    for sel in sels:
        multi = jnp.where(sel, 1.0, multi)
    ri = lax.broadcasted_iota(jnp.int32, (tm, tm), 0)
    ci = lax.broadcasted_iota(jnp.int32, (tm, tm), 1)
    before = jnp.dot(multi.astype(BF16), (ri < ci).astype(BF16), preferred_element_type=F32)
    before = before + base[...]
    ranks = [jnp.sum(jnp.where(sel, before, 0.0), axis=0, keepdims=True) for sel in sels]
    idx_ref[...] = jnp.concatenate(idxs + ranks, axis=0).astype(jnp.int32)
    gate_ref[...] = jnp.concatenate([e / den for e in es] + [jnp.zeros_like(den)] * TOP_K, axis=0)
    base[...] = base[...] + jnp.sum(multi, axis=1, keepdims=True)
    cnt_ref[...] = jnp.broadcast_to(base[...], cnt_ref.shape).astype(jnp.int32)


def _merge(att, ssm, proj, xf, wap, wsp, wo, fg, wr, br):
    T = xf.shape[0]
    tm = min(512, T)
    full = lambda a: pl.BlockSpec(a.shape, lambda i: (0, 0))
    return pl.pallas_call(
        functools.partial(_merge_kernel, tm=tm),
        grid=(T // tm,),
        in_specs=[
            pl.BlockSpec((tm, ATT_WIDTH), lambda i: (i, 0)),
            pl.BlockSpec((tm, SSM_D_INNER), lambda i: (i, 0)),
            pl.BlockSpec((tm, D_MODEL), lambda i: (i, COL_GA // D_MODEL)),
            pl.BlockSpec((tm, D_MODEL), lambda i: (i, COL_GS // D_MODEL)),
            pl.BlockSpec((tm, D_MODEL), lambda i: (i, 0)),
            full(wap), full(wsp), full(wo), full(fg), full(wr), full(br),
        ],
        out_specs=[
            pl.BlockSpec((tm, D_MODEL), lambda i: (i, 0)),
            pl.BlockSpec((tm * ROW_SUB, LANES), lambda i: (i, 0)),
            pl.BlockSpec((2 * TOP_K, tm), lambda i: (0, i)),
            pl.BlockSpec((2 * TOP_K, tm), lambda i: (0, i)),
            pl.BlockSpec((N_EXPERTS, LANES), lambda i: (0, 0)),
        ],
        out_shape=[
            jax.ShapeDtypeStruct((T, D_MODEL), F32),
            jax.ShapeDtypeStruct((T * ROW_SUB, LANES), jnp.uint32),
            jax.ShapeDtypeStruct((2 * TOP_K, T), jnp.int32),
            jax.ShapeDtypeStruct((2 * TOP_K, T), F32),
            jax.ShapeDtypeStruct((N_EXPERTS, LANES), jnp.int32),
        ],
        scratch_shapes=[pltpu.VMEM((N_EXPERTS, 1), F32)],
        compiler_params=_cparams(("arbitrary",)),
        name="merge_router",
    )(att, ssm, proj, proj, xf, wap, wsp, wo, fg, wr, br)


DMA_UNROLL = 8


def _dispatch_kernel(lb_ref, cnt_ref, dest_ref, h_ref, rows_ref, zeros, zsem, sem, *, tt, bm):
    @pl.when(pl.program_id(0) == 0)
    def _():
        zeros[...] = jnp.zeros_like(zeros)
        for wait in (False, True):
            for e in range(N_EXPERTS):
                @pl.when(cnt_ref[e] > 0)
                def _():
                    start = pl.multiple_of(lb_ref[e], bm)
                    cp = pltpu.make_async_copy(zeros, rows_ref.at[pl.ds(start * ROW_SUB, bm * ROW_SUB)], zsem)
                    cp.wait() if wait else cp.start()

    def row_copy(t, k):
        return pltpu.make_async_copy(_row(h_ref, t), _row(rows_ref, dest_ref[k, t]), sem)

    for wait in (False, True):
        def body(g, carry):
            for u in range(DMA_UNROLL):
                for k in range(TOP_K):
                    cp = row_copy(g * DMA_UNROLL + u, k)
                    cp.wait() if wait else cp.start(priority=k % 2)
            return carry
        lax.fori_loop(0, tt // DMA_UNROLL, body, 0)


def _dispatch(last_block, cnt, dest, h2p, n_rows, bm):
    T = h2p.shape[0] // ROW_SUB
    tt = min(1024, T)
    grid_spec = pltpu.PrefetchScalarGridSpec(
        num_scalar_prefetch=2,
        grid=(T // tt,),
        in_specs=[
            pl.BlockSpec((TOP_K, tt), lambda i, lb, c: (0, i), memory_space=pltpu.SMEM),
            pl.BlockSpec((tt * ROW_SUB, LANES), lambda i, lb, c: (i, 0)),
        ],
        out_specs=pl.BlockSpec(memory_space=pl.ANY),
        scratch_shapes=[pltpu.VMEM((bm * ROW_SUB, LANES), jnp.uint32),
                        pltpu.SemaphoreType.DMA, pltpu.SemaphoreType.DMA],
    )
    return pl.pallas_call(
        functools.partial(_dispatch_kernel, tt=tt, bm=bm),
        grid_spec=grid_spec,
        out_shape=jax.ShapeDtypeStruct((n_rows * ROW_SUB, LANES), jnp.uint32),
        compiler_params=_cparams(("arbitrary",)),
        name="dispatch",
    )(last_block, cnt, dest, h2p)


def _expert_kernel(be_ref, nu_ref, x_ref, wgu_ref, bgu_ref, wd_ref, bd_ref, o_ref, wgu_bf, wd_bf):
    i = pl.program_id(0)

    @pl.when(i < nu_ref[0])
    def _():
        @pl.when((i == 0) | (be_ref[i] != be_ref[jnp.maximum(i - 1, 0)]))
        def _():
            wgu_bf[...] = wgu_ref[0].astype(BF16)
            wd_bf[...] = wd_ref[0].astype(BF16)

        lo, hi = _unpack_rows(_load_rows(x_ref))
        x = jnp.concatenate([lo, hi], axis=1).astype(BF16)
        gu = jnp.dot(x, wgu_bf[...], preferred_element_type=F32) + bgu_ref[0]
        gate = jnp.minimum(gu[:, :D_FF], SWIGLU_LIMIT)
        up = jnp.clip(gu[:, D_FF:], -SWIGLU_LIMIT, SWIGLU_LIMIT)
        glu = gate * jax.nn.sigmoid(SWIGLU_ALPHA * gate)
        act = ((up + 1.0) * glu).astype(BF16)
        _store_rows(o_ref, _pack_rows(jnp.dot(act, wd_bf[...], preferred_element_type=F32) + bd_ref[0]))


def _experts(block_expert, n_used, rows, wgu, bgu, wd, bd, bm):
    n_rows = rows.shape[0] // ROW_SUB
    nb = n_rows // bm
    row_map = lambda i, be, nu: (jnp.minimum(i, nu[0] - 1), 0)
    exp_map = lambda i, be, nu: (be[jnp.minimum(i, nu[0] - 1)], 0, 0)
    grid_spec = pltpu.PrefetchScalarGridSpec(
        num_scalar_prefetch=2,
        grid=(nb,),
        in_specs=[
            pl.BlockSpec((bm * ROW_SUB, LANES), row_map),
            pl.BlockSpec((1, D_MODEL, 2 * D_FF), exp_map),
            pl.BlockSpec((1, 1, 2 * D_FF), exp_map),
            pl.BlockSpec((1, D_FF, D_MODEL), exp_map),
            pl.BlockSpec((1, 1, D_MODEL), exp_map),
        ],
        out_specs=pl.BlockSpec((bm * ROW_SUB, LANES), row_map),
        scratch_shapes=[pltpu.VMEM((D_MODEL, 2 * D_FF), BF16), pltpu.VMEM((D_FF, D_MODEL), BF16)],
    )
    return pl.pallas_call(
        _expert_kernel,
        grid_spec=grid_spec,
        out_shape=jax.ShapeDtypeStruct((n_rows * ROW_SUB, LANES), jnp.uint32),
        compiler_params=_cparams(("arbitrary",)),
        name="experts",
    )(block_expert, n_used, rows, wgu, bgu, wd, bd)


def _combine_kernel(dest_ref, next_ref, x1_ref, gate_ref, y_ref, o_ref, ybuf, sems, *, tc):
    i = pl.program_id(0)
    n = pl.num_programs(0)

    def gather(idx_ref, slot, wait):
        def row_copy(t, k):
            return pltpu.make_async_copy(_row(y_ref, idx_ref[k, t]), _row(ybuf.at[slot, k], t),
                                         sems.at[slot])

        def body(g, carry):
            for u in range(DMA_UNROLL):
                for k in range(TOP_K):
                    cp = row_copy(g * DMA_UNROLL + u, k)
                    cp.wait() if wait else cp.start(priority=k % 2)
            return carry
        lax.fori_loop(0, tc // DMA_UNROLL, body, 0)

    slot = i % 2

    @pl.when(i == 0)
    def _():
        gather(dest_ref, 0, False)

    @pl.when(i + 1 < n)
    def _():
        gather(next_ref, 1 - slot, False)

    gather(dest_ref, slot, True)

    x1 = x1_ref[...]
    acc_lo, acc_hi = x1[:, :PACKED], x1[:, PACKED:]
    g = gate_ref[...].T
    for k in range(TOP_K):
        lo, hi = _unpack_rows(_load_rows(ybuf.at[slot, k]))
        acc_lo = acc_lo + g[:, k:k + 1] * lo
        acc_hi = acc_hi + g[:, k:k + 1] * hi
    o_ref[:, :PACKED] = acc_lo
    o_ref[:, PACKED:] = acc_hi


def _combine(dest, x1, gates, y_rows):
    T = x1.shape[0]
    tc = min(256, T)
    n = T // tc
    return pl.pallas_call(
        functools.partial(_combine_kernel, tc=tc),
        grid=(n,),
        in_specs=[
            pl.BlockSpec((TOP_K, tc), lambda i: (0, i), memory_space=pltpu.SMEM),
            pl.BlockSpec((TOP_K, tc), lambda i: (0, jnp.minimum(i + 1, n - 1)), memory_space=pltpu.SMEM),
            pl.BlockSpec((tc, D_MODEL), lambda i: (i, 0)),
            pl.BlockSpec((2 * TOP_K, tc), lambda i: (0, i)),
            pl.BlockSpec(memory_space=pl.ANY),
        ],
        out_specs=pl.BlockSpec((tc, D_MODEL), lambda i: (i, 0)),
        out_shape=jax.ShapeDtypeStruct((T, D_MODEL), F32),
        scratch_shapes=[pltpu.VMEM((2, TOP_K, tc * ROW_SUB, LANES), jnp.uint32),
                        pltpu.SemaphoreType.DMA((2,))],
        compiler_params=_cparams(("arbitrary",)),
        name="combine",
    )(dest, dest, x1, gates, y_rows)


def _pad_lanes(v):
    return jnp.pad(v, ((0, 0), (0, LANES - v.shape[-1])))


def kernel(x, mix_norm_g, w_in, q_norm_g, k_norm_g, lambda_q1, lambda_k1, lambda_q2, lambda_k2,
           attn_subln_g, conv_w, conv_b, dt_bias, a_log, d_skip, ssm_norm_g, w_attn_proj,
           w_ssm_proj, w_out, ffn_norm_g, w_router, b_router, w_gate_up, b_gate_up, w_down, b_down):
    B, S, D = x.shape
    T = B * S
    xf = x.reshape(T, D)
    layer = 0

    wi = w_in[layer]
    o_dt = 3 * ATT_WIDTH + SSM_D_INNER + SSM_XBC
    o_ga = o_dt + SSM_HEADS
    w_dt = _pad_lanes(wi[:, o_dt:o_ga]).astype(BF16)
    proj, dt_raw = _in_proj(xf, mix_norm_g[layer][None, :], wi[:, :o_dt].astype(BF16), wi[:, o_ga:].astype(BF16), w_dt)

    half = ATT_HEAD_DIM // 2
    inv = ROPE_THETA ** (-jnp.arange(0, ATT_HEAD_DIM, 2, dtype=F32) / ATT_HEAD_DIM)
    ang = jnp.arange(S, dtype=F32)[:, None] * inv[None, :]
    cos2 = jnp.tile(jnp.cos(ang), (1, LANES // half))
    sin2 = jnp.tile(jnp.concatenate([-jnp.sin(ang), jnp.sin(ang)], axis=1), (1, LANES // ATT_HEAD_DIM))
    qg2 = jnp.tile(q_norm_g[layer], 2)[None, :]
    kg2 = jnp.tile(k_norm_g[layer], 2)[None, :]
    partner = lambda g: jnp.tile(jnp.roll(g, half), 2)[None, :]
    q_scale = ATT_HEAD_DIM ** -0.5 * LOG2E
    att = _attention(proj, cos2 * qg2 * q_scale, sin2 * partner(q_norm_g[layer]) * q_scale,
                     cos2 * kg2, sin2 * partner(k_norm_g[layer]), qg2, kg2,
                     lambda_q1[layer][None, :], lambda_k1[layer][None, :],
                     lambda_q2[layer][None, :], lambda_k2[layer][None, :],
                     attn_subln_g[layer][None, :], B, S)

    ssm = _ssd(proj, dt_raw, conv_w[layer], conv_b[layer][None, :],
               _pad_lanes(dt_bias[layer][None, :]), _pad_lanes(a_log[layer][None, :]),
               jnp.repeat(d_skip[layer], SSM_HEAD_DIM)[None, :], ssm_norm_g[layer][None, :], B, S)

    x1, h2p, idx_rank_t, gates_t, counts = _merge(
        att, ssm, proj, xf, w_attn_proj[layer].astype(BF16), w_ssm_proj[layer].astype(BF16),
        w_out[layer].astype(BF16), ffn_norm_g[layer][None, :],
        _pad_lanes(w_router[layer]).astype(BF16), _pad_lanes(b_router[layer][None, :]))

    bm = 512
    A = T * TOP_K
    n_rows = (A + N_EXPERTS * (bm - 1)) // bm * bm
    cnt = counts[:, 0]
    padded = (cnt + bm - 1) // bm * bm
    pend = jnp.cumsum(padded)
    pstart = pend - padded
    n_used = (pend[-1:] // bm).astype(jnp.int32)
    block_start = jnp.arange(n_rows // bm, dtype=jnp.int32) * bm
    block_expert = jnp.minimum(jnp.sum(block_start[:, None] >= pend[None, :], axis=1),
                               N_EXPERTS - 1).astype(jnp.int32)
    onehot = idx_rank_t[:TOP_K, :, None] == jnp.arange(N_EXPERTS, dtype=jnp.int32)[None, None, :]
    dest = (jnp.sum(jnp.where(onehot, pstart[None, None, :], 0), axis=-1)
            + idx_rank_t[TOP_K:]).astype(jnp.int32)

    rows = _dispatch((pend - bm).astype(jnp.int32), cnt.astype(jnp.int32), dest, h2p, n_rows, bm)
    y_rows = _experts(block_expert, n_used, rows, w_gate_up[layer], b_gate_up[layer][:, None, :],
                      w_down[layer], b_down[layer][:, None, :], bm)
    out = _combine(dest, x1, gates_t, y_rows)
    return out.reshape(B, S, D)
```

```python
import functools
import math

import jax
import jax.numpy as jnp
from jax import lax
from jax.experimental import pallas as pl
from jax.experimental.pallas import tpu as pltpu

F32 = jnp.float32
BF16 = jnp.bfloat16

D_MODEL = 1024
EPS = 1e-6
ATT_HEADS = 8
ATT_HEAD_DIM = 64
ATT_V_DIM = 2 * ATT_HEAD_DIM
ATT_WIDTH = ATT_HEADS * ATT_V_DIM
ROPE_THETA = 10000.0
SSM_D_INNER = 2 * D_MODEL
SSM_HEAD_DIM = 64
SSM_HEADS = SSM_D_INNER // SSM_HEAD_DIM
SSM_GROUPS = 8
SSM_STATE = 128
SSM_CONV = 4
SSM_CHUNK = 128
SSM_XBC = SSM_D_INNER + 2 * SSM_GROUPS * SSM_STATE
N_EXPERTS = 32
TOP_K = 4
D_FF = D_MODEL
SWIGLU_LIMIT = 7.0
SWIGLU_ALPHA = 1.702
LAM_INIT = 0.8 - 0.6 * math.exp(-0.3 * 0)

LANES = 128
HALO = 16

COL_XBC = 0
COL_Z = SSM_XBC
COL_Q = COL_Z + SSM_D_INNER
COL_K = COL_Q + ATT_WIDTH
COL_V = COL_K + ATT_WIDTH
COL_GA = COL_V + ATT_WIDTH
COL_GS = COL_GA + D_MODEL
PROJ_COLS = COL_GS + D_MODEL

VMEM_LIMIT = 56 * 1024 * 1024


def _cparams(sem):
    return pltpu.CompilerParams(dimension_semantics=sem, vmem_limit_bytes=VMEM_LIMIT)


PACKED = D_MODEL // 2


def _pack_rows(x):
    lo = lax.bitcast_convert_type(x[:, :PACKED].astype(BF16).astype(F32), jnp.uint32)
    hi = lax.bitcast_convert_type(x[:, PACKED:].astype(BF16).astype(F32), jnp.uint32)
    return hi | (lo >> 16)


def _unpack_rows(p):
    lo = lax.bitcast_convert_type(p << 16, F32)
    hi = lax.bitcast_convert_type(p & jnp.uint32(0xFFFF0000), F32)
    return lo, hi


ROW_SUB = PACKED // LANES


def _store_rows(ref, packed):
    for s in range(ROW_SUB):
        ref[pl.ds(s, packed.shape[0], stride=ROW_SUB), :] = packed[:, s * LANES:(s + 1) * LANES]


def _load_rows(ref):
    n = ref.shape[0] // ROW_SUB
    return jnp.concatenate([ref[pl.ds(s, n, stride=ROW_SUB), :] for s in range(ROW_SUB)], axis=1)


def _row(ref, r):
    return ref.at[pl.ds(pl.multiple_of(r * ROW_SUB, ROW_SUB), ROW_SUB)]


PROJ_TN = 1024


def _in_proj_kernel(x_ref, g_ref, w_ref, wdt_ref, o_ref, dt_ref, h_scr):
    @pl.when(pl.program_id(1) == 0)
    def _():
        x = x_ref[...]
        ms = jnp.mean(x * x, axis=-1, keepdims=True)
        hb = (x * lax.rsqrt(ms + EPS) * g_ref[...]).astype(BF16)
        h_scr[...] = hb
        dt_ref[...] = jnp.dot(hb, wdt_ref[...], preferred_element_type=F32)

    o_ref[...] = jnp.dot(h_scr[...], w_ref[...], preferred_element_type=F32).astype(BF16)


def _in_proj(xf, g, w_main, w_dt):
    T = xf.shape[0]
    tm = min(2048, T)
    tn = PROJ_TN
    return pl.pallas_call(
        _in_proj_kernel,
        grid=(T // tm, PROJ_COLS // tn),
        in_specs=[
            pl.BlockSpec((tm, D_MODEL), lambda i, j: (i, 0)),
            pl.BlockSpec((1, D_MODEL), lambda i, j: (0, 0)),
            pl.BlockSpec((D_MODEL, tn), lambda i, j: (0, j)),
            pl.BlockSpec((D_MODEL, LANES), lambda i, j: (0, 0)),
        ],
        out_specs=[
            pl.BlockSpec((tm, tn), lambda i, j: (i, j)),
            pl.BlockSpec((tm, LANES), lambda i, j: (i, 0)),
        ],
        out_shape=[
            jax.ShapeDtypeStruct((T, PROJ_COLS), BF16),
            jax.ShapeDtypeStruct((T, LANES), F32),
        ],
        scratch_shapes=[pltpu.VMEM((tm, D_MODEL), BF16)],
        compiler_params=_cparams(("arbitrary", "arbitrary")),
        name="in_proj",
    )(xf, g, w_main, w_dt)


LOG2E = 1.4426950408889634
SHIFT_LIMIT = 57.0
BOUND_MARGIN = 1.02


def _attn_kernel(q_ref, k_ref, v_ref, qc_ref, qs_ref, kc_ref, ks_ref, qg_ref, kg_ref,
                 lq1_ref, lk1_ref, lq2_ref, lk2_ref, sg_ref, o_ref, k_scr, v_scr, q1_scr, q2_scr, *, tq, seq):
    p = pl.program_id(2)
    nq = seq // tq
    paired = nq > 1
    lane = lax.broadcasted_iota(jnp.int32, (1, LANES), 1)
    first = lane < ATT_HEAD_DIM
    one_col = jnp.where(lane == 0, 1.0, 0.0)

    ri = lax.broadcasted_iota(jnp.int32, (LANES, LANES), 0)
    ci = lax.broadcasted_iota(jnp.int32, (LANES, LANES), 1)
    same_comp = (ri // ATT_HEAD_DIM == ci // ATT_HEAD_DIM).astype(BF16)
    swap_half = ((ri // ATT_HEAD_DIM == ci // ATT_HEAD_DIM)
                 & ((ri - ci == ATT_HEAD_DIM // 2) | (ci - ri == ATT_HEAD_DIM // 2))).astype(BF16)

    def norm_rope(xb, gcos, gsin):
        xf = xb.astype(F32)
        sq = xf * xf
        hi = sq.astype(BF16)
        lo = (sq - hi.astype(F32)).astype(BF16)
        ms = (jnp.dot(hi, same_comp, preferred_element_type=F32)
              + jnp.dot(lo, same_comp, preferred_element_type=F32)) * (1.0 / ATT_HEAD_DIM)
        xr = jnp.dot(xb, swap_half, preferred_element_type=F32)
        return lax.rsqrt(ms + EPS) * (xf * gcos + xr * gsin)

    ub = (ATT_HEAD_DIM * ATT_HEAD_DIM ** -0.5 * LOG2E * BOUND_MARGIN
          * jnp.max(jnp.abs(qg_ref[...])) * jnp.max(jnp.abs(kg_ref[...])))
    safe = ub <= SHIFT_LIMIT

    @pl.when(p == 0)
    def _():
        ones = jnp.broadcast_to(one_col, (tq, LANES)).astype(BF16)
        shift = jnp.broadcast_to(jnp.where(lane == 0, -ub, 0.0), (tq, LANES)).astype(BF16)
        for r in range(0, seq, tq):
            rows = slice(r, r + tq)
            k_scr[rows, :LANES] = norm_rope(k_ref[rows, :], kc_ref[rows, :], ks_ref[rows, :]).astype(BF16)
            k_scr[rows, LANES:] = ones
            v_scr[rows, :LANES] = v_ref[rows, :]
            v_scr[rows, LANES:] = ones
            qb = norm_rope(q_ref[rows, :], qc_ref[rows, :], qs_ref[rows, :]).astype(BF16)
            q1_scr[rows, :LANES] = jnp.where(first, qb, jnp.zeros_like(qb))
            q1_scr[rows, LANES:] = shift
            q2_scr[rows, :LANES] = jnp.where(first, jnp.zeros_like(qb), qb)
            q2_scr[rows, LANES:] = shift

    keep = (lax.broadcasted_iota(jnp.int32, (tq, tq), 1) <= lax.broadcasted_iota(jnp.int32, (tq, tq), 0))
    nt = (((1,), (1,)), ((), ()))
    rows_a = pl.multiple_of(p * tq, tq)
    rows_b = pl.multiple_of((nq - 1 - p) * tq, tq)

    def finish(o1, o2, rows0):
        lam = (jnp.exp(jnp.sum(lq1_ref[...] * lk1_ref[...], axis=-1, keepdims=True))
               - jnp.exp(jnp.sum(lq2_ref[...] * lk2_ref[...], axis=-1, keepdims=True)) + LAM_INIT)
        o = o1 - lam * o2
        ms = jnp.mean(o * o, axis=-1, keepdims=True)
        o_ref[pl.ds(rows0, tq), :] = (o * lax.rsqrt(ms + EPS) * sg_ref[...] * (1.0 - LAM_INIT)).astype(BF16)

    @pl.when(safe)
    def _():
        def block(q_scr, qrow0, krow0, rows, cols, mask):
            t = lax.dot_general(q_scr[pl.ds(qrow0, rows), :], k_scr[pl.ds(krow0, cols), :], nt,
                                preferred_element_type=F32)
            if mask is not None:
                t = jnp.where(mask, t, -jnp.inf)
            return jnp.dot(jnp.exp2(t).astype(BF16), v_scr[pl.ds(krow0, cols), :], preferred_element_type=F32)

        def tile(qrow0, krow0, masked):
            if not masked:
                return (block(q1_scr, qrow0, krow0, tq, tq, None), block(q2_scr, qrow0, krow0, tq, tq, None))
            h = tq // 2
            q_low = pl.multiple_of(qrow0 + h, h)
            k_hi = pl.multiple_of(krow0 + h, h)
            outs = []
            for q_scr in (q1_scr, q2_scr):
                left_cols = block(q_scr, qrow0, krow0, tq, h, keep[:, :h])
                corner = block(q_scr, q_low, k_hi, h, h, keep[:h, :h])
                outs.append(jnp.concatenate([left_cols[:h], left_cols[h:] + corner], axis=0))
            return tuple(outs)

        def normalised(acc):
            return acc[:, :LANES] / acc[:, LANES:LANES + 1]

        a1, a2 = tile(rows_a, rows_a, True)
        if paired:
            b1, b2 = tile(rows_b, rows_b, True)
            for s in range(nq - 1):
                to_a = s < p
                qrow0 = jnp.where(to_a, rows_a, rows_b)
                krow0 = pl.multiple_of(jnp.where(to_a, s, s - p) * tq, tq)
                d1, d2 = tile(pl.multiple_of(qrow0, tq), krow0, False)
                a1 = a1 + jnp.where(to_a, d1, 0.0)
                a2 = a2 + jnp.where(to_a, d2, 0.0)
                b1 = b1 + jnp.where(to_a, 0.0, d1)
                b2 = b2 + jnp.where(to_a, 0.0, d2)
            finish(normalised(b1), normalised(b2), rows_b)
        finish(normalised(a1), normalised(a2), rows_a)

    @pl.when(jnp.logical_not(safe))
    def _():
        def update(t, vt, m, l, acc):
            m_new = jnp.maximum(m, jnp.max(t, axis=-1, keepdims=True))
            alpha = jnp.exp2(m - m_new)
            pr = jnp.exp2(t - m_new)
            l_new = alpha * l + jnp.sum(pr, axis=-1, keepdims=True)
            acc_new = alpha * acc + jnp.dot(pr.astype(BF16), vt, preferred_element_type=F32)
            return m_new, l_new, acc_new

        def online(rows0, n_off):
            q1 = q1_scr[pl.ds(rows0, tq), :LANES]
            q2 = q2_scr[pl.ds(rows0, tq), :LANES]

            def step(krow0, carry, masked):
                m1, l1, c1, m2, l2, c2 = carry
                kt = k_scr[pl.ds(krow0, tq), :LANES]
                vt = v_scr[pl.ds(krow0, tq), :LANES]
                t1 = lax.dot_general(q1, kt, nt, preferred_element_type=F32)
                t2 = lax.dot_general(q2, kt, nt, preferred_element_type=F32)
                if masked:
                    t1 = jnp.where(keep, t1, -jnp.inf)
                    t2 = jnp.where(keep, t2, -jnp.inf)
                m1, l1, c1 = update(t1, vt, m1, l1, c1)
                m2, l2, c2 = update(t2, vt, m2, l2, c2)
                return m1, l1, c1, m2, l2, c2

            neg = jnp.full((tq, 1), -jnp.inf, F32)
            zero1 = jnp.zeros((tq, 1), F32)
            zacc = jnp.zeros((tq, ATT_V_DIM), F32)
            carry = step(rows0, (neg, zero1, zacc, neg, zero1, zacc), True)
            m1, l1, c1, m2, l2, c2 = lax.fori_loop(
                0, n_off, lambda j, c: step(pl.multiple_of(j * tq, tq), c, False), carry)
            finish(c1 / l1, c2 / l2, rows0)

        online(rows_a, p)
        if paired:
            online(rows_b, nq - 1 - p)


def _attention(proj, qcos, qsin, kcos, ksin, qg2, kg2, lq1, lk1, lq2, lk2, sg, batch, seq):
    T = batch * seq
    tq = min(512, seq)
    nq = seq // tq
    assert nq == 1 or nq % 2 == 0, "query tiles are processed in balanced pairs"
    vec = lambda n: pl.BlockSpec((1, n), lambda b, h, i: (0, 0))
    table = pl.BlockSpec((seq, LANES), lambda b, h, i: (0, 0))
    head = lambda col: pl.BlockSpec((seq, LANES), lambda b, h, i: (b, col // LANES + h))
    return pl.pallas_call(
        functools.partial(_attn_kernel, tq=tq, seq=seq),
        grid=(batch, ATT_HEADS, max(nq // 2, 1)),
        in_specs=[
            head(COL_Q), head(COL_K), head(COL_V),
            table, table, table, table,
            vec(LANES), vec(LANES),
            vec(ATT_HEAD_DIM), vec(ATT_HEAD_DIM), vec(ATT_HEAD_DIM), vec(ATT_HEAD_DIM),
            vec(LANES),
        ],
        out_specs=pl.BlockSpec((seq, LANES), lambda b, h, i: (b, h)),
        out_shape=jax.ShapeDtypeStruct((T, ATT_WIDTH), BF16),
        scratch_shapes=[pltpu.VMEM((seq, 2 * LANES), BF16) for _ in range(4)],
        compiler_params=_cparams(("arbitrary", "arbitrary", "arbitrary")),
        name="attention",
    )(proj, proj, proj, qcos, qsin, kcos, ksin, qg2, kg2, lq1, lk1, lq2, lk2, sg)


SSD_SUB = 2


def _ssd_kernel(xbc_ref, z_ref, dt_ref, cw_ref, cb_ref, dtb_ref, alog_ref, dsk_ref, ng_ref,
                o_ref, halo, state, xs_scr, bc_scr, y_scr):
    Q = SSM_CHUNK
    N = SSM_STATE
    P2 = 2 * SSM_HEAD_DIM

    @pl.when(pl.program_id(1) == 0)
    def _():
        halo[...] = jnp.zeros_like(halo)
        state[...] = jnp.zeros_like(state)

    taps = SSM_CONV - 1
    sr = lax.broadcasted_iota(jnp.int32, (taps * Q, HALO + Q), 0)
    sc = lax.broadcasted_iota(jnp.int32, (taps * Q, HALO + Q), 1)
    shift = (sc == (sr % Q) + HALO - taps + sr // Q).astype(BF16)
    ri = lax.broadcasted_iota(jnp.int32, (Q, Q), 0)
    ci = lax.broadcasted_iota(jnp.int32, (Q, Q), 1)
    tri = ri >= ci
    lane = lax.broadcasted_iota(jnp.int32, (1, P2), 1)
    left = lane < SSM_HEAD_DIM
    gw = SSM_D_INNER // SSM_GROUPS

    for sub in range(SSD_SUB):
        r0 = sub * Q
        xs_sub, bc_sub, y_sub = xs_scr.at[sub], bc_scr.at[sub], y_scr.at[sub]

        cwid = 512
        for c0 in range(0, SSM_XBC, cwid):
            cur = xbc_ref[r0:r0 + Q, c0:c0 + cwid]
            hist = halo[:, c0:c0 + cwid] if sub == 0 else xbc_ref[r0 - HALO:r0, c0:c0 + cwid]
            ext = jnp.concatenate([hist, cur], axis=0)
            shifted = jnp.dot(shift, ext, preferred_element_type=F32)
            acc = cb_ref[:, c0:c0 + cwid] + cw_ref[taps:SSM_CONV, c0:c0 + cwid] * cur.astype(F32)
            for k in range(taps):
                acc = acc + cw_ref[k:k + 1, c0:c0 + cwid] * shifted[k * Q:(k + 1) * Q, :]
            act = acc * jax.nn.sigmoid(acc)
            if c0 < SSM_D_INNER:
                xs_sub[:, c0:c0 + cwid] = act
            else:
                bc_sub[:, c0 - SSM_D_INNER:c0 - SSM_D_INNER + cwid] = act.astype(BF16)

        dt = jax.nn.softplus(dt_ref[r0:r0 + Q, :] + dtb_ref[...])
        da = dt * (-jnp.exp(alog_ref[...]) * LOG2E)
        acs = jnp.dot(tri.astype(F32), da, preferred_element_type=F32,
                      precision=lax.Precision.HIGHEST)
        acs_t = acs.T
        dt_t = dt.T
        w_t = dt_t * jnp.exp2(acs_t[:, Q - 1:Q] - acs_t)

        for g in range(SSM_GROUPS):
            bm = bc_sub[:, g * N:(g + 1) * N]
            cm = bc_sub[:, SSM_GROUPS * N + g * N:SSM_GROUPS * N + (g + 1) * N]
            cb = lax.dot_general(cm, bm, (((1,), (1,)), ((), ())), preferred_element_type=F32)
            bm_t = bm.astype(F32).T
            cm_f = cm.astype(F32)
            for pr in range(2):
                pair = 2 * g + pr
                xs_pair = xs_sub[:, pair * P2:(pair + 1) * P2].astype(BF16)
                prev = state[pair]
                rhs = jnp.concatenate([xs_pair, prev.astype(BF16)], axis=0)
                ys, sts, decs = [], [], []
                for r in range(2):
                    h = 2 * pair + r
                    a_col = acs[:, h:h + 1]
                    seg = a_col - acs_t[h:h + 1, :]
                    decay = jnp.exp2(jnp.where(tri, seg, -jnp.inf))
                    m_h = cb * decay * dt_t[h:h + 1, :]
                    e_h = cm_f * jnp.exp2(a_col)
                    lhs = jnp.concatenate([m_h, e_h], axis=1).astype(BF16)
                    ys.append(jnp.dot(lhs, rhs, preferred_element_type=F32))
                    sts.append(jnp.dot((bm_t * w_t[h:h + 1, :]).astype(BF16), xs_pair,
                                       preferred_element_type=F32))
                    decs.append(jnp.exp2(acs[Q - 1:Q, h:h + 1]))
                y_sub[:, pair * P2:(pair + 1) * P2] = jnp.where(left, ys[0], ys[1])
                dec = jnp.where(left, decs[0], decs[1])
                state[pair] = dec * prev + jnp.where(left, sts[0], sts[1])

        y = y_sub[...] + dsk_ref[...] * xs_sub[...]
        zf = z_ref[r0:r0 + Q, :].astype(F32)
        y = y * (zf * jax.nn.sigmoid(zf))
        for g in range(SSM_GROUPS):
            yg = y[:, g * gw:(g + 1) * gw]
            ms = jnp.mean(yg * yg, axis=-1, keepdims=True)
            o_ref[r0:r0 + Q, g * gw:(g + 1) * gw] = (yg * lax.rsqrt(ms + EPS)
                                                     * ng_ref[:, g * gw:(g + 1) * gw]).astype(BF16)

    halo[...] = xbc_ref[SSD_SUB * Q - HALO:SSD_SUB * Q, :]


def _ssd(proj, dt_raw, conv_w, conv_b, dt_bias, a_log, dsk, ng, batch, seq):
    T = batch * seq
    Q = SSM_CHUNK
    rows = SSD_SUB * Q
    nc = seq // rows
    vec = lambda r, n: pl.BlockSpec((r, n), lambda b, c: (0, 0))
    return pl.pallas_call(
        _ssd_kernel,
        grid=(batch, nc),
        in_specs=[
            pl.BlockSpec((rows, SSM_XBC), lambda b, c: (b * nc + c, COL_XBC // SSM_XBC)),
            pl.BlockSpec((rows, SSM_D_INNER), lambda b, c: (b * nc + c, COL_Z // SSM_D_INNER)),
            pl.BlockSpec((rows, LANES), lambda b, c: (b * nc + c, 0)),
            vec(SSM_CONV, SSM_XBC), vec(1, SSM_XBC), vec(1, LANES), vec(1, LANES),
            vec(1, SSM_D_INNER), vec(1, SSM_D_INNER),
        ],
        out_specs=pl.BlockSpec((rows, SSM_D_INNER), lambda b, c: (b * nc + c, 0)),
        out_shape=jax.ShapeDtypeStruct((T, SSM_D_INNER), BF16),
        scratch_shapes=[
            pltpu.VMEM((HALO, SSM_XBC), BF16),
            pltpu.VMEM((SSM_HEADS // 2, SSM_STATE, 2 * SSM_HEAD_DIM), F32),
            pltpu.VMEM((SSD_SUB, Q, SSM_D_INNER), F32),
            pltpu.VMEM((SSD_SUB, Q, 2 * SSM_GROUPS * SSM_STATE), BF16),
            pltpu.VMEM((SSD_SUB, Q, SSM_D_INNER), F32),
        ],
        compiler_params=_cparams(("arbitrary", "arbitrary")),
        name="ssd",
    )(proj, proj, dt_raw, conv_w, conv_b, dt_bias, a_log, dsk, ng)


def _merge_kernel(att_ref, ssm_ref, ga_ref, gs_ref, x_ref, wap_ref, wsp_ref, wo_ref, fg_ref,
                  wr_ref, br_ref, x1_ref, h2_ref, idx_ref, gate_ref, cnt_ref, base, *, tm):
    i = pl.program_id(0)

    @pl.when(i == 0)
    def _():
        base[...] = jnp.zeros_like(base)

    pa = jnp.dot(att_ref[...], wap_ref[...], preferred_element_type=F32)
    ps = jnp.dot(ssm_ref[...], wsp_ref[...], preferred_element_type=F32)
    merged = (jax.nn.sigmoid(ga_ref[...].astype(F32)) * pa
              + jax.nn.sigmoid(gs_ref[...].astype(F32)) * ps)
    x1 = x_ref[...] + jnp.dot(merged.astype(BF16), wo_ref[...], preferred_element_type=F32)
    x1_ref[...] = x1
    ms = jnp.mean(x1 * x1, axis=-1, keepdims=True)
    h2 = x1 * lax.rsqrt(ms + EPS) * fg_ref[...]
    _store_rows(h2_ref, _pack_rows(h2))

    logits = jnp.dot(h2.astype(BF16), wr_ref[...], preferred_element_type=F32) + br_ref[...]
    lg = logits.T[:N_EXPERTS, :]
    row_f = lax.broadcasted_iota(jnp.int32, (N_EXPERTS, tm), 0).astype(F32)
    vals, idxs, sels = [], [], []
    for _ in range(TOP_K):
        m = jnp.max(lg, axis=0, keepdims=True)
        idx = jnp.min(jnp.where(lg == m, row_f, float(N_EXPERTS)), axis=0, keepdims=True)
        sel = row_f == idx
        vals.append(m)
        idxs.append(idx)
        sels.append(sel)
        lg = jnp.where(sel, -jnp.inf, lg)
    es = [jnp.exp(v - vals[0]) for v in vals]
    den = es[0] + es[1] + es[2] + es[3]

    multi = jnp.zeros((N_EXPERTS, tm), F32)
    for sel in sels:
        multi = jnp.where(sel, 1.0, multi)
    ri = lax.broadcasted_iota(jnp.int32, (tm, tm), 0)
    ci = lax.broadcasted_iota(jnp.int32, (tm, tm), 1)
    before = jnp.dot(multi.astype(BF16), (ri < ci).astype(BF16), preferred_element_type=F32)
    before = before + base[...]
    ranks = [jnp.sum(jnp.where(sel, before, 0.0), axis=0, keepdims=True) for sel in sels]
    idx_ref[...] = jnp.concatenate(idxs + ranks, axis=0).astype(jnp.int32)
    gate_ref[...] = jnp.concatenate([e / den for e in es] + [jnp.zeros_like(den)] * TOP_K, axis=0)
    base[...] = base[...] + jnp.sum(multi, axis=1, keepdims=True)
    cnt_ref[...] = jnp.broadcast_to(base[...], cnt_ref.shape).astype(jnp.int32)


def _merge(att, ssm, proj, xf, wap, wsp, wo, fg, wr, br):
    T = xf.shape[0]
    tm = min(512, T)
    full = lambda a: pl.BlockSpec(a.shape, lambda i: (0, 0))
    return pl.pallas_call(
        functools.partial(_merge_kernel, tm=tm),
        grid=(T // tm,),
        in_specs=[
            pl.BlockSpec((tm, ATT_WIDTH), lambda i: (i, 0)),
            pl.BlockSpec((tm, SSM_D_INNER), lambda i: (i, 0)),
            pl.BlockSpec((tm, D_MODEL), lambda i: (i, COL_GA // D_MODEL)),
            pl.BlockSpec((tm, D_MODEL), lambda i: (i, COL_GS // D_MODEL)),
            pl.BlockSpec((tm, D_MODEL), lambda i: (i, 0)),
            full(wap), full(wsp), full(wo), full(fg), full(wr), full(br),
        ],
        out_specs=[
            pl.BlockSpec((tm, D_MODEL), lambda i: (i, 0)),
            pl.BlockSpec((tm * ROW_SUB, LANES), lambda i: (i, 0)),
            pl.BlockSpec((2 * TOP_K, tm), lambda i: (0, i)),
            pl.BlockSpec((2 * TOP_K, tm), lambda i: (0, i)),
            pl.BlockSpec((N_EXPERTS, LANES), lambda i: (0, 0)),
        ],
        out_shape=[
            jax.ShapeDtypeStruct((T, D_MODEL), F32),
            jax.ShapeDtypeStruct((T * ROW_SUB, LANES), jnp.uint32),
            jax.ShapeDtypeStruct((2 * TOP_K, T), jnp.int32),
            jax.ShapeDtypeStruct((2 * TOP_K, T), F32),
            jax.ShapeDtypeStruct((N_EXPERTS, LANES), jnp.int32),
        ],
        scratch_shapes=[pltpu.VMEM((N_EXPERTS, 1), F32)],
        compiler_params=_cparams(("arbitrary",)),
        name="merge_router",
    )(att, ssm, proj, proj, xf, wap, wsp, wo, fg, wr, br)


DMA_UNROLL = 8


def _dispatch_kernel(lb_ref, cnt_ref, dest_ref, h_ref, rows_ref, zeros, zsem, sem, *, tt, bm):
    @pl.when(pl.program_id(0) == 0)
    def _():
        zeros[...] = jnp.zeros_like(zeros)
        for wait in (False, True):
            for e in range(N_EXPERTS):
                @pl.when(cnt_ref[e] > 0)
                def _():
                    start = pl.multiple_of(lb_ref[e], bm)
                    cp = pltpu.make_async_copy(zeros, rows_ref.at[pl.ds(start * ROW_SUB, bm * ROW_SUB)], zsem)
                    cp.wait() if wait else cp.start()

    def row_copy(t, k):
        return pltpu.make_async_copy(_row(h_ref, t), _row(rows_ref, dest_ref[t * TOP_K + k]), sem)

    for wait in (False, True):
        def body(g, carry):
            for u in range(DMA_UNROLL):
                for k in range(TOP_K):
                    cp = row_copy(g * DMA_UNROLL + u, k)
                    cp.wait() if wait else cp.start(priority=k % 2)
            return carry
        lax.fori_loop(0, tt // DMA_UNROLL, body, 0)


def _dispatch(last_block, cnt, dest, h2p, n_rows, bm):
    T = h2p.shape[0] // ROW_SUB
    tt = min(1024, T)
    grid_spec = pltpu.PrefetchScalarGridSpec(
        num_scalar_prefetch=2,
        grid=(T // tt,),
        in_specs=[
            pl.BlockSpec((tt * TOP_K,), lambda i, lb, c: (i,), memory_space=pltpu.SMEM),
            pl.BlockSpec((tt * ROW_SUB, LANES), lambda i, lb, c: (i, 0)),
        ],
        out_specs=pl.BlockSpec(memory_space=pl.ANY),
        scratch_shapes=[pltpu.VMEM((bm * ROW_SUB, LANES), jnp.uint32),
                        pltpu.SemaphoreType.DMA, pltpu.SemaphoreType.DMA],
    )
    return pl.pallas_call(
        functools.partial(_dispatch_kernel, tt=tt, bm=bm),
        grid_spec=grid_spec,
        out_shape=jax.ShapeDtypeStruct((n_rows * ROW_SUB, LANES), jnp.uint32),
        compiler_params=_cparams(("arbitrary",)),
        name="dispatch",
    )(last_block, cnt, dest, h2p)


def _expert_kernel(be_ref, nu_ref, x_ref, wgu_ref, bgu_ref, wd_ref, bd_ref, o_ref, wgu_bf, wd_bf):
    i = pl.program_id(0)

    @pl.when(i < nu_ref[0])
    def _():
        @pl.when((i == 0) | (be_ref[i] != be_ref[jnp.maximum(i - 1, 0)]))
        def _():
            wgu_bf[...] = wgu_ref[0].astype(BF16)
            wd_bf[...] = wd_ref[0].astype(BF16)

        lo, hi = _unpack_rows(_load_rows(x_ref))
        x = jnp.concatenate([lo, hi], axis=1).astype(BF16)
        gu = jnp.dot(x, wgu_bf[...], preferred_element_type=F32) + bgu_ref[0]
        gate = jnp.minimum(gu[:, :D_FF], SWIGLU_LIMIT)
        up = jnp.clip(gu[:, D_FF:], -SWIGLU_LIMIT, SWIGLU_LIMIT)
        glu = gate * jax.nn.sigmoid(SWIGLU_ALPHA * gate)
        act = ((up + 1.0) * glu).astype(BF16)
        _store_rows(o_ref, _pack_rows(jnp.dot(act, wd_bf[...], preferred_element_type=F32) + bd_ref[0]))


def _experts(block_expert, n_used, rows, wgu, bgu, wd, bd, bm):
    n_rows = rows.shape[0] // ROW_SUB
    nb = n_rows // bm
    row_map = lambda i, be, nu: (jnp.minimum(i, nu[0] - 1), 0)
    exp_map = lambda i, be, nu: (be[jnp.minimum(i, nu[0] - 1)], 0, 0)
    grid_spec = pltpu.PrefetchScalarGridSpec(
        num_scalar_prefetch=2,
        grid=(nb,),
        in_specs=[
            pl.BlockSpec((bm * ROW_SUB, LANES), row_map),
            pl.BlockSpec((1, D_MODEL, 2 * D_FF), exp_map),
            pl.BlockSpec((1, 1, 2 * D_FF), exp_map),
            pl.BlockSpec((1, D_FF, D_MODEL), exp_map),
            pl.BlockSpec((1, 1, D_MODEL), exp_map),
        ],
        out_specs=pl.BlockSpec((bm * ROW_SUB, LANES), row_map),
        scratch_shapes=[pltpu.VMEM((D_MODEL, 2 * D_FF), BF16), pltpu.VMEM((D_FF, D_MODEL), BF16)],
    )
    return pl.pallas_call(
        _expert_kernel,
        grid_spec=grid_spec,
        out_shape=jax.ShapeDtypeStruct((n_rows * ROW_SUB, LANES), jnp.uint32),
        compiler_params=_cparams(("arbitrary",)),
        name="experts",
    )(block_expert, n_used, rows, wgu, bgu, wd, bd)


def _combine_kernel(dest_ref, next_ref, x1_ref, gate_ref, y_ref, o_ref, buf_a, buf_b, sems, *, tc):
    i = pl.program_id(0)
    n = pl.num_programs(0)
    bufs = (buf_a, buf_b)

    def row_copy(idx_ref, buf, sem, t, k):
        return pltpu.make_async_copy(_row(y_ref, idx_ref[t * TOP_K + k]), _row(buf.at[k], t), sem)

    def gather_loop(idx_ref, buf, sem, wait):
        def body(g, carry):
            for u in range(DMA_UNROLL):
                for k in range(TOP_K):
                    cp = row_copy(idx_ref, buf, sem, g * DMA_UNROLL + u, k)
                    cp.wait() if wait else cp.start(priority=k % 2)
            return carry
        lax.fori_loop(0, tc // DMA_UNROLL, body, 0)

    @pl.when(i == 0)
    def _():
        gather_loop(dest_ref, buf_a, sems.at[0], False)

    for par in range(2):
        @pl.when(i % 2 == par)
        def _():
            cur, nxt = bufs[par], bufs[1 - par]
            gather_loop(dest_ref, cur, sems.at[par], True)
            for t in range(tc):
                for k in range(TOP_K):
                    row_copy(next_ref, nxt, sems.at[1 - par], t, k).start(priority=k % 2)
            x1 = x1_ref[...]
            acc_lo, acc_hi = x1[:, :PACKED], x1[:, PACKED:]
            g = gate_ref[...].T
            for k in range(TOP_K):
                lo, hi = _unpack_rows(_load_rows(cur.at[k]))
                acc_lo = acc_lo + g[:, k:k + 1] * lo
                acc_hi = acc_hi + g[:, k:k + 1] * hi
            o_ref[:, :PACKED] = acc_lo
            o_ref[:, PACKED:] = acc_hi

    @pl.when(i == n - 1)
    def _():
        for par in range(2):
            @pl.when(i % 2 == par)
            def _():
                gather_loop(next_ref, bufs[1 - par], sems.at[1 - par], True)


def _combine(dest, x1, gates, y_rows):
    T = x1.shape[0]
    tc = min(256, T)
    n = T // tc
    return pl.pallas_call(
        functools.partial(_combine_kernel, tc=tc),
        grid=(n,),
        in_specs=[
            pl.BlockSpec((tc * TOP_K,), lambda i: (i,), memory_space=pltpu.SMEM),
            pl.BlockSpec((tc * TOP_K,), lambda i: (jnp.minimum(i + 1, n - 1),), memory_space=pltpu.SMEM),
            pl.BlockSpec((tc, D_MODEL), lambda i: (i, 0)),
            pl.BlockSpec((2 * TOP_K, tc), lambda i: (0, i)),
            pl.BlockSpec(memory_space=pl.ANY),
        ],
        out_specs=pl.BlockSpec((tc, D_MODEL), lambda i: (i, 0)),
        out_shape=jax.ShapeDtypeStruct((T, D_MODEL), F32),
        scratch_shapes=[pltpu.VMEM((TOP_K, tc * ROW_SUB, LANES), jnp.uint32),
                        pltpu.VMEM((TOP_K, tc * ROW_SUB, LANES), jnp.uint32),
                        pltpu.SemaphoreType.DMA((2,))],
        compiler_params=_cparams(("arbitrary",)),
        name="combine",
    )(dest, dest, x1, gates, y_rows)


def _pad_lanes(v):
    return jnp.pad(v, ((0, 0), (0, LANES - v.shape[-1])))


def kernel(x, mix_norm_g, w_in, q_norm_g, k_norm_g, lambda_q1, lambda_k1, lambda_q2, lambda_k2,
           attn_subln_g, conv_w, conv_b, dt_bias, a_log, d_skip, ssm_norm_g, w_attn_proj,
           w_ssm_proj, w_out, ffn_norm_g, w_router, b_router, w_gate_up, b_gate_up, w_down, b_down):
    B, S, D = x.shape
    T = B * S
    xf = x.reshape(T, D)
    layer = 0

    wi = w_in[layer]
    o_z = 3 * ATT_WIDTH
    o_xbc = o_z + SSM_D_INNER
    o_dt = o_xbc + SSM_XBC
    o_ga = o_dt + SSM_HEADS
    w_main = jnp.concatenate([wi[:, o_xbc:o_dt], wi[:, o_z:o_xbc], wi[:, :o_z], wi[:, o_ga:]],
                             axis=1).astype(BF16)
    w_dt = _pad_lanes(wi[:, o_dt:o_ga]).astype(BF16)
    proj, dt_raw = _in_proj(xf, mix_norm_g[layer][None, :], w_main, w_dt)

    half = ATT_HEAD_DIM // 2
    inv = ROPE_THETA ** (-jnp.arange(0, ATT_HEAD_DIM, 2, dtype=F32) / ATT_HEAD_DIM)
    ang = jnp.arange(S, dtype=F32)[:, None] * inv[None, :]
    cos2 = jnp.tile(jnp.cos(ang), (1, LANES // half))
    sin2 = jnp.tile(jnp.concatenate([-jnp.sin(ang), jnp.sin(ang)], axis=1), (1, LANES // ATT_HEAD_DIM))
    qg2 = jnp.tile(q_norm_g[layer], 2)[None, :]
    kg2 = jnp.tile(k_norm_g[layer], 2)[None, :]
    partner = lambda g: jnp.tile(jnp.roll(g, half), 2)[None, :]
    q_scale = ATT_HEAD_DIM ** -0.5 * LOG2E
    att = _attention(proj, cos2 * qg2 * q_scale, sin2 * partner(q_norm_g[layer]) * q_scale,
                     cos2 * kg2, sin2 * partner(k_norm_g[layer]), qg2, kg2,
                     lambda_q1[layer][None, :], lambda_k1[layer][None, :],
                     lambda_q2[layer][None, :], lambda_k2[layer][None, :],
                     attn_subln_g[layer][None, :], B, S)

    ssm = _ssd(proj, dt_raw, conv_w[layer], conv_b[layer][None, :],
               _pad_lanes(dt_bias[layer][None, :]), _pad_lanes(a_log[layer][None, :]),
               jnp.repeat(d_skip[layer], SSM_HEAD_DIM)[None, :], ssm_norm_g[layer][None, :], B, S)

    x1, h2p, idx_rank_t, gates_t, counts = _merge(
        att, ssm, proj, xf, w_attn_proj[layer].astype(BF16), w_ssm_proj[layer].astype(BF16),
        w_out[layer].astype(BF16), ffn_norm_g[layer][None, :],
        _pad_lanes(w_router[layer]).astype(BF16), _pad_lanes(b_router[layer][None, :]))

    bm = 512
    A = T * TOP_K
    n_rows = (A + N_EXPERTS * (bm - 1)) // bm * bm
    cnt = counts[:, 0]
    padded = (cnt + bm - 1) // bm * bm
    pend = jnp.cumsum(padded)
    pstart = pend - padded
    n_used = (pend[-1:] // bm).astype(jnp.int32)
    block_start = jnp.arange(n_rows // bm, dtype=jnp.int32) * bm
    block_expert = jnp.minimum(jnp.sum(block_start[:, None] >= pend[None, :], axis=1),
                               N_EXPERTS - 1).astype(jnp.int32)
    idx_rank = idx_rank_t.T
    onehot = idx_rank[:, :TOP_K, None] == jnp.arange(N_EXPERTS, dtype=jnp.int32)[None, None, :]
    dest = (jnp.sum(jnp.where(onehot, pstart[None, None, :], 0), axis=-1)
            + idx_rank[:, TOP_K:]).reshape(A).astype(jnp.int32)

    rows = _dispatch((pend - bm).astype(jnp.int32), cnt.astype(jnp.int32), dest, h2p, n_rows, bm)
    y_rows = _experts(block_expert, n_used, rows, w_gate_up[layer], b_gate_up[layer][:, None, :],
                      w_down[layer], b_down[layer][:, None, :], bm)
    out = _combine(dest, x1, gates_t, y_rows)
    return out.reshape(B, S, D)
```

```python
import functools
import math

import jax
import jax.numpy as jnp
from jax import lax
from jax.experimental import pallas as pl
from jax.experimental.pallas import tpu as pltpu

F32 = jnp.float32
BF16 = jnp.bfloat16

D_MODEL = 1024
EPS = 1e-6
ATT_HEADS = 8
ATT_HEAD_DIM = 64
ATT_V_DIM = 2 * ATT_HEAD_DIM
ATT_WIDTH = ATT_HEADS * ATT_V_DIM
ROPE_THETA = 10000.0
SSM_D_INNER = 2 * D_MODEL
SSM_HEAD_DIM = 64
SSM_HEADS = SSM_D_INNER // SSM_HEAD_DIM
SSM_GROUPS = 8
SSM_STATE = 128
SSM_CONV = 4
SSM_CHUNK = 128
SSM_XBC = SSM_D_INNER + 2 * SSM_GROUPS * SSM_STATE
N_EXPERTS = 32
TOP_K = 4
D_FF = D_MODEL
SWIGLU_LIMIT = 7.0
SWIGLU_ALPHA = 1.702
LAM_INIT = 0.8 - 0.6 * math.exp(-0.3 * 0)

LANES = 128
HALO = 16

COL_XBC = 0
COL_Z = SSM_XBC
COL_Q = COL_Z + SSM_D_INNER
COL_K = COL_Q + ATT_WIDTH
COL_V = COL_K + ATT_WIDTH
COL_GA = COL_V + ATT_WIDTH
COL_GS = COL_GA + D_MODEL
PROJ_COLS = COL_GS + D_MODEL

VMEM_LIMIT = 56 * 1024 * 1024

PROJ_TM = 2048
PROJ_TN = 1024
ATT_TQ = 512
MERGE_TM = 512
DISPATCH_TT = 1024
EXPERT_BM = 512
COMBINE_TC = 256


def _cparams(sem):
    return pltpu.CompilerParams(dimension_semantics=sem, vmem_limit_bytes=VMEM_LIMIT)


PACKED = D_MODEL // 2


def _pack_rows(x):
    lo = lax.bitcast_convert_type(x[:, :PACKED].astype(BF16).astype(F32), jnp.uint32)
    hi = lax.bitcast_convert_type(x[:, PACKED:].astype(BF16).astype(F32), jnp.uint32)
    return hi | (lo >> 16)


def _unpack_rows(p):
    lo = lax.bitcast_convert_type(p << 16, F32)
    hi = lax.bitcast_convert_type(p & jnp.uint32(0xFFFF0000), F32)
    return lo, hi


ROW_SUB = PACKED // LANES


def _store_rows(ref, packed):
    for s in range(ROW_SUB):
        ref[pl.ds(s, packed.shape[0], stride=ROW_SUB), :] = packed[:, s * LANES:(s + 1) * LANES]


def _load_rows(ref):
    n = ref.shape[0] // ROW_SUB
    return jnp.concatenate([ref[pl.ds(s, n, stride=ROW_SUB), :] for s in range(ROW_SUB)], axis=1)


def _row(ref, r):
    return ref.at[pl.ds(pl.multiple_of(r * ROW_SUB, ROW_SUB), ROW_SUB)]


def _in_proj_kernel(x_ref, g_ref, w_ref, wdt_ref, o_ref, dt_ref, h_scr):
    @pl.when(pl.program_id(1) == 0)
    def _():
        x = x_ref[...]
        ms = jnp.mean(x * x, axis=-1, keepdims=True)
        hb = (x * lax.rsqrt(ms + EPS) * g_ref[...]).astype(BF16)
        h_scr[...] = hb
        dt_ref[...] = jnp.dot(hb, wdt_ref[...], preferred_element_type=F32)

    o_ref[...] = jnp.dot(h_scr[...], w_ref[...], preferred_element_type=F32).astype(BF16)


def _in_proj(xf, g, w_main, w_dt):
    T = xf.shape[0]
    tm = min(PROJ_TM, T)
    tn = PROJ_TN
    return pl.pallas_call(
        _in_proj_kernel,
        grid=(T // tm, PROJ_COLS // tn),
        in_specs=[
            pl.BlockSpec((tm, D_MODEL), lambda i, j: (i, 0)),
            pl.BlockSpec((1, D_MODEL), lambda i, j: (0, 0)),
            pl.BlockSpec((D_MODEL, tn), lambda i, j: (0, j)),
            pl.BlockSpec((D_MODEL, LANES), lambda i, j: (0, 0)),
        ],
        out_specs=[
            pl.BlockSpec((tm, tn), lambda i, j: (i, j)),
            pl.BlockSpec((tm, LANES), lambda i, j: (i, 0)),
        ],
        out_shape=[
            jax.ShapeDtypeStruct((T, PROJ_COLS), BF16),
            jax.ShapeDtypeStruct((T, LANES), F32),
        ],
        scratch_shapes=[pltpu.VMEM((tm, D_MODEL), BF16)],
        compiler_params=_cparams(("arbitrary", "arbitrary")),
        name="in_proj",
    )(xf, g, w_main, w_dt)


LOG2E = 1.4426950408889634
SHIFT_LIMIT = 57.0
BOUND_MARGIN = 1.02


def _attn_kernel(q_ref, k_ref, v_ref, qc_ref, qs_ref, kc_ref, ks_ref, qg_ref, kg_ref,
                 lq1_ref, lk1_ref, lq2_ref, lk2_ref, sg_ref, o_ref, k_scr, v_scr, q1_scr, q2_scr, *, tq, seq):
    p = pl.program_id(2)
    nq = seq // tq
    paired = nq > 1
    lane = lax.broadcasted_iota(jnp.int32, (1, LANES), 1)
    first = lane < ATT_HEAD_DIM
    one_col = jnp.where(lane == 0, 1.0, 0.0)

    ri = lax.broadcasted_iota(jnp.int32, (LANES, LANES), 0)
    ci = lax.broadcasted_iota(jnp.int32, (LANES, LANES), 1)
    same_comp = (ri // ATT_HEAD_DIM == ci // ATT_HEAD_DIM).astype(BF16)
    swap_half = ((ri // ATT_HEAD_DIM == ci // ATT_HEAD_DIM)
                 & ((ri - ci == ATT_HEAD_DIM // 2) | (ci - ri == ATT_HEAD_DIM // 2))).astype(BF16)

    def norm_rope(xb, gcos, gsin):
        xf = xb.astype(F32)
        sq = xf * xf
        hi = sq.astype(BF16)
        lo = (sq - hi.astype(F32)).astype(BF16)
        ms = jnp.dot(jnp.concatenate([hi, lo], axis=1), jnp.concatenate([same_comp, same_comp], axis=0),
                     preferred_element_type=F32) * (1.0 / ATT_HEAD_DIM)
        xr = jnp.dot(xb, swap_half, preferred_element_type=F32)
        return lax.rsqrt(ms + EPS) * (xf * gcos + xr * gsin)

    ub = (ATT_HEAD_DIM * ATT_HEAD_DIM ** -0.5 * LOG2E * BOUND_MARGIN
          * jnp.max(jnp.abs(qg_ref[...])) * jnp.max(jnp.abs(kg_ref[...])))
    safe = ub <= SHIFT_LIMIT

    @pl.when(p == 0)
    def _():
        ones = jnp.broadcast_to(one_col, (tq, LANES)).astype(BF16)
        shift = jnp.broadcast_to(jnp.where(lane == 0, -ub, 0.0), (tq, LANES)).astype(BF16)
        for r in range(0, seq, tq):
            rows = slice(r, r + tq)
            k_scr[rows, :LANES] = norm_rope(k_ref[rows, :], kc_ref[rows, :], ks_ref[rows, :]).astype(BF16)
            k_scr[rows, LANES:] = ones
            v_scr[rows, :LANES] = v_ref[rows, :]
            v_scr[rows, LANES:] = ones
            qb = norm_rope(q_ref[rows, :], qc_ref[rows, :], qs_ref[rows, :]).astype(BF16)
            q1_scr[rows, :LANES] = jnp.where(first, qb, jnp.zeros_like(qb))
            q1_scr[rows, LANES:] = shift
            q2_scr[rows, :LANES] = jnp.where(first, jnp.zeros_like(qb), qb)
            q2_scr[rows, LANES:] = shift

    keep = (lax.broadcasted_iota(jnp.int32, (tq, tq), 1) <= lax.broadcasted_iota(jnp.int32, (tq, tq), 0))
    nt = (((1,), (1,)), ((), ()))
    rows_a = pl.multiple_of(p * tq, tq)
    rows_b = pl.multiple_of((nq - 1 - p) * tq, tq)

    def finish(o1, o2, rows0):
        lam = (jnp.exp(jnp.sum(lq1_ref[...] * lk1_ref[...], axis=-1, keepdims=True))
               - jnp.exp(jnp.sum(lq2_ref[...] * lk2_ref[...], axis=-1, keepdims=True)) + LAM_INIT)
        o = o1 - lam * o2
        ms = jnp.mean(o * o, axis=-1, keepdims=True)
        o_ref[pl.ds(rows0, tq), :] = (o * lax.rsqrt(ms + EPS) * sg_ref[...] * (1.0 - LAM_INIT)).astype(BF16)

    @pl.when(safe)
    def _():
        def block(q_scr, qrow0, krow0, rows, cols, mask):
            t = lax.dot_general(q_scr[pl.ds(qrow0, rows), :], k_scr[pl.ds(krow0, cols), :], nt,
                                preferred_element_type=F32)
            if mask is not None:
                t = jnp.where(mask, t, -jnp.inf)
            return jnp.dot(jnp.exp2(t).astype(BF16), v_scr[pl.ds(krow0, cols), :], preferred_element_type=F32)

        def tile(qrow0, krow0, masked):
            if not masked:
                return (block(q1_scr, qrow0, krow0, tq, tq, None), block(q2_scr, qrow0, krow0, tq, tq, None))
            h = tq // 2
            q_low = pl.multiple_of(qrow0 + h, h)
            k_hi = pl.multiple_of(krow0 + h, h)
            outs = []
            for q_scr in (q1_scr, q2_scr):
                left_cols = block(q_scr, qrow0, krow0, tq, h, keep[:, :h])
                corner = block(q_scr, q_low, k_hi, h, h, keep[:h, :h])
                outs.append(jnp.concatenate([left_cols[:h], left_cols[h:] + corner], axis=0))
            return tuple(outs)

        def normalised(acc):
            return acc[:, :LANES] / acc[:, LANES:LANES + 1]

        a1, a2 = tile(rows_a, rows_a, True)
        if paired:
            b1, b2 = tile(rows_b, rows_b, True)
            for s in range(nq - 1):
                to_a = s < p
                qrow0 = jnp.where(to_a, rows_a, rows_b)
                krow0 = pl.multiple_of(jnp.where(to_a, s, s - p) * tq, tq)
                d1, d2 = tile(pl.multiple_of(qrow0, tq), krow0, False)
                a1 = a1 + jnp.where(to_a, d1, 0.0)
                a2 = a2 + jnp.where(to_a, d2, 0.0)
                b1 = b1 + jnp.where(to_a, 0.0, d1)
                b2 = b2 + jnp.where(to_a, 0.0, d2)
            finish(normalised(b1), normalised(b2), rows_b)
        finish(normalised(a1), normalised(a2), rows_a)

    @pl.when(jnp.logical_not(safe))
    def _():
        def update(t, vt, m, l, acc):
            m_new = jnp.maximum(m, jnp.max(t, axis=-1, keepdims=True))
            alpha = jnp.exp2(m - m_new)
            pr = jnp.exp2(t - m_new)
            l_new = alpha * l + jnp.sum(pr, axis=-1, keepdims=True)
            acc_new = alpha * acc + jnp.dot(pr.astype(BF16), vt, preferred_element_type=F32)
            return m_new, l_new, acc_new

        def online(rows0, n_off):
            q1 = q1_scr[pl.ds(rows0, tq), :LANES]
            q2 = q2_scr[pl.ds(rows0, tq), :LANES]

            def step(krow0, carry, masked):
                m1, l1, c1, m2, l2, c2 = carry
                kt = k_scr[pl.ds(krow0, tq), :LANES]
                vt = v_scr[pl.ds(krow0, tq), :LANES]
                t1 = lax.dot_general(q1, kt, nt, preferred_element_type=F32)
                t2 = lax.dot_general(q2, kt, nt, preferred_element_type=F32)
                if masked:
                    t1 = jnp.where(keep, t1, -jnp.inf)
                    t2 = jnp.where(keep, t2, -jnp.inf)
                m1, l1, c1 = update(t1, vt, m1, l1, c1)
                m2, l2, c2 = update(t2, vt, m2, l2, c2)
                return m1, l1, c1, m2, l2, c2

            neg = jnp.full((tq, 1), -jnp.inf, F32)
            zero1 = jnp.zeros((tq, 1), F32)
            zacc = jnp.zeros((tq, ATT_V_DIM), F32)
            carry = step(rows0, (neg, zero1, zacc, neg, zero1, zacc), True)
            m1, l1, c1, m2, l2, c2 = lax.fori_loop(
                0, n_off, lambda j, c: step(pl.multiple_of(j * tq, tq), c, False), carry)
            finish(c1 / l1, c2 / l2, rows0)

        online(rows_a, p)
        if paired:
            online(rows_b, nq - 1 - p)


def _attention(proj, qcos, qsin, kcos, ksin, qg2, kg2, lq1, lk1, lq2, lk2, sg, batch, seq):
    T = batch * seq
    tq = min(ATT_TQ, seq)
    nq = seq // tq
    assert nq == 1 or nq % 2 == 0, "query tiles are processed in balanced pairs"
    vec = lambda n: pl.BlockSpec((1, n), lambda b, h, i: (0, 0))
    table = pl.BlockSpec((seq, LANES), lambda b, h, i: (0, 0))
    head = lambda col: pl.BlockSpec((seq, LANES), lambda b, h, i: (b, col // LANES + h))
    return pl.pallas_call(
        functools.partial(_attn_kernel, tq=tq, seq=seq),
        grid=(batch, ATT_HEADS, max(nq // 2, 1)),
        in_specs=[
            head(COL_Q), head(COL_K), head(COL_V),
            table, table, table, table,
            vec(LANES), vec(LANES),
            vec(ATT_HEAD_DIM), vec(ATT_HEAD_DIM), vec(ATT_HEAD_DIM), vec(ATT_HEAD_DIM),
            vec(LANES),
        ],
        out_specs=pl.BlockSpec((seq, LANES), lambda b, h, i: (b, h)),
        out_shape=jax.ShapeDtypeStruct((T, ATT_WIDTH), BF16),
        scratch_shapes=[pltpu.VMEM((seq, 2 * LANES), BF16) for _ in range(4)],
        compiler_params=_cparams(("arbitrary", "arbitrary", "arbitrary")),
        name="attention",
    )(proj, proj, proj, qcos, qsin, kcos, ksin, qg2, kg2, lq1, lk1, lq2, lk2, sg)


SSD_SUB = 2


def _ssd_kernel(xbc_ref, z_ref, dt_ref, cw_ref, cb_ref, dtb_ref, alog_ref, dsk_ref, ng_ref,
                o_ref, halo, state, xs_scr, bc_scr, y_scr):
    Q = SSM_CHUNK
    N = SSM_STATE
    P2 = 2 * SSM_HEAD_DIM

    @pl.when(pl.program_id(1) == 0)
    def _():
        halo[...] = jnp.zeros_like(halo)
        state[...] = jnp.zeros_like(state)

    taps = SSM_CONV - 1
    sr = lax.broadcasted_iota(jnp.int32, (taps * Q, HALO + Q), 0)
    sc = lax.broadcasted_iota(jnp.int32, (taps * Q, HALO + Q), 1)
    shift = (sc == (sr % Q) + HALO - taps + sr // Q).astype(BF16)
    ri = lax.broadcasted_iota(jnp.int32, (Q, Q), 0)
    ci = lax.broadcasted_iota(jnp.int32, (Q, Q), 1)
    tri = ri >= ci
    lane = lax.broadcasted_iota(jnp.int32, (1, P2), 1)
    left = lane < SSM_HEAD_DIM
    gw = SSM_D_INNER // SSM_GROUPS

    for sub in range(SSD_SUB):
        r0 = sub * Q
        xs_sub, bc_sub, y_sub = xs_scr.at[sub], bc_scr.at[sub], y_scr.at[sub]

        cwid = 512
        for c0 in range(0, SSM_XBC, cwid):
            cur = xbc_ref[r0:r0 + Q, c0:c0 + cwid]
            hist = halo[:, c0:c0 + cwid] if sub == 0 else xbc_ref[r0 - HALO:r0, c0:c0 + cwid]
            ext = jnp.concatenate([hist, cur], axis=0)
            shifted = jnp.dot(shift, ext, preferred_element_type=F32)
            acc = cb_ref[:, c0:c0 + cwid] + cw_ref[taps:SSM_CONV, c0:c0 + cwid] * cur.astype(F32)
            for k in range(taps):
                acc = acc + cw_ref[k:k + 1, c0:c0 + cwid] * shifted[k * Q:(k + 1) * Q, :]
            act = acc * jax.nn.sigmoid(acc)
            if c0 < SSM_D_INNER:
                xs_sub[:, c0:c0 + cwid] = act
            else:
                bc_sub[:, c0 - SSM_D_INNER:c0 - SSM_D_INNER + cwid] = act.astype(BF16)

        dt = jax.nn.softplus(dt_ref[r0:r0 + Q, :] + dtb_ref[...])
        da = dt * (-jnp.exp(alog_ref[...]) * LOG2E)
        acs = jnp.dot(tri.astype(F32), da, preferred_element_type=F32,
                      precision=lax.Precision.HIGHEST)
        acs_t = acs.T
        dt_t = dt.T
        w_t = dt_t * jnp.exp2(acs_t[:, Q - 1:Q] - acs_t)

        for g in range(SSM_GROUPS):
            bm = bc_sub[:, g * N:(g + 1) * N]
            cm = bc_sub[:, SSM_GROUPS * N + g * N:SSM_GROUPS * N + (g + 1) * N]
            cb = lax.dot_general(cm, bm, (((1,), (1,)), ((), ())), preferred_element_type=F32)
            bm_t = bm.astype(F32).T
            cm_f = cm.astype(F32)
            for pr in range(2):
                pair = 2 * g + pr
                xs_pair = xs_sub[:, pair * P2:(pair + 1) * P2].astype(BF16)
                prev = state[pair]
                rhs = jnp.concatenate([xs_pair, prev.astype(BF16)], axis=0)
                ys, sts, decs = [], [], []
                for r in range(2):
                    h = 2 * pair + r
                    a_col = acs[:, h:h + 1]
                    seg = a_col - acs_t[h:h + 1, :]
                    decay = jnp.exp2(jnp.where(tri, seg, -jnp.inf))
                    m_h = cb * decay * dt_t[h:h + 1, :]
                    e_h = cm_f * jnp.exp2(a_col)
                    lhs = jnp.concatenate([m_h, e_h], axis=1).astype(BF16)
                    ys.append(jnp.dot(lhs, rhs, preferred_element_type=F32))
                    sts.append(jnp.dot((bm_t * w_t[h:h + 1, :]).astype(BF16), xs_pair,
                                       preferred_element_type=F32))
                    decs.append(jnp.exp2(acs[Q - 1:Q, h:h + 1]))
                y_sub[:, pair * P2:(pair + 1) * P2] = jnp.where(left, ys[0], ys[1])
                dec = jnp.where(left, decs[0], decs[1])
                state[pair] = dec * prev + jnp.where(left, sts[0], sts[1])

        y = y_sub[...] + dsk_ref[...] * xs_sub[...]
        zf = z_ref[r0:r0 + Q, :].astype(F32)
        y = y * (zf * jax.nn.sigmoid(zf))
        for g in range(SSM_GROUPS):
            yg = y[:, g * gw:(g + 1) * gw]
            ms = jnp.mean(yg * yg, axis=-1, keepdims=True)
            o_ref[r0:r0 + Q, g * gw:(g + 1) * gw] = (yg * lax.rsqrt(ms + EPS)
                                                     * ng_ref[:, g * gw:(g + 1) * gw]).astype(BF16)

    halo[...] = xbc_ref[SSD_SUB * Q - HALO:SSD_SUB * Q, :]


def _ssd(proj, dt_raw, conv_w, conv_b, dt_bias, a_log, dsk, ng, batch, seq):
    T = batch * seq
    Q = SSM_CHUNK
    rows = SSD_SUB * Q
    nc = seq // rows
    vec = lambda r, n: pl.BlockSpec((r, n), lambda b, c: (0, 0))
    return pl.pallas_call(
        _ssd_kernel,
        grid=(batch, nc),
        in_specs=[
            pl.BlockSpec((rows, SSM_XBC), lambda b, c: (b * nc + c, COL_XBC // SSM_XBC)),
            pl.BlockSpec((rows, SSM_D_INNER), lambda b, c: (b * nc + c, COL_Z // SSM_D_INNER)),
            pl.BlockSpec((rows, LANES), lambda b, c: (b * nc + c, 0)),
            vec(SSM_CONV, SSM_XBC), vec(1, SSM_XBC), vec(1, LANES), vec(1, LANES),
            vec(1, SSM_D_INNER), vec(1, SSM_D_INNER),
        ],
        out_specs=pl.BlockSpec((rows, SSM_D_INNER), lambda b, c: (b * nc + c, 0)),
        out_shape=jax.ShapeDtypeStruct((T, SSM_D_INNER), BF16),
        scratch_shapes=[
            pltpu.VMEM((HALO, SSM_XBC), BF16),
            pltpu.VMEM((SSM_HEADS // 2, SSM_STATE, 2 * SSM_HEAD_DIM), F32),
            pltpu.VMEM((SSD_SUB, Q, SSM_D_INNER), F32),
            pltpu.VMEM((SSD_SUB, Q, 2 * SSM_GROUPS * SSM_STATE), BF16),
            pltpu.VMEM((SSD_SUB, Q, SSM_D_INNER), F32),
        ],
        compiler_params=_cparams(("arbitrary", "arbitrary")),
        name="ssd",
    )(proj, proj, dt_raw, conv_w, conv_b, dt_bias, a_log, dsk, ng)


def _merge_kernel(att_ref, ssm_ref, ga_ref, gs_ref, x_ref, wap_ref, wsp_ref, wo_ref, fg_ref,
                  wr_ref, br_ref, x1_ref, h2_ref, idx_ref, gate_ref, cnt_ref, base, *, tm):
    i = pl.program_id(0)

    @pl.when(i == 0)
    def _():
        base[...] = jnp.zeros_like(base)

    pa = jnp.dot(att_ref[...], wap_ref[...], preferred_element_type=F32)
    ps = jnp.dot(ssm_ref[...], wsp_ref[...], preferred_element_type=F32)
    merged = (jax.nn.sigmoid(ga_ref[...].astype(F32)) * pa
              + jax.nn.sigmoid(gs_ref[...].astype(F32)) * ps)
    x1 = x_ref[...] + jnp.dot(merged.astype(BF16), wo_ref[...], preferred_element_type=F32)
    x1_ref[...] = x1
    ms = jnp.mean(x1 * x1, axis=-1, keepdims=True)
    h2 = x1 * lax.rsqrt(ms + EPS) * fg_ref[...]
    _store_rows(h2_ref, _pack_rows(h2))

    logits = jnp.dot(h2.astype(BF16), wr_ref[...], preferred_element_type=F32) + br_ref[...]
    lg = logits.T[:N_EXPERTS, :]
    row_f = lax.broadcasted_iota(jnp.int32, (N_EXPERTS, tm), 0).astype(F32)
    vals, idxs, sels = [], [], []
    for _ in range(TOP_K):
        m = jnp.max(lg, axis=0, keepdims=True)
        idx = jnp.min(jnp.where(lg == m, row_f, float(N_EXPERTS)), axis=0, keepdims=True)
        sel = row_f == idx
        vals.append(m)
        idxs.append(idx)
        sels.append(sel)
        lg = jnp.where(sel, -jnp.inf, lg)
    es = [jnp.exp(v - vals[0]) for v in vals]
    den = es[0] + es[1] + es[2] + es[3]

    multi = jnp.zeros((N_EXPERTS, tm), F32)
    for sel in sels:
        multi = jnp.where(sel, 1.0, multi)
    ri = lax.broadcasted_iota(jnp.int32, (tm, tm), 0)
    ci = lax.broadcasted_iota(jnp.int32, (tm, tm), 1)
    before = jnp.dot(multi.astype(BF16), (ri < ci).astype(BF16), preferred_element_type=F32)
    before = before + base[...]
    ranks = [jnp.sum(jnp.where(sel, before, 0.0), axis=0, keepdims=True) for sel in sels]
    idx_ref[...] = jnp.concatenate(idxs + ranks, axis=0).astype(jnp.int32)
    gate_ref[...] = jnp.concatenate([e / den for e in es] + [jnp.zeros_like(den)] * TOP_K, axis=0)
    base[...] = base[...] + jnp.sum(multi, axis=1, keepdims=True)
    cnt_ref[...] = jnp.broadcast_to(base[...], cnt_ref.shape).astype(jnp.int32)


def _merge(att, ssm, proj, xf, wap, wsp, wo, fg, wr, br):
    T = xf.shape[0]
    tm = min(MERGE_TM, T)
    full = lambda a: pl.BlockSpec(a.shape, lambda i: (0, 0))
    return pl.pallas_call(
        functools.partial(_merge_kernel, tm=tm),
        grid=(T // tm,),
        in_specs=[
            pl.BlockSpec((tm, ATT_WIDTH), lambda i: (i, 0)),
            pl.BlockSpec((tm, SSM_D_INNER), lambda i: (i, 0)),
            pl.BlockSpec((tm, D_MODEL), lambda i: (i, COL_GA // D_MODEL)),
            pl.BlockSpec((tm, D_MODEL), lambda i: (i, COL_GS // D_MODEL)),
            pl.BlockSpec((tm, D_MODEL), lambda i: (i, 0)),
            full(wap), full(wsp), full(wo), full(fg), full(wr), full(br),
        ],
        out_specs=[
            pl.BlockSpec((tm, D_MODEL), lambda i: (i, 0)),
            pl.BlockSpec((tm * ROW_SUB, LANES), lambda i: (i, 0)),
            pl.BlockSpec((2 * TOP_K, tm), lambda i: (0, i)),
            pl.BlockSpec((2 * TOP_K, tm), lambda i: (0, i)),
            pl.BlockSpec((N_EXPERTS, LANES), lambda i: (0, 0)),
        ],
        out_shape=[
            jax.ShapeDtypeStruct((T, D_MODEL), F32),
            jax.ShapeDtypeStruct((T * ROW_SUB, LANES), jnp.uint32),
            jax.ShapeDtypeStruct((2 * TOP_K, T), jnp.int32),
            jax.ShapeDtypeStruct((2 * TOP_K, T), F32),
            jax.ShapeDtypeStruct((N_EXPERTS, LANES), jnp.int32),
        ],
        scratch_shapes=[pltpu.VMEM((N_EXPERTS, 1), F32)],
        compiler_params=_cparams(("arbitrary",)),
        name="merge_router",
    )(att, ssm, proj, proj, xf, wap, wsp, wo, fg, wr, br)


DMA_UNROLL = 8


def _dispatch_kernel(lb_ref, cnt_ref, dest_ref, h_ref, rows_ref, zeros, zsem, sem, *, tt, bm):
    @pl.when(pl.program_id(0) == 0)
    def _():
        zeros[...] = jnp.zeros_like(zeros)
        for wait in (False, True):
            for e in range(N_EXPERTS):
                @pl.when(cnt_ref[e] > 0)
                def _():
                    start = pl.multiple_of(lb_ref[e], bm)
                    cp = pltpu.make_async_copy(zeros, rows_ref.at[pl.ds(start * ROW_SUB, bm * ROW_SUB)], zsem)
                    cp.wait() if wait else cp.start()

    def row_copy(t, k):
        return pltpu.make_async_copy(_row(h_ref, t), _row(rows_ref, dest_ref[t * TOP_K + k]), sem)

    for wait in (False, True):
        def body(g, carry):
            for u in range(DMA_UNROLL):
                for k in range(TOP_K):
                    cp = row_copy(g * DMA_UNROLL + u, k)
                    cp.wait() if wait else cp.start(priority=k % 2)
            return carry
        lax.fori_loop(0, tt // DMA_UNROLL, body, 0)


def _dispatch(last_block, cnt, dest, h2p, n_rows, bm):
    T = h2p.shape[0] // ROW_SUB
    tt = min(DISPATCH_TT, T)
    grid_spec = pltpu.PrefetchScalarGridSpec(
        num_scalar_prefetch=2,
        grid=(T // tt,),
        in_specs=[
            pl.BlockSpec((tt * TOP_K,), lambda i, lb, c: (i,), memory_space=pltpu.SMEM),
            pl.BlockSpec((tt * ROW_SUB, LANES), lambda i, lb, c: (i, 0)),
        ],
        out_specs=pl.BlockSpec(memory_space=pl.ANY),
        scratch_shapes=[pltpu.VMEM((bm * ROW_SUB, LANES), jnp.uint32),
                        pltpu.SemaphoreType.DMA, pltpu.SemaphoreType.DMA],
    )
    return pl.pallas_call(
        functools.partial(_dispatch_kernel, tt=tt, bm=bm),
        grid_spec=grid_spec,
        out_shape=jax.ShapeDtypeStruct((n_rows * ROW_SUB, LANES), jnp.uint32),
        compiler_params=_cparams(("arbitrary",)),
        name="dispatch",
    )(last_block, cnt, dest, h2p)


def _expert_kernel(be_ref, nu_ref, x_ref, wgu_ref, bgu_ref, wd_ref, bd_ref, o_ref, wgu_bf, wd_bf):
    i = pl.program_id(0)

    @pl.when(i < nu_ref[0])
    def _():
        @pl.when((i == 0) | (be_ref[i] != be_ref[jnp.maximum(i - 1, 0)]))
        def _():
            wgu_bf[...] = wgu_ref[0].astype(BF16)
            wd_bf[...] = wd_ref[0].astype(BF16)

        lo, hi = _unpack_rows(_load_rows(x_ref))
        x = jnp.concatenate([lo, hi], axis=1).astype(BF16)
        gu = jnp.dot(x, wgu_bf[...], preferred_element_type=F32) + bgu_ref[0]
        gate = jnp.minimum(gu[:, :D_FF], SWIGLU_LIMIT)
        up = jnp.clip(gu[:, D_FF:], -SWIGLU_LIMIT, SWIGLU_LIMIT)
        glu = gate * jax.nn.sigmoid(SWIGLU_ALPHA * gate)
        act = ((up + 1.0) * glu).astype(BF16)
        _store_rows(o_ref, _pack_rows(jnp.dot(act, wd_bf[...], preferred_element_type=F32) + bd_ref[0]))


def _experts(block_expert, n_used, rows, wgu, bgu, wd, bd, bm):
    n_rows = rows.shape[0] // ROW_SUB
    nb = n_rows // bm
    row_map = lambda i, be, nu: (jnp.minimum(i, nu[0] - 1), 0)
    exp_map = lambda i, be, nu: (be[jnp.minimum(i, nu[0] - 1)], 0, 0)
    grid_spec = pltpu.PrefetchScalarGridSpec(
        num_scalar_prefetch=2,
        grid=(nb,),
        in_specs=[
            pl.BlockSpec((bm * ROW_SUB, LANES), row_map),
            pl.BlockSpec((1, D_MODEL, 2 * D_FF), exp_map),
            pl.BlockSpec((1, 1, 2 * D_FF), exp_map),
            pl.BlockSpec((1, D_FF, D_MODEL), exp_map),
            pl.BlockSpec((1, 1, D_MODEL), exp_map),
        ],
        out_specs=pl.BlockSpec((bm * ROW_SUB, LANES), row_map),
        scratch_shapes=[pltpu.VMEM((D_MODEL, 2 * D_FF), BF16), pltpu.VMEM((D_FF, D_MODEL), BF16)],
    )
    return pl.pallas_call(
        _expert_kernel,
        grid_spec=grid_spec,
        out_shape=jax.ShapeDtypeStruct((n_rows * ROW_SUB, LANES), jnp.uint32),
        compiler_params=_cparams(("arbitrary",)),
        name="experts",
    )(block_expert, n_used, rows, wgu, bgu, wd, bd)


def _combine_kernel(dest_ref, next_ref, x1_ref, gate_ref, y_ref, o_ref, ybuf, sems, *, tc):
    i = pl.program_id(0)
    n = pl.num_programs(0)

    def gather(idx_ref, slot, wait):
        def row_copy(t, k):
            return pltpu.make_async_copy(_row(y_ref, idx_ref[t * TOP_K + k]), _row(ybuf.at[slot, k], t),
                                         sems.at[slot])

        def body(g, carry):
            for u in range(DMA_UNROLL):
                for k in range(TOP_K):
                    cp = row_copy(g * DMA_UNROLL + u, k)
                    cp.wait() if wait else cp.start(priority=k % 2)
            return carry
        lax.fori_loop(0, tc // DMA_UNROLL, body, 0)

    slot = i % 2

    @pl.when(i == 0)
    def _():
        gather(dest_ref, 0, False)

    @pl.when(i + 1 < n)
    def _():
        gather(next_ref, 1 - slot, False)

    gather(dest_ref, slot, True)

    x1 = x1_ref[...]
    acc_lo, acc_hi = x1[:, :PACKED], x1[:, PACKED:]
    g = gate_ref[...].T
    for k in range(TOP_K):
        lo, hi = _unpack_rows(_load_rows(ybuf.at[slot, k]))
        acc_lo = acc_lo + g[:, k:k + 1] * lo
        acc_hi = acc_hi + g[:, k:k + 1] * hi
    o_ref[:, :PACKED] = acc_lo
    o_ref[:, PACKED:] = acc_hi


def _combine(dest, x1, gates, y_rows):
    T = x1.shape[0]
    tc = min(COMBINE_TC, T)
    n = T // tc
    return pl.pallas_call(
        functools.partial(_combine_kernel, tc=tc),
        grid=(n,),
        in_specs=[
            pl.BlockSpec((tc * TOP_K,), lambda i: (i,), memory_space=pltpu.SMEM),
            pl.BlockSpec((tc * TOP_K,), lambda i: (jnp.minimum(i + 1, n - 1),), memory_space=pltpu.SMEM),
            pl.BlockSpec((tc, D_MODEL), lambda i: (i, 0)),
            pl.BlockSpec((2 * TOP_K, tc), lambda i: (0, i)),
            pl.BlockSpec(memory_space=pl.ANY),
        ],
        out_specs=pl.BlockSpec((tc, D_MODEL), lambda i: (i, 0)),
        out_shape=jax.ShapeDtypeStruct((T, D_MODEL), F32),
        scratch_shapes=[pltpu.VMEM((2, TOP_K, tc * ROW_SUB, LANES), jnp.uint32),
                        pltpu.SemaphoreType.DMA((2,))],
        compiler_params=_cparams(("arbitrary",)),
        name="combine",
    )(dest, dest, x1, gates, y_rows)


def _pad_lanes(v):
    return jnp.pad(v, ((0, 0), (0, LANES - v.shape[-1])))


def kernel(x, mix_norm_g, w_in, q_norm_g, k_norm_g, lambda_q1, lambda_k1, lambda_q2, lambda_k2,
           attn_subln_g, conv_w, conv_b, dt_bias, a_log, d_skip, ssm_norm_g, w_attn_proj,
           w_ssm_proj, w_out, ffn_norm_g, w_router, b_router, w_gate_up, b_gate_up, w_down, b_down):
    B, S, D = x.shape
    T = B * S
    xf = x.reshape(T, D)
    layer = 0

    wi = w_in[layer]
    o_z = 3 * ATT_WIDTH
    o_xbc = o_z + SSM_D_INNER
    o_dt = o_xbc + SSM_XBC
    o_ga = o_dt + SSM_HEADS
    w_main = jnp.concatenate([wi[:, o_xbc:o_dt], wi[:, o_z:o_xbc], wi[:, :o_z], wi[:, o_ga:]],
                             axis=1).astype(BF16)
    w_dt = _pad_lanes(wi[:, o_dt:o_ga]).astype(BF16)
    proj, dt_raw = _in_proj(xf, mix_norm_g[layer][None, :], w_main, w_dt)

    half = ATT_HEAD_DIM // 2
    inv = ROPE_THETA ** (-jnp.arange(0, ATT_HEAD_DIM, 2, dtype=F32) / ATT_HEAD_DIM)
    ang = jnp.arange(S, dtype=F32)[:, None] * inv[None, :]
    cos2 = jnp.tile(jnp.cos(ang), (1, LANES // half))
    sin2 = jnp.tile(jnp.concatenate([-jnp.sin(ang), jnp.sin(ang)], axis=1), (1, LANES // ATT_HEAD_DIM))
    qg2 = jnp.tile(q_norm_g[layer], 2)[None, :]
    kg2 = jnp.tile(k_norm_g[layer], 2)[None, :]
    partner = lambda g: jnp.tile(jnp.roll(g, half), 2)[None, :]
    q_scale = ATT_HEAD_DIM ** -0.5 * LOG2E
    att = _attention(proj, cos2 * qg2 * q_scale, sin2 * partner(q_norm_g[layer]) * q_scale,
                     cos2 * kg2, sin2 * partner(k_norm_g[layer]), qg2, kg2,
                     lambda_q1[layer][None, :], lambda_k1[layer][None, :],
                     lambda_q2[layer][None, :], lambda_k2[layer][None, :],
                     attn_subln_g[layer][None, :], B, S)

    ssm = _ssd(proj, dt_raw, conv_w[layer], conv_b[layer][None, :],
               _pad_lanes(dt_bias[layer][None, :]), _pad_lanes(a_log[layer][None, :]),
               jnp.repeat(d_skip[layer], SSM_HEAD_DIM)[None, :], ssm_norm_g[layer][None, :], B, S)

    x1, h2p, idx_rank_t, gates_t, counts = _merge(
        att, ssm, proj, xf, w_attn_proj[layer].astype(BF16), w_ssm_proj[layer].astype(BF16),
        w_out[layer].astype(BF16), ffn_norm_g[layer][None, :],
        _pad_lanes(w_router[layer]).astype(BF16), _pad_lanes(b_router[layer][None, :]))

    bm = EXPERT_BM
    A = T * TOP_K
    n_rows = (A + N_EXPERTS * (bm - 1)) // bm * bm
    cnt = counts[:, 0]
    padded = (cnt + bm - 1) // bm * bm
    pend = jnp.cumsum(padded)
    pstart = pend - padded
    n_used = (pend[-1:] // bm).astype(jnp.int32)
    block_start = jnp.arange(n_rows // bm, dtype=jnp.int32) * bm
    block_expert = jnp.minimum(jnp.sum(block_start[:, None] >= pend[None, :], axis=1),
                               N_EXPERTS - 1).astype(jnp.int32)
    idx_rank = idx_rank_t.T
    onehot = idx_rank[:, :TOP_K, None] == jnp.arange(N_EXPERTS, dtype=jnp.int32)[None, None, :]
    dest = (jnp.sum(jnp.where(onehot, pstart[None, None, :], 0), axis=-1)
            + idx_rank[:, TOP_K:]).reshape(A).astype(jnp.int32)

    rows = _dispatch((pend - bm).astype(jnp.int32), cnt.astype(jnp.int32), dest, h2p, n_rows, bm)
    y_rows = _experts(block_expert, n_used, rows, w_gate_up[layer], b_gate_up[layer][:, None, :],
                      w_down[layer], b_down[layer][:, None, :], bm)
    out = _combine(dest, x1, gates_t, y_rows)
    return out.reshape(B, S, D)
```

```python
import functools
import math

import jax
import jax.numpy as jnp
from jax import lax
from jax.experimental import pallas as pl
from jax.experimental.pallas import tpu as pltpu

F32 = jnp.float32
BF16 = jnp.bfloat16

D_MODEL = 1024
EPS = 1e-6
ATT_HEADS = 8
ATT_HEAD_DIM = 64
ATT_V_DIM = 2 * ATT_HEAD_DIM
ATT_WIDTH = ATT_HEADS * ATT_V_DIM
ROPE_THETA = 10000.0
SSM_D_INNER = 2 * D_MODEL
SSM_HEAD_DIM = 64
SSM_HEADS = SSM_D_INNER // SSM_HEAD_DIM
SSM_GROUPS = 8
SSM_STATE = 128
SSM_CONV = 4
SSM_CHUNK = 128
SSM_XBC = SSM_D_INNER + 2 * SSM_GROUPS * SSM_STATE
N_EXPERTS = 32
TOP_K = 4
D_FF = D_MODEL
SWIGLU_LIMIT = 7.0
SWIGLU_ALPHA = 1.702
LAM_INIT = 0.8 - 0.6 * math.exp(-0.3 * 0)

LANES = 128
HALO = 16

COL_XBC = 0
COL_Z = SSM_XBC
COL_Q = COL_Z + SSM_D_INNER
COL_K = COL_Q + ATT_WIDTH
COL_V = COL_K + ATT_WIDTH
COL_GA = COL_V + ATT_WIDTH
COL_GS = COL_GA + D_MODEL
PROJ_COLS = COL_GS + D_MODEL

VMEM_LIMIT = 56 * 1024 * 1024

PROJ_TM = 2048
PROJ_TN = 1024
ATT_TQ = 512
MERGE_TM = 512
DISPATCH_TT = 2048
EXPERT_BM = 512
COMBINE_TC = 256


def _cparams(sem):
    return pltpu.CompilerParams(dimension_semantics=sem, vmem_limit_bytes=VMEM_LIMIT)


PACKED = D_MODEL // 2


def _pack_rows(x):
    lo = lax.bitcast_convert_type(x[:, :PACKED].astype(BF16).astype(F32), jnp.uint32)
    hi = lax.bitcast_convert_type(x[:, PACKED:].astype(BF16).astype(F32), jnp.uint32)
    return hi | (lo >> 16)


def _unpack_rows(p):
    lo = lax.bitcast_convert_type(p << 16, F32)
    hi = lax.bitcast_convert_type(p & jnp.uint32(0xFFFF0000), F32)
    return lo, hi


ROW_SUB = PACKED // LANES


def _store_rows(ref, packed):
    for s in range(ROW_SUB):
        ref[pl.ds(s, packed.shape[0], stride=ROW_SUB), :] = packed[:, s * LANES:(s + 1) * LANES]


def _load_rows(ref):
    n = ref.shape[0] // ROW_SUB
    return jnp.concatenate([ref[pl.ds(s, n, stride=ROW_SUB), :] for s in range(ROW_SUB)], axis=1)


def _row(ref, r):
    return ref.at[pl.ds(pl.multiple_of(r * ROW_SUB, ROW_SUB), ROW_SUB)]


def _in_proj_kernel(x_ref, g_ref, w_ref, wdt_ref, o_ref, dt_ref, h_scr):
    @pl.when(pl.program_id(1) == 0)
    def _():
        x = x_ref[...]
        ms = jnp.mean(x * x, axis=-1, keepdims=True)
        hb = (x * lax.rsqrt(ms + EPS) * g_ref[...]).astype(BF16)
        h_scr[...] = hb
        dt_ref[...] = jnp.dot(hb, wdt_ref[...], preferred_element_type=F32)

    o_ref[...] = jnp.dot(h_scr[...], w_ref[...], preferred_element_type=F32).astype(BF16)


def _in_proj(xf, g, w_main, w_dt):
    T = xf.shape[0]
    tm = min(PROJ_TM, T)
    tn = PROJ_TN
    return pl.pallas_call(
        _in_proj_kernel,
        grid=(T // tm, PROJ_COLS // tn),
        in_specs=[
            pl.BlockSpec((tm, D_MODEL), lambda i, j: (i, 0)),
            pl.BlockSpec((1, D_MODEL), lambda i, j: (0, 0)),
            pl.BlockSpec((D_MODEL, tn), lambda i, j: (0, j)),
            pl.BlockSpec((D_MODEL, LANES), lambda i, j: (0, 0)),
        ],
        out_specs=[
            pl.BlockSpec((tm, tn), lambda i, j: (i, j)),
            pl.BlockSpec((tm, LANES), lambda i, j: (i, 0)),
        ],
        out_shape=[
            jax.ShapeDtypeStruct((T, PROJ_COLS), BF16),
            jax.ShapeDtypeStruct((T, LANES), F32),
        ],
        scratch_shapes=[pltpu.VMEM((tm, D_MODEL), BF16)],
        compiler_params=_cparams(("arbitrary", "arbitrary")),
        name="in_proj",
    )(xf, g, w_main, w_dt)


LOG2E = 1.4426950408889634
SHIFT_LIMIT = 57.0
BOUND_MARGIN = 1.02


def _attn_kernel(q_ref, k_ref, v_ref, qc_ref, qs_ref, kc_ref, ks_ref, qg_ref, kg_ref,
                 lq1_ref, lk1_ref, lq2_ref, lk2_ref, sg_ref, o_ref, k_scr, v_scr, q1_scr, q2_scr, *, tq, seq):
    p = pl.program_id(2)
    nq = seq // tq
    paired = nq > 1
    lane = lax.broadcasted_iota(jnp.int32, (1, LANES), 1)
    first = lane < ATT_HEAD_DIM
    one_col = jnp.where(lane == 0, 1.0, 0.0)

    ri = lax.broadcasted_iota(jnp.int32, (LANES, LANES), 0)
    ci = lax.broadcasted_iota(jnp.int32, (LANES, LANES), 1)
    same_comp = (ri // ATT_HEAD_DIM == ci // ATT_HEAD_DIM).astype(BF16)
    swap_half = ((ri // ATT_HEAD_DIM == ci // ATT_HEAD_DIM)
                 & ((ri - ci == ATT_HEAD_DIM // 2) | (ci - ri == ATT_HEAD_DIM // 2))).astype(BF16)

    def norm_rope(xb, gcos, gsin):
        xf = xb.astype(F32)
        sq = xf * xf
        hi = sq.astype(BF16)
        lo = (sq - hi.astype(F32)).astype(BF16)
        ms = jnp.dot(jnp.concatenate([hi, lo], axis=1), jnp.concatenate([same_comp, same_comp], axis=0),
                     preferred_element_type=F32) * (1.0 / ATT_HEAD_DIM)
        xr = jnp.dot(xb, swap_half, preferred_element_type=F32)
        return lax.rsqrt(ms + EPS) * (xf * gcos + xr * gsin)

    ub = (ATT_HEAD_DIM * ATT_HEAD_DIM ** -0.5 * LOG2E * BOUND_MARGIN
          * jnp.max(jnp.abs(qg_ref[...])) * jnp.max(jnp.abs(kg_ref[...])))
    safe = ub <= SHIFT_LIMIT

    @pl.when(p == 0)
    def _():
        ones = jnp.broadcast_to(one_col, (tq, LANES)).astype(BF16)
        shift = jnp.broadcast_to(jnp.where(lane == 0, -ub, 0.0), (tq, LANES)).astype(BF16)
        for r in range(0, seq, tq):
            rows = slice(r, r + tq)
            k_scr[rows, :LANES] = norm_rope(k_ref[rows, :], kc_ref[rows, :], ks_ref[rows, :]).astype(BF16)
            k_scr[rows, LANES:] = ones
            v_scr[rows, :LANES] = v_ref[rows, :]
            v_scr[rows, LANES:] = ones
            qb = norm_rope(q_ref[rows, :], qc_ref[rows, :], qs_ref[rows, :]).astype(BF16)
            q1_scr[rows, :LANES] = jnp.where(first, qb, jnp.zeros_like(qb))
            q1_scr[rows, LANES:] = shift
            q2_scr[rows, :LANES] = jnp.where(first, jnp.zeros_like(qb), qb)
            q2_scr[rows, LANES:] = shift

    keep = (lax.broadcasted_iota(jnp.int32, (tq, tq), 1) <= lax.broadcasted_iota(jnp.int32, (tq, tq), 0))
    nt = (((1,), (1,)), ((), ()))
    rows_a = pl.multiple_of(p * tq, tq)
    rows_b = pl.multiple_of((nq - 1 - p) * tq, tq)

    def finish(o1, o2, rows0):
        lam = (jnp.exp(jnp.sum(lq1_ref[...] * lk1_ref[...], axis=-1, keepdims=True))
               - jnp.exp(jnp.sum(lq2_ref[...] * lk2_ref[...], axis=-1, keepdims=True)) + LAM_INIT)
        o = o1 - lam * o2
        ms = jnp.mean(o * o, axis=-1, keepdims=True)
        o_ref[pl.ds(rows0, tq), :] = (o * lax.rsqrt(ms + EPS) * sg_ref[...] * (1.0 - LAM_INIT)).astype(BF16)

    @pl.when(safe)
    def _():
        def block(q_scr, qrow0, krow0, rows, cols, mask):
            t = lax.dot_general(q_scr[pl.ds(qrow0, rows), :], k_scr[pl.ds(krow0, cols), :], nt,
                                preferred_element_type=F32)
            if mask is not None:
                t = jnp.where(mask, t, -jnp.inf)
            return jnp.dot(jnp.exp2(t).astype(BF16), v_scr[pl.ds(krow0, cols), :], preferred_element_type=F32)

        def tile(qrow0, krow0, masked):
            if not masked:
                return (block(q1_scr, qrow0, krow0, tq, tq, None), block(q2_scr, qrow0, krow0, tq, tq, None))
            h = tq // 2
            q_low = pl.multiple_of(qrow0 + h, h)
            k_hi = pl.multiple_of(krow0 + h, h)
            outs = []
            for q_scr in (q1_scr, q2_scr):
                left_cols = block(q_scr, qrow0, krow0, tq, h, keep[:, :h])
                corner = block(q_scr, q_low, k_hi, h, h, keep[:h, :h])
                outs.append(jnp.concatenate([left_cols[:h], left_cols[h:] + corner], axis=0))
            return tuple(outs)

        def normalised(acc):
            return acc[:, :LANES] / acc[:, LANES:LANES + 1]

        a1, a2 = tile(rows_a, rows_a, True)
        if paired:
            b1, b2 = tile(rows_b, rows_b, True)
            for s in range(nq - 1):
                to_a = s < p
                qrow0 = jnp.where(to_a, rows_a, rows_b)
                krow0 = pl.multiple_of(jnp.where(to_a, s, s - p) * tq, tq)
                d1, d2 = tile(pl.multiple_of(qrow0, tq), krow0, False)
                a1 = a1 + jnp.where(to_a, d1, 0.0)
                a2 = a2 + jnp.where(to_a, d2, 0.0)
                b1 = b1 + jnp.where(to_a, 0.0, d1)
                b2 = b2 + jnp.where(to_a, 0.0, d2)
            finish(normalised(b1), normalised(b2), rows_b)
        finish(normalised(a1), normalised(a2), rows_a)

    @pl.when(jnp.logical_not(safe))
    def _():
        def update(t, vt, m, l, acc):
            m_new = jnp.maximum(m, jnp.max(t, axis=-1, keepdims=True))
            alpha = jnp.exp2(m - m_new)
            pr = jnp.exp2(t - m_new)
            l_new = alpha * l + jnp.sum(pr, axis=-1, keepdims=True)
            acc_new = alpha * acc + jnp.dot(pr.astype(BF16), vt, preferred_element_type=F32)
            return m_new, l_new, acc_new

        def online(rows0, n_off):
            q1 = q1_scr[pl.ds(rows0, tq), :LANES]
            q2 = q2_scr[pl.ds(rows0, tq), :LANES]

            def step(krow0, carry, masked):
                m1, l1, c1, m2, l2, c2 = carry
                kt = k_scr[pl.ds(krow0, tq), :LANES]
                vt = v_scr[pl.ds(krow0, tq), :LANES]
                t1 = lax.dot_general(q1, kt, nt, preferred_element_type=F32)
                t2 = lax.dot_general(q2, kt, nt, preferred_element_type=F32)
                if masked:
                    t1 = jnp.where(keep, t1, -jnp.inf)
                    t2 = jnp.where(keep, t2, -jnp.inf)
                m1, l1, c1 = update(t1, vt, m1, l1, c1)
                m2, l2, c2 = update(t2, vt, m2, l2, c2)
                return m1, l1, c1, m2, l2, c2

            neg = jnp.full((tq, 1), -jnp.inf, F32)
            zero1 = jnp.zeros((tq, 1), F32)
            zacc = jnp.zeros((tq, ATT_V_DIM), F32)
            carry = step(rows0, (neg, zero1, zacc, neg, zero1, zacc), True)
            m1, l1, c1, m2, l2, c2 = lax.fori_loop(
                0, n_off, lambda j, c: step(pl.multiple_of(j * tq, tq), c, False), carry)
            finish(c1 / l1, c2 / l2, rows0)

        online(rows_a, p)
        if paired:
            online(rows_b, nq - 1 - p)


def _attention(proj, qcos, qsin, kcos, ksin, qg2, kg2, lq1, lk1, lq2, lk2, sg, batch, seq):
    T = batch * seq
    tq = min(ATT_TQ, seq)
    nq = seq // tq
    assert nq == 1 or nq % 2 == 0, "query tiles are processed in balanced pairs"
    vec = lambda n: pl.BlockSpec((1, n), lambda b, h, i: (0, 0))
    table = pl.BlockSpec((seq, LANES), lambda b, h, i: (0, 0))
    head = lambda col: pl.BlockSpec((seq, LANES), lambda b, h, i: (b, col // LANES + h))
    return pl.pallas_call(
        functools.partial(_attn_kernel, tq=tq, seq=seq),
        grid=(batch, ATT_HEADS, max(nq // 2, 1)),
        in_specs=[
            head(COL_Q), head(COL_K), head(COL_V),
            table, table, table, table,
            vec(LANES), vec(LANES),
            vec(ATT_HEAD_DIM), vec(ATT_HEAD_DIM), vec(ATT_HEAD_DIM), vec(ATT_HEAD_DIM),
            vec(LANES),
        ],
        out_specs=pl.BlockSpec((seq, LANES), lambda b, h, i: (b, h)),
        out_shape=jax.ShapeDtypeStruct((T, ATT_WIDTH), BF16),
        scratch_shapes=[pltpu.VMEM((seq, 2 * LANES), BF16) for _ in range(4)],
        compiler_params=_cparams(("arbitrary", "arbitrary", "arbitrary")),
        name="attention",
    )(proj, proj, proj, qcos, qsin, kcos, ksin, qg2, kg2, lq1, lk1, lq2, lk2, sg)


SSD_SUB = 2


def _ssd_kernel(xbc_ref, z_ref, dt_ref, cw_ref, cb_ref, dtb_ref, alog_ref, dsk_ref, ng_ref,
                o_ref, halo, state, xs_scr, bc_scr, y_scr):
    Q = SSM_CHUNK
    N = SSM_STATE
    P2 = 2 * SSM_HEAD_DIM

    @pl.when(pl.program_id(1) == 0)
    def _():
        halo[...] = jnp.zeros_like(halo)
        state[...] = jnp.zeros_like(state)

    taps = SSM_CONV - 1
    sr = lax.broadcasted_iota(jnp.int32, (taps * Q, HALO + Q), 0)
    sc = lax.broadcasted_iota(jnp.int32, (taps * Q, HALO + Q), 1)
    shift = (sc == (sr % Q) + HALO - taps + sr // Q).astype(BF16)
    ri = lax.broadcasted_iota(jnp.int32, (Q, Q), 0)
    ci = lax.broadcasted_iota(jnp.int32, (Q, Q), 1)
    tri = ri >= ci
    lane = lax.broadcasted_iota(jnp.int32, (1, P2), 1)
    left = lane < SSM_HEAD_DIM
    gw = SSM_D_INNER // SSM_GROUPS

    for sub in range(SSD_SUB):
        r0 = sub * Q
        xs_sub, bc_sub, y_sub = xs_scr.at[sub], bc_scr.at[sub], y_scr.at[sub]

        cwid = 512
        for c0 in range(0, SSM_XBC, cwid):
            cur = xbc_ref[r0:r0 + Q, c0:c0 + cwid]
            hist = halo[:, c0:c0 + cwid] if sub == 0 else xbc_ref[r0 - HALO:r0, c0:c0 + cwid]
            ext = jnp.concatenate([hist, cur], axis=0)
            shifted = jnp.dot(shift, ext, preferred_element_type=F32)
            acc = cb_ref[:, c0:c0 + cwid] + cw_ref[taps:SSM_CONV, c0:c0 + cwid] * cur.astype(F32)
            for k in range(taps):
                acc = acc + cw_ref[k:k + 1, c0:c0 + cwid] * shifted[k * Q:(k + 1) * Q, :]
            act = acc * jax.nn.sigmoid(acc)
            if c0 < SSM_D_INNER:
                xs_sub[:, c0:c0 + cwid] = act
            else:
                bc_sub[:, c0 - SSM_D_INNER:c0 - SSM_D_INNER + cwid] = act.astype(BF16)

        dt = jax.nn.softplus(dt_ref[r0:r0 + Q, :] + dtb_ref[...])
        da = dt * (-jnp.exp(alog_ref[...]) * LOG2E)
        acs = jnp.dot(tri.astype(F32), da, preferred_element_type=F32,
                      precision=lax.Precision.HIGHEST)
        acs_t = acs.T
        dt_t = dt.T
        w_t = dt_t * jnp.exp2(acs_t[:, Q - 1:Q] - acs_t)

        for g in range(SSM_GROUPS):
            bm = bc_sub[:, g * N:(g + 1) * N]
            cm = bc_sub[:, SSM_GROUPS * N + g * N:SSM_GROUPS * N + (g + 1) * N]
            cb = lax.dot_general(cm, bm, (((1,), (1,)), ((), ())), preferred_element_type=F32)
            bm_t = bm.astype(F32).T
            cm_f = cm.astype(F32)
            for pr in range(2):
                pair = 2 * g + pr
                xs_pair = xs_sub[:, pair * P2:(pair + 1) * P2].astype(BF16)
                prev = state[pair]
                rhs = jnp.concatenate([xs_pair, prev.astype(BF16)], axis=0)
                ys, sts, decs = [], [], []
                for r in range(2):
                    h = 2 * pair + r
                    a_col = acs[:, h:h + 1]
                    seg = a_col - acs_t[h:h + 1, :]
                    decay = jnp.exp2(jnp.where(tri, seg, -jnp.inf))
                    m_h = cb * decay * dt_t[h:h + 1, :]
                    e_h = cm_f * jnp.exp2(a_col)
                    lhs = jnp.concatenate([m_h, e_h], axis=1).astype(BF16)
                    ys.append(jnp.dot(lhs, rhs, preferred_element_type=F32))
                    sts.append(jnp.dot((bm_t * w_t[h:h + 1, :]).astype(BF16), xs_pair,
                                       preferred_element_type=F32))
                    decs.append(jnp.exp2(acs[Q - 1:Q, h:h + 1]))
                y_sub[:, pair * P2:(pair + 1) * P2] = jnp.where(left, ys[0], ys[1])
                dec = jnp.where(left, decs[0], decs[1])
                state[pair] = dec * prev + jnp.where(left, sts[0], sts[1])

        y = y_sub[...] + dsk_ref[...] * xs_sub[...]
        zf = z_ref[r0:r0 + Q, :].astype(F32)
        y = y * (zf * jax.nn.sigmoid(zf))
        for g in range(SSM_GROUPS):
            yg = y[:, g * gw:(g + 1) * gw]
            ms = jnp.mean(yg * yg, axis=-1, keepdims=True)
            o_ref[r0:r0 + Q, g * gw:(g + 1) * gw] = (yg * lax.rsqrt(ms + EPS)
                                                     * ng_ref[:, g * gw:(g + 1) * gw]).astype(BF16)

    halo[...] = xbc_ref[SSD_SUB * Q - HALO:SSD_SUB * Q, :]


def _ssd(proj, dt_raw, conv_w, conv_b, dt_bias, a_log, dsk, ng, batch, seq):
    T = batch * seq
    Q = SSM_CHUNK
    rows = SSD_SUB * Q
    nc = seq // rows
    vec = lambda r, n: pl.BlockSpec((r, n), lambda b, c: (0, 0))
    return pl.pallas_call(
        _ssd_kernel,
        grid=(batch, nc),
        in_specs=[
            pl.BlockSpec((rows, SSM_XBC), lambda b, c: (b * nc + c, COL_XBC // SSM_XBC)),
            pl.BlockSpec((rows, SSM_D_INNER), lambda b, c: (b * nc + c, COL_Z // SSM_D_INNER)),
            pl.BlockSpec((rows, LANES), lambda b, c: (b * nc + c, 0)),
            vec(SSM_CONV, SSM_XBC), vec(1, SSM_XBC), vec(1, LANES), vec(1, LANES),
            vec(1, SSM_D_INNER), vec(1, SSM_D_INNER),
        ],
        out_specs=pl.BlockSpec((rows, SSM_D_INNER), lambda b, c: (b * nc + c, 0)),
        out_shape=jax.ShapeDtypeStruct((T, SSM_D_INNER), BF16),
        scratch_shapes=[
            pltpu.VMEM((HALO, SSM_XBC), BF16),
            pltpu.VMEM((SSM_HEADS // 2, SSM_STATE, 2 * SSM_HEAD_DIM), F32),
            pltpu.VMEM((SSD_SUB, Q, SSM_D_INNER), F32),
            pltpu.VMEM((SSD_SUB, Q, 2 * SSM_GROUPS * SSM_STATE), BF16),
            pltpu.VMEM((SSD_SUB, Q, SSM_D_INNER), F32),
        ],
        compiler_params=_cparams(("arbitrary", "arbitrary")),
        name="ssd",
    )(proj, proj, dt_raw, conv_w, conv_b, dt_bias, a_log, dsk, ng)


def _merge_kernel(att_ref, ssm_ref, ga_ref, gs_ref, x_ref, wap_ref, wsp_ref, wo_ref, fg_ref,
                  wr_ref, br_ref, x1_ref, h2_ref, idx_ref, gate_ref, cnt_ref, base, *, tm):
    i = pl.program_id(0)

    @pl.when(i == 0)
    def _():
        base[...] = jnp.zeros_like(base)

    pa = jnp.dot(att_ref[...], wap_ref[...], preferred_element_type=F32)
    ps = jnp.dot(ssm_ref[...], wsp_ref[...], preferred_element_type=F32)
    merged = (jax.nn.sigmoid(ga_ref[...].astype(F32)) * pa
              + jax.nn.sigmoid(gs_ref[...].astype(F32)) * ps)
    x1 = x_ref[...] + jnp.dot(merged.astype(BF16), wo_ref[...], preferred_element_type=F32)
    x1_ref[...] = x1
    ms = jnp.mean(x1 * x1, axis=-1, keepdims=True)
    h2 = x1 * lax.rsqrt(ms + EPS) * fg_ref[...]
    _store_rows(h2_ref, _pack_rows(h2))

    logits = jnp.dot(h2.astype(BF16), wr_ref[...], preferred_element_type=F32) + br_ref[...]
    lg = logits.T[:N_EXPERTS, :]
    row_f = lax.broadcasted_iota(jnp.int32, (N_EXPERTS, tm), 0).astype(F32)
    vals, idxs, sels = [], [], []
    for _ in range(TOP_K):
        m = jnp.max(lg, axis=0, keepdims=True)
        idx = jnp.min(jnp.where(lg == m, row_f, float(N_EXPERTS)), axis=0, keepdims=True)
        sel = row_f == idx
        vals.append(m)
        idxs.append(idx)
        sels.append(sel)
        lg = jnp.where(sel, -jnp.inf, lg)
    es = [jnp.exp(v - vals[0]) for v in vals]
    den = es[0] + es[1] + es[2] + es[3]

    multi = jnp.zeros((N_EXPERTS, tm), F32)
    for sel in sels:
        multi = jnp.where(sel, 1.0, multi)
    ri = lax.broadcasted_iota(jnp.int32, (tm, tm), 0)
    ci = lax.broadcasted_iota(jnp.int32, (tm, tm), 1)
    before = jnp.dot(multi.astype(BF16), (ri < ci).astype(BF16), preferred_element_type=F32)
    before = before + base[...]
    ranks = [jnp.sum(jnp.where(sel, before, 0.0), axis=0, keepdims=True) for sel in sels]
    idx_ref[...] = jnp.concatenate(idxs + ranks, axis=0).astype(jnp.int32)
    gate_ref[...] = jnp.concatenate([e / den for e in es] + [jnp.zeros_like(den)] * TOP_K, axis=0)
    base[...] = base[...] + jnp.sum(multi, axis=1, keepdims=True)
    cnt_ref[...] = jnp.broadcast_to(base[...], cnt_ref.shape).astype(jnp.int32)


def _merge(att, ssm, proj, xf, wap, wsp, wo, fg, wr, br):
    T = xf.shape[0]
    tm = min(MERGE_TM, T)
    full = lambda a: pl.BlockSpec(a.shape, lambda i: (0, 0))
    return pl.pallas_call(
        functools.partial(_merge_kernel, tm=tm),
        grid=(T // tm,),
        in_specs=[
            pl.BlockSpec((tm, ATT_WIDTH), lambda i: (i, 0)),
            pl.BlockSpec((tm, SSM_D_INNER), lambda i: (i, 0)),
            pl.BlockSpec((tm, D_MODEL), lambda i: (i, COL_GA // D_MODEL)),
            pl.BlockSpec((tm, D_MODEL), lambda i: (i, COL_GS // D_MODEL)),
            pl.BlockSpec((tm, D_MODEL), lambda i: (i, 0)),
            full(wap), full(wsp), full(wo), full(fg), full(wr), full(br),
        ],
        out_specs=[
            pl.BlockSpec((tm, D_MODEL), lambda i: (i, 0)),
            pl.BlockSpec((tm * ROW_SUB, LANES), lambda i: (i, 0)),
            pl.BlockSpec((2 * TOP_K, tm), lambda i: (0, i)),
            pl.BlockSpec((2 * TOP_K, tm), lambda i: (0, i)),
            pl.BlockSpec((N_EXPERTS, LANES), lambda i: (0, 0)),
        ],
        out_shape=[
            jax.ShapeDtypeStruct((T, D_MODEL), F32),
            jax.ShapeDtypeStruct((T * ROW_SUB, LANES), jnp.uint32),
            jax.ShapeDtypeStruct((2 * TOP_K, T), jnp.int32),
            jax.ShapeDtypeStruct((2 * TOP_K, T), F32),
            jax.ShapeDtypeStruct((N_EXPERTS, LANES), jnp.int32),
        ],
        scratch_shapes=[pltpu.VMEM((N_EXPERTS, 1), F32)],
        compiler_params=_cparams(("arbitrary",)),
        name="merge_router",
    )(att, ssm, proj, proj, xf, wap, wsp, wo, fg, wr, br)


DMA_UNROLL = 8


def _dispatch_kernel(lb_ref, cnt_ref, dest_ref, h_ref, rows_ref, zeros, zsem, sem, *, tt, bm):
    @pl.when(pl.program_id(0) == 0)
    def _():
        zeros[...] = jnp.zeros_like(zeros)
        for wait in (False, True):
            for e in range(N_EXPERTS):
                @pl.when(cnt_ref[e] > 0)
                def _():
                    start = pl.multiple_of(lb_ref[e], bm)
                    cp = pltpu.make_async_copy(zeros, rows_ref.at[pl.ds(start * ROW_SUB, bm * ROW_SUB)], zsem)
                    cp.wait() if wait else cp.start()

    def row_copy(t, k):
        return pltpu.make_async_copy(_row(h_ref, t), _row(rows_ref, dest_ref[t * TOP_K + k]), sem)

    for wait in (False, True):
        def body(g, carry):
            for u in range(DMA_UNROLL):
                for k in range(TOP_K):
                    cp = row_copy(g * DMA_UNROLL + u, k)
                    cp.wait() if wait else cp.start(priority=k % 2)
            return carry
        lax.fori_loop(0, tt // DMA_UNROLL, body, 0)


def _dispatch(last_block, cnt, dest, h2p, n_rows, bm):
    T = h2p.shape[0] // ROW_SUB
    tt = min(DISPATCH_TT, T)
    grid_spec = pltpu.PrefetchScalarGridSpec(
        num_scalar_prefetch=2,
        grid=(T // tt,),
        in_specs=[
            pl.BlockSpec((tt * TOP_K,), lambda i, lb, c: (i,), memory_space=pltpu.SMEM),
            pl.BlockSpec((tt * ROW_SUB, LANES), lambda i, lb, c: (i, 0)),
        ],
        out_specs=pl.BlockSpec(memory_space=pl.ANY),
        scratch_shapes=[pltpu.VMEM((bm * ROW_SUB, LANES), jnp.uint32),
                        pltpu.SemaphoreType.DMA, pltpu.SemaphoreType.DMA],
    )
    return pl.pallas_call(
        functools.partial(_dispatch_kernel, tt=tt, bm=bm),
        grid_spec=grid_spec,
        out_shape=jax.ShapeDtypeStruct((n_rows * ROW_SUB, LANES), jnp.uint32),
        compiler_params=_cparams(("arbitrary",)),
        name="dispatch",
    )(last_block, cnt, dest, h2p)


def _expert_kernel(be_ref, nu_ref, fill_ref, x_ref, wgu_ref, bgu_ref, wd_ref, bd_ref, o_ref, wgu_bf, wd_bf):
    i = pl.program_id(0)
    bm = x_ref.shape[0] // ROW_SUB

    def ffn(rows):
        sub = pl.ds(0, rows * ROW_SUB)
        lo, hi = _unpack_rows(_load_rows(x_ref.at[sub]))
        x = jnp.concatenate([lo, hi], axis=1).astype(BF16)
        gu = jnp.dot(x, wgu_bf[...], preferred_element_type=F32) + bgu_ref[0]
        gate = jnp.minimum(gu[:, :D_FF], SWIGLU_LIMIT)
        up = jnp.clip(gu[:, D_FF:], -SWIGLU_LIMIT, SWIGLU_LIMIT)
        glu = gate * jax.nn.sigmoid(SWIGLU_ALPHA * gate)
        act = ((up + 1.0) * glu).astype(BF16)
        _store_rows(o_ref.at[sub], _pack_rows(jnp.dot(act, wd_bf[...], preferred_element_type=F32) + bd_ref[0]))

    @pl.when(i < nu_ref[0])
    def _():
        @pl.when((i == 0) | (be_ref[i] != be_ref[jnp.maximum(i - 1, 0)]))
        def _():
            wgu_bf[...] = wgu_ref[0].astype(BF16)
            wd_bf[...] = wd_ref[0].astype(BF16)

        @pl.when(fill_ref[i] > bm // 2)
        def _():
            ffn(bm)

        @pl.when(fill_ref[i] <= bm // 2)
        def _():
            ffn(bm // 2)
            o_ref[pl.ds(bm // 2 * ROW_SUB, bm // 2 * ROW_SUB), :] = jnp.zeros((bm // 2 * ROW_SUB, LANES), jnp.uint32)


def _experts(block_expert, n_used, fill, rows, wgu, bgu, wd, bd, bm):
    n_rows = rows.shape[0] // ROW_SUB
    nb = n_rows // bm
    row_map = lambda i, be, nu, fill: (jnp.minimum(i, nu[0] - 1), 0)
    exp_map = lambda i, be, nu, fill: (be[jnp.minimum(i, nu[0] - 1)], 0, 0)
    grid_spec = pltpu.PrefetchScalarGridSpec(
        num_scalar_prefetch=3,
        grid=(nb,),
        in_specs=[
            pl.BlockSpec((bm * ROW_SUB, LANES), row_map),
            pl.BlockSpec((1, D_MODEL, 2 * D_FF), exp_map),
            pl.BlockSpec((1, 1, 2 * D_FF), exp_map),
            pl.BlockSpec((1, D_FF, D_MODEL), exp_map),
            pl.BlockSpec((1, 1, D_MODEL), exp_map),
        ],
        out_specs=pl.BlockSpec((bm * ROW_SUB, LANES), row_map),
        scratch_shapes=[pltpu.VMEM((D_MODEL, 2 * D_FF), BF16), pltpu.VMEM((D_FF, D_MODEL), BF16)],
    )
    return pl.pallas_call(
        _expert_kernel,
        grid_spec=grid_spec,
        out_shape=jax.ShapeDtypeStruct((n_rows * ROW_SUB, LANES), jnp.uint32),
        compiler_params=_cparams(("arbitrary",)),
        name="experts",
    )(block_expert, n_used, fill, rows, wgu, bgu, wd, bd)


def _combine_kernel(dest_ref, next_ref, x1_ref, gate_ref, y_ref, o_ref, ybuf, sems, *, tc):
    i = pl.program_id(0)
    n = pl.num_programs(0)

    def gather(idx_ref, slot, wait):
        def row_copy(t, k):
            return pltpu.make_async_copy(_row(y_ref, idx_ref[t * TOP_K + k]), _row(ybuf.at[slot, k], t),
                                         sems.at[slot])

        def body(g, carry):
            for u in range(DMA_UNROLL):
                for k in range(TOP_K):
                    cp = row_copy(g * DMA_UNROLL + u, k)
                    cp.wait() if wait else cp.start(priority=k % 2)
            return carry
        lax.fori_loop(0, tc // DMA_UNROLL, body, 0)

    slot = i % 2

    @pl.when(i == 0)
    def _():
        gather(dest_ref, 0, False)

    @pl.when(i + 1 < n)
    def _():
        gather(next_ref, 1 - slot, False)

    gather(dest_ref, slot, True)

    x1 = x1_ref[...]
    acc_lo, acc_hi = x1[:, :PACKED], x1[:, PACKED:]
    g = gate_ref[...].T
    for k in range(TOP_K):
        lo, hi = _unpack_rows(_load_rows(ybuf.at[slot, k]))
        acc_lo = acc_lo + g[:, k:k + 1] * lo
        acc_hi = acc_hi + g[:, k:k + 1] * hi
    o_ref[:, :PACKED] = acc_lo
    o_ref[:, PACKED:] = acc_hi


def _combine(dest, x1, gates, y_rows):
    T = x1.shape[0]
    tc = min(COMBINE_TC, T)
    n = T // tc
    return pl.pallas_call(
        functools.partial(_combine_kernel, tc=tc),
        grid=(n,),
        in_specs=[
            pl.BlockSpec((tc * TOP_K,), lambda i: (i,), memory_space=pltpu.SMEM),
            pl.BlockSpec((tc * TOP_K,), lambda i: (jnp.minimum(i + 1, n - 1),), memory_space=pltpu.SMEM),
            pl.BlockSpec((tc, D_MODEL), lambda i: (i, 0)),
            pl.BlockSpec((2 * TOP_K, tc), lambda i: (0, i)),
            pl.BlockSpec(memory_space=pl.ANY),
        ],
        out_specs=pl.BlockSpec((tc, D_MODEL), lambda i: (i, 0)),
        out_shape=jax.ShapeDtypeStruct((T, D_MODEL), F32),
        scratch_shapes=[pltpu.VMEM((2, TOP_K, tc * ROW_SUB, LANES), jnp.uint32),
                        pltpu.SemaphoreType.DMA((2,))],
        compiler_params=_cparams(("arbitrary",)),
        name="combine",
    )(dest, dest, x1, gates, y_rows)


def _pad_lanes(v):
    return jnp.pad(v, ((0, 0), (0, LANES - v.shape[-1])))


def kernel(x, mix_norm_g, w_in, q_norm_g, k_norm_g, lambda_q1, lambda_k1, lambda_q2, lambda_k2,
           attn_subln_g, conv_w, conv_b, dt_bias, a_log, d_skip, ssm_norm_g, w_attn_proj,
           w_ssm_proj, w_out, ffn_norm_g, w_router, b_router, w_gate_up, b_gate_up, w_down, b_down):
    B, S, D = x.shape
    T = B * S
    xf = x.reshape(T, D)
    layer = 0

    wi = w_in[layer]
    o_z = 3 * ATT_WIDTH
    o_xbc = o_z + SSM_D_INNER
    o_dt = o_xbc + SSM_XBC
    o_ga = o_dt + SSM_HEADS
    w_main = jnp.concatenate([wi[:, o_xbc:o_dt], wi[:, o_z:o_xbc], wi[:, :o_z], wi[:, o_ga:]],
                             axis=1).astype(BF16)
    w_dt = _pad_lanes(wi[:, o_dt:o_ga]).astype(BF16)
    proj, dt_raw = _in_proj(xf, mix_norm_g[layer][None, :], w_main, w_dt)

    half = ATT_HEAD_DIM // 2
    inv = ROPE_THETA ** (-jnp.arange(0, ATT_HEAD_DIM, 2, dtype=F32) / ATT_HEAD_DIM)
    ang = jnp.arange(S, dtype=F32)[:, None] * inv[None, :]
    cos2 = jnp.tile(jnp.cos(ang), (1, LANES // half))
    sin2 = jnp.tile(jnp.concatenate([-jnp.sin(ang), jnp.sin(ang)], axis=1), (1, LANES // ATT_HEAD_DIM))
    qg2 = jnp.tile(q_norm_g[layer], 2)[None, :]
    kg2 = jnp.tile(k_norm_g[layer], 2)[None, :]
    partner = lambda g: jnp.tile(jnp.roll(g, half), 2)[None, :]
    q_scale = ATT_HEAD_DIM ** -0.5 * LOG2E
    att = _attention(proj, cos2 * qg2 * q_scale, sin2 * partner(q_norm_g[layer]) * q_scale,
                     cos2 * kg2, sin2 * partner(k_norm_g[layer]), qg2, kg2,
                     lambda_q1[layer][None, :], lambda_k1[layer][None, :],
                     lambda_q2[layer][None, :], lambda_k2[layer][None, :],
                     attn_subln_g[layer][None, :], B, S)

    ssm = _ssd(proj, dt_raw, conv_w[layer], conv_b[layer][None, :],
               _pad_lanes(dt_bias[layer][None, :]), _pad_lanes(a_log[layer][None, :]),
               jnp.repeat(d_skip[layer], SSM_HEAD_DIM)[None, :], ssm_norm_g[layer][None, :], B, S)

    x1, h2p, idx_rank_t, gates_t, counts = _merge(
        att, ssm, proj, xf, w_attn_proj[layer].astype(BF16), w_ssm_proj[layer].astype(BF16),
        w_out[layer].astype(BF16), ffn_norm_g[layer][None, :],
        _pad_lanes(w_router[layer]).astype(BF16), _pad_lanes(b_router[layer][None, :]))

    bm = EXPERT_BM
    A = T * TOP_K
    n_rows = (A + N_EXPERTS * (bm - 1)) // bm * bm
    cnt = counts[:, 0]
    padded = (cnt + bm - 1) // bm * bm
    pend = jnp.cumsum(padded)
    pstart = pend - padded
    n_used = (pend[-1:] // bm).astype(jnp.int32)
    block_start = jnp.arange(n_rows // bm, dtype=jnp.int32) * bm
    block_expert = jnp.minimum(jnp.sum(block_start[:, None] >= pend[None, :], axis=1),
                               N_EXPERTS - 1).astype(jnp.int32)
    idx_rank = idx_rank_t.T
    onehot = idx_rank[:, :TOP_K, None] == jnp.arange(N_EXPERTS, dtype=jnp.int32)[None, None, :]
    dest = (jnp.sum(jnp.where(onehot, pstart[None, None, :], 0), axis=-1)
            + idx_rank[:, TOP_K:]).reshape(A).astype(jnp.int32)

    rows = _dispatch((pend - bm).astype(jnp.int32), cnt.astype(jnp.int32), dest, h2p, n_rows, bm)
    fill = jnp.clip((pstart + cnt)[block_expert] - block_start, 0, bm).astype(jnp.int32)
    y_rows = _experts(block_expert, n_used, fill, rows, w_gate_up[layer], b_gate_up[layer][:, None, :],
                      w_down[layer], b_down[layer][:, None, :], bm)
    out = _combine(dest, x1, gates_t, y_rows)
    return out.reshape(B, S, D)
```

```python
import functools
import math

import jax
import jax.numpy as jnp
from jax import lax
from jax.experimental import pallas as pl
from jax.experimental.pallas import tpu as pltpu

F32 = jnp.float32
BF16 = jnp.bfloat16

D_MODEL = 1024
EPS = 1e-6
ATT_HEADS = 8
ATT_HEAD_DIM = 64
ATT_V_DIM = 2 * ATT_HEAD_DIM
ATT_WIDTH = ATT_HEADS * ATT_V_DIM
ROPE_THETA = 10000.0
SSM_D_INNER = 2 * D_MODEL
SSM_HEAD_DIM = 64
SSM_HEADS = SSM_D_INNER // SSM_HEAD_DIM
SSM_GROUPS = 8
SSM_STATE = 128
SSM_CONV = 4
SSM_CHUNK = 128
SSM_XBC = SSM_D_INNER + 2 * SSM_GROUPS * SSM_STATE
N_EXPERTS = 32
TOP_K = 4
D_FF = D_MODEL
SWIGLU_LIMIT = 7.0
SWIGLU_ALPHA = 1.702
LAM_INIT = 0.8 - 0.6 * math.exp(-0.3 * 0)

LANES = 128
HALO = 16

COL_XBC = 0
COL_Z = SSM_XBC
COL_Q = COL_Z + SSM_D_INNER
COL_K = COL_Q + ATT_WIDTH
COL_V = COL_K + ATT_WIDTH
COL_GA = COL_V + ATT_WIDTH
COL_GS = COL_GA + D_MODEL
PROJ_COLS = COL_GS + D_MODEL

VMEM_LIMIT = 56 * 1024 * 1024

PROJ_TM = 2048
PROJ_TN = 1024
ATT_TQ = 512
MERGE_TM = 512
DISPATCH_TT = 2048
EXPERT_BM = 512
COMBINE_TC = 256


def _cparams(sem):
    return pltpu.CompilerParams(dimension_semantics=sem, vmem_limit_bytes=VMEM_LIMIT)


PACKED = D_MODEL // 2


def _pack_rows(x):
    lo = lax.bitcast_convert_type(x[:, :PACKED].astype(BF16).astype(F32), jnp.uint32)
    hi = lax.bitcast_convert_type(x[:, PACKED:].astype(BF16).astype(F32), jnp.uint32)
    return hi | (lo >> 16)


def _unpack_rows(p):
    lo = lax.bitcast_convert_type(p << 16, F32)
    hi = lax.bitcast_convert_type(p & jnp.uint32(0xFFFF0000), F32)
    return lo, hi


ROW_SUB = PACKED // LANES


def _store_rows(ref, packed):
    for s in range(ROW_SUB):
        ref[pl.ds(s, packed.shape[0], stride=ROW_SUB), :] = packed[:, s * LANES:(s + 1) * LANES]


def _load_rows(ref):
    n = ref.shape[0] // ROW_SUB
    return jnp.concatenate([ref[pl.ds(s, n, stride=ROW_SUB), :] for s in range(ROW_SUB)], axis=1)


def _row(ref, r):
    return ref.at[pl.ds(pl.multiple_of(r * ROW_SUB, ROW_SUB), ROW_SUB)]


def _in_proj_kernel(x_ref, g_ref, w_ref, wdt_ref, o_ref, dt_ref, h_scr):
    @pl.when(pl.program_id(1) == 0)
    def _():
        x = x_ref[...]
        ms = jnp.mean(x * x, axis=-1, keepdims=True)
        hb = (x * lax.rsqrt(ms + EPS) * g_ref[...]).astype(BF16)
        h_scr[...] = hb
        dt_ref[...] = jnp.dot(hb, wdt_ref[...], preferred_element_type=F32)

    o_ref[...] = jnp.dot(h_scr[...], w_ref[...], preferred_element_type=F32).astype(BF16)


def _in_proj(xf, g, w_main, w_dt):
    T = xf.shape[0]
    tm = min(PROJ_TM, T)
    tn = PROJ_TN
    return pl.pallas_call(
        _in_proj_kernel,
        grid=(T // tm, PROJ_COLS // tn),
        in_specs=[
            pl.BlockSpec((tm, D_MODEL), lambda i, j: (i, 0)),
            pl.BlockSpec((1, D_MODEL), lambda i, j: (0, 0)),
            pl.BlockSpec((D_MODEL, tn), lambda i, j: (0, j)),
            pl.BlockSpec((D_MODEL, LANES), lambda i, j: (0, 0)),
        ],
        out_specs=[
            pl.BlockSpec((tm, tn), lambda i, j: (i, j)),
            pl.BlockSpec((tm, LANES), lambda i, j: (i, 0)),
        ],
        out_shape=[
            jax.ShapeDtypeStruct((T, PROJ_COLS), BF16),
            jax.ShapeDtypeStruct((T, LANES), F32),
        ],
        scratch_shapes=[pltpu.VMEM((tm, D_MODEL), BF16)],
        compiler_params=_cparams(("arbitrary", "arbitrary")),
        name="in_proj",
    )(xf, g, w_main, w_dt)


LOG2E = 1.4426950408889634
SHIFT_LIMIT = 57.0
BOUND_MARGIN = 1.02


def _attn_kernel(q_ref, k_ref, v_ref, qc_ref, qs_ref, kc_ref, ks_ref, qg_ref, kg_ref,
                 lq1_ref, lk1_ref, lq2_ref, lk2_ref, sg_ref, o_ref, k_scr, v_scr, q1_scr, q2_scr, *, tq, seq):
    p = pl.program_id(2)
    nq = seq // tq
    paired = nq > 1
    lane = lax.broadcasted_iota(jnp.int32, (1, LANES), 1)
    first = lane < ATT_HEAD_DIM
    one_col = jnp.where(lane == 0, 1.0, 0.0)

    ri = lax.broadcasted_iota(jnp.int32, (LANES, LANES), 0)
    ci = lax.broadcasted_iota(jnp.int32, (LANES, LANES), 1)
    same_comp = (ri // ATT_HEAD_DIM == ci // ATT_HEAD_DIM).astype(BF16)
    swap_half = ((ri // ATT_HEAD_DIM == ci // ATT_HEAD_DIM)
                 & ((ri - ci == ATT_HEAD_DIM // 2) | (ci - ri == ATT_HEAD_DIM // 2))).astype(BF16)

    def norm_rope(xb, gcos, gsin):
        xf = xb.astype(F32)
        sq = xf * xf
        hi = sq.astype(BF16)
        lo = (sq - hi.astype(F32)).astype(BF16)
        ms = jnp.dot(jnp.concatenate([hi, lo], axis=1), jnp.concatenate([same_comp, same_comp], axis=0),
                     preferred_element_type=F32) * (1.0 / ATT_HEAD_DIM)
        xr = jnp.dot(xb, swap_half, preferred_element_type=F32)
        return lax.rsqrt(ms + EPS) * (xf * gcos + xr * gsin)

    ub = (ATT_HEAD_DIM * ATT_HEAD_DIM ** -0.5 * LOG2E * BOUND_MARGIN
          * jnp.max(jnp.abs(qg_ref[...])) * jnp.max(jnp.abs(kg_ref[...])))
    safe = ub <= SHIFT_LIMIT

    @pl.when(p == 0)
    def _():
        ones = jnp.broadcast_to(one_col, (tq, LANES)).astype(BF16)
        shift = jnp.broadcast_to(jnp.where(lane == 0, -ub, 0.0), (tq, LANES)).astype(BF16)
        for r in range(0, seq, tq):
            rows = slice(r, r + tq)
            k_scr[rows, :LANES] = norm_rope(k_ref[rows, :], kc_ref[rows, :], ks_ref[rows, :]).astype(BF16)
            k_scr[rows, LANES:] = ones
            v_scr[rows, :LANES] = v_ref[rows, :]
            v_scr[rows, LANES:] = ones
            qb = norm_rope(q_ref[rows, :], qc_ref[rows, :], qs_ref[rows, :]).astype(BF16)
            q1_scr[rows, :LANES] = jnp.where(first, qb, jnp.zeros_like(qb))
            q1_scr[rows, LANES:] = shift
            q2_scr[rows, :LANES] = jnp.where(first, jnp.zeros_like(qb), qb)
            q2_scr[rows, LANES:] = shift

    keep = (lax.broadcasted_iota(jnp.int32, (tq, tq), 1) <= lax.broadcasted_iota(jnp.int32, (tq, tq), 0))
    nt = (((1,), (1,)), ((), ()))
    rows_a = pl.multiple_of(p * tq, tq)
    rows_b = pl.multiple_of((nq - 1 - p) * tq, tq)

    def finish(o1, o2, rows0):
        lam = (jnp.exp(jnp.sum(lq1_ref[...] * lk1_ref[...], axis=-1, keepdims=True))
               - jnp.exp(jnp.sum(lq2_ref[...] * lk2_ref[...], axis=-1, keepdims=True)) + LAM_INIT)
        o = o1 - lam * o2
        ms = jnp.mean(o * o, axis=-1, keepdims=True)
        o_ref[pl.ds(rows0, tq), :] = (o * lax.rsqrt(ms + EPS) * sg_ref[...] * (1.0 - LAM_INIT)).astype(BF16)

    @pl.when(safe)
    def _():
        def block(q_scr, qrow0, krow0, rows, cols, mask):
            t = lax.dot_general(q_scr[pl.ds(qrow0, rows), :], k_scr[pl.ds(krow0, cols), :], nt,
                                preferred_element_type=F32)
            if mask is not None:
                t = jnp.where(mask, t, -jnp.inf)
            return jnp.dot(jnp.exp2(t).astype(BF16), v_scr[pl.ds(krow0, cols), :], preferred_element_type=F32)

        def tile(qrow0, krow0, masked):
            if not masked:
                return (block(q1_scr, qrow0, krow0, tq, tq, None), block(q2_scr, qrow0, krow0, tq, tq, None))
            h = tq // 2
            q_low = pl.multiple_of(qrow0 + h, h)
            k_hi = pl.multiple_of(krow0 + h, h)
            outs = []
            for q_scr in (q1_scr, q2_scr):
                left_cols = block(q_scr, qrow0, krow0, tq, h, keep[:, :h])
                corner = block(q_scr, q_low, k_hi, h, h, keep[:h, :h])
                outs.append(jnp.concatenate([left_cols[:h], left_cols[h:] + corner], axis=0))
            return tuple(outs)

        def normalised(acc):
            return acc[:, :LANES] / acc[:, LANES:LANES + 1]

        a1, a2 = tile(rows_a, rows_a, True)
        if paired:
            b1, b2 = tile(rows_b, rows_b, True)
            for s in range(nq - 1):
                to_a = s < p
                qrow0 = jnp.where(to_a, rows_a, rows_b)
                krow0 = pl.multiple_of(jnp.where(to_a, s, s - p) * tq, tq)
                d1, d2 = tile(pl.multiple_of(qrow0, tq), krow0, False)
                a1 = a1 + jnp.where(to_a, d1, 0.0)
                a2 = a2 + jnp.where(to_a, d2, 0.0)
                b1 = b1 + jnp.where(to_a, 0.0, d1)
                b2 = b2 + jnp.where(to_a, 0.0, d2)
            finish(normalised(b1), normalised(b2), rows_b)
        finish(normalised(a1), normalised(a2), rows_a)

    @pl.when(jnp.logical_not(safe))
    def _():
        def update(t, vt, m, l, acc):
            m_new = jnp.maximum(m, jnp.max(t, axis=-1, keepdims=True))
            alpha = jnp.exp2(m - m_new)
            pr = jnp.exp2(t - m_new)
            l_new = alpha * l + jnp.sum(pr, axis=-1, keepdims=True)
            acc_new = alpha * acc + jnp.dot(pr.astype(BF16), vt, preferred_element_type=F32)
            return m_new, l_new, acc_new

        def online(rows0, n_off):
            q1 = q1_scr[pl.ds(rows0, tq), :LANES]
            q2 = q2_scr[pl.ds(rows0, tq), :LANES]

            def step(krow0, carry, masked):
                m1, l1, c1, m2, l2, c2 = carry
                kt = k_scr[pl.ds(krow0, tq), :LANES]
                vt = v_scr[pl.ds(krow0, tq), :LANES]
                t1 = lax.dot_general(q1, kt, nt, preferred_element_type=F32)
                t2 = lax.dot_general(q2, kt, nt, preferred_element_type=F32)
                if masked:
                    t1 = jnp.where(keep, t1, -jnp.inf)
                    t2 = jnp.where(keep, t2, -jnp.inf)
                m1, l1, c1 = update(t1, vt, m1, l1, c1)
                m2, l2, c2 = update(t2, vt, m2, l2, c2)
                return m1, l1, c1, m2, l2, c2

            neg = jnp.full((tq, 1), -jnp.inf, F32)
            zero1 = jnp.zeros((tq, 1), F32)
            zacc = jnp.zeros((tq, ATT_V_DIM), F32)
            carry = step(rows0, (neg, zero1, zacc, neg, zero1, zacc), True)
            m1, l1, c1, m2, l2, c2 = lax.fori_loop(
                0, n_off, lambda j, c: step(pl.multiple_of(j * tq, tq), c, False), carry)
            finish(c1 / l1, c2 / l2, rows0)

        online(rows_a, p)
        if paired:
            online(rows_b, nq - 1 - p)


def _attention(proj, qcos, qsin, kcos, ksin, qg2, kg2, lq1, lk1, lq2, lk2, sg, batch, seq):
    T = batch * seq
    tq = min(ATT_TQ, seq)
    nq = seq // tq
    assert nq == 1 or nq % 2 == 0, "query tiles are processed in balanced pairs"
    vec = lambda n: pl.BlockSpec((1, n), lambda b, h, i: (0, 0))
    table = pl.BlockSpec((seq, LANES), lambda b, h, i: (0, 0))
    head = lambda col: pl.BlockSpec((seq, LANES), lambda b, h, i: (b, col // LANES + h))
    return pl.pallas_call(
        functools.partial(_attn_kernel, tq=tq, seq=seq),
        grid=(batch, ATT_HEADS, max(nq // 2, 1)),
        in_specs=[
            head(COL_Q), head(COL_K), head(COL_V),
            table, table, table, table,
            vec(LANES), vec(LANES),
            vec(ATT_HEAD_DIM), vec(ATT_HEAD_DIM), vec(ATT_HEAD_DIM), vec(ATT_HEAD_DIM),
            vec(LANES),
        ],
        out_specs=pl.BlockSpec((seq, LANES), lambda b, h, i: (b, h)),
        out_shape=jax.ShapeDtypeStruct((T, ATT_WIDTH), BF16),
        scratch_shapes=[pltpu.VMEM((seq, 2 * LANES), BF16) for _ in range(4)],
        compiler_params=_cparams(("arbitrary", "arbitrary", "arbitrary")),
        name="attention",
    )(proj, proj, proj, qcos, qsin, kcos, ksin, qg2, kg2, lq1, lk1, lq2, lk2, sg)


SSD_SUB = 2


def _ssd_kernel(xbc_ref, z_ref, dt_ref, cw_ref, cb_ref, dtb_ref, alog_ref, dsk_ref, ng_ref,
                o_ref, halo, state, xs_scr, bc_scr, y_scr):
    Q = SSM_CHUNK
    N = SSM_STATE
    P2 = 2 * SSM_HEAD_DIM

    @pl.when(pl.program_id(1) == 0)
    def _():
        halo[...] = jnp.zeros_like(halo)
        state[...] = jnp.zeros_like(state)

    taps = SSM_CONV - 1
    sr = lax.broadcasted_iota(jnp.int32, (taps * Q, HALO + Q), 0)
    sc = lax.broadcasted_iota(jnp.int32, (taps * Q, HALO + Q), 1)
    shift = (sc == (sr % Q) + HALO - taps + sr // Q).astype(BF16)
    ri = lax.broadcasted_iota(jnp.int32, (Q, Q), 0)
    ci = lax.broadcasted_iota(jnp.int32, (Q, Q), 1)
    tri = ri >= ci
    lane = lax.broadcasted_iota(jnp.int32, (1, P2), 1)
    left = lane < SSM_HEAD_DIM
    gw = SSM_D_INNER // SSM_GROUPS

    for sub in range(SSD_SUB):
        r0 = sub * Q
        xs_sub, bc_sub, y_sub = xs_scr.at[sub], bc_scr.at[sub], y_scr.at[sub]

        cwid = 512
        for c0 in range(0, SSM_XBC, cwid):
            cur = xbc_ref[r0:r0 + Q, c0:c0 + cwid]
            hist = halo[:, c0:c0 + cwid] if sub == 0 else xbc_ref[r0 - HALO:r0, c0:c0 + cwid]
            ext = jnp.concatenate([hist, cur], axis=0)
            shifted = jnp.dot(shift, ext, preferred_element_type=F32)
            acc = cb_ref[:, c0:c0 + cwid] + cw_ref[taps:SSM_CONV, c0:c0 + cwid] * cur.astype(F32)
            for k in range(taps):
                acc = acc + cw_ref[k:k + 1, c0:c0 + cwid] * shifted[k * Q:(k + 1) * Q, :]
            act = acc * jax.nn.sigmoid(acc)
            if c0 < SSM_D_INNER:
                xs_sub[:, c0:c0 + cwid] = act
            else:
                bc_sub[:, c0 - SSM_D_INNER:c0 - SSM_D_INNER + cwid] = act.astype(BF16)

        dt = jax.nn.softplus(dt_ref[r0:r0 + Q, :] + dtb_ref[...])
        da = dt * (-jnp.exp(alog_ref[...]) * LOG2E)
        acs = jnp.dot(tri.astype(F32), da, preferred_element_type=F32,
                      precision=lax.Precision.HIGHEST)
        acs_t = acs.T
        dt_t = dt.T
        w_t = dt_t * jnp.exp2(acs_t[:, Q - 1:Q] - acs_t)

        for g in range(SSM_GROUPS):
            bm = bc_sub[:, g * N:(g + 1) * N]
            cm = bc_sub[:, SSM_GROUPS * N + g * N:SSM_GROUPS * N + (g + 1) * N]
            cb = lax.dot_general(cm, bm, (((1,), (1,)), ((), ())), preferred_element_type=F32)
            bm_t = bm.astype(F32).T
            cm_f = cm.astype(F32)
            for pr in range(2):
                pair = 2 * g + pr
                xs_pair = xs_sub[:, pair * P2:(pair + 1) * P2].astype(BF16)
                prev = state[pair]
                rhs = jnp.concatenate([xs_pair, prev.astype(BF16)], axis=0)
                ys, sts, decs = [], [], []
                for r in range(2):
                    h = 2 * pair + r
                    a_col = acs[:, h:h + 1]
                    seg = a_col - acs_t[h:h + 1, :]
                    decay = jnp.exp2(jnp.where(tri, seg, -jnp.inf))
                    m_h = cb * decay * dt_t[h:h + 1, :]
                    e_h = cm_f * jnp.exp2(a_col)
                    lhs = jnp.concatenate([m_h, e_h], axis=1).astype(BF16)
                    ys.append(jnp.dot(lhs, rhs, preferred_element_type=F32))
                    sts.append(jnp.dot((bm_t * w_t[h:h + 1, :]).astype(BF16), xs_pair,
                                       preferred_element_type=F32))
                    decs.append(jnp.exp2(acs[Q - 1:Q, h:h + 1]))
                y_sub[:, pair * P2:(pair + 1) * P2] = jnp.where(left, ys[0], ys[1])
                dec = jnp.where(left, decs[0], decs[1])
                state[pair] = dec * prev + jnp.where(left, sts[0], sts[1])

        y = y_sub[...] + dsk_ref[...] * xs_sub[...]
        zf = z_ref[r0:r0 + Q, :].astype(F32)
        y = y * (zf * jax.nn.sigmoid(zf))
        for g in range(SSM_GROUPS):
            yg = y[:, g * gw:(g + 1) * gw]
            ms = jnp.mean(yg * yg, axis=-1, keepdims=True)
            o_ref[r0:r0 + Q, g * gw:(g + 1) * gw] = (yg * lax.rsqrt(ms + EPS)
                                                     * ng_ref[:, g * gw:(g + 1) * gw]).astype(BF16)

    halo[...] = xbc_ref[SSD_SUB * Q - HALO:SSD_SUB * Q, :]


def _ssd(proj, dt_raw, conv_w, conv_b, dt_bias, a_log, dsk, ng, batch, seq):
    T = batch * seq
    Q = SSM_CHUNK
    rows = SSD_SUB * Q
    nc = seq // rows
    vec = lambda r, n: pl.BlockSpec((r, n), lambda b, c: (0, 0))
    return pl.pallas_call(
        _ssd_kernel,
        grid=(batch, nc),
        in_specs=[
            pl.BlockSpec((rows, SSM_XBC), lambda b, c: (b * nc + c, COL_XBC // SSM_XBC)),
            pl.BlockSpec((rows, SSM_D_INNER), lambda b, c: (b * nc + c, COL_Z // SSM_D_INNER)),
            pl.BlockSpec((rows, LANES), lambda b, c: (b * nc + c, 0)),
            vec(SSM_CONV, SSM_XBC), vec(1, SSM_XBC), vec(1, LANES), vec(1, LANES),
            vec(1, SSM_D_INNER), vec(1, SSM_D_INNER),
        ],
        out_specs=pl.BlockSpec((rows, SSM_D_INNER), lambda b, c: (b * nc + c, 0)),
        out_shape=jax.ShapeDtypeStruct((T, SSM_D_INNER), BF16),
        scratch_shapes=[
            pltpu.VMEM((HALO, SSM_XBC), BF16),
            pltpu.VMEM((SSM_HEADS // 2, SSM_STATE, 2 * SSM_HEAD_DIM), F32),
            pltpu.VMEM((SSD_SUB, Q, SSM_D_INNER), F32),
            pltpu.VMEM((SSD_SUB, Q, 2 * SSM_GROUPS * SSM_STATE), BF16),
            pltpu.VMEM((SSD_SUB, Q, SSM_D_INNER), F32),
        ],
        compiler_params=_cparams(("arbitrary", "arbitrary")),
        name="ssd",
    )(proj, proj, dt_raw, conv_w, conv_b, dt_bias, a_log, dsk, ng)


def _merge_kernel(att_ref, ssm_ref, ga_ref, gs_ref, x_ref, wap_ref, wsp_ref, wo_ref, fg_ref,
                  wr_ref, br_ref, x1_ref, h2_ref, idx_ref, gate_ref, cnt_ref, base, *, tm):
    i = pl.program_id(0)

    @pl.when(i == 0)
    def _():
        base[...] = jnp.zeros_like(base)

    pa = jnp.dot(att_ref[...], wap_ref[...], preferred_element_type=F32)
    ps = jnp.dot(ssm_ref[...], wsp_ref[...], preferred_element_type=F32)
    merged = (jax.nn.sigmoid(ga_ref[...].astype(F32)) * pa
              + jax.nn.sigmoid(gs_ref[...].astype(F32)) * ps)
    x1 = x_ref[...] + jnp.dot(merged.astype(BF16), wo_ref[...], preferred_element_type=F32)
    x1_ref[...] = x1
    ms = jnp.mean(x1 * x1, axis=-1, keepdims=True)
    h2 = x1 * lax.rsqrt(ms + EPS) * fg_ref[...]
    _store_rows(h2_ref, _pack_rows(h2))

    logits = jnp.dot(h2.astype(BF16), wr_ref[...], preferred_element_type=F32) + br_ref[...]
    lg = logits.T[:N_EXPERTS, :]
    row_f = lax.broadcasted_iota(jnp.int32, (N_EXPERTS, tm), 0).astype(F32)
    vals, idxs, sels = [], [], []
    for _ in range(TOP_K):
        m = jnp.max(lg, axis=0, keepdims=True)
        idx = jnp.min(jnp.where(lg == m, row_f, float(N_EXPERTS)), axis=0, keepdims=True)
        sel = row_f == idx
        vals.append(m)
        idxs.append(idx)
        sels.append(sel)
        lg = jnp.where(sel, -jnp.inf, lg)
    es = [jnp.exp(v - vals[0]) for v in vals]
    den = es[0] + es[1] + es[2] + es[3]

    multi = jnp.zeros((N_EXPERTS, tm), F32)
    for sel in sels:
        multi = jnp.where(sel, 1.0, multi)
    ri = lax.broadcasted_iota(jnp.int32, (tm, tm), 0)
    ci = lax.broadcasted_iota(jnp.int32, (tm, tm), 1)
    before = jnp.dot(multi.astype(BF16), (ri < ci).astype(BF16), preferred_element_type=F32)
    before = before + base[...]
    ranks = [jnp.sum(jnp.where(sel, before, 0.0), axis=0, keepdims=True) for sel in sels]
    idx_ref[...] = jnp.concatenate(idxs + ranks, axis=0).astype(jnp.int32)
    gate_ref[...] = jnp.concatenate([e / den for e in es] + [jnp.zeros_like(den)] * TOP_K, axis=0)
    base[...] = base[...] + jnp.sum(multi, axis=1, keepdims=True)
    cnt_ref[...] = jnp.broadcast_to(base[...], cnt_ref.shape).astype(jnp.int32)


def _merge(att, ssm, proj, xf, wap, wsp, wo, fg, wr, br):
    T = xf.shape[0]
    tm = min(MERGE_TM, T)
    full = lambda a: pl.BlockSpec(a.shape, lambda i: (0, 0))
    return pl.pallas_call(
        functools.partial(_merge_kernel, tm=tm),
        grid=(T // tm,),
        in_specs=[
            pl.BlockSpec((tm, ATT_WIDTH), lambda i: (i, 0)),
            pl.BlockSpec((tm, SSM_D_INNER), lambda i: (i, 0)),
            pl.BlockSpec((tm, D_MODEL), lambda i: (i, COL_GA // D_MODEL)),
            pl.BlockSpec((tm, D_MODEL), lambda i: (i, COL_GS // D_MODEL)),
            pl.BlockSpec((tm, D_MODEL), lambda i: (i, 0)),
            full(wap), full(wsp), full(wo), full(fg), full(wr), full(br),
        ],
        out_specs=[
            pl.BlockSpec((tm, D_MODEL), lambda i: (i, 0)),
            pl.BlockSpec((tm * ROW_SUB, LANES), lambda i: (i, 0)),
            pl.BlockSpec((2 * TOP_K, tm), lambda i: (0, i)),
            pl.BlockSpec((2 * TOP_K, tm), lambda i: (0, i)),
            pl.BlockSpec((N_EXPERTS, LANES), lambda i: (0, 0)),
        ],
        out_shape=[
            jax.ShapeDtypeStruct((T, D_MODEL), F32),
            jax.ShapeDtypeStruct((T * ROW_SUB, LANES), jnp.uint32),
            jax.ShapeDtypeStruct((2 * TOP_K, T), jnp.int32),
            jax.ShapeDtypeStruct((2 * TOP_K, T), F32),
            jax.ShapeDtypeStruct((N_EXPERTS, LANES), jnp.int32),
        ],
        scratch_shapes=[pltpu.VMEM((N_EXPERTS, 1), F32)],
        compiler_params=_cparams(("arbitrary",)),
        name="merge_router",
    )(att, ssm, proj, proj, xf, wap, wsp, wo, fg, wr, br)


DMA_UNROLL = 8


def _dispatch_kernel(lb_ref, cnt_ref, dest_ref, h_ref, rows_ref, zeros, zsem, sem, *, tt, bm):
    @pl.when(pl.program_id(0) == 0)
    def _():
        zeros[...] = jnp.zeros_like(zeros)
        for wait in (False, True):
            for e in range(N_EXPERTS):
                @pl.when(cnt_ref[e] > 0)
                def _():
                    start = pl.multiple_of(lb_ref[e], bm)
                    cp = pltpu.make_async_copy(zeros, rows_ref.at[pl.ds(start * ROW_SUB, bm * ROW_SUB)], zsem)
                    cp.wait() if wait else cp.start()

    def row_copy(t, k):
        return pltpu.make_async_copy(_row(h_ref, t), _row(rows_ref, dest_ref[t * TOP_K + k]), sem)

    for wait in (False, True):
        def body(g, carry):
            for u in range(DMA_UNROLL):
                for k in range(TOP_K):
                    cp = row_copy(g * DMA_UNROLL + u, k)
                    cp.wait() if wait else cp.start(priority=k % 2)
            return carry
        lax.fori_loop(0, tt // DMA_UNROLL, body, 0)


def _dispatch(last_block, cnt, dest, h2p, n_rows, bm):
    T = h2p.shape[0] // ROW_SUB
    tt = min(DISPATCH_TT, T)
    grid_spec = pltpu.PrefetchScalarGridSpec(
        num_scalar_prefetch=2,
        grid=(T // tt,),
        in_specs=[
            pl.BlockSpec((tt * TOP_K,), lambda i, lb, c: (i,), memory_space=pltpu.SMEM),
            pl.BlockSpec((tt * ROW_SUB, LANES), lambda i, lb, c: (i, 0)),
        ],
        out_specs=pl.BlockSpec(memory_space=pl.ANY),
        scratch_shapes=[pltpu.VMEM((bm * ROW_SUB, LANES), jnp.uint32),
                        pltpu.SemaphoreType.DMA, pltpu.SemaphoreType.DMA],
    )
    return pl.pallas_call(
        functools.partial(_dispatch_kernel, tt=tt, bm=bm),
        grid_spec=grid_spec,
        out_shape=jax.ShapeDtypeStruct((n_rows * ROW_SUB, LANES), jnp.uint32),
        compiler_params=_cparams(("arbitrary",)),
        name="dispatch",
    )(last_block, cnt, dest, h2p)


def _expert_kernel(be_ref, nu_ref, x_ref, wgu_ref, bgu_ref, wd_ref, bd_ref, o_ref, wgu_bf, wd_bf):
    i = pl.program_id(0)

    @pl.when(i < nu_ref[0])
    def _():
        @pl.when((i == 0) | (be_ref[i] != be_ref[jnp.maximum(i - 1, 0)]))
        def _():
            wgu_bf[...] = wgu_ref[0].astype(BF16)
            wd_bf[...] = wd_ref[0].astype(BF16)

        lo, hi = _unpack_rows(_load_rows(x_ref))
        x = jnp.concatenate([lo, hi], axis=1).astype(BF16)
        gu = jnp.dot(x, wgu_bf[...], preferred_element_type=F32) + bgu_ref[0]
        gate = jnp.minimum(gu[:, :D_FF], SWIGLU_LIMIT)
        up = jnp.clip(gu[:, D_FF:], -SWIGLU_LIMIT, SWIGLU_LIMIT)
        glu = gate * jax.nn.sigmoid(SWIGLU_ALPHA * gate)
        act = ((up + 1.0) * glu).astype(BF16)
        _store_rows(o_ref, _pack_rows(jnp.dot(act, wd_bf[...], preferred_element_type=F32) + bd_ref[0]))


def _experts(block_expert, n_used, rows, wgu, bgu, wd, bd, bm):
    n_rows = rows.shape[0] // ROW_SUB
    nb = n_rows // bm
    row_map = lambda i, be, nu: (jnp.minimum(i, nu[0] - 1), 0)
    exp_map = lambda i, be, nu: (be[jnp.minimum(i, nu[0] - 1)], 0, 0)
    grid_spec = pltpu.PrefetchScalarGridSpec(
        num_scalar_prefetch=2,
        grid=(nb,),
        in_specs=[
            pl.BlockSpec((bm * ROW_SUB, LANES), row_map),
            pl.BlockSpec((1, D_MODEL, 2 * D_FF), exp_map),
            pl.BlockSpec((1, 1, 2 * D_FF), exp_map),
            pl.BlockSpec((1, D_FF, D_MODEL), exp_map),
            pl.BlockSpec((1, 1, D_MODEL), exp_map),
        ],
        out_specs=pl.BlockSpec((bm * ROW_SUB, LANES), row_map),
        scratch_shapes=[pltpu.VMEM((D_MODEL, 2 * D_FF), BF16), pltpu.VMEM((D_FF, D_MODEL), BF16)],
    )
    return pl.pallas_call(
        _expert_kernel,
        grid_spec=grid_spec,
        out_shape=jax.ShapeDtypeStruct((n_rows * ROW_SUB, LANES), jnp.uint32),
        compiler_params=_cparams(("arbitrary",)),
        name="experts",
    )(block_expert, n_used, rows, wgu, bgu, wd, bd)


def _combine_kernel(dest_ref, next_ref, x1_ref, gate_ref, y_ref, o_ref, ybuf, sems, *, tc):
    i = pl.program_id(0)
    n = pl.num_programs(0)

    def gather(idx_ref, slot, wait):
        def row_copy(t, k):
            return pltpu.make_async_copy(_row(y_ref, idx_ref[t * TOP_K + k]), _row(ybuf.at[slot, k], t),
                                         sems.at[slot])

        def body(g, carry):
            for u in range(DMA_UNROLL):
                for k in range(TOP_K):
                    cp = row_copy(g * DMA_UNROLL + u, k)
                    cp.wait() if wait else cp.start(priority=k % 2)
            return carry
        lax.fori_loop(0, tc // DMA_UNROLL, body, 0)

    slot = i % 2

    @pl.when(i == 0)
    def _():
        gather(dest_ref, 0, False)

    @pl.when(i + 1 < n)
    def _():
        gather(next_ref, 1 - slot, False)

    gather(dest_ref, slot, True)

    x1 = x1_ref[...]
    acc_lo, acc_hi = x1[:, :PACKED], x1[:, PACKED:]
    g = gate_ref[...].T
    for k in range(TOP_K):
        lo, hi = _unpack_rows(_load_rows(ybuf.at[slot, k]))
        acc_lo = acc_lo + g[:, k:k + 1] * lo
        acc_hi = acc_hi + g[:, k:k + 1] * hi
    o_ref[:, :PACKED] = acc_lo
    o_ref[:, PACKED:] = acc_hi


def _combine(dest, x1, gates, y_rows):
    T = x1.shape[0]
    tc = min(COMBINE_TC, T)
    n = T // tc
    return pl.pallas_call(
        functools.partial(_combine_kernel, tc=tc),
        grid=(n,),
        in_specs=[
            pl.BlockSpec((tc * TOP_K,), lambda i: (i,), memory_space=pltpu.SMEM),
            pl.BlockSpec((tc * TOP_K,), lambda i: (jnp.minimum(i + 1, n - 1),), memory_space=pltpu.SMEM),
            pl.BlockSpec((tc, D_MODEL), lambda i: (i, 0)),
            pl.BlockSpec((2 * TOP_K, tc), lambda i: (0, i)),
            pl.BlockSpec(memory_space=pl.ANY),
        ],
        out_specs=pl.BlockSpec((tc, D_MODEL), lambda i: (i, 0)),
        out_shape=jax.ShapeDtypeStruct((T, D_MODEL), F32),
        scratch_shapes=[pltpu.VMEM((2, TOP_K, tc * ROW_SUB, LANES), jnp.uint32),
                        pltpu.SemaphoreType.DMA((2,))],
        compiler_params=_cparams(("arbitrary",)),
        name="combine",
    )(dest, dest, x1, gates, y_rows)


def _pad_lanes(v):
    return jnp.pad(v, ((0, 0), (0, LANES - v.shape[-1])))


def kernel(x, mix_norm_g, w_in, q_norm_g, k_norm_g, lambda_q1, lambda_k1, lambda_q2, lambda_k2,
           attn_subln_g, conv_w, conv_b, dt_bias, a_log, d_skip, ssm_norm_g, w_attn_proj,
           w_ssm_proj, w_out, ffn_norm_g, w_router, b_router, w_gate_up, b_gate_up, w_down, b_down):
    B, S, D = x.shape
    T = B * S
    xf = x.reshape(T, D)
    layer = 0

    wi = w_in[layer]
    o_z = 3 * ATT_WIDTH
    o_xbc = o_z + SSM_D_INNER
    o_dt = o_xbc + SSM_XBC
    o_ga = o_dt + SSM_HEADS
    w_main = jnp.concatenate([wi[:, o_xbc:o_dt], wi[:, o_z:o_xbc], wi[:, :o_z], wi[:, o_ga:]],
                             axis=1).astype(BF16)
    w_dt = _pad_lanes(wi[:, o_dt:o_ga]).astype(BF16)
    proj, dt_raw = _in_proj(xf, mix_norm_g[layer][None, :], w_main, w_dt)

    half = ATT_HEAD_DIM // 2
    inv = ROPE_THETA ** (-jnp.arange(0, ATT_HEAD_DIM, 2, dtype=F32) / ATT_HEAD_DIM)
    ang = jnp.arange(S, dtype=F32)[:, None] * inv[None, :]
    cos2 = jnp.tile(jnp.cos(ang), (1, LANES // half))
    sin2 = jnp.tile(jnp.concatenate([-jnp.sin(ang), jnp.sin(ang)], axis=1), (1, LANES // ATT_HEAD_DIM))
    qg2 = jnp.tile(q_norm_g[layer], 2)[None, :]
    kg2 = jnp.tile(k_norm_g[layer], 2)[None, :]
    partner = lambda g: jnp.tile(jnp.roll(g, half), 2)[None, :]
    q_scale = ATT_HEAD_DIM ** -0.5 * LOG2E
    att = _attention(proj, cos2 * qg2 * q_scale, sin2 * partner(q_norm_g[layer]) * q_scale,
                     cos2 * kg2, sin2 * partner(k_norm_g[layer]), qg2, kg2,
                     lambda_q1[layer][None, :], lambda_k1[layer][None, :],
                     lambda_q2[layer][None, :], lambda_k2[layer][None, :],
                     attn_subln_g[layer][None, :], B, S)

    ssm = _ssd(proj, dt_raw, conv_w[layer], conv_b[layer][None, :],
               _pad_lanes(dt_bias[layer][None, :]), _pad_lanes(a_log[layer][None, :]),
               jnp.repeat(d_skip[layer], SSM_HEAD_DIM)[None, :], ssm_norm_g[layer][None, :], B, S)

    x1, h2p, idx_rank_t, gates_t, counts = _merge(
        att, ssm, proj, xf, w_attn_proj[layer].astype(BF16), w_ssm_proj[layer].astype(BF16),
        w_out[layer].astype(BF16), ffn_norm_g[layer][None, :],
        _pad_lanes(w_router[layer]).astype(BF16), _pad_lanes(b_router[layer][None, :]))

    bm = EXPERT_BM
    A = T * TOP_K
    n_rows = (A + N_EXPERTS * (bm - 1)) // bm * bm
    cnt = counts[:, 0]
    padded = (cnt + bm - 1) // bm * bm
    pend = jnp.cumsum(padded)
    pstart = pend - padded
    n_used = (pend[-1:] // bm).astype(jnp.int32)
    block_start = jnp.arange(n_rows // bm, dtype=jnp.int32) * bm
    block_expert = jnp.minimum(jnp.sum(block_start[:, None] >= pend[None, :], axis=1),
                               N_EXPERTS - 1).astype(jnp.int32)
    idx_rank = idx_rank_t.T
    onehot = idx_rank[:, :TOP_K, None] == jnp.arange(N_EXPERTS, dtype=jnp.int32)[None, None, :]
    dest = (jnp.sum(jnp.where(onehot, pstart[None, None, :], 0), axis=-1)
            + idx_rank[:, TOP_K:]).reshape(A).astype(jnp.int32)

    rows = _dispatch((pend - bm).astype(jnp.int32), cnt.astype(jnp.int32), dest, h2p, n_rows, bm)
    y_rows = _experts(block_expert, n_used, rows, w_gate_up[layer], b_gate_up[layer][:, None, :],
                      w_down[layer], b_down[layer][:, None, :], bm)
    out = _combine(dest, x1, gates_t, y_rows)
    return out.reshape(B, S, D)
```

```python
import functools
import math

import jax
import jax.numpy as jnp
from jax import lax
from jax.experimental import pallas as pl
from jax.experimental.pallas import tpu as pltpu

F32 = jnp.float32
BF16 = jnp.bfloat16

D_MODEL = 1024
EPS = 1e-6
ATT_HEADS = 8
ATT_HEAD_DIM = 64
ATT_V_DIM = 2 * ATT_HEAD_DIM
ATT_WIDTH = ATT_HEADS * ATT_V_DIM
ROPE_THETA = 10000.0
SSM_D_INNER = 2 * D_MODEL
SSM_HEAD_DIM = 64
SSM_HEADS = SSM_D_INNER // SSM_HEAD_DIM
SSM_GROUPS = 8
SSM_STATE = 128
SSM_CONV = 4
SSM_CHUNK = 128
SSM_XBC = SSM_D_INNER + 2 * SSM_GROUPS * SSM_STATE
N_EXPERTS = 32
TOP_K = 4
D_FF = D_MODEL
SWIGLU_LIMIT = 7.0
SWIGLU_ALPHA = 1.702
LAM_INIT = 0.8 - 0.6 * math.exp(-0.3 * 0)

LANES = 128
HALO = 16

COL_XBC = 0
COL_Z = SSM_XBC
COL_Q = COL_Z + SSM_D_INNER
COL_K = COL_Q + ATT_WIDTH
COL_V = COL_K + ATT_WIDTH
COL_GA = COL_V + ATT_WIDTH
COL_GS = COL_GA + D_MODEL
PROJ_COLS = COL_GS + D_MODEL

VMEM_LIMIT = 56 * 1024 * 1024

PROJ_TM = 2048
PROJ_TN = 1024
ATT_TQ = 512
MERGE_TM = 512
DISPATCH_TT = 2048
EXPERT_BM = 1024
COMBINE_TC = 256


def _cparams(sem):
    return pltpu.CompilerParams(dimension_semantics=sem, vmem_limit_bytes=VMEM_LIMIT)


PACKED = D_MODEL // 2


def _pack_rows(x):
    lo = lax.bitcast_convert_type(x[:, :PACKED].astype(BF16).astype(F32), jnp.uint32)
    hi = lax.bitcast_convert_type(x[:, PACKED:].astype(BF16).astype(F32), jnp.uint32)
    return hi | (lo >> 16)


def _unpack_rows(p):
    lo = lax.bitcast_convert_type(p << 16, F32)
    hi = lax.bitcast_convert_type(p & jnp.uint32(0xFFFF0000), F32)
    return lo, hi


ROW_SUB = PACKED // LANES


def _store_rows(ref, packed):
    for s in range(ROW_SUB):
        ref[pl.ds(s, packed.shape[0], stride=ROW_SUB), :] = packed[:, s * LANES:(s + 1) * LANES]


def _load_rows(ref):
    n = ref.shape[0] // ROW_SUB
    return jnp.concatenate([ref[pl.ds(s, n, stride=ROW_SUB), :] for s in range(ROW_SUB)], axis=1)


def _row(ref, r):
    return ref.at[pl.ds(pl.multiple_of(r * ROW_SUB, ROW_SUB), ROW_SUB)]


def _in_proj_kernel(x_ref, g_ref, w_ref, wdt_ref, o_ref, dt_ref, h_scr):
    @pl.when(pl.program_id(1) == 0)
    def _():
        x = x_ref[...]
        ms = jnp.mean(x * x, axis=-1, keepdims=True)
        hb = (x * lax.rsqrt(ms + EPS) * g_ref[...]).astype(BF16)
        h_scr[...] = hb
        dt_ref[...] = jnp.dot(hb, wdt_ref[...], preferred_element_type=F32)

    o_ref[...] = jnp.dot(h_scr[...], w_ref[...], preferred_element_type=F32).astype(BF16)


def _in_proj(xf, g, w_main, w_dt):
    T = xf.shape[0]
    tm = min(PROJ_TM, T)
    tn = PROJ_TN
    return pl.pallas_call(
        _in_proj_kernel,
        grid=(T // tm, PROJ_COLS // tn),
        in_specs=[
            pl.BlockSpec((tm, D_MODEL), lambda i, j: (i, 0)),
            pl.BlockSpec((1, D_MODEL), lambda i, j: (0, 0)),
            pl.BlockSpec((D_MODEL, tn), lambda i, j: (0, j)),
            pl.BlockSpec((D_MODEL, LANES), lambda i, j: (0, 0)),
        ],
        out_specs=[
            pl.BlockSpec((tm, tn), lambda i, j: (i, j)),
            pl.BlockSpec((tm, LANES), lambda i, j: (i, 0)),
        ],
        out_shape=[
            jax.ShapeDtypeStruct((T, PROJ_COLS), BF16),
            jax.ShapeDtypeStruct((T, LANES), F32),
        ],
        scratch_shapes=[pltpu.VMEM((tm, D_MODEL), BF16)],
        compiler_params=_cparams(("arbitrary", "arbitrary")),
        name="in_proj",
    )(xf, g, w_main, w_dt)


LOG2E = 1.4426950408889634
SHIFT_LIMIT = 57.0
BOUND_MARGIN = 1.02


def _attn_kernel(q_ref, k_ref, v_ref, qc_ref, qs_ref, kc_ref, ks_ref, qg_ref, kg_ref,
                 lq1_ref, lk1_ref, lq2_ref, lk2_ref, sg_ref, o_ref, k_scr, v_scr, q1_scr, q2_scr, *, tq, seq):
    p = pl.program_id(2)
    nq = seq // tq
    paired = nq > 1
    lane = lax.broadcasted_iota(jnp.int32, (1, LANES), 1)
    first = lane < ATT_HEAD_DIM
    one_col = jnp.where(lane == 0, 1.0, 0.0)

    ri = lax.broadcasted_iota(jnp.int32, (LANES, LANES), 0)
    ci = lax.broadcasted_iota(jnp.int32, (LANES, LANES), 1)
    same_comp = (ri // ATT_HEAD_DIM == ci // ATT_HEAD_DIM).astype(BF16)
    swap_half = ((ri // ATT_HEAD_DIM == ci // ATT_HEAD_DIM)
                 & ((ri - ci == ATT_HEAD_DIM // 2) | (ci - ri == ATT_HEAD_DIM // 2))).astype(BF16)

    def norm_rope(xb, gcos, gsin):
        xf = xb.astype(F32)
        sq = xf * xf
        hi = sq.astype(BF16)
        lo = (sq - hi.astype(F32)).astype(BF16)
        ms = jnp.dot(jnp.concatenate([hi, lo], axis=1), jnp.concatenate([same_comp, same_comp], axis=0),
                     preferred_element_type=F32) * (1.0 / ATT_HEAD_DIM)
        xr = jnp.dot(xb, swap_half, preferred_element_type=F32)
        return lax.rsqrt(ms + EPS) * (xf * gcos + xr * gsin)

    ub = (ATT_HEAD_DIM * ATT_HEAD_DIM ** -0.5 * LOG2E * BOUND_MARGIN
          * jnp.max(jnp.abs(qg_ref[...])) * jnp.max(jnp.abs(kg_ref[...])))
    safe = ub <= SHIFT_LIMIT

    @pl.when(p == 0)
    def _():
        ones = jnp.broadcast_to(one_col, (tq, LANES)).astype(BF16)
        shift = jnp.broadcast_to(jnp.where(lane == 0, -ub, 0.0), (tq, LANES)).astype(BF16)
        for r in range(0, seq, tq):
            rows = slice(r, r + tq)
            k_scr[rows, :LANES] = norm_rope(k_ref[rows, :], kc_ref[rows, :], ks_ref[rows, :]).astype(BF16)
            k_scr[rows, LANES:] = ones
            v_scr[rows, :LANES] = v_ref[rows, :]
            v_scr[rows, LANES:] = ones
            qb = norm_rope(q_ref[rows, :], qc_ref[rows, :], qs_ref[rows, :]).astype(BF16)
            q1_scr[rows, :LANES] = jnp.where(first, qb, jnp.zeros_like(qb))
            q1_scr[rows, LANES:] = shift
            q2_scr[rows, :LANES] = jnp.where(first, jnp.zeros_like(qb), qb)
            q2_scr[rows, LANES:] = shift

    keep = (lax.broadcasted_iota(jnp.int32, (tq, tq), 1) <= lax.broadcasted_iota(jnp.int32, (tq, tq), 0))
    nt = (((1,), (1,)), ((), ()))
    rows_a = pl.multiple_of(p * tq, tq)
    rows_b = pl.multiple_of((nq - 1 - p) * tq, tq)

    def finish(o1, o2, rows0):
        lam = (jnp.exp(jnp.sum(lq1_ref[...] * lk1_ref[...], axis=-1, keepdims=True))
               - jnp.exp(jnp.sum(lq2_ref[...] * lk2_ref[...], axis=-1, keepdims=True)) + LAM_INIT)
        o = o1 - lam * o2
        ms = jnp.mean(o * o, axis=-1, keepdims=True)
        o_ref[pl.ds(rows0, tq), :] = (o * lax.rsqrt(ms + EPS) * sg_ref[...] * (1.0 - LAM_INIT)).astype(BF16)

    @pl.when(safe)
    def _():
        def block(q_scr, qrow0, krow0, rows, cols, mask):
            t = lax.dot_general(q_scr[pl.ds(qrow0, rows), :], k_scr[pl.ds(krow0, cols), :], nt,
                                preferred_element_type=F32)
            if mask is not None:
                t = jnp.where(mask, t, -jnp.inf)
            return jnp.dot(jnp.exp2(t).astype(BF16), v_scr[pl.ds(krow0, cols), :], preferred_element_type=F32)

        def tile(qrow0, krow0, masked):
            if not masked:
                return (block(q1_scr, qrow0, krow0, tq, tq, None), block(q2_scr, qrow0, krow0, tq, tq, None))
            h = tq // 2
            q_low = pl.multiple_of(qrow0 + h, h)
            k_hi = pl.multiple_of(krow0 + h, h)
            outs = []
            for q_scr in (q1_scr, q2_scr):
                left_cols = block(q_scr, qrow0, krow0, tq, h, keep[:, :h])
                corner = block(q_scr, q_low, k_hi, h, h, keep[:h, :h])
                outs.append(jnp.concatenate([left_cols[:h], left_cols[h:] + corner], axis=0))
            return tuple(outs)

        def normalised(acc):
            return acc[:, :LANES] / acc[:, LANES:LANES + 1]

        a1, a2 = tile(rows_a, rows_a, True)
        if paired:
            b1, b2 = tile(rows_b, rows_b, True)
            for s in range(nq - 1):
                to_a = s < p
                qrow0 = jnp.where(to_a, rows_a, rows_b)
                krow0 = pl.multiple_of(jnp.where(to_a, s, s - p) * tq, tq)
                d1, d2 = tile(pl.multiple_of(qrow0, tq), krow0, False)
                a1 = a1 + jnp.where(to_a, d1, 0.0)
                a2 = a2 + jnp.where(to_a, d2, 0.0)
                b1 = b1 + jnp.where(to_a, 0.0, d1)
                b2 = b2 + jnp.where(to_a, 0.0, d2)
            finish(normalised(b1), normalised(b2), rows_b)
        finish(normalised(a1), normalised(a2), rows_a)

    @pl.when(jnp.logical_not(safe))
    def _():
        def update(t, vt, m, l, acc):
            m_new = jnp.maximum(m, jnp.max(t, axis=-1, keepdims=True))
            alpha = jnp.exp2(m - m_new)
            pr = jnp.exp2(t - m_new)
            l_new = alpha * l + jnp.sum(pr, axis=-1, keepdims=True)
            acc_new = alpha * acc + jnp.dot(pr.astype(BF16), vt, preferred_element_type=F32)
            return m_new, l_new, acc_new

        def online(rows0, n_off):
            q1 = q1_scr[pl.ds(rows0, tq), :LANES]
            q2 = q2_scr[pl.ds(rows0, tq), :LANES]

            def step(krow0, carry, masked):
                m1, l1, c1, m2, l2, c2 = carry
                kt = k_scr[pl.ds(krow0, tq), :LANES]
                vt = v_scr[pl.ds(krow0, tq), :LANES]
                t1 = lax.dot_general(q1, kt, nt, preferred_element_type=F32)
                t2 = lax.dot_general(q2, kt, nt, preferred_element_type=F32)
                if masked:
                    t1 = jnp.where(keep, t1, -jnp.inf)
                    t2 = jnp.where(keep, t2, -jnp.inf)
                m1, l1, c1 = update(t1, vt, m1, l1, c1)
                m2, l2, c2 = update(t2, vt, m2, l2, c2)
                return m1, l1, c1, m2, l2, c2

            neg = jnp.full((tq, 1), -jnp.inf, F32)
            zero1 = jnp.zeros((tq, 1), F32)
            zacc = jnp.zeros((tq, ATT_V_DIM), F32)
            carry = step(rows0, (neg, zero1, zacc, neg, zero1, zacc), True)
            m1, l1, c1, m2, l2, c2 = lax.fori_loop(
                0, n_off, lambda j, c: step(pl.multiple_of(j * tq, tq), c, False), carry)
            finish(c1 / l1, c2 / l2, rows0)

        online(rows_a, p)
        if paired:
            online(rows_b, nq - 1 - p)


def _attention(proj, qcos, qsin, kcos, ksin, qg2, kg2, lq1, lk1, lq2, lk2, sg, batch, seq):
    T = batch * seq
    tq = min(ATT_TQ, seq)
    nq = seq // tq
    assert nq == 1 or nq % 2 == 0, "query tiles are processed in balanced pairs"
    vec = lambda n: pl.BlockSpec((1, n), lambda b, h, i: (0, 0))
    table = pl.BlockSpec((seq, LANES), lambda b, h, i: (0, 0))
    head = lambda col: pl.BlockSpec((seq, LANES), lambda b, h, i: (b, col // LANES + h))
    return pl.pallas_call(
        functools.partial(_attn_kernel, tq=tq, seq=seq),
        grid=(batch, ATT_HEADS, max(nq // 2, 1)),
        in_specs=[
            head(COL_Q), head(COL_K), head(COL_V),
            table, table, table, table,
            vec(LANES), vec(LANES),
            vec(ATT_HEAD_DIM), vec(ATT_HEAD_DIM), vec(ATT_HEAD_DIM), vec(ATT_HEAD_DIM),
            vec(LANES),
        ],
        out_specs=pl.BlockSpec((seq, LANES), lambda b, h, i: (b, h)),
        out_shape=jax.ShapeDtypeStruct((T, ATT_WIDTH), BF16),
        scratch_shapes=[pltpu.VMEM((seq, 2 * LANES), BF16) for _ in range(4)],
        compiler_params=_cparams(("arbitrary", "arbitrary", "arbitrary")),
        name="attention",
    )(proj, proj, proj, qcos, qsin, kcos, ksin, qg2, kg2, lq1, lk1, lq2, lk2, sg)


SSD_SUB = 2


def _ssd_kernel(xbc_ref, z_ref, dt_ref, cw_ref, cb_ref, dtb_ref, alog_ref, dsk_ref, ng_ref,
                o_ref, halo, state, xs_scr, bc_scr, y_scr):
    Q = SSM_CHUNK
    N = SSM_STATE
    P2 = 2 * SSM_HEAD_DIM

    @pl.when(pl.program_id(1) == 0)
    def _():
        halo[...] = jnp.zeros_like(halo)
        state[...] = jnp.zeros_like(state)

    taps = SSM_CONV - 1
    sr = lax.broadcasted_iota(jnp.int32, (taps * Q, HALO + Q), 0)
    sc = lax.broadcasted_iota(jnp.int32, (taps * Q, HALO + Q), 1)
    shift = (sc == (sr % Q) + HALO - taps + sr // Q).astype(BF16)
    ri = lax.broadcasted_iota(jnp.int32, (Q, Q), 0)
    ci = lax.broadcasted_iota(jnp.int32, (Q, Q), 1)
    tri = ri >= ci
    lane = lax.broadcasted_iota(jnp.int32, (1, P2), 1)
    left = lane < SSM_HEAD_DIM
    gw = SSM_D_INNER // SSM_GROUPS

    for sub in range(SSD_SUB):
        r0 = sub * Q
        xs_sub, bc_sub, y_sub = xs_scr.at[sub], bc_scr.at[sub], y_scr.at[sub]

        cwid = 512
        for c0 in range(0, SSM_XBC, cwid):
            cur = xbc_ref[r0:r0 + Q, c0:c0 + cwid]
            hist = halo[:, c0:c0 + cwid] if sub == 0 else xbc_ref[r0 - HALO:r0, c0:c0 + cwid]
            ext = jnp.concatenate([hist, cur], axis=0)
            shifted = jnp.dot(shift, ext, preferred_element_type=F32)
            acc = cb_ref[:, c0:c0 + cwid] + cw_ref[taps:SSM_CONV, c0:c0 + cwid] * cur.astype(F32)
            for k in range(taps):
                acc = acc + cw_ref[k:k + 1, c0:c0 + cwid] * shifted[k * Q:(k + 1) * Q, :]
            act = acc * jax.nn.sigmoid(acc)
            if c0 < SSM_D_INNER:
                xs_sub[:, c0:c0 + cwid] = act
            else:
                bc_sub[:, c0 - SSM_D_INNER:c0 - SSM_D_INNER + cwid] = act.astype(BF16)

        dt = jax.nn.softplus(dt_ref[r0:r0 + Q, :] + dtb_ref[...])
        da = dt * (-jnp.exp(alog_ref[...]) * LOG2E)
        acs = jnp.dot(tri.astype(F32), da, preferred_element_type=F32,
                      precision=lax.Precision.HIGHEST)
        acs_t = acs.T
        dt_t = dt.T
        w_t = dt_t * jnp.exp2(acs_t[:, Q - 1:Q] - acs_t)

        for g in range(SSM_GROUPS):
            bm = bc_sub[:, g * N:(g + 1) * N]
            cm = bc_sub[:, SSM_GROUPS * N + g * N:SSM_GROUPS * N + (g + 1) * N]
            cb = lax.dot_general(cm, bm, (((1,), (1,)), ((), ())), preferred_element_type=F32)
            bm_t = bm.astype(F32).T
            cm_f = cm.astype(F32)
            for pr in range(2):
                pair = 2 * g + pr
                xs_pair = xs_sub[:, pair * P2:(pair + 1) * P2].astype(BF16)
                prev = state[pair]
                rhs = jnp.concatenate([xs_pair, prev.astype(BF16)], axis=0)
                ys, sts, decs = [], [], []
                for r in range(2):
                    h = 2 * pair + r
                    a_col = acs[:, h:h + 1]
                    seg = a_col - acs_t[h:h + 1, :]
                    decay = jnp.exp2(jnp.where(tri, seg, -jnp.inf))
                    m_h = cb * decay * dt_t[h:h + 1, :]
                    e_h = cm_f * jnp.exp2(a_col)
                    lhs = jnp.concatenate([m_h, e_h], axis=1).astype(BF16)
                    ys.append(jnp.dot(lhs, rhs, preferred_element_type=F32))
                    sts.append(jnp.dot((bm_t * w_t[h:h + 1, :]).astype(BF16), xs_pair,
                                       preferred_element_type=F32))
                    decs.append(jnp.exp2(acs[Q - 1:Q, h:h + 1]))
                y_sub[:, pair * P2:(pair + 1) * P2] = jnp.where(left, ys[0], ys[1])
                dec = jnp.where(left, decs[0], decs[1])
                state[pair] = dec * prev + jnp.where(left, sts[0], sts[1])

        y = y_sub[...] + dsk_ref[...] * xs_sub[...]
        zf = z_ref[r0:r0 + Q, :].astype(F32)
        y = y * (zf * jax.nn.sigmoid(zf))
        for g in range(SSM_GROUPS):
            yg = y[:, g * gw:(g + 1) * gw]
            ms = jnp.mean(yg * yg, axis=-1, keepdims=True)
            o_ref[r0:r0 + Q, g * gw:(g + 1) * gw] = (yg * lax.rsqrt(ms + EPS)
                                                     * ng_ref[:, g * gw:(g + 1) * gw]).astype(BF16)

    halo[...] = xbc_ref[SSD_SUB * Q - HALO:SSD_SUB * Q, :]


def _ssd(proj, dt_raw, conv_w, conv_b, dt_bias, a_log, dsk, ng, batch, seq):
    T = batch * seq
    Q = SSM_CHUNK
    rows = SSD_SUB * Q
    nc = seq // rows
    vec = lambda r, n: pl.BlockSpec((r, n), lambda b, c: (0, 0))
    return pl.pallas_call(
        _ssd_kernel,
        grid=(batch, nc),
        in_specs=[
            pl.BlockSpec((rows, SSM_XBC), lambda b, c: (b * nc + c, COL_XBC // SSM_XBC)),
            pl.BlockSpec((rows, SSM_D_INNER), lambda b, c: (b * nc + c, COL_Z // SSM_D_INNER)),
            pl.BlockSpec((rows, LANES), lambda b, c: (b * nc + c, 0)),
            vec(SSM_CONV, SSM_XBC), vec(1, SSM_XBC), vec(1, LANES), vec(1, LANES),
            vec(1, SSM_D_INNER), vec(1, SSM_D_INNER),
        ],
        out_specs=pl.BlockSpec((rows, SSM_D_INNER), lambda b, c: (b * nc + c, 0)),
        out_shape=jax.ShapeDtypeStruct((T, SSM_D_INNER), BF16),
        scratch_shapes=[
            pltpu.VMEM((HALO, SSM_XBC), BF16),
            pltpu.VMEM((SSM_HEADS // 2, SSM_STATE, 2 * SSM_HEAD_DIM), F32),
            pltpu.VMEM((SSD_SUB, Q, SSM_D_INNER), F32),
            pltpu.VMEM((SSD_SUB, Q, 2 * SSM_GROUPS * SSM_STATE), BF16),
            pltpu.VMEM((SSD_SUB, Q, SSM_D_INNER), F32),
        ],
        compiler_params=_cparams(("arbitrary", "arbitrary")),
        name="ssd",
    )(proj, proj, dt_raw, conv_w, conv_b, dt_bias, a_log, dsk, ng)


def _merge_kernel(att_ref, ssm_ref, ga_ref, gs_ref, x_ref, wap_ref, wsp_ref, wo_ref, fg_ref,
                  wr_ref, br_ref, x1_ref, h2_ref, idx_ref, gate_ref, cnt_ref, base, *, tm):
    i = pl.program_id(0)

    @pl.when(i == 0)
    def _():
        base[...] = jnp.zeros_like(base)

    pa = jnp.dot(att_ref[...], wap_ref[...], preferred_element_type=F32)
    ps = jnp.dot(ssm_ref[...], wsp_ref[...], preferred_element_type=F32)
    merged = (jax.nn.sigmoid(ga_ref[...].astype(F32)) * pa
              + jax.nn.sigmoid(gs_ref[...].astype(F32)) * ps)
    x1 = x_ref[...] + jnp.dot(merged.astype(BF16), wo_ref[...], preferred_element_type=F32)
    x1_ref[...] = x1
    ms = jnp.mean(x1 * x1, axis=-1, keepdims=True)
    h2 = x1 * lax.rsqrt(ms + EPS) * fg_ref[...]
    _store_rows(h2_ref, _pack_rows(h2))

    logits = jnp.dot(h2.astype(BF16), wr_ref[...], preferred_element_type=F32) + br_ref[...]
    lg = logits.T[:N_EXPERTS, :]
    row_f = lax.broadcasted_iota(jnp.int32, (N_EXPERTS, tm), 0).astype(F32)
    vals, idxs, sels = [], [], []
    for _ in range(TOP_K):
        m = jnp.max(lg, axis=0, keepdims=True)
        idx = jnp.min(jnp.where(lg == m, row_f, float(N_EXPERTS)), axis=0, keepdims=True)
        sel = row_f == idx
        vals.append(m)
        idxs.append(idx)
        sels.append(sel)
        lg = jnp.where(sel, -jnp.inf, lg)
    es = [jnp.exp(v - vals[0]) for v in vals]
    den = es[0] + es[1] + es[2] + es[3]

    multi = jnp.zeros((N_EXPERTS, tm), F32)
    for sel in sels:
        multi = jnp.where(sel, 1.0, multi)
    ri = lax.broadcasted_iota(jnp.int32, (tm, tm), 0)
    ci = lax.broadcasted_iota(jnp.int32, (tm, tm), 1)
    before = jnp.dot(multi.astype(BF16), (ri < ci).astype(BF16), preferred_element_type=F32)
    before = before + base[...]
    ranks = [jnp.sum(jnp.where(sel, before, 0.0), axis=0, keepdims=True) for sel in sels]
    idx_ref[...] = jnp.concatenate(idxs + ranks, axis=0).astype(jnp.int32)
    gate_ref[...] = jnp.concatenate([e / den for e in es] + [jnp.zeros_like(den)] * TOP_K, axis=0)
    base[...] = base[...] + jnp.sum(multi, axis=1, keepdims=True)
    cnt_ref[...] = jnp.broadcast_to(base[...], cnt_ref.shape).astype(jnp.int32)


def _merge(att, ssm, proj, xf, wap, wsp, wo, fg, wr, br):
    T = xf.shape[0]
    tm = min(MERGE_TM, T)
    full = lambda a: pl.BlockSpec(a.shape, lambda i: (0, 0))
    return pl.pallas_call(
        functools.partial(_merge_kernel, tm=tm),
        grid=(T // tm,),
        in_specs=[
            pl.BlockSpec((tm, ATT_WIDTH), lambda i: (i, 0)),
            pl.BlockSpec((tm, SSM_D_INNER), lambda i: (i, 0)),
            pl.BlockSpec((tm, D_MODEL), lambda i: (i, COL_GA // D_MODEL)),
            pl.BlockSpec((tm, D_MODEL), lambda i: (i, COL_GS // D_MODEL)),
            pl.BlockSpec((tm, D_MODEL), lambda i: (i, 0)),
            full(wap), full(wsp), full(wo), full(fg), full(wr), full(br),
        ],
        out_specs=[
            pl.BlockSpec((tm, D_MODEL), lambda i: (i, 0)),
            pl.BlockSpec((tm * ROW_SUB, LANES), lambda i: (i, 0)),
            pl.BlockSpec((2 * TOP_K, tm), lambda i: (0, i)),
            pl.BlockSpec((2 * TOP_K, tm), lambda i: (0, i)),
            pl.BlockSpec((N_EXPERTS, LANES), lambda i: (0, 0)),
        ],
        out_shape=[
            jax.ShapeDtypeStruct((T, D_MODEL), F32),
            jax.ShapeDtypeStruct((T * ROW_SUB, LANES), jnp.uint32),
            jax.ShapeDtypeStruct((2 * TOP_K, T), jnp.int32),
            jax.ShapeDtypeStruct((2 * TOP_K, T), F32),
            jax.ShapeDtypeStruct((N_EXPERTS, LANES), jnp.int32),
        ],
        scratch_shapes=[pltpu.VMEM((N_EXPERTS, 1), F32)],
        compiler_params=_cparams(("arbitrary",)),
        name="merge_router",
    )(att, ssm, proj, proj, xf, wap, wsp, wo, fg, wr, br)


DMA_UNROLL = 8


def _dispatch_kernel(lb_ref, cnt_ref, dest_ref, h_ref, rows_ref, zeros, zsem, sem, *, tt, bm):
    @pl.when(pl.program_id(0) == 0)
    def _():
        zeros[...] = jnp.zeros_like(zeros)
        for wait in (False, True):
            for e in range(N_EXPERTS):
                @pl.when(cnt_ref[e] > 0)
                def _():
                    start = pl.multiple_of(lb_ref[e], bm)
                    cp = pltpu.make_async_copy(zeros, rows_ref.at[pl.ds(start * ROW_SUB, bm * ROW_SUB)], zsem)
                    cp.wait() if wait else cp.start()

    def row_copy(t, k):
        return pltpu.make_async_copy(_row(h_ref, t), _row(rows_ref, dest_ref[t * TOP_K + k]), sem)

    for wait in (False, True):
        def body(g, carry):
            for u in range(DMA_UNROLL):
                for k in range(TOP_K):
                    cp = row_copy(g * DMA_UNROLL + u, k)
                    cp.wait() if wait else cp.start(priority=k % 2)
            return carry
        lax.fori_loop(0, tt // DMA_UNROLL, body, 0)


def _dispatch(last_block, cnt, dest, h2p, n_rows, bm):
    T = h2p.shape[0] // ROW_SUB
    tt = min(DISPATCH_TT, T)
    grid_spec = pltpu.PrefetchScalarGridSpec(
        num_scalar_prefetch=2,
        grid=(T // tt,),
        in_specs=[
            pl.BlockSpec((tt * TOP_K,), lambda i, lb, c: (i,), memory_space=pltpu.SMEM),
            pl.BlockSpec((tt * ROW_SUB, LANES), lambda i, lb, c: (i, 0)),
        ],
        out_specs=pl.BlockSpec(memory_space=pl.ANY),
        scratch_shapes=[pltpu.VMEM((bm * ROW_SUB, LANES), jnp.uint32),
                        pltpu.SemaphoreType.DMA, pltpu.SemaphoreType.DMA],
    )
    return pl.pallas_call(
        functools.partial(_dispatch_kernel, tt=tt, bm=bm),
        grid_spec=grid_spec,
        out_shape=jax.ShapeDtypeStruct((n_rows * ROW_SUB, LANES), jnp.uint32),
        compiler_params=_cparams(("arbitrary",)),
        name="dispatch",
    )(last_block, cnt, dest, h2p)


def _expert_kernel(be_ref, nu_ref, x_ref, wgu_ref, bgu_ref, wd_ref, bd_ref, o_ref, wgu_bf, wd_bf):
    i = pl.program_id(0)

    @pl.when(i < nu_ref[0])
    def _():
        @pl.when((i == 0) | (be_ref[i] != be_ref[jnp.maximum(i - 1, 0)]))
        def _():
            wgu_bf[...] = wgu_ref[0].astype(BF16)
            wd_bf[...] = wd_ref[0].astype(BF16)

        lo, hi = _unpack_rows(_load_rows(x_ref))
        x = jnp.concatenate([lo, hi], axis=1).astype(BF16)
        gu = jnp.dot(x, wgu_bf[...], preferred_element_type=F32) + bgu_ref[0]
        gate = jnp.minimum(gu[:, :D_FF], SWIGLU_LIMIT)
        up = jnp.clip(gu[:, D_FF:], -SWIGLU_LIMIT, SWIGLU_LIMIT)
        glu = gate * jax.nn.sigmoid(SWIGLU_ALPHA * gate)
        act = ((up + 1.0) * glu).astype(BF16)
        _store_rows(o_ref, _pack_rows(jnp.dot(act, wd_bf[...], preferred_element_type=F32) + bd_ref[0]))


def _experts(block_expert, n_used, rows, wgu, bgu, wd, bd, bm):
    n_rows = rows.shape[0] // ROW_SUB
    nb = n_rows // bm
    row_map = lambda i, be, nu: (jnp.minimum(i, nu[0] - 1), 0)
    exp_map = lambda i, be, nu: (be[jnp.minimum(i, nu[0] - 1)], 0, 0)
    grid_spec = pltpu.PrefetchScalarGridSpec(
        num_scalar_prefetch=2,
        grid=(nb,),
        in_specs=[
            pl.BlockSpec((bm * ROW_SUB, LANES), row_map),
            pl.BlockSpec((1, D_MODEL, 2 * D_FF), exp_map),
            pl.BlockSpec((1, 1, 2 * D_FF), exp_map),
            pl.BlockSpec((1, D_FF, D_MODEL), exp_map),
            pl.BlockSpec((1, 1, D_MODEL), exp_map),
        ],
        out_specs=pl.BlockSpec((bm * ROW_SUB, LANES), row_map),
        scratch_shapes=[pltpu.VMEM((D_MODEL, 2 * D_FF), BF16), pltpu.VMEM((D_FF, D_MODEL), BF16)],
    )
    return pl.pallas_call(
        _expert_kernel,
        grid_spec=grid_spec,
        out_shape=jax.ShapeDtypeStruct((n_rows * ROW_SUB, LANES), jnp.uint32),
        compiler_params=_cparams(("arbitrary",)),
        name="experts",
    )(block_expert, n_used, rows, wgu, bgu, wd, bd)


def _combine_kernel(dest_ref, next_ref, x1_ref, gate_ref, y_ref, o_ref, ybuf, sems, *, tc):
    i = pl.program_id(0)
    n = pl.num_programs(0)

    def gather(idx_ref, slot, wait):
        def row_copy(t, k):
            return pltpu.make_async_copy(_row(y_ref, idx_ref[t * TOP_K + k]), _row(ybuf.at[slot, k], t),
                                         sems.at[slot])

        def body(g, carry):
            for u in range(DMA_UNROLL):
                for k in range(TOP_K):
                    cp = row_copy(g * DMA_UNROLL + u, k)
                    cp.wait() if wait else cp.start(priority=k % 2)
            return carry
        lax.fori_loop(0, tc // DMA_UNROLL, body, 0)

    slot = i % 2

    @pl.when(i == 0)
    def _():
        gather(dest_ref, 0, False)

    @pl.when(i + 1 < n)
    def _():
        gather(next_ref, 1 - slot, False)

    gather(dest_ref, slot, True)

    x1 = x1_ref[...]
    acc_lo, acc_hi = x1[:, :PACKED], x1[:, PACKED:]
    g = gate_ref[...].T
    for k in range(TOP_K):
        lo, hi = _unpack_rows(_load_rows(ybuf.at[slot, k]))
        acc_lo = acc_lo + g[:, k:k + 1] * lo
        acc_hi = acc_hi + g[:, k:k + 1] * hi
    o_ref[:, :PACKED] = acc_lo
    o_ref[:, PACKED:] = acc_hi


def _combine(dest, x1, gates, y_rows):
    T = x1.shape[0]
    tc = min(COMBINE_TC, T)
    n = T // tc
    return pl.pallas_call(
        functools.partial(_combine_kernel, tc=tc),
        grid=(n,),
        in_specs=[
            pl.BlockSpec((tc * TOP_K,), lambda i: (i,), memory_space=pltpu.SMEM),
            pl.BlockSpec((tc * TOP_K,), lambda i: (jnp.minimum(i + 1, n - 1),), memory_space=pltpu.SMEM),
            pl.BlockSpec((tc, D_MODEL), lambda i: (i, 0)),
            pl.BlockSpec((2 * TOP_K, tc), lambda i: (0, i)),
            pl.BlockSpec(memory_space=pl.ANY),
        ],
        out_specs=pl.BlockSpec((tc, D_MODEL), lambda i: (i, 0)),
        out_shape=jax.ShapeDtypeStruct((T, D_MODEL), F32),
        scratch_shapes=[pltpu.VMEM((2, TOP_K, tc * ROW_SUB, LANES), jnp.uint32),
                        pltpu.SemaphoreType.DMA((2,))],
        compiler_params=_cparams(("arbitrary",)),
        name="combine",
    )(dest, dest, x1, gates, y_rows)


def _pad_lanes(v):
    return jnp.pad(v, ((0, 0), (0, LANES - v.shape[-1])))


def kernel(x, mix_norm_g, w_in, q_norm_g, k_norm_g, lambda_q1, lambda_k1, lambda_q2, lambda_k2,
           attn_subln_g, conv_w, conv_b, dt_bias, a_log, d_skip, ssm_norm_g, w_attn_proj,
           w_ssm_proj, w_out, ffn_norm_g, w_router, b_router, w_gate_up, b_gate_up, w_down, b_down):
    B, S, D = x.shape
    T = B * S
    xf = x.reshape(T, D)
    layer = 0

    wi = w_in[layer]
    o_z = 3 * ATT_WIDTH
    o_xbc = o_z + SSM_D_INNER
    o_dt = o_xbc + SSM_XBC
    o_ga = o_dt + SSM_HEADS
    w_main = jnp.concatenate([wi[:, o_xbc:o_dt], wi[:, o_z:o_xbc], wi[:, :o_z], wi[:, o_ga:]],
                             axis=1).astype(BF16)
    w_dt = _pad_lanes(wi[:, o_dt:o_ga]).astype(BF16)
    proj, dt_raw = _in_proj(xf, mix_norm_g[layer][None, :], w_main, w_dt)

    half = ATT_HEAD_DIM // 2
    inv = ROPE_THETA ** (-jnp.arange(0, ATT_HEAD_DIM, 2, dtype=F32) / ATT_HEAD_DIM)
    ang = jnp.arange(S, dtype=F32)[:, None] * inv[None, :]
    cos2 = jnp.tile(jnp.cos(ang), (1, LANES // half))
    sin2 = jnp.tile(jnp.concatenate([-jnp.sin(ang), jnp.sin(ang)], axis=1), (1, LANES // ATT_HEAD_DIM))
    qg2 = jnp.tile(q_norm_g[layer], 2)[None, :]
    kg2 = jnp.tile(k_norm_g[layer], 2)[None, :]
    partner = lambda g: jnp.tile(jnp.roll(g, half), 2)[None, :]
    q_scale = ATT_HEAD_DIM ** -0.5 * LOG2E
    att = _attention(proj, cos2 * qg2 * q_scale, sin2 * partner(q_norm_g[layer]) * q_scale,
                     cos2 * kg2, sin2 * partner(k_norm_g[layer]), qg2, kg2,
                     lambda_q1[layer][None, :], lambda_k1[layer][None, :],
                     lambda_q2[layer][None, :], lambda_k2[layer][None, :],
                     attn_subln_g[layer][None, :], B, S)

    ssm = _ssd(proj, dt_raw, conv_w[layer], conv_b[layer][None, :],
               _pad_lanes(dt_bias[layer][None, :]), _pad_lanes(a_log[layer][None, :]),
               jnp.repeat(d_skip[layer], SSM_HEAD_DIM)[None, :], ssm_norm_g[layer][None, :], B, S)

    x1, h2p, idx_rank_t, gates_t, counts = _merge(
        att, ssm, proj, xf, w_attn_proj[layer].astype(BF16), w_ssm_proj[layer].astype(BF16),
        w_out[layer].astype(BF16), ffn_norm_g[layer][None, :],
        _pad_lanes(w_router[layer]).astype(BF16), _pad_lanes(b_router[layer][None, :]))

    bm = EXPERT_BM
    A = T * TOP_K
    n_rows = (A + N_EXPERTS * (bm - 1)) // bm * bm
    cnt = counts[:, 0]
    padded = (cnt + bm - 1) // bm * bm
    pend = jnp.cumsum(padded)
    pstart = pend - padded
    n_used = (pend[-1:] // bm).astype(jnp.int32)
    block_start = jnp.arange(n_rows // bm, dtype=jnp.int32) * bm
    block_expert = jnp.minimum(jnp.sum(block_start[:, None] >= pend[None, :], axis=1),
                               N_EXPERTS - 1).astype(jnp.int32)
    idx_rank = idx_rank_t.T
    onehot = idx_rank[:, :TOP_K, None] == jnp.arange(N_EXPERTS, dtype=jnp.int32)[None, None, :]
    dest = (jnp.sum(jnp.where(onehot, pstart[None, None, :], 0), axis=-1)
            + idx_rank[:, TOP_K:]).reshape(A).astype(jnp.int32)

    rows = _dispatch((pend - bm).astype(jnp.int32), cnt.astype(jnp.int32), dest, h2p, n_rows, bm)
    y_rows = _experts(block_expert, n_used, rows, w_gate_up[layer], b_gate_up[layer][:, None, :],
                      w_down[layer], b_down[layer][:, None, :], bm)
    out = _combine(dest, x1, gates_t, y_rows)
    return out.reshape(B, S, D)
```

```python
import functools
import math

import jax
import jax.numpy as jnp
from jax import lax
from jax.experimental import pallas as pl
from jax.experimental.pallas import tpu as pltpu

F32 = jnp.float32
BF16 = jnp.bfloat16

D_MODEL = 1024
EPS = 1e-6
ATT_HEADS = 8
ATT_HEAD_DIM = 64
ATT_V_DIM = 2 * ATT_HEAD_DIM
ATT_WIDTH = ATT_HEADS * ATT_V_DIM
ROPE_THETA = 10000.0
SSM_D_INNER = 2 * D_MODEL
SSM_HEAD_DIM = 64
SSM_HEADS = SSM_D_INNER // SSM_HEAD_DIM
SSM_GROUPS = 8
SSM_STATE = 128
SSM_CONV = 4
SSM_CHUNK = 128
SSM_XBC = SSM_D_INNER + 2 * SSM_GROUPS * SSM_STATE
N_EXPERTS = 32
TOP_K = 4
D_FF = D_MODEL
SWIGLU_LIMIT = 7.0
SWIGLU_ALPHA = 1.702
LAM_INIT = 0.8 - 0.6 * math.exp(-0.3 * 0)

LANES = 128
HALO = 16

COL_XBC = 0
COL_Z = SSM_XBC
COL_Q = COL_Z + SSM_D_INNER
COL_K = COL_Q + ATT_WIDTH
COL_V = COL_K + ATT_WIDTH
COL_GA = COL_V + ATT_WIDTH
COL_GS = COL_GA + D_MODEL
PROJ_COLS = COL_GS + D_MODEL

VMEM_LIMIT = 56 * 1024 * 1024

PROJ_TM = 2048
PROJ_TN = 1024
ATT_TQ = 512
MERGE_TM = 512
DISPATCH_TT = 2048
EXPERT_BM = 1024
COMBINE_TC = 256


def _cparams(sem):
    return pltpu.CompilerParams(dimension_semantics=sem, vmem_limit_bytes=VMEM_LIMIT)


PACKED = D_MODEL // 2


def _pack_rows(x):
    lo = lax.bitcast_convert_type(x[:, :PACKED].astype(BF16).astype(F32), jnp.uint32)
    hi = lax.bitcast_convert_type(x[:, PACKED:].astype(BF16).astype(F32), jnp.uint32)
    return hi | (lo >> 16)


def _unpack_rows(p):
    lo = lax.bitcast_convert_type(p << 16, F32)
    hi = lax.bitcast_convert_type(p & jnp.uint32(0xFFFF0000), F32)
    return lo, hi


ROW_SUB = PACKED // LANES


def _store_rows(ref, packed):
    for s in range(ROW_SUB):
        ref[pl.ds(s, packed.shape[0], stride=ROW_SUB), :] = packed[:, s * LANES:(s + 1) * LANES]


def _load_rows(ref):
    n = ref.shape[0] // ROW_SUB
    return jnp.concatenate([ref[pl.ds(s, n, stride=ROW_SUB), :] for s in range(ROW_SUB)], axis=1)


def _row(ref, r):
    return ref.at[pl.ds(pl.multiple_of(r * ROW_SUB, ROW_SUB), ROW_SUB)]


def _in_proj_kernel(x_ref, g_ref, w_ref, wdt_ref, o_ref, dt_ref, h_scr):
    @pl.when(pl.program_id(1) == 0)
    def _():
        x = x_ref[...]
        ms = jnp.mean(x * x, axis=-1, keepdims=True)
        hb = (x * lax.rsqrt(ms + EPS) * g_ref[...]).astype(BF16)
        h_scr[...] = hb
        dt_ref[...] = jnp.dot(hb, wdt_ref[...], preferred_element_type=F32)

    o_ref[...] = jnp.dot(h_scr[...], w_ref[...], preferred_element_type=F32).astype(BF16)


def _in_proj(xf, g, w_main, w_dt):
    T = xf.shape[0]
    tm = min(PROJ_TM, T)
    tn = PROJ_TN
    return pl.pallas_call(
        _in_proj_kernel,
        grid=(T // tm, PROJ_COLS // tn),
        in_specs=[
            pl.BlockSpec((tm, D_MODEL), lambda i, j: (i, 0)),
            pl.BlockSpec((1, D_MODEL), lambda i, j: (0, 0)),
            pl.BlockSpec((D_MODEL, tn), lambda i, j: (0, j)),
            pl.BlockSpec((D_MODEL, LANES), lambda i, j: (0, 0)),
        ],
        out_specs=[
            pl.BlockSpec((tm, tn), lambda i, j: (i, j)),
            pl.BlockSpec((tm, LANES), lambda i, j: (i, 0)),
        ],
        out_shape=[
            jax.ShapeDtypeStruct((T, PROJ_COLS), BF16),
            jax.ShapeDtypeStruct((T, LANES), F32),
        ],
        scratch_shapes=[pltpu.VMEM((tm, D_MODEL), BF16)],
        compiler_params=_cparams(("arbitrary", "arbitrary")),
        name="in_proj",
    )(xf, g, w_main, w_dt)


LOG2E = 1.4426950408889634
SHIFT_LIMIT = 57.0
BOUND_MARGIN = 1.02


def _attn_kernel(q_ref, k_ref, v_ref, qc_ref, qs_ref, kc_ref, ks_ref, qg_ref, kg_ref,
                 lq1_ref, lk1_ref, lq2_ref, lk2_ref, sg_ref, o_ref, k_scr, v_scr, q1_scr, q2_scr, *, tq, seq):
    nq = seq // tq
    lane = lax.broadcasted_iota(jnp.int32, (1, LANES), 1)
    first = lane < ATT_HEAD_DIM
    one_col = jnp.where(lane == 0, 1.0, 0.0)

    ri = lax.broadcasted_iota(jnp.int32, (LANES, LANES), 0)
    ci = lax.broadcasted_iota(jnp.int32, (LANES, LANES), 1)
    same_comp = (ri // ATT_HEAD_DIM == ci // ATT_HEAD_DIM).astype(BF16)
    swap_half = ((ri // ATT_HEAD_DIM == ci // ATT_HEAD_DIM)
                 & ((ri - ci == ATT_HEAD_DIM // 2) | (ci - ri == ATT_HEAD_DIM // 2))).astype(BF16)

    def norm_rope(xb, gcos, gsin):
        xf = xb.astype(F32)
        sq = xf * xf
        hi = sq.astype(BF16)
        lo = (sq - hi.astype(F32)).astype(BF16)
        ms = jnp.dot(jnp.concatenate([hi, lo], axis=1), jnp.concatenate([same_comp, same_comp], axis=0),
                     preferred_element_type=F32) * (1.0 / ATT_HEAD_DIM)
        xr = jnp.dot(xb, swap_half, preferred_element_type=F32)
        return lax.rsqrt(ms + EPS) * (xf * gcos + xr * gsin)

    ub = (ATT_HEAD_DIM * ATT_HEAD_DIM ** -0.5 * LOG2E * BOUND_MARGIN
          * jnp.max(jnp.abs(qg_ref[...])) * jnp.max(jnp.abs(kg_ref[...])))
    safe = ub <= SHIFT_LIMIT

    ones = jnp.broadcast_to(one_col, (tq, LANES)).astype(BF16)
    shift = jnp.broadcast_to(jnp.where(lane == 0, -ub, 0.0), (tq, LANES)).astype(BF16)
    for r in range(0, seq, tq):
        rows = slice(r, r + tq)
        k_scr[rows, :LANES] = norm_rope(k_ref[rows, :], kc_ref[rows, :], ks_ref[rows, :]).astype(BF16)
        k_scr[rows, LANES:] = ones
        v_scr[rows, :LANES] = v_ref[rows, :]
        v_scr[rows, LANES:] = ones
        qb = norm_rope(q_ref[rows, :], qc_ref[rows, :], qs_ref[rows, :]).astype(BF16)
        q1_scr[rows, :LANES] = jnp.where(first, qb, jnp.zeros_like(qb))
        q1_scr[rows, LANES:] = shift
        q2_scr[rows, :LANES] = jnp.where(first, jnp.zeros_like(qb), qb)
        q2_scr[rows, LANES:] = shift

    keep = (lax.broadcasted_iota(jnp.int32, (tq, tq), 1) <= lax.broadcasted_iota(jnp.int32, (tq, tq), 0))
    nt = (((1,), (1,)), ((), ()))

    def finish(o1, o2, rows0):
        lam = (jnp.exp(jnp.sum(lq1_ref[...] * lk1_ref[...], axis=-1, keepdims=True))
               - jnp.exp(jnp.sum(lq2_ref[...] * lk2_ref[...], axis=-1, keepdims=True)) + LAM_INIT)
        o = o1 - lam * o2
        ms = jnp.mean(o * o, axis=-1, keepdims=True)
        o_ref[pl.ds(rows0, tq), :] = (o * lax.rsqrt(ms + EPS) * sg_ref[...] * (1.0 - LAM_INIT)).astype(BF16)

    @pl.when(safe)
    def _():
        def block(q_scr, qrow0, krow0, rows, cols, mask):
            t = lax.dot_general(q_scr[pl.ds(qrow0, rows), :], k_scr[pl.ds(krow0, cols), :], nt,
                                preferred_element_type=F32)
            if mask is not None:
                t = jnp.where(mask, t, -jnp.inf)
            return jnp.dot(jnp.exp2(t).astype(BF16), v_scr[pl.ds(krow0, cols), :], preferred_element_type=F32)

        def tile(qrow0, krow0, masked):
            if not masked:
                return (block(q1_scr, qrow0, krow0, tq, tq, None), block(q2_scr, qrow0, krow0, tq, tq, None))
            h = tq // 2
            outs = []
            for q_scr in (q1_scr, q2_scr):
                left_cols = block(q_scr, qrow0, krow0, tq, h, keep[:, :h])
                corner = block(q_scr, qrow0 + h, krow0 + h, h, h, keep[:h, :h])
                outs.append(jnp.concatenate([left_cols[:h], left_cols[h:] + corner], axis=0))
            return tuple(outs)

        def normalised(acc):
            return acc[:, :LANES] / acc[:, LANES:LANES + 1]

        for qi in range(nq):
            a1, a2 = tile(qi * tq, qi * tq, True)
            for j in range(qi):
                d1, d2 = tile(qi * tq, j * tq, False)
                a1, a2 = a1 + d1, a2 + d2
            finish(normalised(a1), normalised(a2), qi * tq)

    @pl.when(jnp.logical_not(safe))
    def _():
        def update(t, vt, m, l, acc):
            m_new = jnp.maximum(m, jnp.max(t, axis=-1, keepdims=True))
            alpha = jnp.exp2(m - m_new)
            pr = jnp.exp2(t - m_new)
            l_new = alpha * l + jnp.sum(pr, axis=-1, keepdims=True)
            acc_new = alpha * acc + jnp.dot(pr.astype(BF16), vt, preferred_element_type=F32)
            return m_new, l_new, acc_new

        def online(rows0, n_off):
            q1 = q1_scr[pl.ds(rows0, tq), :LANES]
            q2 = q2_scr[pl.ds(rows0, tq), :LANES]

            def step(krow0, carry, masked):
                m1, l1, c1, m2, l2, c2 = carry
                kt = k_scr[pl.ds(krow0, tq), :LANES]
                vt = v_scr[pl.ds(krow0, tq), :LANES]
                t1 = lax.dot_general(q1, kt, nt, preferred_element_type=F32)
                t2 = lax.dot_general(q2, kt, nt, preferred_element_type=F32)
                if masked:
                    t1 = jnp.where(keep, t1, -jnp.inf)
                    t2 = jnp.where(keep, t2, -jnp.inf)
                m1, l1, c1 = update(t1, vt, m1, l1, c1)
                m2, l2, c2 = update(t2, vt, m2, l2, c2)
                return m1, l1, c1, m2, l2, c2

            neg = jnp.full((tq, 1), -jnp.inf, F32)
            zero1 = jnp.zeros((tq, 1), F32)
            zacc = jnp.zeros((tq, ATT_V_DIM), F32)
            carry = step(rows0, (neg, zero1, zacc, neg, zero1, zacc), True)
            m1, l1, c1, m2, l2, c2 = lax.fori_loop(
                0, n_off, lambda j, c: step(pl.multiple_of(j * tq, tq), c, False), carry)
            finish(c1 / l1, c2 / l2, rows0)

        for qi in range(nq):
            online(qi * tq, qi)


def _attention(proj, qcos, qsin, kcos, ksin, qg2, kg2, lq1, lk1, lq2, lk2, sg, batch, seq):
    T = batch * seq
    tq = min(ATT_TQ, seq)
    nq = seq // tq
    vec = lambda n: pl.BlockSpec((1, n), lambda b, h: (0, 0))
    table = pl.BlockSpec((seq, LANES), lambda b, h: (0, 0))
    head = lambda col: pl.BlockSpec((seq, LANES), lambda b, h: (b, col // LANES + h))
    return pl.pallas_call(
        functools.partial(_attn_kernel, tq=tq, seq=seq),
        grid=(batch, ATT_HEADS),
        in_specs=[
            head(COL_Q), head(COL_K), head(COL_V),
            table, table, table, table,
            vec(LANES), vec(LANES),
            vec(ATT_HEAD_DIM), vec(ATT_HEAD_DIM), vec(ATT_HEAD_DIM), vec(ATT_HEAD_DIM),
            vec(LANES),
        ],
        out_specs=pl.BlockSpec((seq, LANES), lambda b, h: (b, h)),
        out_shape=jax.ShapeDtypeStruct((T, ATT_WIDTH), BF16),
        scratch_shapes=[pltpu.VMEM((seq, 2 * LANES), BF16) for _ in range(4)],
        compiler_params=_cparams(("arbitrary", "arbitrary")),
        name="attention",
    )(proj, proj, proj, qcos, qsin, kcos, ksin, qg2, kg2, lq1, lk1, lq2, lk2, sg)


SSD_SUB = 2


def _ssd_kernel(xbc_ref, z_ref, dt_ref, cw_ref, cb_ref, dtb_ref, alog_ref, dsk_ref, ng_ref,
                o_ref, halo, state, xs_scr, bc_scr, y_scr):
    Q = SSM_CHUNK
    N = SSM_STATE
    P2 = 2 * SSM_HEAD_DIM

    @pl.when(pl.program_id(1) == 0)
    def _():
        halo[...] = jnp.zeros_like(halo)
        state[...] = jnp.zeros_like(state)

    taps = SSM_CONV - 1
    sr = lax.broadcasted_iota(jnp.int32, (taps * Q, HALO + Q), 0)
    sc = lax.broadcasted_iota(jnp.int32, (taps * Q, HALO + Q), 1)
    shift = (sc == (sr % Q) + HALO - taps + sr // Q).astype(BF16)
    ri = lax.broadcasted_iota(jnp.int32, (Q, Q), 0)
    ci = lax.broadcasted_iota(jnp.int32, (Q, Q), 1)
    tri = ri >= ci
    lane = lax.broadcasted_iota(jnp.int32, (1, P2), 1)
    left = lane < SSM_HEAD_DIM
    gw = SSM_D_INNER // SSM_GROUPS

    for sub in range(SSD_SUB):
        r0 = sub * Q
        xs_sub, bc_sub, y_sub = xs_scr.at[sub], bc_scr.at[sub], y_scr.at[sub]

        cwid = 512
        for c0 in range(0, SSM_XBC, cwid):
            cur = xbc_ref[r0:r0 + Q, c0:c0 + cwid]
            hist = halo[:, c0:c0 + cwid] if sub == 0 else xbc_ref[r0 - HALO:r0, c0:c0 + cwid]
            ext = jnp.concatenate([hist, cur], axis=0)
            shifted = jnp.dot(shift, ext, preferred_element_type=F32)
            acc = cb_ref[:, c0:c0 + cwid] + cw_ref[taps:SSM_CONV, c0:c0 + cwid] * cur.astype(F32)
            for k in range(taps):
                acc = acc + cw_ref[k:k + 1, c0:c0 + cwid] * shifted[k * Q:(k + 1) * Q, :]
            act = acc * jax.nn.sigmoid(acc)
            if c0 < SSM_D_INNER:
                xs_sub[:, c0:c0 + cwid] = act
            else:
                bc_sub[:, c0 - SSM_D_INNER:c0 - SSM_D_INNER + cwid] = act.astype(BF16)

        dt = jax.nn.softplus(dt_ref[r0:r0 + Q, :] + dtb_ref[...])
        da = dt * (-jnp.exp(alog_ref[...]) * LOG2E)
        acs = jnp.dot(tri.astype(F32), da, preferred_element_type=F32,
                      precision=lax.Precision.HIGHEST)
        acs_t = acs.T
        dt_t = dt.T
        w_t = dt_t * jnp.exp2(acs_t[:, Q - 1:Q] - acs_t)

        for g in range(SSM_GROUPS):
            bm = bc_sub[:, g * N:(g + 1) * N]
            cm = bc_sub[:, SSM_GROUPS * N + g * N:SSM_GROUPS * N + (g + 1) * N]
            cb = lax.dot_general(cm, bm, (((1,), (1,)), ((), ())), preferred_element_type=F32)
            bm_t = bm.astype(F32).T
            cm_f = cm.astype(F32)
            for pr in range(2):
                pair = 2 * g + pr
                xs_pair = xs_sub[:, pair * P2:(pair + 1) * P2].astype(BF16)
                prev = state[pair]
                rhs = jnp.concatenate([xs_pair, prev.astype(BF16)], axis=0)
                ys, sts, decs = [], [], []
                for r in range(2):
                    h = 2 * pair + r
                    a_col = acs[:, h:h + 1]
                    seg = a_col - acs_t[h:h + 1, :]
                    decay = jnp.exp2(jnp.where(tri, seg, -jnp.inf))
                    m_h = cb * decay * dt_t[h:h + 1, :]
                    e_h = cm_f * jnp.exp2(a_col)
                    lhs = jnp.concatenate([m_h, e_h], axis=1).astype(BF16)
                    ys.append(jnp.dot(lhs, rhs, preferred_element_type=F32))
                    sts.append(jnp.dot((bm_t * w_t[h:h + 1, :]).astype(BF16), xs_pair,
                                       preferred_element_type=F32))
                    decs.append(jnp.exp2(acs[Q - 1:Q, h:h + 1]))
                y_sub[:, pair * P2:(pair + 1) * P2] = jnp.where(left, ys[0], ys[1])
                dec = jnp.where(left, decs[0], decs[1])
                state[pair] = dec * prev + jnp.where(left, sts[0], sts[1])

        y = y_sub[...] + dsk_ref[...] * xs_sub[...]
        zf = z_ref[r0:r0 + Q, :].astype(F32)
        y = y * (zf * jax.nn.sigmoid(zf))
        for g in range(SSM_GROUPS):
            yg = y[:, g * gw:(g + 1) * gw]
            ms = jnp.mean(yg * yg, axis=-1, keepdims=True)
            o_ref[r0:r0 + Q, g * gw:(g + 1) * gw] = (yg * lax.rsqrt(ms + EPS)
                                                     * ng_ref[:, g * gw:(g + 1) * gw]).astype(BF16)

    halo[...] = xbc_ref[SSD_SUB * Q - HALO:SSD_SUB * Q, :]


def _ssd(proj, dt_raw, conv_w, conv_b, dt_bias, a_log, dsk, ng, batch, seq):
    T = batch * seq
    Q = SSM_CHUNK
    rows = SSD_SUB * Q
    nc = seq // rows
    vec = lambda r, n: pl.BlockSpec((r, n), lambda b, c: (0, 0))
    return pl.pallas_call(
        _ssd_kernel,
        grid=(batch, nc),
        in_specs=[
            pl.BlockSpec((rows, SSM_XBC), lambda b, c: (b * nc + c, COL_XBC // SSM_XBC)),
            pl.BlockSpec((rows, SSM_D_INNER), lambda b, c: (b * nc + c, COL_Z // SSM_D_INNER)),
            pl.BlockSpec((rows, LANES), lambda b, c: (b * nc + c, 0)),
            vec(SSM_CONV, SSM_XBC), vec(1, SSM_XBC), vec(1, LANES), vec(1, LANES),
            vec(1, SSM_D_INNER), vec(1, SSM_D_INNER),
        ],
        out_specs=pl.BlockSpec((rows, SSM_D_INNER), lambda b, c: (b * nc + c, 0)),
        out_shape=jax.ShapeDtypeStruct((T, SSM_D_INNER), BF16),
        scratch_shapes=[
            pltpu.VMEM((HALO, SSM_XBC), BF16),
            pltpu.VMEM((SSM_HEADS // 2, SSM_STATE, 2 * SSM_HEAD_DIM), F32),
            pltpu.VMEM((SSD_SUB, Q, SSM_D_INNER), F32),
            pltpu.VMEM((SSD_SUB, Q, 2 * SSM_GROUPS * SSM_STATE), BF16),
            pltpu.VMEM((SSD_SUB, Q, SSM_D_INNER), F32),
        ],
        compiler_params=_cparams(("arbitrary", "arbitrary")),
        name="ssd",
    )(proj, proj, dt_raw, conv_w, conv_b, dt_bias, a_log, dsk, ng)


def _merge_kernel(att_ref, ssm_ref, ga_ref, gs_ref, x_ref, wap_ref, wsp_ref, wo_ref, fg_ref,
                  wr_ref, br_ref, x1_ref, h2_ref, idx_ref, gate_ref, cnt_ref, base, *, tm):
    i = pl.program_id(0)

    @pl.when(i == 0)
    def _():
        base[...] = jnp.zeros_like(base)

    pa = jnp.dot(att_ref[...], wap_ref[...], preferred_element_type=F32)
    ps = jnp.dot(ssm_ref[...], wsp_ref[...], preferred_element_type=F32)
    merged = (jax.nn.sigmoid(ga_ref[...].astype(F32)) * pa
              + jax.nn.sigmoid(gs_ref[...].astype(F32)) * ps)
    x1 = x_ref[...] + jnp.dot(merged.astype(BF16), wo_ref[...], preferred_element_type=F32)
    x1_ref[...] = x1
    ms = jnp.mean(x1 * x1, axis=-1, keepdims=True)
    h2 = x1 * lax.rsqrt(ms + EPS) * fg_ref[...]
    _store_rows(h2_ref, _pack_rows(h2))

    logits = jnp.dot(h2.astype(BF16), wr_ref[...], preferred_element_type=F32) + br_ref[...]
    lg = logits.T[:N_EXPERTS, :]
    row_f = lax.broadcasted_iota(jnp.int32, (N_EXPERTS, tm), 0).astype(F32)
    vals, idxs, sels = [], [], []
    for _ in range(TOP_K):
        m = jnp.max(lg, axis=0, keepdims=True)
        idx = jnp.min(jnp.where(lg == m, row_f, float(N_EXPERTS)), axis=0, keepdims=True)
        sel = row_f == idx
        vals.append(m)
        idxs.append(idx)
        sels.append(sel)
        lg = jnp.where(sel, -jnp.inf, lg)
    es = [jnp.exp(v - vals[0]) for v in vals]
    den = es[0] + es[1] + es[2] + es[3]

    multi = jnp.zeros((N_EXPERTS, tm), F32)
    for sel in sels:
        multi = jnp.where(sel, 1.0, multi)
    ri = lax.broadcasted_iota(jnp.int32, (tm, tm), 0)
    ci = lax.broadcasted_iota(jnp.int32, (tm, tm), 1)
    before = jnp.dot(multi.astype(BF16), (ri < ci).astype(BF16), preferred_element_type=F32)
    before = before + base[...]
    ranks = [jnp.sum(jnp.where(sel, before, 0.0), axis=0, keepdims=True) for sel in sels]
    idx_ref[...] = jnp.concatenate(idxs + ranks, axis=0).astype(jnp.int32)
    gate_ref[...] = jnp.concatenate([e / den for e in es] + [jnp.zeros_like(den)] * TOP_K, axis=0)
    base[...] = base[...] + jnp.sum(multi, axis=1, keepdims=True)
    cnt_ref[...] = jnp.broadcast_to(base[...], cnt_ref.shape).astype(jnp.int32)


def _merge(att, ssm, proj, xf, wap, wsp, wo, fg, wr, br):
    T = xf.shape[0]
    tm = min(MERGE_TM, T)
    full = lambda a: pl.BlockSpec(a.shape, lambda i: (0, 0))
    return pl.pallas_call(
        functools.partial(_merge_kernel, tm=tm),
        grid=(T // tm,),
        in_specs=[
            pl.BlockSpec((tm, ATT_WIDTH), lambda i: (i, 0)),
            pl.BlockSpec((tm, SSM_D_INNER), lambda i: (i, 0)),
            pl.BlockSpec((tm, D_MODEL), lambda i: (i, COL_GA // D_MODEL)),
            pl.BlockSpec((tm, D_MODEL), lambda i: (i, COL_GS // D_MODEL)),
            pl.BlockSpec((tm, D_MODEL), lambda i: (i, 0)),
            full(wap), full(wsp), full(wo), full(fg), full(wr), full(br),
        ],
        out_specs=[
            pl.BlockSpec((tm, D_MODEL), lambda i: (i, 0)),
            pl.BlockSpec((tm * ROW_SUB, LANES), lambda i: (i, 0)),
            pl.BlockSpec((2 * TOP_K, tm), lambda i: (0, i)),
            pl.BlockSpec((2 * TOP_K, tm), lambda i: (0, i)),
            pl.BlockSpec((N_EXPERTS, LANES), lambda i: (0, 0)),
        ],
        out_shape=[
            jax.ShapeDtypeStruct((T, D_MODEL), F32),
            jax.ShapeDtypeStruct((T * ROW_SUB, LANES), jnp.uint32),
            jax.ShapeDtypeStruct((2 * TOP_K, T), jnp.int32),
            jax.ShapeDtypeStruct((2 * TOP_K, T), F32),
            jax.ShapeDtypeStruct((N_EXPERTS, LANES), jnp.int32),
        ],
        scratch_shapes=[pltpu.VMEM((N_EXPERTS, 1), F32)],
        compiler_params=_cparams(("arbitrary",)),
        name="merge_router",
    )(att, ssm, proj, proj, xf, wap, wsp, wo, fg, wr, br)


DMA_UNROLL = 8


def _dispatch_kernel(lb_ref, cnt_ref, dest_ref, h_ref, rows_ref, zeros, zsem, sem, *, tt, bm):
    @pl.when(pl.program_id(0) == 0)
    def _():
        zeros[...] = jnp.zeros_like(zeros)
        for wait in (False, True):
            for e in range(N_EXPERTS):
                @pl.when(cnt_ref[e] > 0)
                def _():
                    start = pl.multiple_of(lb_ref[e], bm)
                    cp = pltpu.make_async_copy(zeros, rows_ref.at[pl.ds(start * ROW_SUB, bm * ROW_SUB)], zsem)
                    cp.wait() if wait else cp.start()

    def row_copy(t, k):
        return pltpu.make_async_copy(_row(h_ref, t), _row(rows_ref, dest_ref[t * TOP_K + k]), sem)

    for wait in (False, True):
        def body(g, carry):
            for u in range(DMA_UNROLL):
                for k in range(TOP_K):
                    cp = row_copy(g * DMA_UNROLL + u, k)
                    cp.wait() if wait else cp.start(priority=k % 2)
            return carry
        lax.fori_loop(0, tt // DMA_UNROLL, body, 0)


def _dispatch(last_block, cnt, dest, h2p, n_rows, bm):
    T = h2p.shape[0] // ROW_SUB
    tt = min(DISPATCH_TT, T)
    grid_spec = pltpu.PrefetchScalarGridSpec(
        num_scalar_prefetch=2,
        grid=(T // tt,),
        in_specs=[
            pl.BlockSpec((tt * TOP_K,), lambda i, lb, c: (i,), memory_space=pltpu.SMEM),
            pl.BlockSpec((tt * ROW_SUB, LANES), lambda i, lb, c: (i, 0)),
        ],
        out_specs=pl.BlockSpec(memory_space=pl.ANY),
        scratch_shapes=[pltpu.VMEM((bm * ROW_SUB, LANES), jnp.uint32),
                        pltpu.SemaphoreType.DMA, pltpu.SemaphoreType.DMA],
    )
    return pl.pallas_call(
        functools.partial(_dispatch_kernel, tt=tt, bm=bm),
        grid_spec=grid_spec,
        out_shape=jax.ShapeDtypeStruct((n_rows * ROW_SUB, LANES), jnp.uint32),
        compiler_params=_cparams(("arbitrary",)),
        name="dispatch",
    )(last_block, cnt, dest, h2p)


def _expert_kernel(be_ref, nu_ref, x_ref, wgu_ref, bgu_ref, wd_ref, bd_ref, o_ref, wgu_bf, wd_bf):
    i = pl.program_id(0)

    @pl.when(i < nu_ref[0])
    def _():
        @pl.when((i == 0) | (be_ref[i] != be_ref[jnp.maximum(i - 1, 0)]))
        def _():
            wgu_bf[...] = wgu_ref[0].astype(BF16)
            wd_bf[...] = wd_ref[0].astype(BF16)

        lo, hi = _unpack_rows(_load_rows(x_ref))
        x = jnp.concatenate([lo, hi], axis=1).astype(BF16)
        gu = jnp.dot(x, wgu_bf[...], preferred_element_type=F32) + bgu_ref[0]
        gate = jnp.minimum(gu[:, :D_FF], SWIGLU_LIMIT)
        up = jnp.clip(gu[:, D_FF:], -SWIGLU_LIMIT, SWIGLU_LIMIT)
        glu = gate * jax.nn.sigmoid(SWIGLU_ALPHA * gate)
        act = ((up + 1.0) * glu).astype(BF16)
        _store_rows(o_ref, _pack_rows(jnp.dot(act, wd_bf[...], preferred_element_type=F32) + bd_ref[0]))


def _experts(block_expert, n_used, rows, wgu, bgu, wd, bd, bm):
    n_rows = rows.shape[0] // ROW_SUB
    nb = n_rows // bm
    row_map = lambda i, be, nu: (jnp.minimum(i, nu[0] - 1), 0)
    exp_map = lambda i, be, nu: (be[jnp.minimum(i, nu[0] - 1)], 0, 0)
    grid_spec = pltpu.PrefetchScalarGridSpec(
        num_scalar_prefetch=2,
        grid=(nb,),
        in_specs=[
            pl.BlockSpec((bm * ROW_SUB, LANES), row_map),
            pl.BlockSpec((1, D_MODEL, 2 * D_FF), exp_map),
            pl.BlockSpec((1, 1, 2 * D_FF), exp_map),
            pl.BlockSpec((1, D_FF, D_MODEL), exp_map),
            pl.BlockSpec((1, 1, D_MODEL), exp_map),
        ],
        out_specs=pl.BlockSpec((bm * ROW_SUB, LANES), row_map),
        scratch_shapes=[pltpu.VMEM((D_MODEL, 2 * D_FF), BF16), pltpu.VMEM((D_FF, D_MODEL), BF16)],
    )
    return pl.pallas_call(
        _expert_kernel,
        grid_spec=grid_spec,
        out_shape=jax.ShapeDtypeStruct((n_rows * ROW_SUB, LANES), jnp.uint32),
        compiler_params=_cparams(("arbitrary",)),
        name="experts",
    )(block_expert, n_used, rows, wgu, bgu, wd, bd)


def _combine_kernel(dest_ref, next_ref, x1_ref, gate_ref, y_ref, o_ref, ybuf, sems, *, tc):
    i = pl.program_id(0)
    n = pl.num_programs(0)

    def gather(idx_ref, slot, wait):
        def row_copy(t, k):
            return pltpu.make_async_copy(_row(y_ref, idx_ref[t * TOP_K + k]), _row(ybuf.at[slot, k], t),
                                         sems.at[slot])

        def body(g, carry):
            for u in range(DMA_UNROLL):
                for k in range(TOP_K):
                    cp = row_copy(g * DMA_UNROLL + u, k)
                    cp.wait() if wait else cp.start(priority=k % 2)
            return carry
        lax.fori_loop(0, tc // DMA_UNROLL, body, 0)

    slot = i % 2

    @pl.when(i == 0)
    def _():
        gather(dest_ref, 0, False)

    @pl.when(i + 1 < n)
    def _():
        gather(next_ref, 1 - slot, False)

    gather(dest_ref, slot, True)

    x1 = x1_ref[...]
    acc_lo, acc_hi = x1[:, :PACKED], x1[:, PACKED:]
    g = gate_ref[...].T
    for k in range(TOP_K):
        lo, hi = _unpack_rows(_load_rows(ybuf.at[slot, k]))
        acc_lo = acc_lo + g[:, k:k + 1] * lo
        acc_hi = acc_hi + g[:, k:k + 1] * hi
    o_ref[:, :PACKED] = acc_lo
    o_ref[:, PACKED:] = acc_hi


def _combine(dest, x1, gates, y_rows):
    T = x1.shape[0]
    tc = min(COMBINE_TC, T)
    n = T // tc
    return pl.pallas_call(
        functools.partial(_combine_kernel, tc=tc),
        grid=(n,),
        in_specs=[
            pl.BlockSpec((tc * TOP_K,), lambda i: (i,), memory_space=pltpu.SMEM),
            pl.BlockSpec((tc * TOP_K,), lambda i: (jnp.minimum(i + 1, n - 1),), memory_space=pltpu.SMEM),
            pl.BlockSpec((tc, D_MODEL), lambda i: (i, 0)),
            pl.BlockSpec((2 * TOP_K, tc), lambda i: (0, i)),
            pl.BlockSpec(memory_space=pl.ANY),
        ],
        out_specs=pl.BlockSpec((tc, D_MODEL), lambda i: (i, 0)),
        out_shape=jax.ShapeDtypeStruct((T, D_MODEL), F32),
        scratch_shapes=[pltpu.VMEM((2, TOP_K, tc * ROW_SUB, LANES), jnp.uint32),
                        pltpu.SemaphoreType.DMA((2,))],
        compiler_params=_cparams(("arbitrary",)),
        name="combine",
    )(dest, dest, x1, gates, y_rows)


def _pad_lanes(v):
    return jnp.pad(v, ((0, 0), (0, LANES - v.shape[-1])))


def kernel(x, mix_norm_g, w_in, q_norm_g, k_norm_g, lambda_q1, lambda_k1, lambda_q2, lambda_k2,
           attn_subln_g, conv_w, conv_b, dt_bias, a_log, d_skip, ssm_norm_g, w_attn_proj,
           w_ssm_proj, w_out, ffn_norm_g, w_router, b_router, w_gate_up, b_gate_up, w_down, b_down):
    B, S, D = x.shape
    T = B * S
    xf = x.reshape(T, D)
    layer = 0

    wi = w_in[layer]
    o_z = 3 * ATT_WIDTH
    o_xbc = o_z + SSM_D_INNER
    o_dt = o_xbc + SSM_XBC
    o_ga = o_dt + SSM_HEADS
    w_main = jnp.concatenate([wi[:, o_xbc:o_dt], wi[:, o_z:o_xbc], wi[:, :o_z], wi[:, o_ga:]],
                             axis=1).astype(BF16)
    w_dt = _pad_lanes(wi[:, o_dt:o_ga]).astype(BF16)
    proj, dt_raw = _in_proj(xf, mix_norm_g[layer][None, :], w_main, w_dt)

    half = ATT_HEAD_DIM // 2
    inv = ROPE_THETA ** (-jnp.arange(0, ATT_HEAD_DIM, 2, dtype=F32) / ATT_HEAD_DIM)
    ang = jnp.arange(S, dtype=F32)[:, None] * inv[None, :]
    cos2 = jnp.tile(jnp.cos(ang), (1, LANES // half))
    sin2 = jnp.tile(jnp.concatenate([-jnp.sin(ang), jnp.sin(ang)], axis=1), (1, LANES // ATT_HEAD_DIM))
    qg2 = jnp.tile(q_norm_g[layer], 2)[None, :]
    kg2 = jnp.tile(k_norm_g[layer], 2)[None, :]
    partner = lambda g: jnp.tile(jnp.roll(g, half), 2)[None, :]
    q_scale = ATT_HEAD_DIM ** -0.5 * LOG2E
    att = _attention(proj, cos2 * qg2 * q_scale, sin2 * partner(q_norm_g[layer]) * q_scale,
                     cos2 * kg2, sin2 * partner(k_norm_g[layer]), qg2, kg2,
                     lambda_q1[layer][None, :], lambda_k1[layer][None, :],
                     lambda_q2[layer][None, :], lambda_k2[layer][None, :],
                     attn_subln_g[layer][None, :], B, S)

    ssm = _ssd(proj, dt_raw, conv_w[layer], conv_b[layer][None, :],
               _pad_lanes(dt_bias[layer][None, :]), _pad_lanes(a_log[layer][None, :]),
               jnp.repeat(d_skip[layer], SSM_HEAD_DIM)[None, :], ssm_norm_g[layer][None, :], B, S)

    x1, h2p, idx_rank_t, gates_t, counts = _merge(
        att, ssm, proj, xf, w_attn_proj[layer].astype(BF16), w_ssm_proj[layer].astype(BF16),
        w_out[layer].astype(BF16), ffn_norm_g[layer][None, :],
        _pad_lanes(w_router[layer]).astype(BF16), _pad_lanes(b_router[layer][None, :]))

    bm = EXPERT_BM
    A = T * TOP_K
    n_rows = (A + N_EXPERTS * (bm - 1)) // bm * bm
    cnt = counts[:, 0]
    padded = (cnt + bm - 1) // bm * bm
    pend = jnp.cumsum(padded)
    pstart = pend - padded
    n_used = (pend[-1:] // bm).astype(jnp.int32)
    block_start = jnp.arange(n_rows // bm, dtype=jnp.int32) * bm
    block_expert = jnp.minimum(jnp.sum(block_start[:, None] >= pend[None, :], axis=1),
                               N_EXPERTS - 1).astype(jnp.int32)
    idx_rank = idx_rank_t.T
    onehot = idx_rank[:, :TOP_K, None] == jnp.arange(N_EXPERTS, dtype=jnp.int32)[None, None, :]
    dest = (jnp.sum(jnp.where(onehot, pstart[None, None, :], 0), axis=-1)
            + idx_rank[:, TOP_K:]).reshape(A).astype(jnp.int32)

    rows = _dispatch((pend - bm).astype(jnp.int32), cnt.astype(jnp.int32), dest, h2p, n_rows, bm)
    y_rows = _experts(block_expert, n_used, rows, w_gate_up[layer], b_gate_up[layer][:, None, :],
                      w_down[layer], b_down[layer][:, None, :], bm)
    out = _combine(dest, x1, gates_t, y_rows)
    return out.reshape(B, S, D)
```

```python
import functools
import math

import jax
import jax.numpy as jnp
from jax import lax
from jax.experimental import pallas as pl
from jax.experimental.pallas import tpu as pltpu

F32 = jnp.float32
BF16 = jnp.bfloat16

D_MODEL = 1024
EPS = 1e-6
ATT_HEADS = 8
ATT_HEAD_DIM = 64
ATT_V_DIM = 2 * ATT_HEAD_DIM
ATT_WIDTH = ATT_HEADS * ATT_V_DIM
ROPE_THETA = 10000.0
SSM_D_INNER = 2 * D_MODEL
SSM_HEAD_DIM = 64
SSM_HEADS = SSM_D_INNER // SSM_HEAD_DIM
SSM_GROUPS = 8
SSM_STATE = 128
SSM_CONV = 4
SSM_CHUNK = 128
SSM_XBC = SSM_D_INNER + 2 * SSM_GROUPS * SSM_STATE
N_EXPERTS = 32
TOP_K = 4
D_FF = D_MODEL
SWIGLU_LIMIT = 7.0
SWIGLU_ALPHA = 1.702
LAM_INIT = 0.8 - 0.6 * math.exp(-0.3 * 0)

LANES = 128
HALO = 16

COL_XBC = 0
COL_Z = SSM_XBC
COL_Q = COL_Z + SSM_D_INNER
COL_K = COL_Q + ATT_WIDTH
COL_V = COL_K + ATT_WIDTH
COL_GA = COL_V + ATT_WIDTH
COL_GS = COL_GA + D_MODEL
PROJ_COLS = COL_GS + D_MODEL

VMEM_LIMIT = 56 * 1024 * 1024

PROJ_TM = 2048
PROJ_TN = 1024
ATT_TQ = 512
ATT_HEADS_PER_STEP = 2
MERGE_TM = 512
DISPATCH_TT = 2048
EXPERT_BM = 1024
COMBINE_TC = 256


def _cparams(sem):
    return pltpu.CompilerParams(dimension_semantics=sem, vmem_limit_bytes=VMEM_LIMIT)


PACKED = D_MODEL // 2


def _pack_rows(x):
    lo = lax.bitcast_convert_type(x[:, :PACKED].astype(BF16).astype(F32), jnp.uint32)
    hi = lax.bitcast_convert_type(x[:, PACKED:].astype(BF16).astype(F32), jnp.uint32)
    return hi | (lo >> 16)


def _unpack_rows(p):
    lo = lax.bitcast_convert_type(p << 16, F32)
    hi = lax.bitcast_convert_type(p & jnp.uint32(0xFFFF0000), F32)
    return lo, hi


ROW_SUB = PACKED // LANES


def _store_rows(ref, packed):
    for s in range(ROW_SUB):
        ref[pl.ds(s, packed.shape[0], stride=ROW_SUB), :] = packed[:, s * LANES:(s + 1) * LANES]


def _load_rows(ref):
    n = ref.shape[0] // ROW_SUB
    return jnp.concatenate([ref[pl.ds(s, n, stride=ROW_SUB), :] for s in range(ROW_SUB)], axis=1)


def _row(ref, r):
    return ref.at[pl.ds(pl.multiple_of(r * ROW_SUB, ROW_SUB), ROW_SUB)]


def _in_proj_kernel(x_ref, g_ref, w_ref, wdt_ref, o_ref, dt_ref, h_scr):
    @pl.when(pl.program_id(1) == 0)
    def _():
        x = x_ref[...]
        ms = jnp.mean(x * x, axis=-1, keepdims=True)
        hb = (x * lax.rsqrt(ms + EPS) * g_ref[...]).astype(BF16)
        h_scr[...] = hb
        dt_ref[...] = jnp.dot(hb, wdt_ref[...], preferred_element_type=F32)

    o_ref[...] = jnp.dot(h_scr[...], w_ref[...], preferred_element_type=F32).astype(BF16)


def _in_proj(xf, g, w_main, w_dt):
    T = xf.shape[0]
    tm = min(PROJ_TM, T)
    tn = PROJ_TN
    return pl.pallas_call(
        _in_proj_kernel,
        grid=(T // tm, PROJ_COLS // tn),
        in_specs=[
            pl.BlockSpec((tm, D_MODEL), lambda i, j: (i, 0)),
            pl.BlockSpec((1, D_MODEL), lambda i, j: (0, 0)),
            pl.BlockSpec((D_MODEL, tn), lambda i, j: (0, j)),
            pl.BlockSpec((D_MODEL, LANES), lambda i, j: (0, 0)),
        ],
        out_specs=[
            pl.BlockSpec((tm, tn), lambda i, j: (i, j)),
            pl.BlockSpec((tm, LANES), lambda i, j: (i, 0)),
        ],
        out_shape=[
            jax.ShapeDtypeStruct((T, PROJ_COLS), BF16),
            jax.ShapeDtypeStruct((T, LANES), F32),
        ],
        scratch_shapes=[pltpu.VMEM((tm, D_MODEL), BF16)],
        compiler_params=_cparams(("arbitrary", "arbitrary")),
        name="in_proj",
    )(xf, g, w_main, w_dt)


LOG2E = 1.4426950408889634
SHIFT_LIMIT = 57.0
BOUND_MARGIN = 1.02


def _attn_kernel(q_ref, k_ref, v_ref, qc_ref, qs_ref, kc_ref, ks_ref, qg_ref, kg_ref,
                 lq1_ref, lk1_ref, lq2_ref, lk2_ref, sg_ref, o_ref, k_scr, v_scr, q1_scr, q2_scr, *, tq, seq):
    nq = seq // tq
    heads = range(ATT_HEADS_PER_STEP)
    col = lambda hh: slice(hh * LANES, (hh + 1) * LANES)
    lane = lax.broadcasted_iota(jnp.int32, (1, LANES), 1)
    first = lane < ATT_HEAD_DIM
    one_col = jnp.where(lane == 0, 1.0, 0.0)

    ri = lax.broadcasted_iota(jnp.int32, (LANES, LANES), 0)
    ci = lax.broadcasted_iota(jnp.int32, (LANES, LANES), 1)
    same_comp = (ri // ATT_HEAD_DIM == ci // ATT_HEAD_DIM).astype(BF16)
    swap_half = ((ri // ATT_HEAD_DIM == ci // ATT_HEAD_DIM)
                 & ((ri - ci == ATT_HEAD_DIM // 2) | (ci - ri == ATT_HEAD_DIM // 2))).astype(BF16)

    def norm_rope(xb, gcos, gsin):
        xf = xb.astype(F32)
        sq = xf * xf
        hi = sq.astype(BF16)
        lo = (sq - hi.astype(F32)).astype(BF16)
        ms = jnp.dot(jnp.concatenate([hi, lo], axis=1), jnp.concatenate([same_comp, same_comp], axis=0),
                     preferred_element_type=F32) * (1.0 / ATT_HEAD_DIM)
        xr = jnp.dot(xb, swap_half, preferred_element_type=F32)
        return lax.rsqrt(ms + EPS) * (xf * gcos + xr * gsin)

    ub = (ATT_HEAD_DIM * ATT_HEAD_DIM ** -0.5 * LOG2E * BOUND_MARGIN
          * jnp.max(jnp.abs(qg_ref[...])) * jnp.max(jnp.abs(kg_ref[...])))
    safe = ub <= SHIFT_LIMIT

    ones = jnp.broadcast_to(one_col, (tq, LANES)).astype(BF16)
    shift = jnp.broadcast_to(jnp.where(lane == 0, -ub, 0.0), (tq, LANES)).astype(BF16)
    for hh in heads:
        for r in range(0, seq, tq):
            rows = slice(r, r + tq)
            kb = norm_rope(k_ref[rows, col(hh)], kc_ref[rows, :], ks_ref[rows, :]).astype(BF16)
            k_scr[hh, rows, :LANES] = kb
            k_scr[hh, rows, LANES:] = ones
            v_scr[hh, rows, :LANES] = v_ref[rows, col(hh)]
            v_scr[hh, rows, LANES:] = ones
            qb = norm_rope(q_ref[rows, col(hh)], qc_ref[rows, :], qs_ref[rows, :]).astype(BF16)
            q1_scr[hh, rows, :LANES] = jnp.where(first, qb, jnp.zeros_like(qb))
            q1_scr[hh, rows, LANES:] = shift
            q2_scr[hh, rows, :LANES] = jnp.where(first, jnp.zeros_like(qb), qb)
            q2_scr[hh, rows, LANES:] = shift

    keep = (lax.broadcasted_iota(jnp.int32, (tq, tq), 1) <= lax.broadcasted_iota(jnp.int32, (tq, tq), 0))
    nt = (((1,), (1,)), ((), ()))

    def finish(hh, o1, o2, rows0):
        lam = (jnp.exp(jnp.sum(lq1_ref[...] * lk1_ref[...], axis=-1, keepdims=True))
               - jnp.exp(jnp.sum(lq2_ref[...] * lk2_ref[...], axis=-1, keepdims=True)) + LAM_INIT)
        o = o1 - lam * o2
        ms = jnp.mean(o * o, axis=-1, keepdims=True)
        o_ref[pl.ds(rows0, tq), col(hh)] = (o * lax.rsqrt(ms + EPS) * sg_ref[...] * (1.0 - LAM_INIT)).astype(BF16)

    @pl.when(safe)
    def _():
        def block(hh, q_scr, qrow0, krow0, rows, cols, mask):
            t = lax.dot_general(q_scr[hh, pl.ds(qrow0, rows), :], k_scr[hh, pl.ds(krow0, cols), :], nt,
                                preferred_element_type=F32)
            if mask is not None:
                t = jnp.where(mask, t, -jnp.inf)
            return jnp.dot(jnp.exp2(t).astype(BF16), v_scr[hh, pl.ds(krow0, cols), :],
                           preferred_element_type=F32)

        def tile(hh, qrow0, krow0, masked):
            if not masked:
                return (block(hh, q1_scr, qrow0, krow0, tq, tq, None),
                        block(hh, q2_scr, qrow0, krow0, tq, tq, None))
            h = tq // 2
            outs = []
            for q_scr in (q1_scr, q2_scr):
                left_cols = block(hh, q_scr, qrow0, krow0, tq, h, keep[:, :h])
                corner = block(hh, q_scr, qrow0 + h, krow0 + h, h, h, keep[:h, :h])
                outs.append(jnp.concatenate([left_cols[:h], left_cols[h:] + corner], axis=0))
            return tuple(outs)

        def normalised(acc):
            return acc[:, :LANES] / acc[:, LANES:LANES + 1]

        for hh in heads:
            for qi in range(nq):
                a1, a2 = tile(hh, qi * tq, qi * tq, True)
                for j in range(qi):
                    d1, d2 = tile(hh, qi * tq, j * tq, False)
                    a1, a2 = a1 + d1, a2 + d2
                finish(hh, normalised(a1), normalised(a2), qi * tq)

    @pl.when(jnp.logical_not(safe))
    def _():
        def update(t, vt, m, l, acc):
            m_new = jnp.maximum(m, jnp.max(t, axis=-1, keepdims=True))
            alpha = jnp.exp2(m - m_new)
            pr = jnp.exp2(t - m_new)
            l_new = alpha * l + jnp.sum(pr, axis=-1, keepdims=True)
            acc_new = alpha * acc + jnp.dot(pr.astype(BF16), vt, preferred_element_type=F32)
            return m_new, l_new, acc_new

        def online(hh, rows0, n_off):
            q1 = q1_scr[hh, pl.ds(rows0, tq), :LANES]
            q2 = q2_scr[hh, pl.ds(rows0, tq), :LANES]

            def step(krow0, carry, masked):
                m1, l1, c1, m2, l2, c2 = carry
                kt = k_scr[hh, pl.ds(krow0, tq), :LANES]
                vt = v_scr[hh, pl.ds(krow0, tq), :LANES]
                t1 = lax.dot_general(q1, kt, nt, preferred_element_type=F32)
                t2 = lax.dot_general(q2, kt, nt, preferred_element_type=F32)
                if masked:
                    t1 = jnp.where(keep, t1, -jnp.inf)
                    t2 = jnp.where(keep, t2, -jnp.inf)
                m1, l1, c1 = update(t1, vt, m1, l1, c1)
                m2, l2, c2 = update(t2, vt, m2, l2, c2)
                return m1, l1, c1, m2, l2, c2

            neg = jnp.full((tq, 1), -jnp.inf, F32)
            zero1 = jnp.zeros((tq, 1), F32)
            zacc = jnp.zeros((tq, ATT_V_DIM), F32)
            carry = step(rows0, (neg, zero1, zacc, neg, zero1, zacc), True)
            m1, l1, c1, m2, l2, c2 = lax.fori_loop(
                0, n_off, lambda j, c: step(pl.multiple_of(j * tq, tq), c, False), carry)
            finish(hh, c1 / l1, c2 / l2, rows0)

        for hh in heads:
            for qi in range(nq):
                online(hh, qi * tq, qi)


def _attention(proj, qcos, qsin, kcos, ksin, qg2, kg2, lq1, lk1, lq2, lk2, sg, batch, seq):
    T = batch * seq
    tq = min(ATT_TQ, seq)
    nq = seq // tq
    vec = lambda n: pl.BlockSpec((1, n), lambda b, h: (0, 0))
    table = pl.BlockSpec((seq, LANES), lambda b, h: (0, 0))
    width = ATT_HEADS_PER_STEP * LANES
    head = lambda col: pl.BlockSpec((seq, width), lambda b, h: (b, col // width + h))
    return pl.pallas_call(
        functools.partial(_attn_kernel, tq=tq, seq=seq),
        grid=(batch, ATT_HEADS // ATT_HEADS_PER_STEP),
        in_specs=[
            head(COL_Q), head(COL_K), head(COL_V),
            table, table, table, table,
            vec(LANES), vec(LANES),
            vec(ATT_HEAD_DIM), vec(ATT_HEAD_DIM), vec(ATT_HEAD_DIM), vec(ATT_HEAD_DIM),
            vec(LANES),
        ],
        out_specs=pl.BlockSpec((seq, width), lambda b, h: (b, h)),
        out_shape=jax.ShapeDtypeStruct((T, ATT_WIDTH), BF16),
        scratch_shapes=[pltpu.VMEM((ATT_HEADS_PER_STEP, seq, 2 * LANES), BF16) for _ in range(4)],
        compiler_params=_cparams(("arbitrary", "arbitrary")),
        name="attention",
    )(proj, proj, proj, qcos, qsin, kcos, ksin, qg2, kg2, lq1, lk1, lq2, lk2, sg)


SSD_SUB = 2


def _ssd_kernel(xbc_ref, z_ref, dt_ref, cw_ref, cb_ref, dtb_ref, alog_ref, dsk_ref, ng_ref,
                o_ref, halo, state, xs_scr, bc_scr, y_scr):
    Q = SSM_CHUNK
    N = SSM_STATE
    P2 = 2 * SSM_HEAD_DIM

    @pl.when(pl.program_id(1) == 0)
    def _():
        halo[...] = jnp.zeros_like(halo)
        state[...] = jnp.zeros_like(state)

    taps = SSM_CONV - 1
    sr = lax.broadcasted_iota(jnp.int32, (taps * Q, HALO + Q), 0)
    sc = lax.broadcasted_iota(jnp.int32, (taps * Q, HALO + Q), 1)
    shift = (sc == (sr % Q) + HALO - taps + sr // Q).astype(BF16)
    ri = lax.broadcasted_iota(jnp.int32, (Q, Q), 0)
    ci = lax.broadcasted_iota(jnp.int32, (Q, Q), 1)
    tri = ri >= ci
    lane = lax.broadcasted_iota(jnp.int32, (1, P2), 1)
    left = lane < SSM_HEAD_DIM
    gw = SSM_D_INNER // SSM_GROUPS

    for sub in range(SSD_SUB):
        r0 = sub * Q
        xs_sub, bc_sub, y_sub = xs_scr.at[sub], bc_scr.at[sub], y_scr.at[sub]

        cwid = 512
        for c0 in range(0, SSM_XBC, cwid):
            cur = xbc_ref[r0:r0 + Q, c0:c0 + cwid]
            hist = halo[:, c0:c0 + cwid] if sub == 0 else xbc_ref[r0 - HALO:r0, c0:c0 + cwid]
            ext = jnp.concatenate([hist, cur], axis=0)
            shifted = jnp.dot(shift, ext, preferred_element_type=F32)
            acc = cb_ref[:, c0:c0 + cwid] + cw_ref[taps:SSM_CONV, c0:c0 + cwid] * cur.astype(F32)
            for k in range(taps):
                acc = acc + cw_ref[k:k + 1, c0:c0 + cwid] * shifted[k * Q:(k + 1) * Q, :]
            act = acc * jax.nn.sigmoid(acc)
            if c0 < SSM_D_INNER:
                xs_sub[:, c0:c0 + cwid] = act
            else:
                bc_sub[:, c0 - SSM_D_INNER:c0 - SSM_D_INNER + cwid] = act.astype(BF16)

        dt = jax.nn.softplus(dt_ref[r0:r0 + Q, :] + dtb_ref[...])
        da = dt * (-jnp.exp(alog_ref[...]) * LOG2E)
        acs = jnp.dot(tri.astype(F32), da, preferred_element_type=F32,
                      precision=lax.Precision.HIGHEST)
        acs_t = acs.T
        dt_t = dt.T
        w_t = dt_t * jnp.exp2(acs_t[:, Q - 1:Q] - acs_t)

        for g in range(SSM_GROUPS):
            bm = bc_sub[:, g * N:(g + 1) * N]
            cm = bc_sub[:, SSM_GROUPS * N + g * N:SSM_GROUPS * N + (g + 1) * N]
            cb = lax.dot_general(cm, bm, (((1,), (1,)), ((), ())), preferred_element_type=F32)
            bm_t = bm.astype(F32).T
            cm_f = cm.astype(F32)
            for pr in range(2):
                pair = 2 * g + pr
                xs_pair = xs_sub[:, pair * P2:(pair + 1) * P2].astype(BF16)
                prev = state[pair]
                rhs = jnp.concatenate([xs_pair, prev.astype(BF16)], axis=0)
                ys, sts, decs = [], [], []
                for r in range(2):
                    h = 2 * pair + r
                    a_col = acs[:, h:h + 1]
                    seg = a_col - acs_t[h:h + 1, :]
                    decay = jnp.exp2(jnp.where(tri, seg, -jnp.inf))
                    m_h = cb * decay * dt_t[h:h + 1, :]
                    e_h = cm_f * jnp.exp2(a_col)
                    lhs = jnp.concatenate([m_h, e_h], axis=1).astype(BF16)
                    ys.append(jnp.dot(lhs, rhs, preferred_element_type=F32))
                    sts.append(jnp.dot((bm_t * w_t[h:h + 1, :]).astype(BF16), xs_pair,
                                       preferred_element_type=F32))
                    decs.append(jnp.exp2(acs[Q - 1:Q, h:h + 1]))
                y_sub[:, pair * P2:(pair + 1) * P2] = jnp.where(left, ys[0], ys[1])
                dec = jnp.where(left, decs[0], decs[1])
                state[pair] = dec * prev + jnp.where(left, sts[0], sts[1])

        y = y_sub[...] + dsk_ref[...] * xs_sub[...]
        zf = z_ref[r0:r0 + Q, :].astype(F32)
        y = y * (zf * jax.nn.sigmoid(zf))
        for g in range(SSM_GROUPS):
            yg = y[:, g * gw:(g + 1) * gw]
            ms = jnp.mean(yg * yg, axis=-1, keepdims=True)
            o_ref[r0:r0 + Q, g * gw:(g + 1) * gw] = (yg * lax.rsqrt(ms + EPS)
                                                     * ng_ref[:, g * gw:(g + 1) * gw]).astype(BF16)

    halo[...] = xbc_ref[SSD_SUB * Q - HALO:SSD_SUB * Q, :]


def _ssd(proj, dt_raw, conv_w, conv_b, dt_bias, a_log, dsk, ng, batch, seq):
    T = batch * seq
    Q = SSM_CHUNK
    rows = SSD_SUB * Q
    nc = seq // rows
    vec = lambda r, n: pl.BlockSpec((r, n), lambda b, c: (0, 0))
    return pl.pallas_call(
        _ssd_kernel,
        grid=(batch, nc),
        in_specs=[
            pl.BlockSpec((rows, SSM_XBC), lambda b, c: (b * nc + c, COL_XBC // SSM_XBC)),
            pl.BlockSpec((rows, SSM_D_INNER), lambda b, c: (b * nc + c, COL_Z // SSM_D_INNER)),
            pl.BlockSpec((rows, LANES), lambda b, c: (b * nc + c, 0)),
            vec(SSM_CONV, SSM_XBC), vec(1, SSM_XBC), vec(1, LANES), vec(1, LANES),
            vec(1, SSM_D_INNER), vec(1, SSM_D_INNER),
        ],
        out_specs=pl.BlockSpec((rows, SSM_D_INNER), lambda b, c: (b * nc + c, 0)),
        out_shape=jax.ShapeDtypeStruct((T, SSM_D_INNER), BF16),
        scratch_shapes=[
            pltpu.VMEM((HALO, SSM_XBC), BF16),
            pltpu.VMEM((SSM_HEADS // 2, SSM_STATE, 2 * SSM_HEAD_DIM), F32),
            pltpu.VMEM((SSD_SUB, Q, SSM_D_INNER), F32),
            pltpu.VMEM((SSD_SUB, Q, 2 * SSM_GROUPS * SSM_STATE), BF16),
            pltpu.VMEM((SSD_SUB, Q, SSM_D_INNER), F32),
        ],
        compiler_params=_cparams(("arbitrary", "arbitrary")),
        name="ssd",
    )(proj, proj, dt_raw, conv_w, conv_b, dt_bias, a_log, dsk, ng)


def _merge_kernel(att_ref, ssm_ref, ga_ref, gs_ref, x_ref, wap_ref, wsp_ref, wo_ref, fg_ref,
                  wr_ref, br_ref, x1_ref, h2_ref, idx_ref, gate_ref, cnt_ref, base, *, tm):
    i = pl.program_id(0)

    @pl.when(i == 0)
    def _():
        base[...] = jnp.zeros_like(base)

    pa = jnp.dot(att_ref[...], wap_ref[...], preferred_element_type=F32)
    ps = jnp.dot(ssm_ref[...], wsp_ref[...], preferred_element_type=F32)
    merged = (jax.nn.sigmoid(ga_ref[...].astype(F32)) * pa
              + jax.nn.sigmoid(gs_ref[...].astype(F32)) * ps)
    x1 = x_ref[...] + jnp.dot(merged.astype(BF16), wo_ref[...], preferred_element_type=F32)
    x1_ref[...] = x1
    ms = jnp.mean(x1 * x1, axis=-1, keepdims=True)
    h2 = x1 * lax.rsqrt(ms + EPS) * fg_ref[...]
    _store_rows(h2_ref, _pack_rows(h2))

    logits = jnp.dot(h2.astype(BF16), wr_ref[...], preferred_element_type=F32) + br_ref[...]
    lg = logits.T[:N_EXPERTS, :]
    row_f = lax.broadcasted_iota(jnp.int32, (N_EXPERTS, tm), 0).astype(F32)
    vals, idxs, sels = [], [], []
    for _ in range(TOP_K):
        m = jnp.max(lg, axis=0, keepdims=True)
        idx = jnp.min(jnp.where(lg == m, row_f, float(N_EXPERTS)), axis=0, keepdims=True)
        sel = row_f == idx
        vals.append(m)
        idxs.append(idx)
        sels.append(sel)
        lg = jnp.where(sel, -jnp.inf, lg)
    es = [jnp.exp(v - vals[0]) for v in vals]
    den = es[0] + es[1] + es[2] + es[3]

    multi = jnp.zeros((N_EXPERTS, tm), F32)
    for sel in sels:
        multi = jnp.where(sel, 1.0, multi)
    ri = lax.broadcasted_iota(jnp.int32, (tm, tm), 0)
    ci = lax.broadcasted_iota(jnp.int32, (tm, tm), 1)
    before = jnp.dot(multi.astype(BF16), (ri < ci).astype(BF16), preferred_element_type=F32)
    before = before + base[...]
    ranks = [jnp.sum(jnp.where(sel, before, 0.0), axis=0, keepdims=True) for sel in sels]
    idx_ref[...] = jnp.concatenate(idxs + ranks, axis=0).astype(jnp.int32)
    gate_ref[...] = jnp.concatenate([e / den for e in es] + [jnp.zeros_like(den)] * TOP_K, axis=0)
    base[...] = base[...] + jnp.sum(multi, axis=1, keepdims=True)
    cnt_ref[...] = jnp.broadcast_to(base[...], cnt_ref.shape).astype(jnp.int32)


def _merge(att, ssm, proj, xf, wap, wsp, wo, fg, wr, br):
    T = xf.shape[0]
    tm = min(MERGE_TM, T)
    full = lambda a: pl.BlockSpec(a.shape, lambda i: (0, 0))
    return pl.pallas_call(
        functools.partial(_merge_kernel, tm=tm),
        grid=(T // tm,),
        in_specs=[
            pl.BlockSpec((tm, ATT_WIDTH), lambda i: (i, 0)),
            pl.BlockSpec((tm, SSM_D_INNER), lambda i: (i, 0)),
            pl.BlockSpec((tm, D_MODEL), lambda i: (i, COL_GA // D_MODEL)),
            pl.BlockSpec((tm, D_MODEL), lambda i: (i, COL_GS // D_MODEL)),
            pl.BlockSpec((tm, D_MODEL), lambda i: (i, 0)),
            full(wap), full(wsp), full(wo), full(fg), full(wr), full(br),
        ],
        out_specs=[
            pl.BlockSpec((tm, D_MODEL), lambda i: (i, 0)),
            pl.BlockSpec((tm * ROW_SUB, LANES), lambda i: (i, 0)),
            pl.BlockSpec((2 * TOP_K, tm), lambda i: (0, i)),
            pl.BlockSpec((2 * TOP_K, tm), lambda i: (0, i)),
            pl.BlockSpec((N_EXPERTS, LANES), lambda i: (0, 0)),
        ],
        out_shape=[
            jax.ShapeDtypeStruct((T, D_MODEL), F32),
            jax.ShapeDtypeStruct((T * ROW_SUB, LANES), jnp.uint32),
            jax.ShapeDtypeStruct((2 * TOP_K, T), jnp.int32),
            jax.ShapeDtypeStruct((2 * TOP_K, T), F32),
            jax.ShapeDtypeStruct((N_EXPERTS, LANES), jnp.int32),
        ],
        scratch_shapes=[pltpu.VMEM((N_EXPERTS, 1), F32)],
        compiler_params=_cparams(("arbitrary",)),
        name="merge_router",
    )(att, ssm, proj, proj, xf, wap, wsp, wo, fg, wr, br)


DMA_UNROLL = 8


def _dispatch_kernel(lb_ref, cnt_ref, dest_ref, h_ref, rows_ref, zeros, zsem, sem, *, tt, bm):
    @pl.when(pl.program_id(0) == 0)
    def _():
        zeros[...] = jnp.zeros_like(zeros)
        for wait in (False, True):
            for e in range(N_EXPERTS):
                @pl.when(cnt_ref[e] > 0)
                def _():
                    start = pl.multiple_of(lb_ref[e], bm)
                    cp = pltpu.make_async_copy(zeros, rows_ref.at[pl.ds(start * ROW_SUB, bm * ROW_SUB)], zsem)
                    cp.wait() if wait else cp.start()

    def row_copy(t, k):
        return pltpu.make_async_copy(_row(h_ref, t), _row(rows_ref, dest_ref[t * TOP_K + k]), sem)

    for wait in (False, True):
        def body(g, carry):
            for u in range(DMA_UNROLL):
                for k in range(TOP_K):
                    cp = row_copy(g * DMA_UNROLL + u, k)
                    cp.wait() if wait else cp.start(priority=k % 2)
            return carry
        lax.fori_loop(0, tt // DMA_UNROLL, body, 0)


def _dispatch(last_block, cnt, dest, h2p, n_rows, bm):
    T = h2p.shape[0] // ROW_SUB
    tt = min(DISPATCH_TT, T)
    grid_spec = pltpu.PrefetchScalarGridSpec(
        num_scalar_prefetch=2,
        grid=(T // tt,),
        in_specs=[
            pl.BlockSpec((tt * TOP_K,), lambda i, lb, c: (i,), memory_space=pltpu.SMEM),
            pl.BlockSpec((tt * ROW_SUB, LANES), lambda i, lb, c: (i, 0)),
        ],
        out_specs=pl.BlockSpec(memory_space=pl.ANY),
        scratch_shapes=[pltpu.VMEM((bm * ROW_SUB, LANES), jnp.uint32),
                        pltpu.SemaphoreType.DMA, pltpu.SemaphoreType.DMA],
    )
    return pl.pallas_call(
        functools.partial(_dispatch_kernel, tt=tt, bm=bm),
        grid_spec=grid_spec,
        out_shape=jax.ShapeDtypeStruct((n_rows * ROW_SUB, LANES), jnp.uint32),
        compiler_params=_cparams(("arbitrary",)),
        name="dispatch",
    )(last_block, cnt, dest, h2p)


def _expert_kernel(be_ref, nu_ref, x_ref, wgu_ref, bgu_ref, wd_ref, bd_ref, o_ref, wgu_bf, wd_bf):
    i = pl.program_id(0)

    @pl.when(i < nu_ref[0])
    def _():
        @pl.when((i == 0) | (be_ref[i] != be_ref[jnp.maximum(i - 1, 0)]))
        def _():
            wgu_bf[...] = wgu_ref[0].astype(BF16)
            wd_bf[...] = wd_ref[0].astype(BF16)

        lo, hi = _unpack_rows(_load_rows(x_ref))
        x = jnp.concatenate([lo, hi], axis=1).astype(BF16)
        gu = jnp.dot(x, wgu_bf[...], preferred_element_type=F32) + bgu_ref[0]
        gate = jnp.minimum(gu[:, :D_FF], SWIGLU_LIMIT)
        up = jnp.clip(gu[:, D_FF:], -SWIGLU_LIMIT, SWIGLU_LIMIT)
        glu = gate * jax.nn.sigmoid(SWIGLU_ALPHA * gate)
        act = ((up + 1.0) * glu).astype(BF16)
        _store_rows(o_ref, _pack_rows(jnp.dot(act, wd_bf[...], preferred_element_type=F32) + bd_ref[0]))


def _experts(block_expert, n_used, rows, wgu, bgu, wd, bd, bm):
    n_rows = rows.shape[0] // ROW_SUB
    nb = n_rows // bm
    row_map = lambda i, be, nu: (jnp.minimum(i, nu[0] - 1), 0)
    exp_map = lambda i, be, nu: (be[jnp.minimum(i, nu[0] - 1)], 0, 0)
    grid_spec = pltpu.PrefetchScalarGridSpec(
        num_scalar_prefetch=2,
        grid=(nb,),
        in_specs=[
            pl.BlockSpec((bm * ROW_SUB, LANES), row_map),
            pl.BlockSpec((1, D_MODEL, 2 * D_FF), exp_map),
            pl.BlockSpec((1, 1, 2 * D_FF), exp_map),
            pl.BlockSpec((1, D_FF, D_MODEL), exp_map),
            pl.BlockSpec((1, 1, D_MODEL), exp_map),
        ],
        out_specs=pl.BlockSpec((bm * ROW_SUB, LANES), row_map),
        scratch_shapes=[pltpu.VMEM((D_MODEL, 2 * D_FF), BF16), pltpu.VMEM((D_FF, D_MODEL), BF16)],
    )
    return pl.pallas_call(
        _expert_kernel,
        grid_spec=grid_spec,
        out_shape=jax.ShapeDtypeStruct((n_rows * ROW_SUB, LANES), jnp.uint32),
        compiler_params=_cparams(("arbitrary",)),
        name="experts",
    )(block_expert, n_used, rows, wgu, bgu, wd, bd)


def _combine_kernel(dest_ref, next_ref, x1_ref, gate_ref, y_ref, o_ref, ybuf, sems, *, tc):
    i = pl.program_id(0)
    n = pl.num_programs(0)

    def gather(idx_ref, slot, wait):
        def row_copy(t, k):
            return pltpu.make_async_copy(_row(y_ref, idx_ref[t * TOP_K + k]), _row(ybuf.at[slot, k], t),
                                         sems.at[slot])

        def body(g, carry):
            for u in range(DMA_UNROLL):
                for k in range(TOP_K):
                    cp = row_copy(g * DMA_UNROLL + u, k)
                    cp.wait() if wait else cp.start(priority=k % 2)
            return carry
        lax.fori_loop(0, tc // DMA_UNROLL, body, 0)

    slot = i % 2

    @pl.when(i == 0)
    def _():
        gather(dest_ref, 0, False)

    @pl.when(i + 1 < n)
    def _():
        gather(next_ref, 1 - slot, False)

    gather(dest_ref, slot, True)

    x1 = x1_ref[...]
    acc_lo, acc_hi = x1[:, :PACKED], x1[:, PACKED:]
    g = gate_ref[...].T
    for k in range(TOP_K):
        lo, hi = _unpack_rows(_load_rows(ybuf.at[slot, k]))
        acc_lo = acc_lo + g[:, k:k + 1] * lo
        acc_hi = acc_hi + g[:, k:k + 1] * hi
    o_ref[:, :PACKED] = acc_lo
    o_ref[:, PACKED:] = acc_hi


def _combine(dest, x1, gates, y_rows):
    T = x1.shape[0]
    tc = min(COMBINE_TC, T)
    n = T // tc
    return pl.pallas_call(
        functools.partial(_combine_kernel, tc=tc),
        grid=(n,),
        in_specs=[
            pl.BlockSpec((tc * TOP_K,), lambda i: (i,), memory_space=pltpu.SMEM),
            pl.BlockSpec((tc * TOP_K,), lambda i: (jnp.minimum(i + 1, n - 1),), memory_space=pltpu.SMEM),
            pl.BlockSpec((tc, D_MODEL), lambda i: (i, 0)),
            pl.BlockSpec((2 * TOP_K, tc), lambda i: (0, i)),
            pl.BlockSpec(memory_space=pl.ANY),
        ],
        out_specs=pl.BlockSpec((tc, D_MODEL), lambda i: (i, 0)),
        out_shape=jax.ShapeDtypeStruct((T, D_MODEL), F32),
        scratch_shapes=[pltpu.VMEM((2, TOP_K, tc * ROW_SUB, LANES), jnp.uint32),
                        pltpu.SemaphoreType.DMA((2,))],
        compiler_params=_cparams(("arbitrary",)),
        name="combine",
    )(dest, dest, x1, gates, y_rows)


def _pad_lanes(v):
    return jnp.pad(v, ((0, 0), (0, LANES - v.shape[-1])))


def kernel(x, mix_norm_g, w_in, q_norm_g, k_norm_g, lambda_q1, lambda_k1, lambda_q2, lambda_k2,
           attn_subln_g, conv_w, conv_b, dt_bias, a_log, d_skip, ssm_norm_g, w_attn_proj,
           w_ssm_proj, w_out, ffn_norm_g, w_router, b_router, w_gate_up, b_gate_up, w_down, b_down):
    B, S, D = x.shape
    T = B * S
    xf = x.reshape(T, D)
    layer = 0

    wi = w_in[layer]
    o_z = 3 * ATT_WIDTH
    o_xbc = o_z + SSM_D_INNER
    o_dt = o_xbc + SSM_XBC
    o_ga = o_dt + SSM_HEADS
    w_main = jnp.concatenate([wi[:, o_xbc:o_dt], wi[:, o_z:o_xbc], wi[:, :o_z], wi[:, o_ga:]],
                             axis=1).astype(BF16)
    w_dt = _pad_lanes(wi[:, o_dt:o_ga]).astype(BF16)
    proj, dt_raw = _in_proj(xf, mix_norm_g[layer][None, :], w_main, w_dt)

    half = ATT_HEAD_DIM // 2
    inv = ROPE_THETA ** (-jnp.arange(0, ATT_HEAD_DIM, 2, dtype=F32) / ATT_HEAD_DIM)
    ang = jnp.arange(S, dtype=F32)[:, None] * inv[None, :]
    cos2 = jnp.tile(jnp.cos(ang), (1, LANES // half))
    sin2 = jnp.tile(jnp.concatenate([-jnp.sin(ang), jnp.sin(ang)], axis=1), (1, LANES // ATT_HEAD_DIM))
    qg2 = jnp.tile(q_norm_g[layer], 2)[None, :]
    kg2 = jnp.tile(k_norm_g[layer], 2)[None, :]
    partner = lambda g: jnp.tile(jnp.roll(g, half), 2)[None, :]
    q_scale = ATT_HEAD_DIM ** -0.5 * LOG2E
    att = _attention(proj, cos2 * qg2 * q_scale, sin2 * partner(q_norm_g[layer]) * q_scale,
                     cos2 * kg2, sin2 * partner(k_norm_g[layer]), qg2, kg2,
                     lambda_q1[layer][None, :], lambda_k1[layer][None, :],
                     lambda_q2[layer][None, :], lambda_k2[layer][None, :],
                     attn_subln_g[layer][None, :], B, S)

    ssm = _ssd(proj, dt_raw, conv_w[layer], conv_b[layer][None, :],
               _pad_lanes(dt_bias[layer][None, :]), _pad_lanes(a_log[layer][None, :]),
               jnp.repeat(d_skip[layer], SSM_HEAD_DIM)[None, :], ssm_norm_g[layer][None, :], B, S)

    x1, h2p, idx_rank_t, gates_t, counts = _merge(
        att, ssm, proj, xf, w_attn_proj[layer].astype(BF16), w_ssm_proj[layer].astype(BF16),
        w_out[layer].astype(BF16), ffn_norm_g[layer][None, :],
        _pad_lanes(w_router[layer]).astype(BF16), _pad_lanes(b_router[layer][None, :]))

    bm = EXPERT_BM
    A = T * TOP_K
    n_rows = (A + N_EXPERTS * (bm - 1)) // bm * bm
    cnt = counts[:, 0]
    padded = (cnt + bm - 1) // bm * bm
    pend = jnp.cumsum(padded)
    pstart = pend - padded
    n_used = (pend[-1:] // bm).astype(jnp.int32)
    block_start = jnp.arange(n_rows // bm, dtype=jnp.int32) * bm
    block_expert = jnp.minimum(jnp.sum(block_start[:, None] >= pend[None, :], axis=1),
                               N_EXPERTS - 1).astype(jnp.int32)
    idx_rank = idx_rank_t.T
    onehot = idx_rank[:, :TOP_K, None] == jnp.arange(N_EXPERTS, dtype=jnp.int32)[None, None, :]
    dest = (jnp.sum(jnp.where(onehot, pstart[None, None, :], 0), axis=-1)
            + idx_rank[:, TOP_K:]).reshape(A).astype(jnp.int32)

    rows = _dispatch((pend - bm).astype(jnp.int32), cnt.astype(jnp.int32), dest, h2p, n_rows, bm)
    y_rows = _experts(block_expert, n_used, rows, w_gate_up[layer], b_gate_up[layer][:, None, :],
                      w_down[layer], b_down[layer][:, None, :], bm)
    out = _combine(dest, x1, gates_t, y_rows)
    return out.reshape(B, S, D)
```

```python
import functools
import math

import jax
import jax.numpy as jnp
from jax import lax
from jax.experimental import pallas as pl
from jax.experimental.pallas import tpu as pltpu

F32 = jnp.float32
BF16 = jnp.bfloat16

D_MODEL = 1024
EPS = 1e-6
ATT_HEADS = 8
ATT_HEAD_DIM = 64
ATT_V_DIM = 2 * ATT_HEAD_DIM
ATT_WIDTH = ATT_HEADS * ATT_V_DIM
ROPE_THETA = 10000.0
SSM_D_INNER = 2 * D_MODEL
SSM_HEAD_DIM = 64
SSM_HEADS = SSM_D_INNER // SSM_HEAD_DIM
SSM_GROUPS = 8
SSM_STATE = 128
SSM_CONV = 4
SSM_CHUNK = 128
SSM_XBC = SSM_D_INNER + 2 * SSM_GROUPS * SSM_STATE
N_EXPERTS = 32
TOP_K = 4
D_FF = D_MODEL
SWIGLU_LIMIT = 7.0
SWIGLU_ALPHA = 1.702
LAM_INIT = 0.8 - 0.6 * math.exp(-0.3 * 0)

LANES = 128
HALO = 16

COL_XBC = 0
COL_Z = SSM_XBC
COL_Q = COL_Z + SSM_D_INNER
COL_K = COL_Q + ATT_WIDTH
COL_V = COL_K + ATT_WIDTH
COL_GA = COL_V + ATT_WIDTH
COL_GS = COL_GA + D_MODEL
PROJ_COLS = COL_GS + D_MODEL

VMEM_LIMIT = 56 * 1024 * 1024

PROJ_TM = 2048
PROJ_TN = 1024
ATT_TQ = 512
ATT_HEADS_PER_STEP = 4
MERGE_TM = 512
DISPATCH_TT = 2048
EXPERT_BM = 1024
COMBINE_TC = 256


def _cparams(sem):
    return pltpu.CompilerParams(dimension_semantics=sem, vmem_limit_bytes=VMEM_LIMIT)


PACKED = D_MODEL // 2


def _pack_rows(x):
    lo = lax.bitcast_convert_type(x[:, :PACKED].astype(BF16).astype(F32), jnp.uint32)
    hi = lax.bitcast_convert_type(x[:, PACKED:].astype(BF16).astype(F32), jnp.uint32)
    return hi | (lo >> 16)


def _unpack_rows(p):
    lo = lax.bitcast_convert_type(p << 16, F32)
    hi = lax.bitcast_convert_type(p & jnp.uint32(0xFFFF0000), F32)
    return lo, hi


ROW_SUB = PACKED // LANES


def _store_rows(ref, packed):
    for s in range(ROW_SUB):
        ref[pl.ds(s, packed.shape[0], stride=ROW_SUB), :] = packed[:, s * LANES:(s + 1) * LANES]


def _load_rows(ref):
    n = ref.shape[0] // ROW_SUB
    return jnp.concatenate([ref[pl.ds(s, n, stride=ROW_SUB), :] for s in range(ROW_SUB)], axis=1)


def _row(ref, r):
    return ref.at[pl.ds(pl.multiple_of(r * ROW_SUB, ROW_SUB), ROW_SUB)]


def _in_proj_kernel(x_ref, g_ref, w_ref, wdt_ref, o_ref, dt_ref, h_scr):
    @pl.when(pl.program_id(1) == 0)
    def _():
        x = x_ref[...]
        ms = jnp.mean(x * x, axis=-1, keepdims=True)
        hb = (x * lax.rsqrt(ms + EPS) * g_ref[...]).astype(BF16)
        h_scr[...] = hb
        dt_ref[...] = jnp.dot(hb, wdt_ref[...], preferred_element_type=F32)

    o_ref[...] = jnp.dot(h_scr[...], w_ref[...], preferred_element_type=F32).astype(BF16)


def _in_proj(xf, g, w_main, w_dt):
    T = xf.shape[0]
    tm = min(PROJ_TM, T)
    tn = PROJ_TN
    return pl.pallas_call(
        _in_proj_kernel,
        grid=(T // tm, PROJ_COLS // tn),
        in_specs=[
            pl.BlockSpec((tm, D_MODEL), lambda i, j: (i, 0)),
            pl.BlockSpec((1, D_MODEL), lambda i, j: (0, 0)),
            pl.BlockSpec((D_MODEL, tn), lambda i, j: (0, j)),
            pl.BlockSpec((D_MODEL, LANES), lambda i, j: (0, 0)),
        ],
        out_specs=[
            pl.BlockSpec((tm, tn), lambda i, j: (i, j)),
            pl.BlockSpec((tm, LANES), lambda i, j: (i, 0)),
        ],
        out_shape=[
            jax.ShapeDtypeStruct((T, PROJ_COLS), BF16),
            jax.ShapeDtypeStruct((T, LANES), F32),
        ],
        scratch_shapes=[pltpu.VMEM((tm, D_MODEL), BF16)],
        compiler_params=_cparams(("arbitrary", "arbitrary")),
        name="in_proj",
    )(xf, g, w_main, w_dt)


LOG2E = 1.4426950408889634
SHIFT_LIMIT = 57.0
BOUND_MARGIN = 1.02


def _attn_kernel(q_ref, k_ref, v_ref, qc_ref, qs_ref, kc_ref, ks_ref, qg_ref, kg_ref,
                 lq1_ref, lk1_ref, lq2_ref, lk2_ref, sg_ref, o_ref, k_scr, v_scr, q1_scr, q2_scr, *, tq, seq):
    nq = seq // tq
    heads = range(ATT_HEADS_PER_STEP)
    col = lambda hh: slice(hh * LANES, (hh + 1) * LANES)
    lane = lax.broadcasted_iota(jnp.int32, (1, LANES), 1)
    first = lane < ATT_HEAD_DIM
    one_col = jnp.where(lane == 0, 1.0, 0.0)

    ri = lax.broadcasted_iota(jnp.int32, (LANES, LANES), 0)
    ci = lax.broadcasted_iota(jnp.int32, (LANES, LANES), 1)
    same_comp = (ri // ATT_HEAD_DIM == ci // ATT_HEAD_DIM).astype(BF16)
    swap_half = ((ri // ATT_HEAD_DIM == ci // ATT_HEAD_DIM)
                 & ((ri - ci == ATT_HEAD_DIM // 2) | (ci - ri == ATT_HEAD_DIM // 2))).astype(BF16)

    def norm_rope(xb, gcos, gsin):
        xf = xb.astype(F32)
        sq = xf * xf
        hi = sq.astype(BF16)
        lo = (sq - hi.astype(F32)).astype(BF16)
        ms = jnp.dot(jnp.concatenate([hi, lo], axis=1), jnp.concatenate([same_comp, same_comp], axis=0),
                     preferred_element_type=F32) * (1.0 / ATT_HEAD_DIM)
        xr = jnp.dot(xb, swap_half, preferred_element_type=F32)
        return lax.rsqrt(ms + EPS) * (xf * gcos + xr * gsin)

    ub = (ATT_HEAD_DIM * ATT_HEAD_DIM ** -0.5 * LOG2E * BOUND_MARGIN
          * jnp.max(jnp.abs(qg_ref[...])) * jnp.max(jnp.abs(kg_ref[...])))
    safe = ub <= SHIFT_LIMIT

    ones = jnp.broadcast_to(one_col, (tq, LANES)).astype(BF16)
    shift = jnp.broadcast_to(jnp.where(lane == 0, -ub, 0.0), (tq, LANES)).astype(BF16)
    for hh in heads:
        for r in range(0, seq, tq):
            rows = slice(r, r + tq)
            kb = norm_rope(k_ref[rows, col(hh)], kc_ref[rows, :], ks_ref[rows, :]).astype(BF16)
            k_scr[hh, rows, :LANES] = kb
            k_scr[hh, rows, LANES:] = ones
            v_scr[hh, rows, :LANES] = v_ref[rows, col(hh)]
            v_scr[hh, rows, LANES:] = ones
            qb = norm_rope(q_ref[rows, col(hh)], qc_ref[rows, :], qs_ref[rows, :]).astype(BF16)
            q1_scr[hh, rows, :LANES] = jnp.where(first, qb, jnp.zeros_like(qb))
            q1_scr[hh, rows, LANES:] = shift
            q2_scr[hh, rows, :LANES] = jnp.where(first, jnp.zeros_like(qb), qb)
            q2_scr[hh, rows, LANES:] = shift

    keep = (lax.broadcasted_iota(jnp.int32, (tq, tq), 1) <= lax.broadcasted_iota(jnp.int32, (tq, tq), 0))
    nt = (((1,), (1,)), ((), ()))

    def finish(hh, o1, o2, rows0):
        lam = (jnp.exp(jnp.sum(lq1_ref[...] * lk1_ref[...], axis=-1, keepdims=True))
               - jnp.exp(jnp.sum(lq2_ref[...] * lk2_ref[...], axis=-1, keepdims=True)) + LAM_INIT)
        o = o1 - lam * o2
        ms = jnp.mean(o * o, axis=-1, keepdims=True)
        o_ref[pl.ds(rows0, tq), col(hh)] = (o * lax.rsqrt(ms + EPS) * sg_ref[...] * (1.0 - LAM_INIT)).astype(BF16)

    @pl.when(safe)
    def _():
        def block(hh, q_scr, qrow0, krow0, rows, cols, mask):
            t = lax.dot_general(q_scr[hh, pl.ds(qrow0, rows), :], k_scr[hh, pl.ds(krow0, cols), :], nt,
                                preferred_element_type=F32)
            if mask is not None:
                t = jnp.where(mask, t, -jnp.inf)
            return jnp.dot(jnp.exp2(t).astype(BF16), v_scr[hh, pl.ds(krow0, cols), :],
                           preferred_element_type=F32)

        def tile(hh, qrow0, krow0, masked):
            if not masked:
                return (block(hh, q1_scr, qrow0, krow0, tq, tq, None),
                        block(hh, q2_scr, qrow0, krow0, tq, tq, None))
            h = tq // 2
            outs = []
            for q_scr in (q1_scr, q2_scr):
                left_cols = block(hh, q_scr, qrow0, krow0, tq, h, keep[:, :h])
                corner = block(hh, q_scr, qrow0 + h, krow0 + h, h, h, keep[:h, :h])
                outs.append(jnp.concatenate([left_cols[:h], left_cols[h:] + corner], axis=0))
            return tuple(outs)

        def normalised(acc):
            return acc[:, :LANES] / acc[:, LANES:LANES + 1]

        for hh in heads:
            for qi in range(nq):
                a1, a2 = tile(hh, qi * tq, qi * tq, True)
                for j in range(qi):
                    d1, d2 = tile(hh, qi * tq, j * tq, False)
                    a1, a2 = a1 + d1, a2 + d2
                finish(hh, normalised(a1), normalised(a2), qi * tq)

    @pl.when(jnp.logical_not(safe))
    def _():
        def update(t, vt, m, l, acc):
            m_new = jnp.maximum(m, jnp.max(t, axis=-1, keepdims=True))
            alpha = jnp.exp2(m - m_new)
            pr = jnp.exp2(t - m_new)
            l_new = alpha * l + jnp.sum(pr, axis=-1, keepdims=True)
            acc_new = alpha * acc + jnp.dot(pr.astype(BF16), vt, preferred_element_type=F32)
            return m_new, l_new, acc_new

        def online(hh, rows0, n_off):
            q1 = q1_scr[hh, pl.ds(rows0, tq), :LANES]
            q2 = q2_scr[hh, pl.ds(rows0, tq), :LANES]

            def step(krow0, carry, masked):
                m1, l1, c1, m2, l2, c2 = carry
                kt = k_scr[hh, pl.ds(krow0, tq), :LANES]
                vt = v_scr[hh, pl.ds(krow0, tq), :LANES]
                t1 = lax.dot_general(q1, kt, nt, preferred_element_type=F32)
                t2 = lax.dot_general(q2, kt, nt, preferred_element_type=F32)
                if masked:
                    t1 = jnp.where(keep, t1, -jnp.inf)
                    t2 = jnp.where(keep, t2, -jnp.inf)
                m1, l1, c1 = update(t1, vt, m1, l1, c1)
                m2, l2, c2 = update(t2, vt, m2, l2, c2)
                return m1, l1, c1, m2, l2, c2

            neg = jnp.full((tq, 1), -jnp.inf, F32)
            zero1 = jnp.zeros((tq, 1), F32)
            zacc = jnp.zeros((tq, ATT_V_DIM), F32)
            carry = step(rows0, (neg, zero1, zacc, neg, zero1, zacc), True)
            m1, l1, c1, m2, l2, c2 = lax.fori_loop(
                0, n_off, lambda j, c: step(pl.multiple_of(j * tq, tq), c, False), carry)
            finish(hh, c1 / l1, c2 / l2, rows0)

        for hh in heads:
            for qi in range(nq):
                online(hh, qi * tq, qi)


def _attention(proj, qcos, qsin, kcos, ksin, qg2, kg2, lq1, lk1, lq2, lk2, sg, batch, seq):
    T = batch * seq
    tq = min(ATT_TQ, seq)
    nq = seq // tq
    vec = lambda n: pl.BlockSpec((1, n), lambda b, h: (0, 0))
    table = pl.BlockSpec((seq, LANES), lambda b, h: (0, 0))
    width = ATT_HEADS_PER_STEP * LANES
    head = lambda col: pl.BlockSpec((seq, width), lambda b, h: (b, col // width + h))
    return pl.pallas_call(
        functools.partial(_attn_kernel, tq=tq, seq=seq),
        grid=(batch, ATT_HEADS // ATT_HEADS_PER_STEP),
        in_specs=[
            head(COL_Q), head(COL_K), head(COL_V),
            table, table, table, table,
            vec(LANES), vec(LANES),
            vec(ATT_HEAD_DIM), vec(ATT_HEAD_DIM), vec(ATT_HEAD_DIM), vec(ATT_HEAD_DIM),
            vec(LANES),
        ],
        out_specs=pl.BlockSpec((seq, width), lambda b, h: (b, h)),
        out_shape=jax.ShapeDtypeStruct((T, ATT_WIDTH), BF16),
        scratch_shapes=[pltpu.VMEM((ATT_HEADS_PER_STEP, seq, 2 * LANES), BF16) for _ in range(4)],
        compiler_params=_cparams(("arbitrary", "arbitrary")),
        name="attention",
    )(proj, proj, proj, qcos, qsin, kcos, ksin, qg2, kg2, lq1, lk1, lq2, lk2, sg)


SSD_SUB = 2


def _ssd_kernel(xbc_ref, z_ref, dt_ref, cw_ref, cb_ref, dtb_ref, alog_ref, dsk_ref, ng_ref,
                o_ref, halo, state, xs_scr, bc_scr, y_scr):
    Q = SSM_CHUNK
    N = SSM_STATE
    P2 = 2 * SSM_HEAD_DIM

    @pl.when(pl.program_id(1) == 0)
    def _():
        halo[...] = jnp.zeros_like(halo)
        state[...] = jnp.zeros_like(state)

    taps = SSM_CONV - 1
    sr = lax.broadcasted_iota(jnp.int32, (taps * Q, HALO + Q), 0)
    sc = lax.broadcasted_iota(jnp.int32, (taps * Q, HALO + Q), 1)
    shift = (sc == (sr % Q) + HALO - taps + sr // Q).astype(BF16)
    ri = lax.broadcasted_iota(jnp.int32, (Q, Q), 0)
    ci = lax.broadcasted_iota(jnp.int32, (Q, Q), 1)
    tri = ri >= ci
    lane = lax.broadcasted_iota(jnp.int32, (1, P2), 1)
    left = lane < SSM_HEAD_DIM
    gw = SSM_D_INNER // SSM_GROUPS

    for sub in range(SSD_SUB):
        r0 = sub * Q
        xs_sub, bc_sub, y_sub = xs_scr.at[sub], bc_scr.at[sub], y_scr.at[sub]

        cwid = 512
        for c0 in range(0, SSM_XBC, cwid):
            cur = xbc_ref[r0:r0 + Q, c0:c0 + cwid]
            hist = halo[:, c0:c0 + cwid] if sub == 0 else xbc_ref[r0 - HALO:r0, c0:c0 + cwid]
            ext = jnp.concatenate([hist, cur], axis=0)
            shifted = jnp.dot(shift, ext, preferred_element_type=F32)
            acc = cb_ref[:, c0:c0 + cwid] + cw_ref[taps:SSM_CONV, c0:c0 + cwid] * cur.astype(F32)
            for k in range(taps):
                acc = acc + cw_ref[k:k + 1, c0:c0 + cwid] * shifted[k * Q:(k + 1) * Q, :]
            act = acc * jax.nn.sigmoid(acc)
            if c0 < SSM_D_INNER:
                xs_sub[:, c0:c0 + cwid] = act
            else:
                bc_sub[:, c0 - SSM_D_INNER:c0 - SSM_D_INNER + cwid] = act.astype(BF16)

        dt = jax.nn.softplus(dt_ref[r0:r0 + Q, :] + dtb_ref[...])
        da = dt * (-jnp.exp(alog_ref[...]) * LOG2E)
        acs = jnp.dot(tri.astype(F32), da, preferred_element_type=F32,
                      precision=lax.Precision.HIGHEST)
        acs_t = acs.T
        dt_t = dt.T
        w_t = dt_t * jnp.exp2(acs_t[:, Q - 1:Q] - acs_t)

        for g in range(SSM_GROUPS):
            bm = bc_sub[:, g * N:(g + 1) * N]
            cm = bc_sub[:, SSM_GROUPS * N + g * N:SSM_GROUPS * N + (g + 1) * N]
            cb = lax.dot_general(cm, bm, (((1,), (1,)), ((), ())), preferred_element_type=F32)
            bm_t = bm.astype(F32).T
            cm_f = cm.astype(F32)
            for pr in range(2):
                pair = 2 * g + pr
                xs_pair = xs_sub[:, pair * P2:(pair + 1) * P2].astype(BF16)
                prev = state[pair]
                rhs = jnp.concatenate([xs_pair, prev.astype(BF16)], axis=0)
                ys, sts, decs = [], [], []
                for r in range(2):
                    h = 2 * pair + r
                    a_col = acs[:, h:h + 1]
                    seg = a_col - acs_t[h:h + 1, :]
                    decay = jnp.exp2(jnp.where(tri, seg, -jnp.inf))
                    m_h = cb * decay * dt_t[h:h + 1, :]
                    e_h = cm_f * jnp.exp2(a_col)
                    lhs = jnp.concatenate([m_h, e_h], axis=1).astype(BF16)
                    ys.append(jnp.dot(lhs, rhs, preferred_element_type=F32))
                    sts.append(jnp.dot((bm_t * w_t[h:h + 1, :]).astype(BF16), xs_pair,
                                       preferred_element_type=F32))
                    decs.append(jnp.exp2(acs[Q - 1:Q, h:h + 1]))
                y_sub[:, pair * P2:(pair + 1) * P2] = jnp.where(left, ys[0], ys[1])
                dec = jnp.where(left, decs[0], decs[1])
                state[pair] = dec * prev + jnp.where(left, sts[0], sts[1])

        y = y_sub[...] + dsk_ref[...] * xs_sub[...]
        zf = z_ref[r0:r0 + Q, :].astype(F32)
        y = y * (zf * jax.nn.sigmoid(zf))
        for g in range(SSM_GROUPS):
            yg = y[:, g * gw:(g + 1) * gw]
            ms = jnp.mean(yg * yg, axis=-1, keepdims=True)
            o_ref[r0:r0 + Q, g * gw:(g + 1) * gw] = (yg * lax.rsqrt(ms + EPS)
                                                     * ng_ref[:, g * gw:(g + 1) * gw]).astype(BF16)

    halo[...] = xbc_ref[SSD_SUB * Q - HALO:SSD_SUB * Q, :]


def _ssd(proj, dt_raw, conv_w, conv_b, dt_bias, a_log, dsk, ng, batch, seq):
    T = batch * seq
    Q = SSM_CHUNK
    rows = SSD_SUB * Q
    nc = seq // rows
    vec = lambda r, n: pl.BlockSpec((r, n), lambda b, c: (0, 0))
    return pl.pallas_call(
        _ssd_kernel,
        grid=(batch, nc),
        in_specs=[
            pl.BlockSpec((rows, SSM_XBC), lambda b, c: (b * nc + c, COL_XBC // SSM_XBC)),
            pl.BlockSpec((rows, SSM_D_INNER), lambda b, c: (b * nc + c, COL_Z // SSM_D_INNER)),
            pl.BlockSpec((rows, LANES), lambda b, c: (b * nc + c, 0)),
            vec(SSM_CONV, SSM_XBC), vec(1, SSM_XBC), vec(1, LANES), vec(1, LANES),
            vec(1, SSM_D_INNER), vec(1, SSM_D_INNER),
        ],
        out_specs=pl.BlockSpec((rows, SSM_D_INNER), lambda b, c: (b * nc + c, 0)),
        out_shape=jax.ShapeDtypeStruct((T, SSM_D_INNER), BF16),
        scratch_shapes=[
            pltpu.VMEM((HALO, SSM_XBC), BF16),
            pltpu.VMEM((SSM_HEADS // 2, SSM_STATE, 2 * SSM_HEAD_DIM), F32),
            pltpu.VMEM((SSD_SUB, Q, SSM_D_INNER), F32),
            pltpu.VMEM((SSD_SUB, Q, 2 * SSM_GROUPS * SSM_STATE), BF16),
            pltpu.VMEM((SSD_SUB, Q, SSM_D_INNER), F32),
        ],
        compiler_params=_cparams(("arbitrary", "arbitrary")),
        name="ssd",
    )(proj, proj, dt_raw, conv_w, conv_b, dt_bias, a_log, dsk, ng)


def _merge_kernel(att_ref, ssm_ref, ga_ref, gs_ref, x_ref, wap_ref, wsp_ref, wo_ref, fg_ref,
                  wr_ref, br_ref, x1_ref, h2_ref, idx_ref, gate_ref, cnt_ref, base, *, tm):
    i = pl.program_id(0)

    @pl.when(i == 0)
    def _():
        base[...] = jnp.zeros_like(base)

    pa = jnp.dot(att_ref[...], wap_ref[...], preferred_element_type=F32)
    ps = jnp.dot(ssm_ref[...], wsp_ref[...], preferred_element_type=F32)
    merged = (jax.nn.sigmoid(ga_ref[...].astype(F32)) * pa
              + jax.nn.sigmoid(gs_ref[...].astype(F32)) * ps)
    x1 = x_ref[...] + jnp.dot(merged.astype(BF16), wo_ref[...], preferred_element_type=F32)
    x1_ref[...] = x1
    ms = jnp.mean(x1 * x1, axis=-1, keepdims=True)
    h2 = x1 * lax.rsqrt(ms + EPS) * fg_ref[...]
    _store_rows(h2_ref, _pack_rows(h2))

    logits = jnp.dot(h2.astype(BF16), wr_ref[...], preferred_element_type=F32) + br_ref[...]
    lg = logits.T[:N_EXPERTS, :]
    row_f = lax.broadcasted_iota(jnp.int32, (N_EXPERTS, tm), 0).astype(F32)
    vals, idxs, sels = [], [], []
    for _ in range(TOP_K):
        m = jnp.max(lg, axis=0, keepdims=True)
        idx = jnp.min(jnp.where(lg == m, row_f, float(N_EXPERTS)), axis=0, keepdims=True)
        sel = row_f == idx
        vals.append(m)
        idxs.append(idx)
        sels.append(sel)
        lg = jnp.where(sel, -jnp.inf, lg)
    es = [jnp.exp(v - vals[0]) for v in vals]
    den = es[0] + es[1] + es[2] + es[3]

    multi = jnp.zeros((N_EXPERTS, tm), F32)
    for sel in sels:
        multi = jnp.where(sel, 1.0, multi)
    ri = lax.broadcasted_iota(jnp.int32, (tm, tm), 0)
    ci = lax.broadcasted_iota(jnp.int32, (tm, tm), 1)
    before = jnp.dot(multi.astype(BF16), (ri < ci).astype(BF16), preferred_element_type=F32)
    before = before + base[...]
    ranks = [jnp.sum(jnp.where(sel, before, 0.0), axis=0, keepdims=True) for sel in sels]
    idx_ref[...] = jnp.concatenate(idxs + ranks, axis=0).astype(jnp.int32)
    gate_ref[...] = jnp.concatenate([e / den for e in es] + [jnp.zeros_like(den)] * TOP_K, axis=0)
    base[...] = base[...] + jnp.sum(multi, axis=1, keepdims=True)
    cnt_ref[...] = jnp.broadcast_to(base[...], cnt_ref.shape).astype(jnp.int32)


def _merge(att, ssm, proj, xf, wap, wsp, wo, fg, wr, br):
    T = xf.shape[0]
    tm = min(MERGE_TM, T)
    full = lambda a: pl.BlockSpec(a.shape, lambda i: (0, 0))
    return pl.pallas_call(
        functools.partial(_merge_kernel, tm=tm),
        grid=(T // tm,),
        in_specs=[
            pl.BlockSpec((tm, ATT_WIDTH), lambda i: (i, 0)),
            pl.BlockSpec((tm, SSM_D_INNER), lambda i: (i, 0)),
            pl.BlockSpec((tm, D_MODEL), lambda i: (i, COL_GA // D_MODEL)),
            pl.BlockSpec((tm, D_MODEL), lambda i: (i, COL_GS // D_MODEL)),
            pl.BlockSpec((tm, D_MODEL), lambda i: (i, 0)),
            full(wap), full(wsp), full(wo), full(fg), full(wr), full(br),
        ],
        out_specs=[
            pl.BlockSpec((tm, D_MODEL), lambda i: (i, 0)),
            pl.BlockSpec((tm * ROW_SUB, LANES), lambda i: (i, 0)),
            pl.BlockSpec((2 * TOP_K, tm), lambda i: (0, i)),
            pl.BlockSpec((2 * TOP_K, tm), lambda i: (0, i)),
            pl.BlockSpec((N_EXPERTS, LANES), lambda i: (0, 0)),
        ],
        out_shape=[
            jax.ShapeDtypeStruct((T, D_MODEL), F32),
            jax.ShapeDtypeStruct((T * ROW_SUB, LANES), jnp.uint32),
            jax.ShapeDtypeStruct((2 * TOP_K, T), jnp.int32),
            jax.ShapeDtypeStruct((2 * TOP_K, T), F32),
            jax.ShapeDtypeStruct((N_EXPERTS, LANES), jnp.int32),
        ],
        scratch_shapes=[pltpu.VMEM((N_EXPERTS, 1), F32)],
        compiler_params=_cparams(("arbitrary",)),
        name="merge_router",
    )(att, ssm, proj, proj, xf, wap, wsp, wo, fg, wr, br)


DMA_UNROLL = 8


def _dispatch_kernel(lb_ref, cnt_ref, dest_ref, h_ref, rows_ref, zeros, zsem, sem, *, tt, bm):
    @pl.when(pl.program_id(0) == 0)
    def _():
        zeros[...] = jnp.zeros_like(zeros)
        for wait in (False, True):
            for e in range(N_EXPERTS):
                @pl.when(cnt_ref[e] > 0)
                def _():
                    start = pl.multiple_of(lb_ref[e], bm)
                    cp = pltpu.make_async_copy(zeros, rows_ref.at[pl.ds(start * ROW_SUB, bm * ROW_SUB)], zsem)
                    cp.wait() if wait else cp.start()

    def row_copy(t, k):
        return pltpu.make_async_copy(_row(h_ref, t), _row(rows_ref, dest_ref[t * TOP_K + k]), sem)

    for wait in (False, True):
        def body(g, carry):
            for u in range(DMA_UNROLL):
                for k in range(TOP_K):
                    cp = row_copy(g * DMA_UNROLL + u, k)
                    cp.wait() if wait else cp.start(priority=k % 2)
            return carry
        lax.fori_loop(0, tt // DMA_UNROLL, body, 0)


def _dispatch(last_block, cnt, dest, h2p, n_rows, bm):
    T = h2p.shape[0] // ROW_SUB
    tt = min(DISPATCH_TT, T)
    grid_spec = pltpu.PrefetchScalarGridSpec(
        num_scalar_prefetch=2,
        grid=(T // tt,),
        in_specs=[
            pl.BlockSpec((tt * TOP_K,), lambda i, lb, c: (i,), memory_space=pltpu.SMEM),
            pl.BlockSpec((tt * ROW_SUB, LANES), lambda i, lb, c: (i, 0)),
        ],
        out_specs=pl.BlockSpec(memory_space=pl.ANY),
        scratch_shapes=[pltpu.VMEM((bm * ROW_SUB, LANES), jnp.uint32),
                        pltpu.SemaphoreType.DMA, pltpu.SemaphoreType.DMA],
    )
    return pl.pallas_call(
        functools.partial(_dispatch_kernel, tt=tt, bm=bm),
        grid_spec=grid_spec,
        out_shape=jax.ShapeDtypeStruct((n_rows * ROW_SUB, LANES), jnp.uint32),
        compiler_params=_cparams(("arbitrary",)),
        name="dispatch",
    )(last_block, cnt, dest, h2p)


def _expert_kernel(be_ref, nu_ref, x_ref, wgu_ref, bgu_ref, wd_ref, bd_ref, o_ref, wgu_bf, wd_bf):
    i = pl.program_id(0)

    @pl.when(i < nu_ref[0])
    def _():
        @pl.when((i == 0) | (be_ref[i] != be_ref[jnp.maximum(i - 1, 0)]))
        def _():
            wgu_bf[...] = wgu_ref[0].astype(BF16)
            wd_bf[...] = wd_ref[0].astype(BF16)

        lo, hi = _unpack_rows(_load_rows(x_ref))
        x = jnp.concatenate([lo, hi], axis=1).astype(BF16)
        gu = jnp.dot(x, wgu_bf[...], preferred_element_type=F32) + bgu_ref[0]
        gate = jnp.minimum(gu[:, :D_FF], SWIGLU_LIMIT)
        up = jnp.clip(gu[:, D_FF:], -SWIGLU_LIMIT, SWIGLU_LIMIT)
        glu = gate * jax.nn.sigmoid(SWIGLU_ALPHA * gate)
        act = ((up + 1.0) * glu).astype(BF16)
        _store_rows(o_ref, _pack_rows(jnp.dot(act, wd_bf[...], preferred_element_type=F32) + bd_ref[0]))


def _experts(block_expert, n_used, rows, wgu, bgu, wd, bd, bm):
    n_rows = rows.shape[0] // ROW_SUB
    nb = n_rows // bm
    row_map = lambda i, be, nu: (jnp.minimum(i, nu[0] - 1), 0)
    exp_map = lambda i, be, nu: (be[jnp.minimum(i, nu[0] - 1)], 0, 0)
    grid_spec = pltpu.PrefetchScalarGridSpec(
        num_scalar_prefetch=2,
        grid=(nb,),
        in_specs=[
            pl.BlockSpec((bm * ROW_SUB, LANES), row_map),
            pl.BlockSpec((1, D_MODEL, 2 * D_FF), exp_map),
            pl.BlockSpec((1, 1, 2 * D_FF), exp_map),
            pl.BlockSpec((1, D_FF, D_MODEL), exp_map),
            pl.BlockSpec((1, 1, D_MODEL), exp_map),
        ],
        out_specs=pl.BlockSpec((bm * ROW_SUB, LANES), row_map),
        scratch_shapes=[pltpu.VMEM((D_MODEL, 2 * D_FF), BF16), pltpu.VMEM((D_FF, D_MODEL), BF16)],
    )
    return pl.pallas_call(
        _expert_kernel,
        grid_spec=grid_spec,
        out_shape=jax.ShapeDtypeStruct((n_rows * ROW_SUB, LANES), jnp.uint32),
        compiler_params=_cparams(("arbitrary",)),
        name="experts",
    )(block_expert, n_used, rows, wgu, bgu, wd, bd)


def _combine_kernel(dest_ref, next_ref, x1_ref, gate_ref, y_ref, o_ref, ybuf, sems, *, tc):
    i = pl.program_id(0)
    n = pl.num_programs(0)

    def gather(idx_ref, slot, wait):
        def row_copy(t, k):
            return pltpu.make_async_copy(_row(y_ref, idx_ref[t * TOP_K + k]), _row(ybuf.at[slot, k], t),
                                         sems.at[slot])

        def body(g, carry):
            for u in range(DMA_UNROLL):
                for k in range(TOP_K):
                    cp = row_copy(g * DMA_UNROLL + u, k)
                    cp.wait() if wait else cp.start(priority=k % 2)
            return carry
        lax.fori_loop(0, tc // DMA_UNROLL, body, 0)

    slot = i % 2

    @pl.when(i == 0)
    def _():
        gather(dest_ref, 0, False)

    @pl.when(i + 1 < n)
    def _():
        gather(next_ref, 1 - slot, False)

    gather(dest_ref, slot, True)

    x1 = x1_ref[...]
    acc_lo, acc_hi = x1[:, :PACKED], x1[:, PACKED:]
    g = gate_ref[...].T
    for k in range(TOP_K):
        lo, hi = _unpack_rows(_load_rows(ybuf.at[slot, k]))
        acc_lo = acc_lo + g[:, k:k + 1] * lo
        acc_hi = acc_hi + g[:, k:k + 1] * hi
    o_ref[:, :PACKED] = acc_lo
    o_ref[:, PACKED:] = acc_hi


def _combine(dest, x1, gates, y_rows):
    T = x1.shape[0]
    tc = min(COMBINE_TC, T)
    n = T // tc
    return pl.pallas_call(
        functools.partial(_combine_kernel, tc=tc),
        grid=(n,),
        in_specs=[
            pl.BlockSpec((tc * TOP_K,), lambda i: (i,), memory_space=pltpu.SMEM),
            pl.BlockSpec((tc * TOP_K,), lambda i: (jnp.minimum(i + 1, n - 1),), memory_space=pltpu.SMEM),
            pl.BlockSpec((tc, D_MODEL), lambda i: (i, 0)),
            pl.BlockSpec((2 * TOP_K, tc), lambda i: (0, i)),
            pl.BlockSpec(memory_space=pl.ANY),
        ],
        out_specs=pl.BlockSpec((tc, D_MODEL), lambda i: (i, 0)),
        out_shape=jax.ShapeDtypeStruct((T, D_MODEL), F32),
        scratch_shapes=[pltpu.VMEM((2, TOP_K, tc * ROW_SUB, LANES), jnp.uint32),
                        pltpu.SemaphoreType.DMA((2,))],
        compiler_params=_cparams(("arbitrary",)),
        name="combine",
    )(dest, dest, x1, gates, y_rows)


def _pad_lanes(v):
    return jnp.pad(v, ((0, 0), (0, LANES - v.shape[-1])))


def kernel(x, mix_norm_g, w_in, q_norm_g, k_norm_g, lambda_q1, lambda_k1, lambda_q2, lambda_k2,
           attn_subln_g, conv_w, conv_b, dt_bias, a_log, d_skip, ssm_norm_g, w_attn_proj,
           w_ssm_proj, w_out, ffn_norm_g, w_router, b_router, w_gate_up, b_gate_up, w_down, b_down):
    B, S, D = x.shape
    T = B * S
    xf = x.reshape(T, D)
    layer = 0

    wi = w_in[layer]
    o_z = 3 * ATT_WIDTH
    o_xbc = o_z + SSM_D_INNER
    o_dt = o_xbc + SSM_XBC
    o_ga = o_dt + SSM_HEADS
    w_main = jnp.concatenate([wi[:, o_xbc:o_dt], wi[:, o_z:o_xbc], wi[:, :o_z], wi[:, o_ga:]],
                             axis=1).astype(BF16)
    w_dt = _pad_lanes(wi[:, o_dt:o_ga]).astype(BF16)
    proj, dt_raw = _in_proj(xf, mix_norm_g[layer][None, :], w_main, w_dt)

    half = ATT_HEAD_DIM // 2
    inv = ROPE_THETA ** (-jnp.arange(0, ATT_HEAD_DIM, 2, dtype=F32) / ATT_HEAD_DIM)
    ang = jnp.arange(S, dtype=F32)[:, None] * inv[None, :]
    cos2 = jnp.tile(jnp.cos(ang), (1, LANES // half))
    sin2 = jnp.tile(jnp.concatenate([-jnp.sin(ang), jnp.sin(ang)], axis=1), (1, LANES // ATT_HEAD_DIM))
    qg2 = jnp.tile(q_norm_g[layer], 2)[None, :]
    kg2 = jnp.tile(k_norm_g[layer], 2)[None, :]
    partner = lambda g: jnp.tile(jnp.roll(g, half), 2)[None, :]
    q_scale = ATT_HEAD_DIM ** -0.5 * LOG2E
    att = _attention(proj, cos2 * qg2 * q_scale, sin2 * partner(q_norm_g[layer]) * q_scale,
                     cos2 * kg2, sin2 * partner(k_norm_g[layer]), qg2, kg2,
                     lambda_q1[layer][None, :], lambda_k1[layer][None, :],
                     lambda_q2[layer][None, :], lambda_k2[layer][None, :],
                     attn_subln_g[layer][None, :], B, S)

    ssm = _ssd(proj, dt_raw, conv_w[layer], conv_b[layer][None, :],
               _pad_lanes(dt_bias[layer][None, :]), _pad_lanes(a_log[layer][None, :]),
               jnp.repeat(d_skip[layer], SSM_HEAD_DIM)[None, :], ssm_norm_g[layer][None, :], B, S)

    x1, h2p, idx_rank_t, gates_t, counts = _merge(
        att, ssm, proj, xf, w_attn_proj[layer].astype(BF16), w_ssm_proj[layer].astype(BF16),
        w_out[layer].astype(BF16), ffn_norm_g[layer][None, :],
        _pad_lanes(w_router[layer]).astype(BF16), _pad_lanes(b_router[layer][None, :]))

    bm = EXPERT_BM
    A = T * TOP_K
    n_rows = (A + N_EXPERTS * (bm - 1)) // bm * bm
    cnt = counts[:, 0]
    padded = (cnt + bm - 1) // bm * bm
    pend = jnp.cumsum(padded)
    pstart = pend - padded
    n_used = (pend[-1:] // bm).astype(jnp.int32)
    block_start = jnp.arange(n_rows // bm, dtype=jnp.int32) * bm
    block_expert = jnp.minimum(jnp.sum(block_start[:, None] >= pend[None, :], axis=1),
                               N_EXPERTS - 1).astype(jnp.int32)
    idx_rank = idx_rank_t.T
    onehot = idx_rank[:, :TOP_K, None] == jnp.arange(N_EXPERTS, dtype=jnp.int32)[None, None, :]
    dest = (jnp.sum(jnp.where(onehot, pstart[None, None, :], 0), axis=-1)
            + idx_rank[:, TOP_K:]).reshape(A).astype(jnp.int32)

    rows = _dispatch((pend - bm).astype(jnp.int32), cnt.astype(jnp.int32), dest, h2p, n_rows, bm)
    y_rows = _experts(block_expert, n_used, rows, w_gate_up[layer], b_gate_up[layer][:, None, :],
                      w_down[layer], b_down[layer][:, None, :], bm)
    out = _combine(dest, x1, gates_t, y_rows)
    return out.reshape(B, S, D)
```

```python
import functools
import math

import jax
import jax.numpy as jnp
from jax import lax
from jax.experimental import pallas as pl
from jax.experimental.pallas import tpu as pltpu

F32 = jnp.float32
BF16 = jnp.bfloat16

D_MODEL = 1024
EPS = 1e-6
ATT_HEADS = 8
ATT_HEAD_DIM = 64
ATT_V_DIM = 2 * ATT_HEAD_DIM
ATT_WIDTH = ATT_HEADS * ATT_V_DIM
ROPE_THETA = 10000.0
SSM_D_INNER = 2 * D_MODEL
SSM_HEAD_DIM = 64
SSM_HEADS = SSM_D_INNER // SSM_HEAD_DIM
SSM_GROUPS = 8
SSM_STATE = 128
SSM_CONV = 4
SSM_CHUNK = 128
SSM_XBC = SSM_D_INNER + 2 * SSM_GROUPS * SSM_STATE
N_EXPERTS = 32
TOP_K = 4
D_FF = D_MODEL
SWIGLU_LIMIT = 7.0
SWIGLU_ALPHA = 1.702
LAM_INIT = 0.8 - 0.6 * math.exp(-0.3 * 0)

LANES = 128
HALO = 16

COL_XBC = 0
COL_Z = SSM_XBC
COL_Q = COL_Z + SSM_D_INNER
COL_K = COL_Q + ATT_WIDTH
COL_V = COL_K + ATT_WIDTH
COL_GA = COL_V + ATT_WIDTH
COL_GS = COL_GA + D_MODEL
PROJ_COLS = COL_GS + D_MODEL

VMEM_LIMIT = 56 * 1024 * 1024

PROJ_TM = 2048
PROJ_TN = 1024
ATT_TQ = 512
ATT_HEADS_PER_STEP = 2
MERGE_TM = 512
DISPATCH_TT = 2048
EXPERT_BM = 1024
COMBINE_TC = 256


def _cparams(sem):
    return pltpu.CompilerParams(dimension_semantics=sem, vmem_limit_bytes=VMEM_LIMIT)


PACKED = D_MODEL // 2


def _pack_rows(x):
    lo = lax.bitcast_convert_type(x[:, :PACKED].astype(BF16).astype(F32), jnp.uint32)
    hi = lax.bitcast_convert_type(x[:, PACKED:].astype(BF16).astype(F32), jnp.uint32)
    return hi | (lo >> 16)


def _unpack_rows(p):
    lo = lax.bitcast_convert_type(p << 16, F32)
    hi = lax.bitcast_convert_type(p & jnp.uint32(0xFFFF0000), F32)
    return lo, hi


ROW_SUB = PACKED // LANES


def _store_rows(ref, packed):
    for s in range(ROW_SUB):
        ref[pl.ds(s, packed.shape[0], stride=ROW_SUB), :] = packed[:, s * LANES:(s + 1) * LANES]


def _load_rows(ref):
    n = ref.shape[0] // ROW_SUB
    return jnp.concatenate([ref[pl.ds(s, n, stride=ROW_SUB), :] for s in range(ROW_SUB)], axis=1)


def _row(ref, r):
    return ref.at[pl.ds(pl.multiple_of(r * ROW_SUB, ROW_SUB), ROW_SUB)]


def _in_proj_kernel(x_ref, g_ref, w_ref, wdt_ref, o_ref, dt_ref, h_scr):
    @pl.when(pl.program_id(1) == 0)
    def _():
        x = x_ref[...]
        ms = jnp.mean(x * x, axis=-1, keepdims=True)
        hb = (x * lax.rsqrt(ms + EPS) * g_ref[...]).astype(BF16)
        h_scr[...] = hb
        dt_ref[...] = jnp.dot(hb, wdt_ref[...], preferred_element_type=F32)

    o_ref[...] = jnp.dot(h_scr[...], w_ref[...], preferred_element_type=F32).astype(BF16)


def _in_proj(xf, g, w_main, w_dt):
    T = xf.shape[0]
    tm = min(PROJ_TM, T)
    tn = PROJ_TN
    return pl.pallas_call(
        _in_proj_kernel,
        grid=(T // tm, PROJ_COLS // tn),
        in_specs=[
            pl.BlockSpec((tm, D_MODEL), lambda i, j: (i, 0)),
            pl.BlockSpec((1, D_MODEL), lambda i, j: (0, 0)),
            pl.BlockSpec((D_MODEL, tn), lambda i, j: (0, j)),
            pl.BlockSpec((D_MODEL, LANES), lambda i, j: (0, 0)),
        ],
        out_specs=[
            pl.BlockSpec((tm, tn), lambda i, j: (i, j)),
            pl.BlockSpec((tm, LANES), lambda i, j: (i, 0)),
        ],
        out_shape=[
            jax.ShapeDtypeStruct((T, PROJ_COLS), BF16),
            jax.ShapeDtypeStruct((T, LANES), F32),
        ],
        scratch_shapes=[pltpu.VMEM((tm, D_MODEL), BF16)],
        compiler_params=_cparams(("arbitrary", "arbitrary")),
        name="in_proj",
    )(xf, g, w_main, w_dt)


LOG2E = 1.4426950408889634
SHIFT_LIMIT = 57.0
BOUND_MARGIN = 1.02


def _attn_kernel(q_ref, k_ref, v_ref, qc_ref, qs_ref, kc_ref, ks_ref, qg_ref, kg_ref,
                 lq1_ref, lk1_ref, lq2_ref, lk2_ref, sg_ref, o_ref, k_scr, v_scr, q1_scr, q2_scr, *, tq, seq):
    nq = seq // tq
    heads = range(ATT_HEADS_PER_STEP)
    col = lambda hh: slice(hh * LANES, (hh + 1) * LANES)
    lane = lax.broadcasted_iota(jnp.int32, (1, LANES), 1)
    first = lane < ATT_HEAD_DIM
    one_col = jnp.where(lane == 0, 1.0, 0.0)

    ri = lax.broadcasted_iota(jnp.int32, (LANES, LANES), 0)
    ci = lax.broadcasted_iota(jnp.int32, (LANES, LANES), 1)
    same_comp = (ri // ATT_HEAD_DIM == ci // ATT_HEAD_DIM).astype(BF16)
    swap_half = ((ri // ATT_HEAD_DIM == ci // ATT_HEAD_DIM)
                 & ((ri - ci == ATT_HEAD_DIM // 2) | (ci - ri == ATT_HEAD_DIM // 2))).astype(BF16)

    def norm_rope(xb, gcos, gsin):
        xf = xb.astype(F32)
        sq = xf * xf
        hi = sq.astype(BF16)
        lo = (sq - hi.astype(F32)).astype(BF16)
        ms = jnp.dot(jnp.concatenate([hi, lo], axis=1), jnp.concatenate([same_comp, same_comp], axis=0),
                     preferred_element_type=F32) * (1.0 / ATT_HEAD_DIM)
        xr = jnp.dot(xb, swap_half, preferred_element_type=F32)
        return lax.rsqrt(ms + EPS) * (xf * gcos + xr * gsin)

    ub = (ATT_HEAD_DIM * ATT_HEAD_DIM ** -0.5 * LOG2E * BOUND_MARGIN
          * jnp.max(jnp.abs(qg_ref[...])) * jnp.max(jnp.abs(kg_ref[...])))
    safe = ub <= SHIFT_LIMIT

    ones = jnp.broadcast_to(one_col, (tq, LANES)).astype(BF16)
    shift = jnp.broadcast_to(jnp.where(lane == 0, -ub, 0.0), (tq, LANES)).astype(BF16)
    for hh in heads:
        for r in range(0, seq, tq):
            rows = slice(r, r + tq)
            kb = norm_rope(k_ref[rows, col(hh)], kc_ref[rows, :], ks_ref[rows, :]).astype(BF16)
            k_scr[hh, rows, :LANES] = kb
            k_scr[hh, rows, LANES:] = ones
            v_scr[hh, rows, :LANES] = v_ref[rows, col(hh)]
            v_scr[hh, rows, LANES:] = ones
            qb = norm_rope(q_ref[rows, col(hh)], qc_ref[rows, :], qs_ref[rows, :]).astype(BF16)
            q1_scr[hh, rows, :LANES] = jnp.where(first, qb, jnp.zeros_like(qb))
            q1_scr[hh, rows, LANES:] = shift
            q2_scr[hh, rows, :LANES] = jnp.where(first, jnp.zeros_like(qb), qb)
            q2_scr[hh, rows, LANES:] = shift

    keep = (lax.broadcasted_iota(jnp.int32, (tq, tq), 1) <= lax.broadcasted_iota(jnp.int32, (tq, tq), 0))
    nt = (((1,), (1,)), ((), ()))

    def finish(hh, o1, o2, rows0):
        lam = (jnp.exp(jnp.sum(lq1_ref[...] * lk1_ref[...], axis=-1, keepdims=True))
               - jnp.exp(jnp.sum(lq2_ref[...] * lk2_ref[...], axis=-1, keepdims=True)) + LAM_INIT)
        o = o1 - lam * o2
        ms = jnp.mean(o * o, axis=-1, keepdims=True)
        o_ref[pl.ds(rows0, tq), col(hh)] = (o * lax.rsqrt(ms + EPS) * sg_ref[...] * (1.0 - LAM_INIT)).astype(BF16)

    @pl.when(safe)
    def _():
        def block(hh, q_scr, qrow0, krow0, rows, cols, mask):
            t = lax.dot_general(q_scr[hh, pl.ds(qrow0, rows), :], k_scr[hh, pl.ds(krow0, cols), :], nt,
                                preferred_element_type=F32)
            if mask is not None:
                t = jnp.where(mask, t, -jnp.inf)
            return jnp.dot(jnp.exp2(t).astype(BF16), v_scr[hh, pl.ds(krow0, cols), :],
                           preferred_element_type=F32)

        def tile(hh, qrow0, krow0, masked):
            if not masked:
                return (block(hh, q1_scr, qrow0, krow0, tq, tq, None),
                        block(hh, q2_scr, qrow0, krow0, tq, tq, None))
            h = tq // 2
            outs = []
            for q_scr in (q1_scr, q2_scr):
                left_cols = block(hh, q_scr, qrow0, krow0, tq, h, keep[:, :h])
                corner = block(hh, q_scr, qrow0 + h, krow0 + h, h, h, keep[:h, :h])
                outs.append(jnp.concatenate([left_cols[:h], left_cols[h:] + corner], axis=0))
            return tuple(outs)

        def normalised(acc):
            return acc[:, :LANES] / acc[:, LANES:LANES + 1]

        for hh in heads:
            for qi in range(nq):
                a1, a2 = tile(hh, qi * tq, qi * tq, True)
                for j in range(qi):
                    d1, d2 = tile(hh, qi * tq, j * tq, False)
                    a1, a2 = a1 + d1, a2 + d2
                finish(hh, normalised(a1), normalised(a2), qi * tq)

    @pl.when(jnp.logical_not(safe))
    def _():
        def update(t, vt, m, l, acc):
            m_new = jnp.maximum(m, jnp.max(t, axis=-1, keepdims=True))
            alpha = jnp.exp2(m - m_new)
            pr = jnp.exp2(t - m_new)
            l_new = alpha * l + jnp.sum(pr, axis=-1, keepdims=True)
            acc_new = alpha * acc + jnp.dot(pr.astype(BF16), vt, preferred_element_type=F32)
            return m_new, l_new, acc_new

        def online(hh, rows0, n_off):
            q1 = q1_scr[hh, pl.ds(rows0, tq), :LANES]
            q2 = q2_scr[hh, pl.ds(rows0, tq), :LANES]

            def step(krow0, carry, masked):
                m1, l1, c1, m2, l2, c2 = carry
                kt = k_scr[hh, pl.ds(krow0, tq), :LANES]
                vt = v_scr[hh, pl.ds(krow0, tq), :LANES]
                t1 = lax.dot_general(q1, kt, nt, preferred_element_type=F32)
                t2 = lax.dot_general(q2, kt, nt, preferred_element_type=F32)
                if masked:
                    t1 = jnp.where(keep, t1, -jnp.inf)
                    t2 = jnp.where(keep, t2, -jnp.inf)
                m1, l1, c1 = update(t1, vt, m1, l1, c1)
                m2, l2, c2 = update(t2, vt, m2, l2, c2)
                return m1, l1, c1, m2, l2, c2

            neg = jnp.full((tq, 1), -jnp.inf, F32)
            zero1 = jnp.zeros((tq, 1), F32)
            zacc = jnp.zeros((tq, ATT_V_DIM), F32)
            carry = step(rows0, (neg, zero1, zacc, neg, zero1, zacc), True)
            m1, l1, c1, m2, l2, c2 = lax.fori_loop(
                0, n_off, lambda j, c: step(pl.multiple_of(j * tq, tq), c, False), carry)
            finish(hh, c1 / l1, c2 / l2, rows0)

        for hh in heads:
            for qi in range(nq):
                online(hh, qi * tq, qi)


def _attention(proj, qcos, qsin, kcos, ksin, qg2, kg2, lq1, lk1, lq2, lk2, sg, batch, seq):
    T = batch * seq
    tq = min(ATT_TQ, seq)
    nq = seq // tq
    vec = lambda n: pl.BlockSpec((1, n), lambda b, h: (0, 0))
    table = pl.BlockSpec((seq, LANES), lambda b, h: (0, 0))
    width = ATT_HEADS_PER_STEP * LANES
    head = lambda col: pl.BlockSpec((seq, width), lambda b, h: (b, col // width + h))
    return pl.pallas_call(
        functools.partial(_attn_kernel, tq=tq, seq=seq),
        grid=(batch, ATT_HEADS // ATT_HEADS_PER_STEP),
        in_specs=[
            head(COL_Q), head(COL_K), head(COL_V),
            table, table, table, table,
            vec(LANES), vec(LANES),
            vec(ATT_HEAD_DIM), vec(ATT_HEAD_DIM), vec(ATT_HEAD_DIM), vec(ATT_HEAD_DIM),
            vec(LANES),
        ],
        out_specs=pl.BlockSpec((seq, width), lambda b, h: (b, h)),
        out_shape=jax.ShapeDtypeStruct((T, ATT_WIDTH), BF16),
        scratch_shapes=[pltpu.VMEM((ATT_HEADS_PER_STEP, seq, 2 * LANES), BF16) for _ in range(4)],
        compiler_params=_cparams(("arbitrary", "arbitrary")),
        name="attention",
    )(proj, proj, proj, qcos, qsin, kcos, ksin, qg2, kg2, lq1, lk1, lq2, lk2, sg)


SSD_SUB = 4


def _ssd_kernel(xbc_ref, z_ref, dt_ref, cw_ref, cb_ref, dtb_ref, alog_ref, dsk_ref, ng_ref,
                o_ref, halo, state, xs_scr, bc_scr, y_scr):
    Q = SSM_CHUNK
    N = SSM_STATE
    P2 = 2 * SSM_HEAD_DIM

    @pl.when(pl.program_id(1) == 0)
    def _():
        halo[...] = jnp.zeros_like(halo)
        state[...] = jnp.zeros_like(state)

    taps = SSM_CONV - 1
    sr = lax.broadcasted_iota(jnp.int32, (taps * Q, HALO + Q), 0)
    sc = lax.broadcasted_iota(jnp.int32, (taps * Q, HALO + Q), 1)
    shift = (sc == (sr % Q) + HALO - taps + sr // Q).astype(BF16)
    ri = lax.broadcasted_iota(jnp.int32, (Q, Q), 0)
    ci = lax.broadcasted_iota(jnp.int32, (Q, Q), 1)
    tri = ri >= ci
    lane = lax.broadcasted_iota(jnp.int32, (1, P2), 1)
    left = lane < SSM_HEAD_DIM
    gw = SSM_D_INNER // SSM_GROUPS

    for sub in range(SSD_SUB):
        r0 = sub * Q
        xs_sub, bc_sub, y_sub = xs_scr.at[sub], bc_scr.at[sub], y_scr.at[sub]

        cwid = 512
        for c0 in range(0, SSM_XBC, cwid):
            cur = xbc_ref[r0:r0 + Q, c0:c0 + cwid]
            hist = halo[:, c0:c0 + cwid] if sub == 0 else xbc_ref[r0 - HALO:r0, c0:c0 + cwid]
            ext = jnp.concatenate([hist, cur], axis=0)
            shifted = jnp.dot(shift, ext, preferred_element_type=F32)
            acc = cb_ref[:, c0:c0 + cwid] + cw_ref[taps:SSM_CONV, c0:c0 + cwid] * cur.astype(F32)
            for k in range(taps):
                acc = acc + cw_ref[k:k + 1, c0:c0 + cwid] * shifted[k * Q:(k + 1) * Q, :]
            act = acc * jax.nn.sigmoid(acc)
            if c0 < SSM_D_INNER:
                xs_sub[:, c0:c0 + cwid] = act
            else:
                bc_sub[:, c0 - SSM_D_INNER:c0 - SSM_D_INNER + cwid] = act.astype(BF16)

        dt = jax.nn.softplus(dt_ref[r0:r0 + Q, :] + dtb_ref[...])
        da = dt * (-jnp.exp(alog_ref[...]) * LOG2E)
        acs = jnp.dot(tri.astype(F32), da, preferred_element_type=F32,
                      precision=lax.Precision.HIGHEST)
        acs_t = acs.T
        dt_t = dt.T
        w_t = dt_t * jnp.exp2(acs_t[:, Q - 1:Q] - acs_t)

        for g in range(SSM_GROUPS):
            bm = bc_sub[:, g * N:(g + 1) * N]
            cm = bc_sub[:, SSM_GROUPS * N + g * N:SSM_GROUPS * N + (g + 1) * N]
            cb = lax.dot_general(cm, bm, (((1,), (1,)), ((), ())), preferred_element_type=F32)
            bm_t = bm.astype(F32).T
            cm_f = cm.astype(F32)
            for pr in range(2):
                pair = 2 * g + pr
                xs_pair = xs_sub[:, pair * P2:(pair + 1) * P2].astype(BF16)
                prev = state[pair]
                rhs = jnp.concatenate([xs_pair, prev.astype(BF16)], axis=0)
                ys, sts, decs = [], [], []
                for r in range(2):
                    h = 2 * pair + r
                    a_col = acs[:, h:h + 1]
                    seg = a_col - acs_t[h:h + 1, :]
                    decay = jnp.exp2(jnp.where(tri, seg, -jnp.inf))
                    m_h = cb * decay * dt_t[h:h + 1, :]
                    e_h = cm_f * jnp.exp2(a_col)
                    lhs = jnp.concatenate([m_h, e_h], axis=1).astype(BF16)
                    ys.append(jnp.dot(lhs, rhs, preferred_element_type=F32))
                    sts.append(jnp.dot((bm_t * w_t[h:h + 1, :]).astype(BF16), xs_pair,
                                       preferred_element_type=F32))
                    decs.append(jnp.exp2(acs[Q - 1:Q, h:h + 1]))
                y_sub[:, pair * P2:(pair + 1) * P2] = jnp.where(left, ys[0], ys[1])
                dec = jnp.where(left, decs[0], decs[1])
                state[pair] = dec * prev + jnp.where(left, sts[0], sts[1])

        y = y_sub[...] + dsk_ref[...] * xs_sub[...]
        zf = z_ref[r0:r0 + Q, :].astype(F32)
        y = y * (zf * jax.nn.sigmoid(zf))
        for g in range(SSM_GROUPS):
            yg = y[:, g * gw:(g + 1) * gw]
            ms = jnp.mean(yg * yg, axis=-1, keepdims=True)
            o_ref[r0:r0 + Q, g * gw:(g + 1) * gw] = (yg * lax.rsqrt(ms + EPS)
                                                     * ng_ref[:, g * gw:(g + 1) * gw]).astype(BF16)

    halo[...] = xbc_ref[SSD_SUB * Q - HALO:SSD_SUB * Q, :]


def _ssd(proj, dt_raw, conv_w, conv_b, dt_bias, a_log, dsk, ng, batch, seq):
    T = batch * seq
    Q = SSM_CHUNK
    rows = SSD_SUB * Q
    nc = seq // rows
    vec = lambda r, n: pl.BlockSpec((r, n), lambda b, c: (0, 0))
    return pl.pallas_call(
        _ssd_kernel,
        grid=(batch, nc),
        in_specs=[
            pl.BlockSpec((rows, SSM_XBC), lambda b, c: (b * nc + c, COL_XBC // SSM_XBC)),
            pl.BlockSpec((rows, SSM_D_INNER), lambda b, c: (b * nc + c, COL_Z // SSM_D_INNER)),
            pl.BlockSpec((rows, LANES), lambda b, c: (b * nc + c, 0)),
            vec(SSM_CONV, SSM_XBC), vec(1, SSM_XBC), vec(1, LANES), vec(1, LANES),
            vec(1, SSM_D_INNER), vec(1, SSM_D_INNER),
        ],
        out_specs=pl.BlockSpec((rows, SSM_D_INNER), lambda b, c: (b * nc + c, 0)),
        out_shape=jax.ShapeDtypeStruct((T, SSM_D_INNER), BF16),
        scratch_shapes=[
            pltpu.VMEM((HALO, SSM_XBC), BF16),
            pltpu.VMEM((SSM_HEADS // 2, SSM_STATE, 2 * SSM_HEAD_DIM), F32),
            pltpu.VMEM((SSD_SUB, Q, SSM_D_INNER), F32),
            pltpu.VMEM((SSD_SUB, Q, 2 * SSM_GROUPS * SSM_STATE), BF16),
            pltpu.VMEM((SSD_SUB, Q, SSM_D_INNER), F32),
        ],
        compiler_params=_cparams(("arbitrary", "arbitrary")),
        name="ssd",
    )(proj, proj, dt_raw, conv_w, conv_b, dt_bias, a_log, dsk, ng)


def _merge_kernel(att_ref, ssm_ref, ga_ref, gs_ref, x_ref, wap_ref, wsp_ref, wo_ref, fg_ref,
                  wr_ref, br_ref, x1_ref, h2_ref, idx_ref, gate_ref, cnt_ref, base, *, tm):
    i = pl.program_id(0)

    @pl.when(i == 0)
    def _():
        base[...] = jnp.zeros_like(base)

    pa = jnp.dot(att_ref[...], wap_ref[...], preferred_element_type=F32)
    ps = jnp.dot(ssm_ref[...], wsp_ref[...], preferred_element_type=F32)
    merged = (jax.nn.sigmoid(ga_ref[...].astype(F32)) * pa
              + jax.nn.sigmoid(gs_ref[...].astype(F32)) * ps)
    x1 = x_ref[...] + jnp.dot(merged.astype(BF16), wo_ref[...], preferred_element_type=F32)
    x1_ref[...] = x1
    ms = jnp.mean(x1 * x1, axis=-1, keepdims=True)
    h2 = x1 * lax.rsqrt(ms + EPS) * fg_ref[...]
    _store_rows(h2_ref, _pack_rows(h2))

    logits = jnp.dot(h2.astype(BF16), wr_ref[...], preferred_element_type=F32) + br_ref[...]
    lg = logits.T[:N_EXPERTS, :]
    row_f = lax.broadcasted_iota(jnp.int32, (N_EXPERTS, tm), 0).astype(F32)
    vals, idxs, sels = [], [], []
    for _ in range(TOP_K):
        m = jnp.max(lg, axis=0, keepdims=True)
        idx = jnp.min(jnp.where(lg == m, row_f, float(N_EXPERTS)), axis=0, keepdims=True)
        sel = row_f == idx
        vals.append(m)
        idxs.append(idx)
        sels.append(sel)
        lg = jnp.where(sel, -jnp.inf, lg)
    es = [jnp.exp(v - vals[0]) for v in vals]
    den = es[0] + es[1] + es[2] + es[3]

    multi = jnp.zeros((N_EXPERTS, tm), F32)
    for sel in sels:
        multi = jnp.where(sel, 1.0, multi)
    ri = lax.broadcasted_iota(jnp.int32, (tm, tm), 0)
    ci = lax.broadcasted_iota(jnp.int32, (tm, tm), 1)
    before = jnp.dot(multi.astype(BF16), (ri < ci).astype(BF16), preferred_element_type=F32)
    before = before + base[...]
    ranks = [jnp.sum(jnp.where(sel, before, 0.0), axis=0, keepdims=True) for sel in sels]
    idx_ref[...] = jnp.concatenate(idxs + ranks, axis=0).astype(jnp.int32)
    gate_ref[...] = jnp.concatenate([e / den for e in es] + [jnp.zeros_like(den)] * TOP_K, axis=0)
    base[...] = base[...] + jnp.sum(multi, axis=1, keepdims=True)
    cnt_ref[...] = jnp.broadcast_to(base[...], cnt_ref.shape).astype(jnp.int32)


def _merge(att, ssm, proj, xf, wap, wsp, wo, fg, wr, br):
    T = xf.shape[0]
    tm = min(MERGE_TM, T)
    full = lambda a: pl.BlockSpec(a.shape, lambda i: (0, 0))
    return pl.pallas_call(
        functools.partial(_merge_kernel, tm=tm),
        grid=(T // tm,),
        in_specs=[
            pl.BlockSpec((tm, ATT_WIDTH), lambda i: (i, 0)),
            pl.BlockSpec((tm, SSM_D_INNER), lambda i: (i, 0)),
            pl.BlockSpec((tm, D_MODEL), lambda i: (i, COL_GA // D_MODEL)),
            pl.BlockSpec((tm, D_MODEL), lambda i: (i, COL_GS // D_MODEL)),
            pl.BlockSpec((tm, D_MODEL), lambda i: (i, 0)),
            full(wap), full(wsp), full(wo), full(fg), full(wr), full(br),
        ],
        out_specs=[
            pl.BlockSpec((tm, D_MODEL), lambda i: (i, 0)),
            pl.BlockSpec((tm * ROW_SUB, LANES), lambda i: (i, 0)),
            pl.BlockSpec((2 * TOP_K, tm), lambda i: (0, i)),
            pl.BlockSpec((2 * TOP_K, tm), lambda i: (0, i)),
            pl.BlockSpec((N_EXPERTS, LANES), lambda i: (0, 0)),
        ],
        out_shape=[
            jax.ShapeDtypeStruct((T, D_MODEL), F32),
            jax.ShapeDtypeStruct((T * ROW_SUB, LANES), jnp.uint32),
            jax.ShapeDtypeStruct((2 * TOP_K, T), jnp.int32),
            jax.ShapeDtypeStruct((2 * TOP_K, T), F32),
            jax.ShapeDtypeStruct((N_EXPERTS, LANES), jnp.int32),
        ],
        scratch_shapes=[pltpu.VMEM((N_EXPERTS, 1), F32)],
        compiler_params=_cparams(("arbitrary",)),
        name="merge_router",
    )(att, ssm, proj, proj, xf, wap, wsp, wo, fg, wr, br)


DMA_UNROLL = 8


def _dispatch_kernel(lb_ref, cnt_ref, dest_ref, h_ref, rows_ref, zeros, zsem, sem, *, tt, bm):
    @pl.when(pl.program_id(0) == 0)
    def _():
        zeros[...] = jnp.zeros_like(zeros)
        for wait in (False, True):
            for e in range(N_EXPERTS):
                @pl.when(cnt_ref[e] > 0)
                def _():
                    start = pl.multiple_of(lb_ref[e], bm)
                    cp = pltpu.make_async_copy(zeros, rows_ref.at[pl.ds(start * ROW_SUB, bm * ROW_SUB)], zsem)
                    cp.wait() if wait else cp.start()

    def row_copy(t, k):
        return pltpu.make_async_copy(_row(h_ref, t), _row(rows_ref, dest_ref[t * TOP_K + k]), sem)

    for wait in (False, True):
        def body(g, carry):
            for u in range(DMA_UNROLL):
                for k in range(TOP_K):
                    cp = row_copy(g * DMA_UNROLL + u, k)
                    cp.wait() if wait else cp.start(priority=k % 2)
            return carry
        lax.fori_loop(0, tt // DMA_UNROLL, body, 0)


def _dispatch(last_block, cnt, dest, h2p, n_rows, bm):
    T = h2p.shape[0] // ROW_SUB
    tt = min(DISPATCH_TT, T)
    grid_spec = pltpu.PrefetchScalarGridSpec(
        num_scalar_prefetch=2,
        grid=(T // tt,),
        in_specs=[
            pl.BlockSpec((tt * TOP_K,), lambda i, lb, c: (i,), memory_space=pltpu.SMEM),
            pl.BlockSpec((tt * ROW_SUB, LANES), lambda i, lb, c: (i, 0)),
        ],
        out_specs=pl.BlockSpec(memory_space=pl.ANY),
        scratch_shapes=[pltpu.VMEM((bm * ROW_SUB, LANES), jnp.uint32),
                        pltpu.SemaphoreType.DMA, pltpu.SemaphoreType.DMA],
    )
    return pl.pallas_call(
        functools.partial(_dispatch_kernel, tt=tt, bm=bm),
        grid_spec=grid_spec,
        out_shape=jax.ShapeDtypeStruct((n_rows * ROW_SUB, LANES), jnp.uint32),
        compiler_params=_cparams(("arbitrary",)),
        name="dispatch",
    )(last_block, cnt, dest, h2p)


def _expert_kernel(be_ref, nu_ref, x_ref, wgu_ref, bgu_ref, wd_ref, bd_ref, o_ref, wgu_bf, wd_bf):
    i = pl.program_id(0)

    @pl.when(i < nu_ref[0])
    def _():
        @pl.when((i == 0) | (be_ref[i] != be_ref[jnp.maximum(i - 1, 0)]))
        def _():
            wgu_bf[...] = wgu_ref[0].astype(BF16)
            wd_bf[...] = wd_ref[0].astype(BF16)

        lo, hi = _unpack_rows(_load_rows(x_ref))
        x = jnp.concatenate([lo, hi], axis=1).astype(BF16)
        gu = jnp.dot(x, wgu_bf[...], preferred_element_type=F32) + bgu_ref[0]
        gate = jnp.minimum(gu[:, :D_FF], SWIGLU_LIMIT)
        up = jnp.clip(gu[:, D_FF:], -SWIGLU_LIMIT, SWIGLU_LIMIT)
        glu = gate * jax.nn.sigmoid(SWIGLU_ALPHA * gate)
        act = ((up + 1.0) * glu).astype(BF16)
        _store_rows(o_ref, _pack_rows(jnp.dot(act, wd_bf[...], preferred_element_type=F32) + bd_ref[0]))


def _experts(block_expert, n_used, rows, wgu, bgu, wd, bd, bm):
    n_rows = rows.shape[0] // ROW_SUB
    nb = n_rows // bm
    row_map = lambda i, be, nu: (jnp.minimum(i, nu[0] - 1), 0)
    exp_map = lambda i, be, nu: (be[jnp.minimum(i, nu[0] - 1)], 0, 0)
    grid_spec = pltpu.PrefetchScalarGridSpec(
        num_scalar_prefetch=2,
        grid=(nb,),
        in_specs=[
            pl.BlockSpec((bm * ROW_SUB, LANES), row_map),
            pl.BlockSpec((1, D_MODEL, 2 * D_FF), exp_map),
            pl.BlockSpec((1, 1, 2 * D_FF), exp_map),
            pl.BlockSpec((1, D_FF, D_MODEL), exp_map),
            pl.BlockSpec((1, 1, D_MODEL), exp_map),
        ],
        out_specs=pl.BlockSpec((bm * ROW_SUB, LANES), row_map),
        scratch_shapes=[pltpu.VMEM((D_MODEL, 2 * D_FF), BF16), pltpu.VMEM((D_FF, D_MODEL), BF16)],
    )
    return pl.pallas_call(
        _expert_kernel,
        grid_spec=grid_spec,
        out_shape=jax.ShapeDtypeStruct((n_rows * ROW_SUB, LANES), jnp.uint32),
        compiler_params=_cparams(("arbitrary",)),
        name="experts",
    )(block_expert, n_used, rows, wgu, bgu, wd, bd)


def _combine_kernel(dest_ref, next_ref, x1_ref, gate_ref, y_ref, o_ref, ybuf, sems, *, tc):
    i = pl.program_id(0)
    n = pl.num_programs(0)

    def gather(idx_ref, slot, wait):
        def row_copy(t, k):
            return pltpu.make_async_copy(_row(y_ref, idx_ref[t * TOP_K + k]), _row(ybuf.at[slot, k], t),
                                         sems.at[slot])

        def body(g, carry):
            for u in range(DMA_UNROLL):
                for k in range(TOP_K):
                    cp = row_copy(g * DMA_UNROLL + u, k)
                    cp.wait() if wait else cp.start(priority=k % 2)
            return carry
        lax.fori_loop(0, tc // DMA_UNROLL, body, 0)

    slot = i % 2

    @pl.when(i == 0)
    def _():
        gather(dest_ref, 0, False)

    @pl.when(i + 1 < n)
    def _():
        gather(next_ref, 1 - slot, False)

    gather(dest_ref, slot, True)

    x1 = x1_ref[...]
    acc_lo, acc_hi = x1[:, :PACKED], x1[:, PACKED:]
    g = gate_ref[...].T
    for k in range(TOP_K):
        lo, hi = _unpack_rows(_load_rows(ybuf.at[slot, k]))
        acc_lo = acc_lo + g[:, k:k + 1] * lo
        acc_hi = acc_hi + g[:, k:k + 1] * hi
    o_ref[:, :PACKED] = acc_lo
    o_ref[:, PACKED:] = acc_hi


def _combine(dest, x1, gates, y_rows):
    T = x1.shape[0]
    tc = min(COMBINE_TC, T)
    n = T // tc
    return pl.pallas_call(
        functools.partial(_combine_kernel, tc=tc),
        grid=(n,),
        in_specs=[
            pl.BlockSpec((tc * TOP_K,), lambda i: (i,), memory_space=pltpu.SMEM),
            pl.BlockSpec((tc * TOP_K,), lambda i: (jnp.minimum(i + 1, n - 1),), memory_space=pltpu.SMEM),
            pl.BlockSpec((tc, D_MODEL), lambda i: (i, 0)),
            pl.BlockSpec((2 * TOP_K, tc), lambda i: (0, i)),
            pl.BlockSpec(memory_space=pl.ANY),
        ],
        out_specs=pl.BlockSpec((tc, D_MODEL), lambda i: (i, 0)),
        out_shape=jax.ShapeDtypeStruct((T, D_MODEL), F32),
        scratch_shapes=[pltpu.VMEM((2, TOP_K, tc * ROW_SUB, LANES), jnp.uint32),
                        pltpu.SemaphoreType.DMA((2,))],
        compiler_params=_cparams(("arbitrary",)),
        name="combine",
    )(dest, dest, x1, gates, y_rows)


def _pad_lanes(v):
    return jnp.pad(v, ((0, 0), (0, LANES - v.shape[-1])))


def kernel(x, mix_norm_g, w_in, q_norm_g, k_norm_g, lambda_q1, lambda_k1, lambda_q2, lambda_k2,
           attn_subln_g, conv_w, conv_b, dt_bias, a_log, d_skip, ssm_norm_g, w_attn_proj,
           w_ssm_proj, w_out, ffn_norm_g, w_router, b_router, w_gate_up, b_gate_up, w_down, b_down):
    B, S, D = x.shape
    T = B * S
    xf = x.reshape(T, D)
    layer = 0

    wi = w_in[layer]
    o_z = 3 * ATT_WIDTH
    o_xbc = o_z + SSM_D_INNER
    o_dt = o_xbc + SSM_XBC
    o_ga = o_dt + SSM_HEADS
    w_main = jnp.concatenate([wi[:, o_xbc:o_dt], wi[:, o_z:o_xbc], wi[:, :o_z], wi[:, o_ga:]],
                             axis=1).astype(BF16)
    w_dt = _pad_lanes(wi[:, o_dt:o_ga]).astype(BF16)
    proj, dt_raw = _in_proj(xf, mix_norm_g[layer][None, :], w_main, w_dt)

    half = ATT_HEAD_DIM // 2
    inv = ROPE_THETA ** (-jnp.arange(0, ATT_HEAD_DIM, 2, dtype=F32) / ATT_HEAD_DIM)
    ang = jnp.arange(S, dtype=F32)[:, None] * inv[None, :]
    cos2 = jnp.tile(jnp.cos(ang), (1, LANES // half))
    sin2 = jnp.tile(jnp.concatenate([-jnp.sin(ang), jnp.sin(ang)], axis=1), (1, LANES // ATT_HEAD_DIM))
    qg2 = jnp.tile(q_norm_g[layer], 2)[None, :]
    kg2 = jnp.tile(k_norm_g[layer], 2)[None, :]
    partner = lambda g: jnp.tile(jnp.roll(g, half), 2)[None, :]
    q_scale = ATT_HEAD_DIM ** -0.5 * LOG2E
    att = _attention(proj, cos2 * qg2 * q_scale, sin2 * partner(q_norm_g[layer]) * q_scale,
                     cos2 * kg2, sin2 * partner(k_norm_g[layer]), qg2, kg2,
                     lambda_q1[layer][None, :], lambda_k1[layer][None, :],
                     lambda_q2[layer][None, :], lambda_k2[layer][None, :],
                     attn_subln_g[layer][None, :], B, S)

    ssm = _ssd(proj, dt_raw, conv_w[layer], conv_b[layer][None, :],
               _pad_lanes(dt_bias[layer][None, :]), _pad_lanes(a_log[layer][None, :]),
               jnp.repeat(d_skip[layer], SSM_HEAD_DIM)[None, :], ssm_norm_g[layer][None, :], B, S)

    x1, h2p, idx_rank_t, gates_t, counts = _merge(
        att, ssm, proj, xf, w_attn_proj[layer].astype(BF16), w_ssm_proj[layer].astype(BF16),
        w_out[layer].astype(BF16), ffn_norm_g[layer][None, :],
        _pad_lanes(w_router[layer]).astype(BF16), _pad_lanes(b_router[layer][None, :]))

    bm = EXPERT_BM
    A = T * TOP_K
    n_rows = (A + N_EXPERTS * (bm - 1)) // bm * bm
    cnt = counts[:, 0]
    padded = (cnt + bm - 1) // bm * bm
    pend = jnp.cumsum(padded)
    pstart = pend - padded
    n_used = (pend[-1:] // bm).astype(jnp.int32)
    block_start = jnp.arange(n_rows // bm, dtype=jnp.int32) * bm
    block_expert = jnp.minimum(jnp.sum(block_start[:, None] >= pend[None, :], axis=1),
                               N_EXPERTS - 1).astype(jnp.int32)
    idx_rank = idx_rank_t.T
    onehot = idx_rank[:, :TOP_K, None] == jnp.arange(N_EXPERTS, dtype=jnp.int32)[None, None, :]
    dest = (jnp.sum(jnp.where(onehot, pstart[None, None, :], 0), axis=-1)
            + idx_rank[:, TOP_K:]).reshape(A).astype(jnp.int32)

    rows = _dispatch((pend - bm).astype(jnp.int32), cnt.astype(jnp.int32), dest, h2p, n_rows, bm)
    y_rows = _experts(block_expert, n_used, rows, w_gate_up[layer], b_gate_up[layer][:, None, :],
                      w_down[layer], b_down[layer][:, None, :], bm)
    out = _combine(dest, x1, gates_t, y_rows)
    return out.reshape(B, S, D)
```

```python
import functools
import math

import jax
import jax.numpy as jnp
from jax import lax
from jax.experimental import pallas as pl
from jax.experimental.pallas import tpu as pltpu

F32 = jnp.float32
BF16 = jnp.bfloat16

D_MODEL = 1024
EPS = 1e-6
ATT_HEADS = 8
ATT_HEAD_DIM = 64
ATT_V_DIM = 2 * ATT_HEAD_DIM
ATT_WIDTH = ATT_HEADS * ATT_V_DIM
ROPE_THETA = 10000.0
SSM_D_INNER = 2 * D_MODEL
SSM_HEAD_DIM = 64
SSM_HEADS = SSM_D_INNER // SSM_HEAD_DIM
SSM_GROUPS = 8
SSM_STATE = 128
SSM_CONV = 4
SSM_CHUNK = 128
SSM_XBC = SSM_D_INNER + 2 * SSM_GROUPS * SSM_STATE
N_EXPERTS = 32
TOP_K = 4
D_FF = D_MODEL
SWIGLU_LIMIT = 7.0
SWIGLU_ALPHA = 1.702
LAM_INIT = 0.8 - 0.6 * math.exp(-0.3 * 0)

LANES = 128
HALO = 16

COL_XBC = 0
COL_Z = SSM_XBC
COL_Q = COL_Z + SSM_D_INNER
COL_K = COL_Q + ATT_WIDTH
COL_V = COL_K + ATT_WIDTH
COL_GA = COL_V + ATT_WIDTH
COL_GS = COL_GA + D_MODEL
PROJ_COLS = COL_GS + D_MODEL

VMEM_LIMIT = 56 * 1024 * 1024

PROJ_TM = 2048
PROJ_TN = 1024
ATT_TQ = 512
ATT_HEADS_PER_STEP = 2
MERGE_TM = 512
DISPATCH_TT = 2048
EXPERT_BM = 1024
COMBINE_TC = 256


def _cparams(sem):
    return pltpu.CompilerParams(dimension_semantics=sem, vmem_limit_bytes=VMEM_LIMIT)


PACKED = D_MODEL // 2


def _pack_rows(x):
    lo = lax.bitcast_convert_type(x[:, :PACKED].astype(BF16).astype(F32), jnp.uint32)
    hi = lax.bitcast_convert_type(x[:, PACKED:].astype(BF16).astype(F32), jnp.uint32)
    return hi | (lo >> 16)


def _unpack_rows(p):
    lo = lax.bitcast_convert_type(p << 16, F32)
    hi = lax.bitcast_convert_type(p & jnp.uint32(0xFFFF0000), F32)
    return lo, hi


ROW_SUB = PACKED // LANES


def _store_rows(ref, packed):
    for s in range(ROW_SUB):
        ref[pl.ds(s, packed.shape[0], stride=ROW_SUB), :] = packed[:, s * LANES:(s + 1) * LANES]


def _load_rows(ref):
    n = ref.shape[0] // ROW_SUB
    return jnp.concatenate([ref[pl.ds(s, n, stride=ROW_SUB), :] for s in range(ROW_SUB)], axis=1)


def _row(ref, r):
    return ref.at[pl.ds(pl.multiple_of(r * ROW_SUB, ROW_SUB), ROW_SUB)]


def _in_proj_kernel(x_ref, g_ref, w_ref, wdt_ref, o_ref, dt_ref, h_scr):
    @pl.when(pl.program_id(1) == 0)
    def _():
        x = x_ref[...]
        ms = jnp.mean(x * x, axis=-1, keepdims=True)
        hb = (x * lax.rsqrt(ms + EPS) * g_ref[...]).astype(BF16)
        h_scr[...] = hb
        dt_ref[...] = jnp.dot(hb, wdt_ref[...], preferred_element_type=F32)

    o_ref[...] = jnp.dot(h_scr[...], w_ref[...], preferred_element_type=F32).astype(BF16)


def _in_proj(xf, g, w_main, w_dt):
    T = xf.shape[0]
    tm = min(PROJ_TM, T)
    tn = PROJ_TN
    return pl.pallas_call(
        _in_proj_kernel,
        grid=(T // tm, PROJ_COLS // tn),
        in_specs=[
            pl.BlockSpec((tm, D_MODEL), lambda i, j: (i, 0)),
            pl.BlockSpec((1, D_MODEL), lambda i, j: (0, 0)),
            pl.BlockSpec((D_MODEL, tn), lambda i, j: (0, j)),
            pl.BlockSpec((D_MODEL, LANES), lambda i, j: (0, 0)),
        ],
        out_specs=[
            pl.BlockSpec((tm, tn), lambda i, j: (i, j)),
            pl.BlockSpec((tm, LANES), lambda i, j: (i, 0)),
        ],
        out_shape=[
            jax.ShapeDtypeStruct((T, PROJ_COLS), BF16),
            jax.ShapeDtypeStruct((T, LANES), F32),
        ],
        scratch_shapes=[pltpu.VMEM((tm, D_MODEL), BF16)],
        compiler_params=_cparams(("arbitrary", "arbitrary")),
        name="in_proj",
    )(xf, g, w_main, w_dt)


LOG2E = 1.4426950408889634
SHIFT_LIMIT = 57.0
BOUND_MARGIN = 1.02


def _attn_kernel(q_ref, k_ref, v_ref, qc_ref, qs_ref, kc_ref, ks_ref, qg_ref, kg_ref,
                 lq1_ref, lk1_ref, lq2_ref, lk2_ref, sg_ref, o_ref, k_scr, v_scr, q1_scr, q2_scr, *, tq, seq):
    nq = seq // tq
    heads = range(ATT_HEADS_PER_STEP)
    col = lambda hh: slice(hh * LANES, (hh + 1) * LANES)
    lane = lax.broadcasted_iota(jnp.int32, (1, LANES), 1)
    first = lane < ATT_HEAD_DIM
    one_col = jnp.where(lane == 0, 1.0, 0.0)

    ri = lax.broadcasted_iota(jnp.int32, (LANES, LANES), 0)
    ci = lax.broadcasted_iota(jnp.int32, (LANES, LANES), 1)
    same_comp = (ri // ATT_HEAD_DIM == ci // ATT_HEAD_DIM).astype(BF16)
    swap_half = ((ri // ATT_HEAD_DIM == ci // ATT_HEAD_DIM)
                 & ((ri - ci == ATT_HEAD_DIM // 2) | (ci - ri == ATT_HEAD_DIM // 2))).astype(BF16)

    def norm_rope(xb, gcos, gsin):
        xf = xb.astype(F32)
        sq = xf * xf
        hi = sq.astype(BF16)
        lo = (sq - hi.astype(F32)).astype(BF16)
        ms = jnp.dot(jnp.concatenate([hi, lo], axis=1), jnp.concatenate([same_comp, same_comp], axis=0),
                     preferred_element_type=F32) * (1.0 / ATT_HEAD_DIM)
        xr = jnp.dot(xb, swap_half, preferred_element_type=F32)
        return lax.rsqrt(ms + EPS) * (xf * gcos + xr * gsin)

    ub = (ATT_HEAD_DIM * ATT_HEAD_DIM ** -0.5 * LOG2E * BOUND_MARGIN
          * jnp.max(jnp.abs(qg_ref[...])) * jnp.max(jnp.abs(kg_ref[...])))
    safe = ub <= SHIFT_LIMIT

    ones = jnp.broadcast_to(one_col, (tq, LANES)).astype(BF16)
    shift = jnp.broadcast_to(jnp.where(lane == 0, -ub, 0.0), (tq, LANES)).astype(BF16)
    for hh in heads:
        for r in range(0, seq, tq):
            rows = slice(r, r + tq)
            kb = norm_rope(k_ref[rows, col(hh)], kc_ref[rows, :], ks_ref[rows, :]).astype(BF16)
            k_scr[hh, rows, :LANES] = kb
            k_scr[hh, rows, LANES:] = ones
            v_scr[hh, rows, :LANES] = v_ref[rows, col(hh)]
            v_scr[hh, rows, LANES:] = ones
            qb = norm_rope(q_ref[rows, col(hh)], qc_ref[rows, :], qs_ref[rows, :]).astype(BF16)
            q1_scr[hh, rows, :LANES] = jnp.where(first, qb, jnp.zeros_like(qb))
            q1_scr[hh, rows, LANES:] = shift
            q2_scr[hh, rows, :LANES] = jnp.where(first, jnp.zeros_like(qb), qb)
            q2_scr[hh, rows, LANES:] = shift

    keep = (lax.broadcasted_iota(jnp.int32, (tq, tq), 1) <= lax.broadcasted_iota(jnp.int32, (tq, tq), 0))
    nt = (((1,), (1,)), ((), ()))

    def finish(hh, o1, o2, rows0):
        lam = (jnp.exp(jnp.sum(lq1_ref[...] * lk1_ref[...], axis=-1, keepdims=True))
               - jnp.exp(jnp.sum(lq2_ref[...] * lk2_ref[...], axis=-1, keepdims=True)) + LAM_INIT)
        o = o1 - lam * o2
        ms = jnp.mean(o * o, axis=-1, keepdims=True)
        o_ref[pl.ds(rows0, tq), col(hh)] = (o * lax.rsqrt(ms + EPS) * sg_ref[...] * (1.0 - LAM_INIT)).astype(BF16)

    @pl.when(safe)
    def _():
        def block(hh, q_scr, qrow0, krow0, rows, cols, mask):
            t = lax.dot_general(q_scr[hh, pl.ds(qrow0, rows), :], k_scr[hh, pl.ds(krow0, cols), :], nt,
                                preferred_element_type=F32)
            if mask is not None:
                t = jnp.where(mask, t, -jnp.inf)
            return jnp.dot(jnp.exp2(t).astype(BF16), v_scr[hh, pl.ds(krow0, cols), :],
                           preferred_element_type=F32)

        def tile(hh, qrow0, krow0, masked):
            if not masked:
                return (block(hh, q1_scr, qrow0, krow0, tq, tq, None),
                        block(hh, q2_scr, qrow0, krow0, tq, tq, None))
            h = tq // 2
            outs = []
            for q_scr in (q1_scr, q2_scr):
                left_cols = block(hh, q_scr, qrow0, krow0, tq, h, keep[:, :h])
                corner = block(hh, q_scr, qrow0 + h, krow0 + h, h, h, keep[:h, :h])
                outs.append(jnp.concatenate([left_cols[:h], left_cols[h:] + corner], axis=0))
            return tuple(outs)

        def normalised(acc):
            return acc[:, :LANES] / acc[:, LANES:LANES + 1]

        for hh in heads:
            for qi in range(nq):
                a1, a2 = tile(hh, qi * tq, qi * tq, True)
                for j in range(qi):
                    d1, d2 = tile(hh, qi * tq, j * tq, False)
                    a1, a2 = a1 + d1, a2 + d2
                finish(hh, normalised(a1), normalised(a2), qi * tq)

    @pl.when(jnp.logical_not(safe))
    def _():
        def update(t, vt, m, l, acc):
            m_new = jnp.maximum(m, jnp.max(t, axis=-1, keepdims=True))
            alpha = jnp.exp2(m - m_new)
            pr = jnp.exp2(t - m_new)
            l_new = alpha * l + jnp.sum(pr, axis=-1, keepdims=True)
            acc_new = alpha * acc + jnp.dot(pr.astype(BF16), vt, preferred_element_type=F32)
            return m_new, l_new, acc_new

        def online(hh, rows0, n_off):
            q1 = q1_scr[hh, pl.ds(rows0, tq), :LANES]
            q2 = q2_scr[hh, pl.ds(rows0, tq), :LANES]

            def step(krow0, carry, masked):
                m1, l1, c1, m2, l2, c2 = carry
                kt = k_scr[hh, pl.ds(krow0, tq), :LANES]
                vt = v_scr[hh, pl.ds(krow0, tq), :LANES]
                t1 = lax.dot_general(q1, kt, nt, preferred_element_type=F32)
                t2 = lax.dot_general(q2, kt, nt, preferred_element_type=F32)
                if masked:
                    t1 = jnp.where(keep, t1, -jnp.inf)
                    t2 = jnp.where(keep, t2, -jnp.inf)
                m1, l1, c1 = update(t1, vt, m1, l1, c1)
                m2, l2, c2 = update(t2, vt, m2, l2, c2)
                return m1, l1, c1, m2, l2, c2

            neg = jnp.full((tq, 1), -jnp.inf, F32)
            zero1 = jnp.zeros((tq, 1), F32)
            zacc = jnp.zeros((tq, ATT_V_DIM), F32)
            carry = step(rows0, (neg, zero1, zacc, neg, zero1, zacc), True)
            m1, l1, c1, m2, l2, c2 = lax.fori_loop(
                0, n_off, lambda j, c: step(pl.multiple_of(j * tq, tq), c, False), carry)
            finish(hh, c1 / l1, c2 / l2, rows0)

        for hh in heads:
            for qi in range(nq):
                online(hh, qi * tq, qi)


def _attention(proj, qcos, qsin, kcos, ksin, qg2, kg2, lq1, lk1, lq2, lk2, sg, batch, seq):
    T = batch * seq
    tq = min(ATT_TQ, seq)
    nq = seq // tq
    vec = lambda n: pl.BlockSpec((1, n), lambda b, h: (0, 0))
    table = pl.BlockSpec((seq, LANES), lambda b, h: (0, 0))
    width = ATT_HEADS_PER_STEP * LANES
    head = lambda col: pl.BlockSpec((seq, width), lambda b, h: (b, col // width + h))
    return pl.pallas_call(
        functools.partial(_attn_kernel, tq=tq, seq=seq),
        grid=(batch, ATT_HEADS // ATT_HEADS_PER_STEP),
        in_specs=[
            head(COL_Q), head(COL_K), head(COL_V),
            table, table, table, table,
            vec(LANES), vec(LANES),
            vec(ATT_HEAD_DIM), vec(ATT_HEAD_DIM), vec(ATT_HEAD_DIM), vec(ATT_HEAD_DIM),
            vec(LANES),
        ],
        out_specs=pl.BlockSpec((seq, width), lambda b, h: (b, h)),
        out_shape=jax.ShapeDtypeStruct((T, ATT_WIDTH), BF16),
        scratch_shapes=[pltpu.VMEM((ATT_HEADS_PER_STEP, seq, 2 * LANES), BF16) for _ in range(4)],
        compiler_params=_cparams(("arbitrary", "arbitrary")),
        name="attention",
    )(proj, proj, proj, qcos, qsin, kcos, ksin, qg2, kg2, lq1, lk1, lq2, lk2, sg)


SSD_SUB = 2


def _ssd_kernel(xbc_ref, z_ref, dt_ref, cw_ref, cb_ref, dtb_ref, alog_ref, dsk_ref, ng_ref,
                o_ref, halo, state, xs_scr, bc_scr, y_scr):
    Q = SSM_CHUNK
    N = SSM_STATE
    P2 = 2 * SSM_HEAD_DIM

    @pl.when(pl.program_id(1) == 0)
    def _():
        halo[...] = jnp.zeros_like(halo)
        state[...] = jnp.zeros_like(state)

    taps = SSM_CONV - 1
    sr = lax.broadcasted_iota(jnp.int32, (taps * Q, HALO + Q), 0)
    sc = lax.broadcasted_iota(jnp.int32, (taps * Q, HALO + Q), 1)
    shift = (sc == (sr % Q) + HALO - taps + sr // Q).astype(BF16)
    ri = lax.broadcasted_iota(jnp.int32, (Q, Q), 0)
    ci = lax.broadcasted_iota(jnp.int32, (Q, Q), 1)
    tri = ri >= ci
    lane = lax.broadcasted_iota(jnp.int32, (1, P2), 1)
    left = lane < SSM_HEAD_DIM
    gw = SSM_D_INNER // SSM_GROUPS

    for sub in range(SSD_SUB):
        r0 = sub * Q
        xs_sub, bc_sub, y_sub = xs_scr.at[sub], bc_scr.at[sub], y_scr.at[sub]

        cwid = 512
        for c0 in range(0, SSM_XBC, cwid):
            cur = xbc_ref[r0:r0 + Q, c0:c0 + cwid]
            hist = halo[:, c0:c0 + cwid] if sub == 0 else xbc_ref[r0 - HALO:r0, c0:c0 + cwid]
            ext = jnp.concatenate([hist, cur], axis=0)
            shifted = jnp.dot(shift, ext, preferred_element_type=F32)
            acc = cb_ref[:, c0:c0 + cwid] + cw_ref[taps:SSM_CONV, c0:c0 + cwid] * cur.astype(F32)
            for k in range(taps):
                acc = acc + cw_ref[k:k + 1, c0:c0 + cwid] * shifted[k * Q:(k + 1) * Q, :]
            act = acc * jax.nn.sigmoid(acc)
            if c0 < SSM_D_INNER:
                xs_sub[:, c0:c0 + cwid] = act
            else:
                bc_sub[:, c0 - SSM_D_INNER:c0 - SSM_D_INNER + cwid] = act.astype(BF16)

        dt = jax.nn.softplus(dt_ref[r0:r0 + Q, :] + dtb_ref[...])
        da = dt * (-jnp.exp(alog_ref[...]) * LOG2E)
        da_hi = da.astype(BF16)
        rest = da - da_hi.astype(F32)
        da_mid = rest.astype(BF16)
        da_lo = (rest - da_mid.astype(F32)).astype(BF16)
        tri_b = tri.astype(BF16)
        acs = jnp.dot(jnp.concatenate([tri_b, tri_b, tri_b], axis=1),
                      jnp.concatenate([da_hi, da_mid, da_lo], axis=0), preferred_element_type=F32)
        acs_t = acs.T
        dt_t = dt.T
        w_t = dt_t * jnp.exp2(acs_t[:, Q - 1:Q] - acs_t)

        for g in range(SSM_GROUPS):
            bm = bc_sub[:, g * N:(g + 1) * N]
            cm = bc_sub[:, SSM_GROUPS * N + g * N:SSM_GROUPS * N + (g + 1) * N]
            cb = lax.dot_general(cm, bm, (((1,), (1,)), ((), ())), preferred_element_type=F32)
            bm_t = bm.astype(F32).T
            cm_f = cm.astype(F32)
            for pr in range(2):
                pair = 2 * g + pr
                xs_pair = xs_sub[:, pair * P2:(pair + 1) * P2].astype(BF16)
                prev = state[pair]
                rhs = jnp.concatenate([xs_pair, prev.astype(BF16)], axis=0)
                ys, sts, decs = [], [], []
                for r in range(2):
                    h = 2 * pair + r
                    a_col = acs[:, h:h + 1]
                    seg = a_col - acs_t[h:h + 1, :]
                    decay = jnp.exp2(jnp.where(tri, seg, -jnp.inf))
                    m_h = cb * decay * dt_t[h:h + 1, :]
                    e_h = cm_f * jnp.exp2(a_col)
                    lhs = jnp.concatenate([m_h, e_h], axis=1).astype(BF16)
                    ys.append(jnp.dot(lhs, rhs, preferred_element_type=F32))
                    sts.append(jnp.dot((bm_t * w_t[h:h + 1, :]).astype(BF16), xs_pair,
                                       preferred_element_type=F32))
                    decs.append(jnp.exp2(acs[Q - 1:Q, h:h + 1]))
                y_sub[:, pair * P2:(pair + 1) * P2] = jnp.where(left, ys[0], ys[1])
                dec = jnp.where(left, decs[0], decs[1])
                state[pair] = dec * prev + jnp.where(left, sts[0], sts[1])

        y = y_sub[...] + dsk_ref[...] * xs_sub[...]
        zf = z_ref[r0:r0 + Q, :].astype(F32)
        y = y * (zf * jax.nn.sigmoid(zf))
        for g in range(SSM_GROUPS):
            yg = y[:, g * gw:(g + 1) * gw]
            ms = jnp.mean(yg * yg, axis=-1, keepdims=True)
            o_ref[r0:r0 + Q, g * gw:(g + 1) * gw] = (yg * lax.rsqrt(ms + EPS)
                                                     * ng_ref[:, g * gw:(g + 1) * gw]).astype(BF16)

    halo[...] = xbc_ref[SSD_SUB * Q - HALO:SSD_SUB * Q, :]


def _ssd(proj, dt_raw, conv_w, conv_b, dt_bias, a_log, dsk, ng, batch, seq):
    T = batch * seq
    Q = SSM_CHUNK
    rows = SSD_SUB * Q
    nc = seq // rows
    vec = lambda r, n: pl.BlockSpec((r, n), lambda b, c: (0, 0))
    return pl.pallas_call(
        _ssd_kernel,
        grid=(batch, nc),
        in_specs=[
            pl.BlockSpec((rows, SSM_XBC), lambda b, c: (b * nc + c, COL_XBC // SSM_XBC)),
            pl.BlockSpec((rows, SSM_D_INNER), lambda b, c: (b * nc + c, COL_Z // SSM_D_INNER)),
            pl.BlockSpec((rows, LANES), lambda b, c: (b * nc + c, 0)),
            vec(SSM_CONV, SSM_XBC), vec(1, SSM_XBC), vec(1, LANES), vec(1, LANES),
            vec(1, SSM_D_INNER), vec(1, SSM_D_INNER),
        ],
        out_specs=pl.BlockSpec((rows, SSM_D_INNER), lambda b, c: (b * nc + c, 0)),
        out_shape=jax.ShapeDtypeStruct((T, SSM_D_INNER), BF16),
        scratch_shapes=[
            pltpu.VMEM((HALO, SSM_XBC), BF16),
            pltpu.VMEM((SSM_HEADS // 2, SSM_STATE, 2 * SSM_HEAD_DIM), F32),
            pltpu.VMEM((SSD_SUB, Q, SSM_D_INNER), F32),
            pltpu.VMEM((SSD_SUB, Q, 2 * SSM_GROUPS * SSM_STATE), BF16),
            pltpu.VMEM((SSD_SUB, Q, SSM_D_INNER), F32),
        ],
        compiler_params=_cparams(("arbitrary", "arbitrary")),
        name="ssd",
    )(proj, proj, dt_raw, conv_w, conv_b, dt_bias, a_log, dsk, ng)


def _merge_kernel(att_ref, ssm_ref, ga_ref, gs_ref, x_ref, wap_ref, wsp_ref, wo_ref, fg_ref,
                  wr_ref, br_ref, x1_ref, h2_ref, idx_ref, gate_ref, cnt_ref, base, *, tm):
    i = pl.program_id(0)

    @pl.when(i == 0)
    def _():
        base[...] = jnp.zeros_like(base)

    pa = jnp.dot(att_ref[...], wap_ref[...], preferred_element_type=F32)
    ps = jnp.dot(ssm_ref[...], wsp_ref[...], preferred_element_type=F32)
    merged = (jax.nn.sigmoid(ga_ref[...].astype(F32)) * pa
              + jax.nn.sigmoid(gs_ref[...].astype(F32)) * ps)
    x1 = x_ref[...] + jnp.dot(merged.astype(BF16), wo_ref[...], preferred_element_type=F32)
    x1_ref[...] = x1
    ms = jnp.mean(x1 * x1, axis=-1, keepdims=True)
    h2 = x1 * lax.rsqrt(ms + EPS) * fg_ref[...]
    _store_rows(h2_ref, _pack_rows(h2))

    logits = jnp.dot(h2.astype(BF16), wr_ref[...], preferred_element_type=F32) + br_ref[...]
    lg = logits.T[:N_EXPERTS, :]
    row_f = lax.broadcasted_iota(jnp.int32, (N_EXPERTS, tm), 0).astype(F32)
    vals, idxs, sels = [], [], []
    for _ in range(TOP_K):
        m = jnp.max(lg, axis=0, keepdims=True)
        idx = jnp.min(jnp.where(lg == m, row_f, float(N_EXPERTS)), axis=0, keepdims=True)
        sel = row_f == idx
        vals.append(m)
        idxs.append(idx)
        sels.append(sel)
        lg = jnp.where(sel, -jnp.inf, lg)
    es = [jnp.exp(v - vals[0]) for v in vals]
    den = es[0] + es[1] + es[2] + es[3]

    multi = jnp.zeros((N_EXPERTS, tm), F32)
    for sel in sels:
        multi = jnp.where(sel, 1.0, multi)
    ri = lax.broadcasted_iota(jnp.int32, (tm, tm), 0)
    ci = lax.broadcasted_iota(jnp.int32, (tm, tm), 1)
    before = jnp.dot(multi.astype(BF16), (ri < ci).astype(BF16), preferred_element_type=F32)
    before = before + base[...]
    ranks = [jnp.sum(jnp.where(sel, before, 0.0), axis=0, keepdims=True) for sel in sels]
    idx_ref[...] = jnp.concatenate(idxs + ranks, axis=0).astype(jnp.int32)
    gate_ref[...] = jnp.concatenate([e / den for e in es] + [jnp.zeros_like(den)] * TOP_K, axis=0)
    base[...] = base[...] + jnp.sum(multi, axis=1, keepdims=True)
    cnt_ref[...] = jnp.broadcast_to(base[...], cnt_ref.shape).astype(jnp.int32)


def _merge(att, ssm, proj, xf, wap, wsp, wo, fg, wr, br):
    T = xf.shape[0]
    tm = min(MERGE_TM, T)
    full = lambda a: pl.BlockSpec(a.shape, lambda i: (0, 0))
    return pl.pallas_call(
        functools.partial(_merge_kernel, tm=tm),
        grid=(T // tm,),
        in_specs=[
            pl.BlockSpec((tm, ATT_WIDTH), lambda i: (i, 0)),
            pl.BlockSpec((tm, SSM_D_INNER), lambda i: (i, 0)),
            pl.BlockSpec((tm, D_MODEL), lambda i: (i, COL_GA // D_MODEL)),
            pl.BlockSpec((tm, D_MODEL), lambda i: (i, COL_GS // D_MODEL)),
            pl.BlockSpec((tm, D_MODEL), lambda i: (i, 0)),
            full(wap), full(wsp), full(wo), full(fg), full(wr), full(br),
        ],
        out_specs=[
            pl.BlockSpec((tm, D_MODEL), lambda i: (i, 0)),
            pl.BlockSpec((tm * ROW_SUB, LANES), lambda i: (i, 0)),
            pl.BlockSpec((2 * TOP_K, tm), lambda i: (0, i)),
            pl.BlockSpec((2 * TOP_K, tm), lambda i: (0, i)),
            pl.BlockSpec((N_EXPERTS, LANES), lambda i: (0, 0)),
        ],
        out_shape=[
            jax.ShapeDtypeStruct((T, D_MODEL), F32),
            jax.ShapeDtypeStruct((T * ROW_SUB, LANES), jnp.uint32),
            jax.ShapeDtypeStruct((2 * TOP_K, T), jnp.int32),
            jax.ShapeDtypeStruct((2 * TOP_K, T), F32),
            jax.ShapeDtypeStruct((N_EXPERTS, LANES), jnp.int32),
        ],
        scratch_shapes=[pltpu.VMEM((N_EXPERTS, 1), F32)],
        compiler_params=_cparams(("arbitrary",)),
        name="merge_router",
    )(att, ssm, proj, proj, xf, wap, wsp, wo, fg, wr, br)


DMA_UNROLL = 8


def _dispatch_kernel(lb_ref, cnt_ref, dest_ref, h_ref, rows_ref, zeros, zsem, sem, *, tt, bm):
    @pl.when(pl.program_id(0) == 0)
    def _():
        zeros[...] = jnp.zeros_like(zeros)
        for wait in (False, True):
            for e in range(N_EXPERTS):
                @pl.when(cnt_ref[e] > 0)
                def _():
                    start = pl.multiple_of(lb_ref[e], bm)
                    cp = pltpu.make_async_copy(zeros, rows_ref.at[pl.ds(start * ROW_SUB, bm * ROW_SUB)], zsem)
                    cp.wait() if wait else cp.start()

    def row_copy(t, k):
        return pltpu.make_async_copy(_row(h_ref, t), _row(rows_ref, dest_ref[t * TOP_K + k]), sem)

    for wait in (False, True):
        def body(g, carry):
            for u in range(DMA_UNROLL):
                for k in range(TOP_K):
                    cp = row_copy(g * DMA_UNROLL + u, k)
                    cp.wait() if wait else cp.start(priority=k % 2)
            return carry
        lax.fori_loop(0, tt // DMA_UNROLL, body, 0)


def _dispatch(last_block, cnt, dest, h2p, n_rows, bm):
    T = h2p.shape[0] // ROW_SUB
    tt = min(DISPATCH_TT, T)
    grid_spec = pltpu.PrefetchScalarGridSpec(
        num_scalar_prefetch=2,
        grid=(T // tt,),
        in_specs=[
            pl.BlockSpec((tt * TOP_K,), lambda i, lb, c: (i,), memory_space=pltpu.SMEM),
            pl.BlockSpec((tt * ROW_SUB, LANES), lambda i, lb, c: (i, 0)),
        ],
        out_specs=pl.BlockSpec(memory_space=pl.ANY),
        scratch_shapes=[pltpu.VMEM((bm * ROW_SUB, LANES), jnp.uint32),
                        pltpu.SemaphoreType.DMA, pltpu.SemaphoreType.DMA],
    )
    return pl.pallas_call(
        functools.partial(_dispatch_kernel, tt=tt, bm=bm),
        grid_spec=grid_spec,
        out_shape=jax.ShapeDtypeStruct((n_rows * ROW_SUB, LANES), jnp.uint32),
        compiler_params=_cparams(("arbitrary",)),
        name="dispatch",
    )(last_block, cnt, dest, h2p)


def _expert_kernel(be_ref, nu_ref, x_ref, wgu_ref, bgu_ref, wd_ref, bd_ref, o_ref, wgu_bf, wd_bf):
    i = pl.program_id(0)

    @pl.when(i < nu_ref[0])
    def _():
        @pl.when((i == 0) | (be_ref[i] != be_ref[jnp.maximum(i - 1, 0)]))
        def _():
            wgu_bf[...] = wgu_ref[0].astype(BF16)
            wd_bf[...] = wd_ref[0].astype(BF16)

        lo, hi = _unpack_rows(_load_rows(x_ref))
        x = jnp.concatenate([lo, hi], axis=1).astype(BF16)
        gu = jnp.dot(x, wgu_bf[...], preferred_element_type=F32) + bgu_ref[0]
        gate = jnp.minimum(gu[:, :D_FF], SWIGLU_LIMIT)
        up = jnp.clip(gu[:, D_FF:], -SWIGLU_LIMIT, SWIGLU_LIMIT)
        glu = gate * jax.nn.sigmoid(SWIGLU_ALPHA * gate)
        act = ((up + 1.0) * glu).astype(BF16)
        _store_rows(o_ref, _pack_rows(jnp.dot(act, wd_bf[...], preferred_element_type=F32) + bd_ref[0]))


def _experts(block_expert, n_used, rows, wgu, bgu, wd, bd, bm):
    n_rows = rows.shape[0] // ROW_SUB
    nb = n_rows // bm
    row_map = lambda i, be, nu: (jnp.minimum(i, nu[0] - 1), 0)
    exp_map = lambda i, be, nu: (be[jnp.minimum(i, nu[0] - 1)], 0, 0)
    grid_spec = pltpu.PrefetchScalarGridSpec(
        num_scalar_prefetch=2,
        grid=(nb,),
        in_specs=[
            pl.BlockSpec((bm * ROW_SUB, LANES), row_map),
            pl.BlockSpec((1, D_MODEL, 2 * D_FF), exp_map),
            pl.BlockSpec((1, 1, 2 * D_FF), exp_map),
            pl.BlockSpec((1, D_FF, D_MODEL), exp_map),
            pl.BlockSpec((1, 1, D_MODEL), exp_map),
        ],
        out_specs=pl.BlockSpec((bm * ROW_SUB, LANES), row_map),
        scratch_shapes=[pltpu.VMEM((D_MODEL, 2 * D_FF), BF16), pltpu.VMEM((D_FF, D_MODEL), BF16)],
    )
    return pl.pallas_call(
        _expert_kernel,
        grid_spec=grid_spec,
        out_shape=jax.ShapeDtypeStruct((n_rows * ROW_SUB, LANES), jnp.uint32),
        compiler_params=_cparams(("arbitrary",)),
        name="experts",
    )(block_expert, n_used, rows, wgu, bgu, wd, bd)


def _combine_kernel(dest_ref, next_ref, x1_ref, gate_ref, y_ref, o_ref, ybuf, sems, *, tc):
    i = pl.program_id(0)
    n = pl.num_programs(0)

    def gather(idx_ref, slot, wait):
        def row_copy(t, k):
            return pltpu.make_async_copy(_row(y_ref, idx_ref[t * TOP_K + k]), _row(ybuf.at[slot, k], t),
                                         sems.at[slot])

        def body(g, carry):
            for u in range(DMA_UNROLL):
                for k in range(TOP_K):
                    cp = row_copy(g * DMA_UNROLL + u, k)
                    cp.wait() if wait else cp.start(priority=k % 2)
            return carry
        lax.fori_loop(0, tc // DMA_UNROLL, body, 0)

    slot = i % 2

    @pl.when(i == 0)
    def _():
        gather(dest_ref, 0, False)

    @pl.when(i + 1 < n)
    def _():
        gather(next_ref, 1 - slot, False)

    gather(dest_ref, slot, True)

    x1 = x1_ref[...]
    acc_lo, acc_hi = x1[:, :PACKED], x1[:, PACKED:]
    g = gate_ref[...].T
    for k in range(TOP_K):
        lo, hi = _unpack_rows(_load_rows(ybuf.at[slot, k]))
        acc_lo = acc_lo + g[:, k:k + 1] * lo
        acc_hi = acc_hi + g[:, k:k + 1] * hi
    o_ref[:, :PACKED] = acc_lo
    o_ref[:, PACKED:] = acc_hi


def _combine(dest, x1, gates, y_rows):
    T = x1.shape[0]
    tc = min(COMBINE_TC, T)
    n = T // tc
    return pl.pallas_call(
        functools.partial(_combine_kernel, tc=tc),
        grid=(n,),
        in_specs=[
            pl.BlockSpec((tc * TOP_K,), lambda i: (i,), memory_space=pltpu.SMEM),
            pl.BlockSpec((tc * TOP_K,), lambda i: (jnp.minimum(i + 1, n - 1),), memory_space=pltpu.SMEM),
            pl.BlockSpec((tc, D_MODEL), lambda i: (i, 0)),
            pl.BlockSpec((2 * TOP_K, tc), lambda i: (0, i)),
            pl.BlockSpec(memory_space=pl.ANY),
        ],
        out_specs=pl.BlockSpec((tc, D_MODEL), lambda i: (i, 0)),
        out_shape=jax.ShapeDtypeStruct((T, D_MODEL), F32),
        scratch_shapes=[pltpu.VMEM((2, TOP_K, tc * ROW_SUB, LANES), jnp.uint32),
                        pltpu.SemaphoreType.DMA((2,))],
        compiler_params=_cparams(("arbitrary",)),
        name="combine",
    )(dest, dest, x1, gates, y_rows)


def _pad_lanes(v):
    return jnp.pad(v, ((0, 0), (0, LANES - v.shape[-1])))


def kernel(x, mix_norm_g, w_in, q_norm_g, k_norm_g, lambda_q1, lambda_k1, lambda_q2, lambda_k2,
           attn_subln_g, conv_w, conv_b, dt_bias, a_log, d_skip, ssm_norm_g, w_attn_proj,
           w_ssm_proj, w_out, ffn_norm_g, w_router, b_router, w_gate_up, b_gate_up, w_down, b_down):
    B, S, D = x.shape
    T = B * S
    xf = x.reshape(T, D)
    layer = 0

    wi = w_in[layer]
    o_z = 3 * ATT_WIDTH
    o_xbc = o_z + SSM_D_INNER
    o_dt = o_xbc + SSM_XBC
    o_ga = o_dt + SSM_HEADS
    w_main = jnp.concatenate([wi[:, o_xbc:o_dt], wi[:, o_z:o_xbc], wi[:, :o_z], wi[:, o_ga:]],
                             axis=1).astype(BF16)
    w_dt = _pad_lanes(wi[:, o_dt:o_ga]).astype(BF16)
    proj, dt_raw = _in_proj(xf, mix_norm_g[layer][None, :], w_main, w_dt)

    half = ATT_HEAD_DIM // 2
    inv = ROPE_THETA ** (-jnp.arange(0, ATT_HEAD_DIM, 2, dtype=F32) / ATT_HEAD_DIM)
    ang = jnp.arange(S, dtype=F32)[:, None] * inv[None, :]
    cos2 = jnp.tile(jnp.cos(ang), (1, LANES // half))
    sin2 = jnp.tile(jnp.concatenate([-jnp.sin(ang), jnp.sin(ang)], axis=1), (1, LANES // ATT_HEAD_DIM))
    qg2 = jnp.tile(q_norm_g[layer], 2)[None, :]
    kg2 = jnp.tile(k_norm_g[layer], 2)[None, :]
    partner = lambda g: jnp.tile(jnp.roll(g, half), 2)[None, :]
    q_scale = ATT_HEAD_DIM ** -0.5 * LOG2E
    att = _attention(proj, cos2 * qg2 * q_scale, sin2 * partner(q_norm_g[layer]) * q_scale,
                     cos2 * kg2, sin2 * partner(k_norm_g[layer]), qg2, kg2,
                     lambda_q1[layer][None, :], lambda_k1[layer][None, :],
                     lambda_q2[layer][None, :], lambda_k2[layer][None, :],
                     attn_subln_g[layer][None, :], B, S)

    ssm = _ssd(proj, dt_raw, conv_w[layer], conv_b[layer][None, :],
               _pad_lanes(dt_bias[layer][None, :]), _pad_lanes(a_log[layer][None, :]),
               jnp.repeat(d_skip[layer], SSM_HEAD_DIM)[None, :], ssm_norm_g[layer][None, :], B, S)

    x1, h2p, idx_rank_t, gates_t, counts = _merge(
        att, ssm, proj, xf, w_attn_proj[layer].astype(BF16), w_ssm_proj[layer].astype(BF16),
        w_out[layer].astype(BF16), ffn_norm_g[layer][None, :],
        _pad_lanes(w_router[layer]).astype(BF16), _pad_lanes(b_router[layer][None, :]))

    bm = EXPERT_BM
    A = T * TOP_K
    n_rows = (A + N_EXPERTS * (bm - 1)) // bm * bm
    cnt = counts[:, 0]
    padded = (cnt + bm - 1) // bm * bm
    pend = jnp.cumsum(padded)
    pstart = pend - padded
    n_used = (pend[-1:] // bm).astype(jnp.int32)
    block_start = jnp.arange(n_rows // bm, dtype=jnp.int32) * bm
    block_expert = jnp.minimum(jnp.sum(block_start[:, None] >= pend[None, :], axis=1),
                               N_EXPERTS - 1).astype(jnp.int32)
    idx_rank = idx_rank_t.T
    onehot = idx_rank[:, :TOP_K, None] == jnp.arange(N_EXPERTS, dtype=jnp.int32)[None, None, :]
    dest = (jnp.sum(jnp.where(onehot, pstart[None, None, :], 0), axis=-1)
            + idx_rank[:, TOP_K:]).reshape(A).astype(jnp.int32)

    rows = _dispatch((pend - bm).astype(jnp.int32), cnt.astype(jnp.int32), dest, h2p, n_rows, bm)
    y_rows = _experts(block_expert, n_used, rows, w_gate_up[layer], b_gate_up[layer][:, None, :],
                      w_down[layer], b_down[layer][:, None, :], bm)
    out = _combine(dest, x1, gates_t, y_rows)
    return out.reshape(B, S, D)
```
